```python
import math
import jax
import jax.numpy as jnp
from jax import lax
import numpy as np

D_MODEL = 1024
BATCH = 8
SEQ = 4096
DEPTH = 4

GRID_W = 64
CTX_LEN = 256

MIXERS = ('pool', 'attn', 'ssm', 'gmlp')
N_MIXERS = len(MIXERS)
CTX_READING_MIXERS = ('attn', 'ssm')

DEEPNORM_ALPHA = (2.0 * DEPTH) ** 0.25
DEEPNORM_BETA = (8.0 * DEPTH) ** -0.25
LN_EPS = 1e-5
N_MODS = 6

POOL_WINDOWS = (2, 4, 8, 16)
N_POOL_GROUPS = len(POOL_WINDOWS)
POOL_GROUP = D_MODEL // N_POOL_GROUPS

HEAD_DIM = 64
N_Q_HEADS = D_MODEL // HEAD_DIM
N_KV_HEADS = N_Q_HEADS // 4
GQA_GROUP = N_Q_HEADS // N_KV_HEADS
Q_WIDTH = N_Q_HEADS * HEAD_DIM
KV_WIDTH = N_KV_HEADS * HEAD_DIM
WINDOW = 128
ATTN_BLOCK = 128
ROPE_BASE = 10000.0
NEG_INF = -1e30

SSM_GROUP = 16
SSM_N_GROUPS = D_MODEL // SSM_GROUP
SSM_STATE = 64
DT_MIN = 1e-3
DT_MAX = 1e-1

GMLP_CHUNK = 128
GMLP_HALF = 2 * D_MODEL
GMLP_HEADS = 8
GMLP_HEAD_DIM = GMLP_HALF // GMLP_HEADS

FFN_HIDDEN = 2816
CONV_WIDTH = 3

kernel_name = 'hybrid_interleaved_diffusion_trunk'


def layer_norm(x, g, b):
    xf = x.astype(jnp.float32)
    mu = jnp.mean(xf, axis=-1, keepdims=True)
    var = jnp.mean(jnp.square(xf - mu), axis=-1, keepdims=True)
    y = (xf - mu) * lax.rsqrt(var + LN_EPS) * g.astype(jnp.float32) + b.astype(jnp.float32)
    return y.astype(x.dtype)


def modulate(x, shift, scale):
    return x * (1.0 + scale) + shift


def ada_modulations(cond, w, b):
    return jnp.split(jax.nn.silu(cond) @ w + b, N_MODS, axis=-1)


def post_norm_residual(x, y, gate, g, b):
    return layer_norm(DEEPNORM_ALPHA * x + gate * y, g, b)


def pool_mixer(h, w, b, scale):
    bsz, n, _ = h.shape
    hf = h.astype(jnp.float32)
    csum = jnp.concatenate([jnp.zeros_like(hf[:, :1]), lax.cumsum(hf, axis=1)], axis=1)
    csum = csum.reshape(bsz, n + 1, N_POOL_GROUPS, POOL_GROUP)
    pos = jnp.arange(n)[:, None]
    win = jnp.array(POOL_WINDOWS)[None, :]
    lo = jnp.clip(pos - win // 2, 0, n)
    hi = jnp.clip(pos - win // 2 + win, 0, n)
    grp = jnp.arange(N_POOL_GROUPS)[None, :]
    mean = (csum[:, hi, grp] - csum[:, lo, grp]) / (hi - lo).astype(jnp.float32)[None, :, :, None]
    mixed = mean - hf.reshape(bsz, n, N_POOL_GROUPS, POOL_GROUP)
    y = jnp.einsum('bngc,gcd->bngd', mixed.astype(h.dtype), w) + b.reshape(N_POOL_GROUPS, POOL_GROUP)
    return y.reshape(bsz, n, D_MODEL) * scale


def rope_1d(x, pos):
    half = x.shape[-1] // 2
    freqs = ROPE_BASE ** (-jnp.arange(half, dtype=jnp.float32) / half)
    ang = pos.astype(jnp.float32)[:, None] * freqs[None, :]
    cos = jnp.cos(ang)[None, :, None, :]
    sin = jnp.sin(ang)[None, :, None, :]
    xf = x.astype(jnp.float32)
    x1, x2 = xf[..., :half], xf[..., half:]
    return jnp.concatenate([x1 * cos - x2 * sin, x1 * sin + x2 * cos], axis=-1).astype(x.dtype)


def axial_rope(x, row_pos, col_pos):
    half = HEAD_DIM // 2
    return jnp.concatenate([rope_1d(x[..., :half], row_pos), rope_1d(x[..., half:], col_pos)], axis=-1)


def banded_attention(q, k, v, k_ctx, v_ctx, sink):
    bsz, s_len = q.shape[:2]
    nb = s_len // ATTN_BLOCK
    scale = HEAD_DIM ** -0.5
    qb = q.reshape(bsz, nb, ATTN_BLOCK, N_KV_HEADS, GQA_GROUP, HEAD_DIM)
    pad = ((0, 0), (ATTN_BLOCK, ATTN_BLOCK), (0, 0), (0, 0))

    def band(t):
        tp = jnp.pad(t, pad).reshape(bsz, nb + 2, ATTN_BLOCK, N_KV_HEADS, HEAD_DIM)
        return jnp.concatenate([tp[:, :-2], tp[:, 1:-1], tp[:, 2:]], axis=2)

    kb, vb = band(k), band(v)
    s_loc = jnp.einsum('bnqhgd,bnkhd->bnhgqk', qb, kb).astype(jnp.float32) * scale
    q_pos = jnp.arange(nb)[:, None] * ATTN_BLOCK + jnp.arange(ATTN_BLOCK)[None, :]
    k_pos = (jnp.arange(nb)[:, None] - 1) * ATTN_BLOCK + jnp.arange(3 * ATTN_BLOCK)[None, :]
    rel = k_pos[:, None, :] - q_pos[:, :, None]
    valid = (jnp.abs(rel) <= WINDOW) & (k_pos[:, None, :] >= 0) & (k_pos[:, None, :] < s_len)
    s_loc = jnp.where(valid[None, :, None, None], s_loc, NEG_INF)
    s_ctx = jnp.einsum('bnqhgd,bkhd->bnhgqk', qb, k_ctx).astype(jnp.float32) * scale
    s_sink = jnp.broadcast_to(sink[None, None, :, :, None, None], s_loc.shape[:-1] + (1,))
    p = jax.nn.softmax(jnp.concatenate([s_loc, s_ctx, s_sink], axis=-1), axis=-1).astype(v.dtype)
    n_loc = 3 * ATTN_BLOCK
    n_ctx = k_ctx.shape[1]
    o = (jnp.einsum('bnhgqk,bnkhd->bnqhgd', p[..., :n_loc], vb)
         + jnp.einsum('bnhgqk,bkhd->bnqhgd', p[..., n_loc:n_loc + n_ctx], v_ctx))
    return o.reshape(bsz, s_len, Q_WIDTH)


def context_attention(q, k, v, sink):
    s = jnp.einsum('bqhgd,bkhd->bhgqk', q, k).astype(jnp.float32) * HEAD_DIM ** -0.5
    s_sink = jnp.broadcast_to(sink[None, :, :, None, None], s.shape[:-1] + (1,))
    p = jax.nn.softmax(jnp.concatenate([s, s_sink], axis=-1), axis=-1).astype(v.dtype)
    o = jnp.einsum('bhgqk,bkhd->bqhgd', p[..., :-1], v)
    return o.reshape(q.shape[0], q.shape[1], Q_WIDTH)


def attn_mixer(h_lat, h_ctx, w_qkv, w_o, sink, row_pos, col_pos, need_ctx_out):
    bsz, s_len, _ = h_lat.shape
    n_ctx = h_ctx.shape[1]
    qkv = h_lat @ w_qkv
    q = axial_rope(qkv[..., :Q_WIDTH].reshape(bsz, s_len, N_Q_HEADS, HEAD_DIM), row_pos, col_pos)
    k = axial_rope(qkv[..., Q_WIDTH:Q_WIDTH + KV_WIDTH].reshape(bsz, s_len, N_KV_HEADS, HEAD_DIM), row_pos, col_pos)
    v = qkv[..., Q_WIDTH + KV_WIDTH:].reshape(bsz, s_len, N_KV_HEADS, HEAD_DIM)
    kv_ctx = h_ctx @ w_qkv[:, Q_WIDTH:]
    k_ctx = kv_ctx[..., :KV_WIDTH].reshape(bsz, n_ctx, N_KV_HEADS, HEAD_DIM)
    v_ctx = kv_ctx[..., KV_WIDTH:].reshape(bsz, n_ctx, N_KV_HEADS, HEAD_DIM)
    sink_logit = sink.astype(jnp.float32).reshape(N_KV_HEADS, GQA_GROUP)
    q = q.reshape(bsz, s_len, N_KV_HEADS, GQA_GROUP, HEAD_DIM)
    y_lat = banded_attention(q, k, v, k_ctx, v_ctx, sink_logit) @ w_o
    y_ctx = None
    if need_ctx_out:
        q_ctx = (h_ctx @ w_qkv[:, :Q_WIDTH]).reshape(bsz, n_ctx, N_KV_HEADS, GQA_GROUP, HEAD_DIM)
        y_ctx = context_attention(q_ctx, k_ctx, v_ctx, sink_logit) @ w_o
    return y_lat, y_ctx


def s5_discretise(lam_re, lam_im, log_dt, b_re, b_im):
    lam = lax.complex(lam_re.astype(jnp.float32), lam_im.astype(jnp.float32))
    dt = jnp.exp(log_dt.astype(jnp.float32))[:, None]
    lam_bar = jnp.exp(lam * dt)
    b = lax.complex(b_re.astype(jnp.float32), b_im.astype(jnp.float32))
    b_bar = ((lam_bar - 1.0) / lam)[..., None] * b
    return lam_bar, b_bar


def _linear_recurrence(e1, e2):
    a1, b1 = e1
    a2, b2 = e2
    return a1 * a2, a2 * b1 + b2


def s5_scan(u, lam_bar, b_bar, s0, reverse):
    bu = jnp.einsum('bngc,gpc->bngp', u.astype(jnp.complex64), b_bar)
    if s0 is not None:
        edge = u.shape[1] - 1 if reverse else 0
        bu = bu.at[:, edge].add(lam_bar[None] * s0)
    a = jnp.broadcast_to(lam_bar, (1, u.shape[1]) + lam_bar.shape)
    _, states = lax.associative_scan(_linear_recurrence, (a, bu), reverse=reverse, axis=1)
    return states


def ssm_mixer(h_lat, h_ctx, lam_re, lam_im, log_dt, b_re, b_im, c_re, c_im, d_skip, w_a, w_b, need_ctx_out):
    def groups(h):
        return h.astype(jnp.float32).reshape(h.shape[0], h.shape[1], SSM_N_GROUPS, SSM_GROUP)

    def readout(states, c_mat):
        y = jnp.real(jnp.einsum('bngp,gcp->bngc', states, c_mat))
        return y.reshape(states.shape[0], states.shape[1], D_MODEL)

    def glu(y, dtype):
        g = jax.nn.gelu(y).astype(dtype)
        return (g @ w_a) * jax.nn.sigmoid(g @ w_b)

    u_lat, u_ctx = groups(h_lat), groups(h_ctx)
    d32 = d_skip.astype(jnp.float32)
    y_lat = d32 * h_lat.astype(jnp.float32)
    y_ctx = d32 * h_ctx.astype(jnp.float32) if need_ctx_out else None
    for direction, reverse in enumerate((False, True)):
        lam_bar, b_bar = s5_discretise(lam_re[direction], lam_im[direction], log_dt[direction],
                                       b_re[direction], b_im[direction])
        c_mat = lax.complex(c_re[direction].astype(jnp.float32), c_im[direction].astype(jnp.float32))
        ctx_states = s5_scan(u_ctx, lam_bar, b_bar, None, reverse)
        ctx_final = ctx_states[:, 0] if reverse else ctx_states[:, -1]
        y_lat = y_lat + readout(s5_scan(u_lat, lam_bar, b_bar, ctx_final, reverse), c_mat)
        if need_ctx_out:
            y_ctx = y_ctx + readout(ctx_states, c_mat)
    out_ctx = glu(y_ctx, h_ctx.dtype) if need_ctx_out else None
    return glu(y_lat, h_lat.dtype), out_ctx


def gmlp_mixer(h, w_in, b_in, ln_g, ln_b, w_s, b_s, w_out):
    bsz, n, _ = h.shape
    z = jax.nn.gelu(h @ w_in + b_in)
    u = z[..., :GMLP_HALF]
    v = layer_norm(z[..., GMLP_HALF:], ln_g, ln_b)
    vc = v.reshape(bsz, n // GMLP_CHUNK, GMLP_CHUNK, GMLP_HEADS, GMLP_HEAD_DIM)
    gate = jnp.einsum('hpq,bnqhc->bnphc', w_s, vc) + b_s.T[None, None, :, :, None]
    return (u * gate.reshape(bsz, n, GMLP_HALF)) @ w_out


def conv_ffn(h, w_up, conv_w, conv_b, w_down):
    n = h.shape[1]
    a = h @ w_up
    pad = CONV_WIDTH // 2
    ap = jnp.pad(a, ((0, 0), (pad, pad), (0, 0)))
    a = conv_b + ap[:, 0:n] * conv_w[0]
    for tap in range(1, CONV_WIDTH):
        a = a + ap[:, tap:tap + n] * conv_w[tap]
    val, gate = a[..., :FFN_HIDDEN], a[..., FFN_HIDDEN:]
    return (val * jax.nn.silu(gate)) @ w_down


def _normal(key, shape, std):
    return std * jax.random.normal(key, shape, dtype=jnp.float32)


def _n_layers_of(kind):
    return len(range(MIXERS.index(kind), DEPTH, N_MIXERS))


def _fwd_setup_inputs(seed: int = 0) -> dict:
    key = jax.random.key(seed)
    keys = iter(jax.random.split(key, 48))
    D = D_MODEL
    n_pool, n_attn, n_ssm, n_gmlp = (_n_layers_of(k) for k in MIXERS)
    qkv_width = Q_WIDTH + 2 * KV_WIDTH
    G, P = SSM_N_GROUPS, SSM_STATE
    return {
        'x': _normal(next(keys), (BATCH, SEQ, D), 1.0),
        'c': _normal(next(keys), (BATCH, D), 1.0),
        'ctx': _normal(next(keys), (BATCH, CTX_LEN, D), 1.0),
        'c_ctx': _normal(next(keys), (D,), 1.0),
        'ada_w': _normal(next(keys), (DEPTH, D, N_MODS * D), 0.5 * D ** -0.5),
        'ada_b': _normal(next(keys), (DEPTH, N_MODS * D), 0.02),
        'ln1_g': 1.0 + _normal(next(keys), (DEPTH, D), 0.02),
        'ln1_b': _normal(next(keys), (DEPTH, D), 0.02),
        'ln2_g': 1.0 + _normal(next(keys), (DEPTH, D), 0.02),
        'ln2_b': _normal(next(keys), (DEPTH, D), 0.02),
        'ffn_w_up': _normal(next(keys), (DEPTH, D, 2 * FFN_HIDDEN), D ** -0.5),
        'ffn_conv_w': _normal(next(keys), (DEPTH, CONV_WIDTH, 2 * FFN_HIDDEN), CONV_WIDTH ** -0.5),
        'ffn_conv_b': _normal(next(keys), (DEPTH, 2 * FFN_HIDDEN), 0.02),
        'ffn_w_down': _normal(next(keys), (DEPTH, FFN_HIDDEN, D), FFN_HIDDEN ** -0.5 * DEEPNORM_BETA),
        'pool_w': _normal(next(keys), (n_pool, N_POOL_GROUPS, POOL_GROUP, POOL_GROUP), POOL_GROUP ** -0.5 * DEEPNORM_BETA),
        'pool_b': _normal(next(keys), (n_pool, D), 0.02),
        'pool_scale': 1.0 + _normal(next(keys), (n_pool, D), 0.02),
        'attn_w_qkv': _normal(next(keys), (n_attn, D, qkv_width), D ** -0.5),
        'attn_w_o': _normal(next(keys), (n_attn, Q_WIDTH, D), Q_WIDTH ** -0.5 * DEEPNORM_BETA),
        'attn_sink': _normal(next(keys), (n_attn, N_Q_HEADS), 0.5),
        'ssm_lambda_re': -0.5 + _normal(next(keys), (n_ssm, 2, G, P), 1e-3),
        'ssm_lambda_im': jnp.pi * jnp.arange(P, dtype=jnp.float32) + _normal(next(keys), (n_ssm, 2, G, P), 1e-3),
        'ssm_log_dt': jax.random.uniform(next(keys), (n_ssm, 2, G), dtype=jnp.float32,
                                         minval=math.log(DT_MIN), maxval=math.log(DT_MAX)),
        'ssm_b_re': _normal(next(keys), (n_ssm, 2, G, P, SSM_GROUP), (2 * SSM_GROUP) ** -0.5),
        'ssm_b_im': _normal(next(keys), (n_ssm, 2, G, P, SSM_GROUP), (2 * SSM_GROUP) ** -0.5),
        'ssm_c_re': _normal(next(keys), (n_ssm, 2, G, SSM_GROUP, P), (2 * P) ** -0.5),
        'ssm_c_im': _normal(next(keys), (n_ssm, 2, G, SSM_GROUP, P), (2 * P) ** -0.5),
        'ssm_d': _normal(next(keys), (n_ssm, D), 1.0),
        'ssm_w_glu_a': _normal(next(keys), (n_ssm, D, D), D ** -0.5 * DEEPNORM_BETA),
        'ssm_w_glu_b': _normal(next(keys), (n_ssm, D, D), D ** -0.5),
        'gmlp_w_in': _normal(next(keys), (n_gmlp, D, 2 * GMLP_HALF), D ** -0.5),
        'gmlp_b_in': _normal(next(keys), (n_gmlp, 2 * GMLP_HALF), 0.02),
        'gmlp_ln_g': 1.0 + _normal(next(keys), (n_gmlp, GMLP_HALF), 0.02),
        'gmlp_ln_b': _normal(next(keys), (n_gmlp, GMLP_HALF), 0.02),
        'gmlp_w_s': _normal(next(keys), (n_gmlp, GMLP_HEADS, GMLP_CHUNK, GMLP_CHUNK), GMLP_CHUNK ** -0.5),
        'gmlp_b_s': 1.0 + _normal(next(keys), (n_gmlp, GMLP_HEADS, GMLP_CHUNK), 0.02),
        'gmlp_w_out': _normal(next(keys), (n_gmlp, GMLP_HALF, D), GMLP_HALF ** -0.5 * DEEPNORM_BETA),
    }


def _fwd_reference(x, c, ctx, c_ctx, ada_w, ada_b, ln1_g, ln1_b, ln2_g, ln2_b,
              ffn_w_up, ffn_conv_w, ffn_conv_b, ffn_w_down,
              pool_w, pool_b, pool_scale,
              attn_w_qkv, attn_w_o, attn_sink,
              ssm_lambda_re, ssm_lambda_im, ssm_log_dt, ssm_b_re, ssm_b_im, ssm_c_re, ssm_c_im,
              ssm_d, ssm_w_glu_a, ssm_w_glu_b,
              gmlp_w_in, gmlp_b_in, gmlp_ln_g, gmlp_ln_b, gmlp_w_s, gmlp_b_s, gmlp_w_out):
    seq_len = x.shape[1]
    ROWS = seq_len // GRID_W
    row_pos = jnp.repeat(jnp.arange(ROWS), GRID_W)
    col_pos = jnp.tile(jnp.arange(GRID_W), ROWS)
    x_lat, x_ctx = x, ctx
    for layer in range(DEPTH):
        kind = MIXERS[layer % N_MIXERS]
        j = layer // N_MIXERS
        ctx_out = any(MIXERS[m % N_MIXERS] in CTX_READING_MIXERS for m in range(layer + 1, DEPTH))
        ctx_in = ctx_out or kind in CTX_READING_MIXERS
        sh1, sc1, gt1, sh2, sc2, gt2 = [m[:, None, :] for m in ada_modulations(c, ada_w[layer], ada_b[layer])]
        h_lat = modulate(x_lat, sh1, sc1)
        h_ctx = None
        if ctx_in:
            csh1, csc1, cgt1, csh2, csc2, cgt2 = ada_modulations(c_ctx, ada_w[layer], ada_b[layer])
            h_ctx = modulate(x_ctx, csh1, csc1)
        if kind == 'pool':
            y_lat = pool_mixer(h_lat, pool_w[j], pool_b[j], pool_scale[j])
            y_ctx = pool_mixer(h_ctx, pool_w[j], pool_b[j], pool_scale[j]) if ctx_out else None
        elif kind == 'attn':
            y_lat, y_ctx = attn_mixer(h_lat, h_ctx, attn_w_qkv[j], attn_w_o[j], attn_sink[j],
                                      row_pos, col_pos, ctx_out)
        elif kind == 'ssm':
            y_lat, y_ctx = ssm_mixer(h_lat, h_ctx, ssm_lambda_re[j], ssm_lambda_im[j], ssm_log_dt[j],
                                     ssm_b_re[j], ssm_b_im[j], ssm_c_re[j], ssm_c_im[j], ssm_d[j],
                                     ssm_w_glu_a[j], ssm_w_glu_b[j], ctx_out)
        else:
            y_lat = gmlp_mixer(h_lat, gmlp_w_in[j], gmlp_b_in[j], gmlp_ln_g[j], gmlp_ln_b[j],
                               gmlp_w_s[j], gmlp_b_s[j], gmlp_w_out[j])
            y_ctx = (gmlp_mixer(h_ctx, gmlp_w_in[j], gmlp_b_in[j], gmlp_ln_g[j], gmlp_ln_b[j],
                                gmlp_w_s[j], gmlp_b_s[j], gmlp_w_out[j]) if ctx_out else None)
        x_lat = post_norm_residual(x_lat, y_lat, gt1, ln1_g[layer], ln1_b[layer])
        f_lat = conv_ffn(modulate(x_lat, sh2, sc2), ffn_w_up[layer], ffn_conv_w[layer],
                         ffn_conv_b[layer], ffn_w_down[layer])
        x_lat = post_norm_residual(x_lat, f_lat, gt2, ln2_g[layer], ln2_b[layer])
        if ctx_out:
            x_ctx = post_norm_residual(x_ctx, y_ctx, cgt1, ln1_g[layer], ln1_b[layer])
            f_ctx = conv_ffn(modulate(x_ctx, csh2, csc2), ffn_w_up[layer], ffn_conv_w[layer],
                             ffn_conv_b[layer], ffn_w_down[layer])
            x_ctx = post_norm_residual(x_ctx, f_ctx, cgt2, ln2_g[layer], ln2_b[layer])
    return x_lat


import jax as _jax
import jax.numpy as _jnp

TWIN_FORMAT = 'train_step'
FWD_PARAMS = ['x', 'c', 'ctx', 'c_ctx', 'ada_w', 'ada_b', 'ln1_g', 'ln1_b', 'ln2_g', 'ln2_b', 'ffn_w_up', 'ffn_conv_w', 'ffn_conv_b', 'ffn_w_down', 'pool_w', 'pool_b', 'pool_scale', 'attn_w_qkv', 'attn_w_o', 'attn_sink', 'ssm_lambda_re', 'ssm_lambda_im', 'ssm_log_dt', 'ssm_b_re', 'ssm_b_im', 'ssm_c_re', 'ssm_c_im', 'ssm_d', 'ssm_w_glu_a', 'ssm_w_glu_b', 'gmlp_w_in', 'gmlp_b_in', 'gmlp_ln_g', 'gmlp_ln_b', 'gmlp_w_s', 'gmlp_b_s', 'gmlp_w_out']
TWIN_WEIGHTS = ['c_ctx', 'ada_w', 'ada_b', 'ln1_g', 'ln1_b', 'ln2_g', 'ln2_b', 'ffn_w_up', 'ffn_conv_w', 'ffn_conv_b', 'ffn_w_down', 'pool_w', 'pool_b', 'pool_scale', 'attn_w_qkv', 'attn_w_o', 'attn_sink', 'ssm_lambda_re', 'ssm_lambda_im', 'ssm_log_dt', 'ssm_b_re', 'ssm_b_im', 'ssm_c_re', 'ssm_c_im', 'ssm_d', 'ssm_w_glu_a', 'ssm_w_glu_b', 'gmlp_w_in', 'gmlp_b_in', 'gmlp_ln_g', 'gmlp_ln_b', 'gmlp_w_s', 'gmlp_b_s', 'gmlp_w_out']
TWIN_DIFF_INPUT = 'x'
TWIN_INPUTS = ['x', 'c', 'ctx', 'c_ctx', 'ada_w', 'ada_b', 'ln1_g', 'ln1_b', 'ln2_g', 'ln2_b', 'ffn_w_up', 'ffn_conv_w', 'ffn_conv_b', 'ffn_w_down', 'pool_w', 'pool_b', 'pool_scale', 'attn_w_qkv', 'attn_w_o', 'attn_sink', 'ssm_lambda_re', 'ssm_lambda_im', 'ssm_log_dt', 'ssm_b_re', 'ssm_b_im', 'ssm_c_re', 'ssm_c_im', 'ssm_d', 'ssm_w_glu_a', 'ssm_w_glu_b', 'gmlp_w_in', 'gmlp_b_in', 'gmlp_ln_g', 'gmlp_ln_b', 'gmlp_w_s', 'gmlp_b_s', 'gmlp_w_out', 'loss_target', 'm_c_ctx', 'm_ada_w', 'm_ada_b', 'm_ln1_g', 'm_ln1_b', 'm_ln2_g', 'm_ln2_b', 'm_ffn_w_up', 'm_ffn_conv_w', 'm_ffn_conv_b', 'm_ffn_w_down', 'm_pool_w', 'm_pool_b', 'm_pool_scale', 'm_attn_w_qkv', 'm_attn_w_o', 'm_attn_sink', 'm_ssm_lambda_re', 'm_ssm_lambda_im', 'm_ssm_log_dt', 'm_ssm_b_re', 'm_ssm_b_im', 'm_ssm_c_re', 'm_ssm_c_im', 'm_ssm_d', 'm_ssm_w_glu_a', 'm_ssm_w_glu_b', 'm_gmlp_w_in', 'm_gmlp_b_in', 'm_gmlp_ln_g', 'm_gmlp_ln_b', 'm_gmlp_w_s', 'm_gmlp_b_s', 'm_gmlp_w_out', 'v_c_ctx', 'v_ada_w', 'v_ada_b', 'v_ln1_g', 'v_ln1_b', 'v_ln2_g', 'v_ln2_b', 'v_ffn_w_up', 'v_ffn_conv_w', 'v_ffn_conv_b', 'v_ffn_w_down', 'v_pool_w', 'v_pool_b', 'v_pool_scale', 'v_attn_w_qkv', 'v_attn_w_o', 'v_attn_sink', 'v_ssm_lambda_re', 'v_ssm_lambda_im', 'v_ssm_log_dt', 'v_ssm_b_re', 'v_ssm_b_im', 'v_ssm_c_re', 'v_ssm_c_im', 'v_ssm_d', 'v_ssm_w_glu_a', 'v_ssm_w_glu_b', 'v_gmlp_w_in', 'v_gmlp_b_in', 'v_gmlp_ln_g', 'v_gmlp_ln_b', 'v_gmlp_w_s', 'v_gmlp_b_s', 'v_gmlp_w_out']
TWIN_OUTPUTS = ['loss', 'grad_x', 'grad_c_ctx', 'grad_ada_w', 'grad_ada_b', 'grad_ln1_g', 'grad_ln1_b', 'grad_ln2_g', 'grad_ln2_b', 'grad_ffn_w_up', 'grad_ffn_conv_w', 'grad_ffn_conv_b', 'grad_ffn_w_down', 'grad_pool_w', 'grad_pool_b', 'grad_pool_scale', 'grad_attn_w_qkv', 'grad_attn_w_o', 'grad_attn_sink', 'grad_ssm_lambda_re', 'grad_ssm_lambda_im', 'grad_ssm_log_dt', 'grad_ssm_b_re', 'grad_ssm_b_im', 'grad_ssm_c_re', 'grad_ssm_c_im', 'grad_ssm_d', 'grad_ssm_w_glu_a', 'grad_ssm_w_glu_b', 'grad_gmlp_w_in', 'grad_gmlp_b_in', 'grad_gmlp_ln_g', 'grad_gmlp_ln_b', 'grad_gmlp_w_s', 'grad_gmlp_b_s', 'grad_gmlp_w_out', 'delta_c_ctx', 'delta_ada_w', 'delta_ada_b', 'delta_ln1_g', 'delta_ln1_b', 'delta_ln2_g', 'delta_ln2_b', 'delta_ffn_w_up', 'delta_ffn_conv_w', 'delta_ffn_conv_b', 'delta_ffn_w_down', 'delta_pool_w', 'delta_pool_b', 'delta_pool_scale', 'delta_attn_w_qkv', 'delta_attn_w_o', 'delta_attn_sink', 'delta_ssm_lambda_re', 'delta_ssm_lambda_im', 'delta_ssm_log_dt', 'delta_ssm_b_re', 'delta_ssm_b_im', 'delta_ssm_c_re', 'delta_ssm_c_im', 'delta_ssm_d', 'delta_ssm_w_glu_a', 'delta_ssm_w_glu_b', 'delta_gmlp_w_in', 'delta_gmlp_b_in', 'delta_gmlp_ln_g', 'delta_gmlp_ln_b', 'delta_gmlp_w_s', 'delta_gmlp_b_s', 'delta_gmlp_w_out', 'new_m_c_ctx', 'new_m_ada_w', 'new_m_ada_b', 'new_m_ln1_g', 'new_m_ln1_b', 'new_m_ln2_g', 'new_m_ln2_b', 'new_m_ffn_w_up', 'new_m_ffn_conv_w', 'new_m_ffn_conv_b', 'new_m_ffn_w_down', 'new_m_pool_w', 'new_m_pool_b', 'new_m_pool_scale', 'new_m_attn_w_qkv', 'new_m_attn_w_o', 'new_m_attn_sink', 'new_m_ssm_lambda_re', 'new_m_ssm_lambda_im', 'new_m_ssm_log_dt', 'new_m_ssm_b_re', 'new_m_ssm_b_im', 'new_m_ssm_c_re', 'new_m_ssm_c_im', 'new_m_ssm_d', 'new_m_ssm_w_glu_a', 'new_m_ssm_w_glu_b', 'new_m_gmlp_w_in', 'new_m_gmlp_b_in', 'new_m_gmlp_ln_g', 'new_m_gmlp_ln_b', 'new_m_gmlp_w_s', 'new_m_gmlp_b_s', 'new_m_gmlp_w_out', 'new_v_c_ctx', 'new_v_ada_w', 'new_v_ada_b', 'new_v_ln1_g', 'new_v_ln1_b', 'new_v_ln2_g', 'new_v_ln2_b', 'new_v_ffn_w_up', 'new_v_ffn_conv_w', 'new_v_ffn_conv_b', 'new_v_ffn_w_down', 'new_v_pool_w', 'new_v_pool_b', 'new_v_pool_scale', 'new_v_attn_w_qkv', 'new_v_attn_w_o', 'new_v_attn_sink', 'new_v_ssm_lambda_re', 'new_v_ssm_lambda_im', 'new_v_ssm_log_dt', 'new_v_ssm_b_re', 'new_v_ssm_b_im', 'new_v_ssm_c_re', 'new_v_ssm_c_im', 'new_v_ssm_d', 'new_v_ssm_w_glu_a', 'new_v_ssm_w_glu_b', 'new_v_gmlp_w_in', 'new_v_gmlp_b_in', 'new_v_gmlp_ln_g', 'new_v_gmlp_ln_b', 'new_v_gmlp_w_s', 'new_v_gmlp_b_s', 'new_v_gmlp_w_out']
TWIN_LEAF_KINDS = {'loss': 'loss', 'grad_x': 'grad_x', 'grad_c_ctx': 'grad_w', 'grad_ada_w': 'grad_w', 'grad_ada_b': 'grad_w', 'grad_ln1_g': 'grad_w', 'grad_ln1_b': 'grad_w', 'grad_ln2_g': 'grad_w', 'grad_ln2_b': 'grad_w', 'grad_ffn_w_up': 'grad_w', 'grad_ffn_conv_w': 'grad_w', 'grad_ffn_conv_b': 'grad_w', 'grad_ffn_w_down': 'grad_w', 'grad_pool_w': 'grad_w', 'grad_pool_b': 'grad_w', 'grad_pool_scale': 'grad_w', 'grad_attn_w_qkv': 'grad_w', 'grad_attn_w_o': 'grad_w', 'grad_attn_sink': 'grad_w', 'grad_ssm_lambda_re': 'grad_w', 'grad_ssm_lambda_im': 'grad_w', 'grad_ssm_log_dt': 'grad_w', 'grad_ssm_b_re': 'grad_w', 'grad_ssm_b_im': 'grad_w', 'grad_ssm_c_re': 'grad_w', 'grad_ssm_c_im': 'grad_w', 'grad_ssm_d': 'grad_w', 'grad_ssm_w_glu_a': 'grad_w', 'grad_ssm_w_glu_b': 'grad_w', 'grad_gmlp_w_in': 'grad_w', 'grad_gmlp_b_in': 'grad_w', 'grad_gmlp_ln_g': 'grad_w', 'grad_gmlp_ln_b': 'grad_w', 'grad_gmlp_w_s': 'grad_w', 'grad_gmlp_b_s': 'grad_w', 'grad_gmlp_w_out': 'grad_w', 'delta_c_ctx': 'delta_w', 'delta_ada_w': 'delta_w', 'delta_ada_b': 'delta_w', 'delta_ln1_g': 'delta_w', 'delta_ln1_b': 'delta_w', 'delta_ln2_g': 'delta_w', 'delta_ln2_b': 'delta_w', 'delta_ffn_w_up': 'delta_w', 'delta_ffn_conv_w': 'delta_w', 'delta_ffn_conv_b': 'delta_w', 'delta_ffn_w_down': 'delta_w', 'delta_pool_w': 'delta_w', 'delta_pool_b': 'delta_w', 'delta_pool_scale': 'delta_w', 'delta_attn_w_qkv': 'delta_w', 'delta_attn_w_o': 'delta_w', 'delta_attn_sink': 'delta_w', 'delta_ssm_lambda_re': 'delta_w', 'delta_ssm_lambda_im': 'delta_w', 'delta_ssm_log_dt': 'delta_w', 'delta_ssm_b_re': 'delta_w', 'delta_ssm_b_im': 'delta_w', 'delta_ssm_c_re': 'delta_w', 'delta_ssm_c_im': 'delta_w', 'delta_ssm_d': 'delta_w', 'delta_ssm_w_glu_a': 'delta_w', 'delta_ssm_w_glu_b': 'delta_w', 'delta_gmlp_w_in': 'delta_w', 'delta_gmlp_b_in': 'delta_w', 'delta_gmlp_ln_g': 'delta_w', 'delta_gmlp_ln_b': 'delta_w', 'delta_gmlp_w_s': 'delta_w', 'delta_gmlp_b_s': 'delta_w', 'delta_gmlp_w_out': 'delta_w', 'new_m_c_ctx': 'new_m', 'new_m_ada_w': 'new_m', 'new_m_ada_b': 'new_m', 'new_m_ln1_g': 'new_m', 'new_m_ln1_b': 'new_m', 'new_m_ln2_g': 'new_m', 'new_m_ln2_b': 'new_m', 'new_m_ffn_w_up': 'new_m', 'new_m_ffn_conv_w': 'new_m', 'new_m_ffn_conv_b': 'new_m', 'new_m_ffn_w_down': 'new_m', 'new_m_pool_w': 'new_m', 'new_m_pool_b': 'new_m', 'new_m_pool_scale': 'new_m', 'new_m_attn_w_qkv': 'new_m', 'new_m_attn_w_o': 'new_m', 'new_m_attn_sink': 'new_m', 'new_m_ssm_lambda_re': 'new_m', 'new_m_ssm_lambda_im': 'new_m', 'new_m_ssm_log_dt': 'new_m', 'new_m_ssm_b_re': 'new_m', 'new_m_ssm_b_im': 'new_m', 'new_m_ssm_c_re': 'new_m', 'new_m_ssm_c_im': 'new_m', 'new_m_ssm_d': 'new_m', 'new_m_ssm_w_glu_a': 'new_m', 'new_m_ssm_w_glu_b': 'new_m', 'new_m_gmlp_w_in': 'new_m', 'new_m_gmlp_b_in': 'new_m', 'new_m_gmlp_ln_g': 'new_m', 'new_m_gmlp_ln_b': 'new_m', 'new_m_gmlp_w_s': 'new_m', 'new_m_gmlp_b_s': 'new_m', 'new_m_gmlp_w_out': 'new_m', 'new_v_c_ctx': 'new_v', 'new_v_ada_w': 'new_v', 'new_v_ada_b': 'new_v', 'new_v_ln1_g': 'new_v', 'new_v_ln1_b': 'new_v', 'new_v_ln2_g': 'new_v', 'new_v_ln2_b': 'new_v', 'new_v_ffn_w_up': 'new_v', 'new_v_ffn_conv_w': 'new_v', 'new_v_ffn_conv_b': 'new_v', 'new_v_ffn_w_down': 'new_v', 'new_v_pool_w': 'new_v', 'new_v_pool_b': 'new_v', 'new_v_pool_scale': 'new_v', 'new_v_attn_w_qkv': 'new_v', 'new_v_attn_w_o': 'new_v', 'new_v_attn_sink': 'new_v', 'new_v_ssm_lambda_re': 'new_v', 'new_v_ssm_lambda_im': 'new_v', 'new_v_ssm_log_dt': 'new_v', 'new_v_ssm_b_re': 'new_v', 'new_v_ssm_b_im': 'new_v', 'new_v_ssm_c_re': 'new_v', 'new_v_ssm_c_im': 'new_v', 'new_v_ssm_d': 'new_v', 'new_v_ssm_w_glu_a': 'new_v', 'new_v_ssm_w_glu_b': 'new_v', 'new_v_gmlp_w_in': 'new_v', 'new_v_gmlp_b_in': 'new_v', 'new_v_gmlp_ln_g': 'new_v', 'new_v_gmlp_ln_b': 'new_v', 'new_v_gmlp_w_s': 'new_v', 'new_v_gmlp_b_s': 'new_v', 'new_v_gmlp_w_out': 'new_v'}


def _forward(args):
    return _fwd_reference(*[args[k] for k in FWD_PARAMS])


def _output_shape():
    def fwd():
        inp = _fwd_setup_inputs(0)
        return _fwd_reference(*[inp[k] for k in FWD_PARAMS])
    out = _jax.eval_shape(fwd)
    return out.shape, out.dtype

N_MICROBATCH = 1
ADAM_LR = 0.001
ADAM_B1 = 0.9
ADAM_B2 = 0.999
ADAM_EPS = 1e-08
ADAM_WD = 0.01
ADAM_STEP = 10
PER_EXAMPLE_BATCH_AXIS = {'x': 0, 'c': 0, 'ctx': 0, 'loss_target': 0}
SHARED_INPUTS = []
_WEIGHT_DTYPES = {'c_ctx': _jnp.float32, 'ada_w': _jnp.float32, 'ada_b': _jnp.float32, 'ln1_g': _jnp.float32, 'ln1_b': _jnp.float32, 'ln2_g': _jnp.float32, 'ln2_b': _jnp.float32, 'ffn_w_up': _jnp.float32, 'ffn_conv_w': _jnp.float32, 'ffn_conv_b': _jnp.float32, 'ffn_w_down': _jnp.float32, 'pool_w': _jnp.float32, 'pool_b': _jnp.float32, 'pool_scale': _jnp.float32, 'attn_w_qkv': _jnp.float32, 'attn_w_o': _jnp.float32, 'attn_sink': _jnp.float32, 'ssm_lambda_re': _jnp.float32, 'ssm_lambda_im': _jnp.float32, 'ssm_log_dt': _jnp.float32, 'ssm_b_re': _jnp.float32, 'ssm_b_im': _jnp.float32, 'ssm_c_re': _jnp.float32, 'ssm_c_im': _jnp.float32, 'ssm_d': _jnp.float32, 'ssm_w_glu_a': _jnp.float32, 'ssm_w_glu_b': _jnp.float32, 'gmlp_w_in': _jnp.float32, 'gmlp_b_in': _jnp.float32, 'gmlp_ln_g': _jnp.float32, 'gmlp_ln_b': _jnp.float32, 'gmlp_w_s': _jnp.float32, 'gmlp_b_s': _jnp.float32, 'gmlp_w_out': _jnp.float32}
MOMENT_SCALE = {'c_ctx': 3.550422e-03, 'ada_w': 1.312514e-02, 'ada_b': 2.377926e-02, 'ln1_g': 1.238935e+00, 'ln1_b': 5.896876e-01, 'ln2_g': 1.612355e+01, 'ln2_b': 1.022211e+00, 'ffn_w_up': 6.025347e-03, 'ffn_conv_w': 6.069674e-03, 'ffn_conv_b': 5.801912e-03, 'ffn_w_down': 2.349530e-02, 'pool_w': 2.937938e-02, 'pool_b': 5.109342e-02, 'pool_scale': 1.278742e-02, 'attn_w_qkv': 4.463831e-03, 'attn_w_o': 1.052887e-02, 'attn_sink': 4.787338e-05, 'ssm_lambda_re': 3.186877e-04, 'ssm_lambda_im': 3.016968e-04, 'ssm_log_dt': 1.497191e-01, 'ssm_b_re': 2.002898e-04, 'ssm_b_im': 2.018890e-04, 'ssm_c_re': 3.908990e-04, 'ssm_c_im': 4.146879e-04, 'ssm_d': 6.639430e-03, 'ssm_w_glu_a': 1.362063e-02, 'ssm_w_glu_b': 1.639651e-03, 'gmlp_w_in': 8.820871e-03, 'gmlp_b_in': 9.225078e-03, 'gmlp_ln_g': 6.678691e-03, 'gmlp_ln_b': 7.383947e-03, 'gmlp_w_s': 9.411628e-03, 'gmlp_b_s': 9.251654e-03, 'gmlp_w_out': 3.418396e-02}


def _to_microbatches(a, axis):
    t = _jnp.moveaxis(a, axis, 0)
    t = t.reshape((N_MICROBATCH, t.shape[0] // N_MICROBATCH) + t.shape[1:])
    return _jnp.moveaxis(t, 1, axis + 1)


def setup_inputs(seed: int = 0) -> dict:
    inp = _fwd_setup_inputs(seed)
    key = _jax.random.fold_in(_jax.random.key(seed), 7919)
    shape, _ = _output_shape()
    out = dict(inp)
    out["loss_target"] = _jax.random.normal(_jax.random.fold_in(key, 0), shape, _jnp.float32)
    for i, name in enumerate(TWIN_WEIGHTS):
        w = inp[name].astype(_jnp.float32)
        if MOMENT_SCALE is None:
            s = _jnp.sqrt(_jnp.mean(_jnp.square(w)) + 1e-30)
        else:
            s = MOMENT_SCALE[name]
        km, kv = _jax.random.split(_jax.random.fold_in(key, i + 1))
        out[name] = w
        out["m_" + name] = s * _jax.random.normal(km, w.shape, _jnp.float32)
        out["v_" + name] = (s * s) * _jax.random.uniform(kv, w.shape, _jnp.float32, 0.5, 1.5)
    if N_MICROBATCH > 1:
        for name, axis in PER_EXAMPLE_BATCH_AXIS.items():
            out[name] = _to_microbatches(out[name], axis)
    return {'x': out['x'], 'c': out['c'], 'ctx': out['ctx'], 'c_ctx': out['c_ctx'], 'ada_w': out['ada_w'], 'ada_b': out['ada_b'], 'ln1_g': out['ln1_g'], 'ln1_b': out['ln1_b'], 'ln2_g': out['ln2_g'], 'ln2_b': out['ln2_b'], 'ffn_w_up': out['ffn_w_up'], 'ffn_conv_w': out['ffn_conv_w'], 'ffn_conv_b': out['ffn_conv_b'], 'ffn_w_down': out['ffn_w_down'], 'pool_w': out['pool_w'], 'pool_b': out['pool_b'], 'pool_scale': out['pool_scale'], 'attn_w_qkv': out['attn_w_qkv'], 'attn_w_o': out['attn_w_o'], 'attn_sink': out['attn_sink'], 'ssm_lambda_re': out['ssm_lambda_re'], 'ssm_lambda_im': out['ssm_lambda_im'], 'ssm_log_dt': out['ssm_log_dt'], 'ssm_b_re': out['ssm_b_re'], 'ssm_b_im': out['ssm_b_im'], 'ssm_c_re': out['ssm_c_re'], 'ssm_c_im': out['ssm_c_im'], 'ssm_d': out['ssm_d'], 'ssm_w_glu_a': out['ssm_w_glu_a'], 'ssm_w_glu_b': out['ssm_w_glu_b'], 'gmlp_w_in': out['gmlp_w_in'], 'gmlp_b_in': out['gmlp_b_in'], 'gmlp_ln_g': out['gmlp_ln_g'], 'gmlp_ln_b': out['gmlp_ln_b'], 'gmlp_w_s': out['gmlp_w_s'], 'gmlp_b_s': out['gmlp_b_s'], 'gmlp_w_out': out['gmlp_w_out'], 'loss_target': out['loss_target'], 'm_c_ctx': out['m_c_ctx'], 'm_ada_w': out['m_ada_w'], 'm_ada_b': out['m_ada_b'], 'm_ln1_g': out['m_ln1_g'], 'm_ln1_b': out['m_ln1_b'], 'm_ln2_g': out['m_ln2_g'], 'm_ln2_b': out['m_ln2_b'], 'm_ffn_w_up': out['m_ffn_w_up'], 'm_ffn_conv_w': out['m_ffn_conv_w'], 'm_ffn_conv_b': out['m_ffn_conv_b'], 'm_ffn_w_down': out['m_ffn_w_down'], 'm_pool_w': out['m_pool_w'], 'm_pool_b': out['m_pool_b'], 'm_pool_scale': out['m_pool_scale'], 'm_attn_w_qkv': out['m_attn_w_qkv'], 'm_attn_w_o': out['m_attn_w_o'], 'm_attn_sink': out['m_attn_sink'], 'm_ssm_lambda_re': out['m_ssm_lambda_re'], 'm_ssm_lambda_im': out['m_ssm_lambda_im'], 'm_ssm_log_dt': out['m_ssm_log_dt'], 'm_ssm_b_re': out['m_ssm_b_re'], 'm_ssm_b_im': out['m_ssm_b_im'], 'm_ssm_c_re': out['m_ssm_c_re'], 'm_ssm_c_im': out['m_ssm_c_im'], 'm_ssm_d': out['m_ssm_d'], 'm_ssm_w_glu_a': out['m_ssm_w_glu_a'], 'm_ssm_w_glu_b': out['m_ssm_w_glu_b'], 'm_gmlp_w_in': out['m_gmlp_w_in'], 'm_gmlp_b_in': out['m_gmlp_b_in'], 'm_gmlp_ln_g': out['m_gmlp_ln_g'], 'm_gmlp_ln_b': out['m_gmlp_ln_b'], 'm_gmlp_w_s': out['m_gmlp_w_s'], 'm_gmlp_b_s': out['m_gmlp_b_s'], 'm_gmlp_w_out': out['m_gmlp_w_out'], 'v_c_ctx': out['v_c_ctx'], 'v_ada_w': out['v_ada_w'], 'v_ada_b': out['v_ada_b'], 'v_ln1_g': out['v_ln1_g'], 'v_ln1_b': out['v_ln1_b'], 'v_ln2_g': out['v_ln2_g'], 'v_ln2_b': out['v_ln2_b'], 'v_ffn_w_up': out['v_ffn_w_up'], 'v_ffn_conv_w': out['v_ffn_conv_w'], 'v_ffn_conv_b': out['v_ffn_conv_b'], 'v_ffn_w_down': out['v_ffn_w_down'], 'v_pool_w': out['v_pool_w'], 'v_pool_b': out['v_pool_b'], 'v_pool_scale': out['v_pool_scale'], 'v_attn_w_qkv': out['v_attn_w_qkv'], 'v_attn_w_o': out['v_attn_w_o'], 'v_attn_sink': out['v_attn_sink'], 'v_ssm_lambda_re': out['v_ssm_lambda_re'], 'v_ssm_lambda_im': out['v_ssm_lambda_im'], 'v_ssm_log_dt': out['v_ssm_log_dt'], 'v_ssm_b_re': out['v_ssm_b_re'], 'v_ssm_b_im': out['v_ssm_b_im'], 'v_ssm_c_re': out['v_ssm_c_re'], 'v_ssm_c_im': out['v_ssm_c_im'], 'v_ssm_d': out['v_ssm_d'], 'v_ssm_w_glu_a': out['v_ssm_w_glu_a'], 'v_ssm_w_glu_b': out['v_ssm_w_glu_b'], 'v_gmlp_w_in': out['v_gmlp_w_in'], 'v_gmlp_b_in': out['v_gmlp_b_in'], 'v_gmlp_ln_g': out['v_gmlp_ln_g'], 'v_gmlp_ln_b': out['v_gmlp_ln_b'], 'v_gmlp_w_s': out['v_gmlp_w_s'], 'v_gmlp_b_s': out['v_gmlp_b_s'], 'v_gmlp_w_out': out['v_gmlp_w_out']}


def _loss(weights, diff, rest, loss_target):
    with _jax.named_scope("forward"):
        args = {**rest, TWIN_DIFF_INPUT: diff, **{k: w.astype(_WEIGHT_DTYPES[k]) for k, w in weights.items()}}
        y = _forward(args)
    with _jax.named_scope("loss_head"):
        err = _jnp.square(y.astype(_jnp.float32) - loss_target)
        return 0.5 * _jnp.sum(_jnp.mean(err, axis=-1)) if err.ndim else 0.5 * err


def _adamw(w, g, m, v):
    m = ADAM_B1 * m + (1.0 - ADAM_B1) * g
    v = ADAM_B2 * v + (1.0 - ADAM_B2) * _jnp.square(g)
    m_hat = m / (1.0 - ADAM_B1 ** ADAM_STEP)
    v_hat = v / (1.0 - ADAM_B2 ** ADAM_STEP)
    delta = -ADAM_LR * (m_hat / (_jnp.sqrt(v_hat) + ADAM_EPS) + ADAM_WD * w)
    return delta, m, v


def reference(x, c, ctx, c_ctx, ada_w, ada_b, ln1_g, ln1_b, ln2_g, ln2_b, ffn_w_up, ffn_conv_w, ffn_conv_b, ffn_w_down, pool_w, pool_b, pool_scale, attn_w_qkv, attn_w_o, attn_sink, ssm_lambda_re, ssm_lambda_im, ssm_log_dt, ssm_b_re, ssm_b_im, ssm_c_re, ssm_c_im, ssm_d, ssm_w_glu_a, ssm_w_glu_b, gmlp_w_in, gmlp_b_in, gmlp_ln_g, gmlp_ln_b, gmlp_w_s, gmlp_b_s, gmlp_w_out, loss_target, m_c_ctx, m_ada_w, m_ada_b, m_ln1_g, m_ln1_b, m_ln2_g, m_ln2_b, m_ffn_w_up, m_ffn_conv_w, m_ffn_conv_b, m_ffn_w_down, m_pool_w, m_pool_b, m_pool_scale, m_attn_w_qkv, m_attn_w_o, m_attn_sink, m_ssm_lambda_re, m_ssm_lambda_im, m_ssm_log_dt, m_ssm_b_re, m_ssm_b_im, m_ssm_c_re, m_ssm_c_im, m_ssm_d, m_ssm_w_glu_a, m_ssm_w_glu_b, m_gmlp_w_in, m_gmlp_b_in, m_gmlp_ln_g, m_gmlp_ln_b, m_gmlp_w_s, m_gmlp_b_s, m_gmlp_w_out, v_c_ctx, v_ada_w, v_ada_b, v_ln1_g, v_ln1_b, v_ln2_g, v_ln2_b, v_ffn_w_up, v_ffn_conv_w, v_ffn_conv_b, v_ffn_w_down, v_pool_w, v_pool_b, v_pool_scale, v_attn_w_qkv, v_attn_w_o, v_attn_sink, v_ssm_lambda_re, v_ssm_lambda_im, v_ssm_log_dt, v_ssm_b_re, v_ssm_b_im, v_ssm_c_re, v_ssm_c_im, v_ssm_d, v_ssm_w_glu_a, v_ssm_w_glu_b, v_gmlp_w_in, v_gmlp_b_in, v_gmlp_ln_g, v_gmlp_ln_b, v_gmlp_w_s, v_gmlp_b_s, v_gmlp_w_out):
    given = dict(x=x, c=c, ctx=ctx, c_ctx=c_ctx, ada_w=ada_w, ada_b=ada_b, ln1_g=ln1_g, ln1_b=ln1_b, ln2_g=ln2_g, ln2_b=ln2_b, ffn_w_up=ffn_w_up, ffn_conv_w=ffn_conv_w, ffn_conv_b=ffn_conv_b, ffn_w_down=ffn_w_down, pool_w=pool_w, pool_b=pool_b, pool_scale=pool_scale, attn_w_qkv=attn_w_qkv, attn_w_o=attn_w_o, attn_sink=attn_sink, ssm_lambda_re=ssm_lambda_re, ssm_lambda_im=ssm_lambda_im, ssm_log_dt=ssm_log_dt, ssm_b_re=ssm_b_re, ssm_b_im=ssm_b_im, ssm_c_re=ssm_c_re, ssm_c_im=ssm_c_im, ssm_d=ssm_d, ssm_w_glu_a=ssm_w_glu_a, ssm_w_glu_b=ssm_w_glu_b, gmlp_w_in=gmlp_w_in, gmlp_b_in=gmlp_b_in, gmlp_ln_g=gmlp_ln_g, gmlp_ln_b=gmlp_ln_b, gmlp_w_s=gmlp_w_s, gmlp_b_s=gmlp_b_s, gmlp_w_out=gmlp_w_out, loss_target=loss_target, m_c_ctx=m_c_ctx, m_ada_w=m_ada_w, m_ada_b=m_ada_b, m_ln1_g=m_ln1_g, m_ln1_b=m_ln1_b, m_ln2_g=m_ln2_g, m_ln2_b=m_ln2_b, m_ffn_w_up=m_ffn_w_up, m_ffn_conv_w=m_ffn_conv_w, m_ffn_conv_b=m_ffn_conv_b, m_ffn_w_down=m_ffn_w_down, m_pool_w=m_pool_w, m_pool_b=m_pool_b, m_pool_scale=m_pool_scale, m_attn_w_qkv=m_attn_w_qkv, m_attn_w_o=m_attn_w_o, m_attn_sink=m_attn_sink, m_ssm_lambda_re=m_ssm_lambda_re, m_ssm_lambda_im=m_ssm_lambda_im, m_ssm_log_dt=m_ssm_log_dt, m_ssm_b_re=m_ssm_b_re, m_ssm_b_im=m_ssm_b_im, m_ssm_c_re=m_ssm_c_re, m_ssm_c_im=m_ssm_c_im, m_ssm_d=m_ssm_d, m_ssm_w_glu_a=m_ssm_w_glu_a, m_ssm_w_glu_b=m_ssm_w_glu_b, m_gmlp_w_in=m_gmlp_w_in, m_gmlp_b_in=m_gmlp_b_in, m_gmlp_ln_g=m_gmlp_ln_g, m_gmlp_ln_b=m_gmlp_ln_b, m_gmlp_w_s=m_gmlp_w_s, m_gmlp_b_s=m_gmlp_b_s, m_gmlp_w_out=m_gmlp_w_out, v_c_ctx=v_c_ctx, v_ada_w=v_ada_w, v_ada_b=v_ada_b, v_ln1_g=v_ln1_g, v_ln1_b=v_ln1_b, v_ln2_g=v_ln2_g, v_ln2_b=v_ln2_b, v_ffn_w_up=v_ffn_w_up, v_ffn_conv_w=v_ffn_conv_w, v_ffn_conv_b=v_ffn_conv_b, v_ffn_w_down=v_ffn_w_down, v_pool_w=v_pool_w, v_pool_b=v_pool_b, v_pool_scale=v_pool_scale, v_attn_w_qkv=v_attn_w_qkv, v_attn_w_o=v_attn_w_o, v_attn_sink=v_attn_sink, v_ssm_lambda_re=v_ssm_lambda_re, v_ssm_lambda_im=v_ssm_lambda_im, v_ssm_log_dt=v_ssm_log_dt, v_ssm_b_re=v_ssm_b_re, v_ssm_b_im=v_ssm_b_im, v_ssm_c_re=v_ssm_c_re, v_ssm_c_im=v_ssm_c_im, v_ssm_d=v_ssm_d, v_ssm_w_glu_a=v_ssm_w_glu_a, v_ssm_w_glu_b=v_ssm_w_glu_b, v_gmlp_w_in=v_gmlp_w_in, v_gmlp_b_in=v_gmlp_b_in, v_gmlp_ln_g=v_gmlp_ln_g, v_gmlp_ln_b=v_gmlp_ln_b, v_gmlp_w_s=v_gmlp_w_s, v_gmlp_b_s=v_gmlp_b_s, v_gmlp_w_out=v_gmlp_w_out)
    weights = {n: given[n] for n in TWIN_WEIGHTS}
    shared = {n: given[n] for n in SHARED_INPUTS}
    per_example = {n: given[n] for n in ['x', 'c', 'ctx']}
    grad_fn = _jax.value_and_grad(_loss, argnums=(0, 1))

    def one_microbatch(ex, loss_target):
        ex = dict(ex)
        diff = ex.pop(TWIN_DIFF_INPUT)
        return grad_fn(weights, diff, {**shared, **ex}, loss_target)

    if N_MICROBATCH == 1:
        loss, (grad_w, grad_x) = one_microbatch(per_example, given["loss_target"])
    else:
        def body(carry, xs):
            loss_sum, grad_sum = carry
            l_k, (gw_k, gx_k) = one_microbatch(xs[0], xs[1])
            with _jax.named_scope("update"):
                return (loss_sum + l_k, _jax.tree.map(_jnp.add, grad_sum, gw_k)), gx_k

        init = (_jnp.zeros((), _jnp.float32), _jax.tree.map(_jnp.zeros_like, weights))
        (loss, grad_w), grad_x = _jax.lax.scan(body, init, (per_example, given["loss_target"]))
    with _jax.named_scope("update"):
        delta_w, new_m, new_v = {}, {}, {}
        for n in TWIN_WEIGHTS:
            delta_w[n], new_m[n], new_v[n] = _adamw(weights[n], grad_w[n], given["m_" + n], given["v_" + n])
    return (loss, grad_x, *[grad_w[n] for n in TWIN_WEIGHTS], *[delta_w[n] for n in TWIN_WEIGHTS],
            *[new_m[n] for n in TWIN_WEIGHTS], *[new_v[n] for n in TWIN_WEIGHTS])
```

```python
import functools
import math

import jax
import jax.numpy as jnp
import numpy as np
from jax import lax
from jax.experimental import pallas as pl
from jax.experimental.pallas import tpu as pltpu

D = 1024
SEQ = 4096
NCTX = 256
T = NCTX + SEQ
DEPTH = 4
NDEV = 8
GRID_W = 64
ALPHA = (2.0 * DEPTH) ** 0.25
LN_EPS = 1e-5
FFN_H = 2816
HEAD_DIM = 64
NQH, NKVH, GQA = 16, 4, 4
WINDOW = 128
ABLK = 128
NEG_INF = -1e30
ROPE_BASE = 10000.0
POOL_WINDOWS = (2, 4, 8, 16)
SSM_G, SSM_P, SSM_C = 64, 64, 16
GM_HALF = 2048
GM_HEADS = 8
GM_HD = GM_HALF // GM_HEADS
GM_CHUNK = 128
B1, B2, LR, EPS, WD, STEP = 0.9, 0.999, 0.001, 1e-8, 0.01, 10

LANE = 128
SUBLANE = 8
VMEM_LIMIT = 56 * 1024 * 1024
TM = 256
NT = T // TM

_MXU = jnp.bfloat16
F32 = jnp.float32


def _pcall(body, **kw):
    return pl.pallas_call(body, **kw)


def _cp(sem):
    return pltpu.CompilerParams(dimension_semantics=sem, vmem_limit_bytes=VMEM_LIMIT)


def _tile(dim, pref, align):
    best = None
    for t in range(align, min(dim, pref) + 1, align):
        if dim % t == 0:
            best = t
    return dim if best is None else best


def _mm(a, b, *, ta=False, tb=False, out_dtype=F32, name):
    if ta:
        K, M = a.shape
    else:
        M, K = a.shape
    if tb:
        N, K2 = b.shape
    else:
        K2, N = b.shape
    assert K == K2, (a.shape, b.shape, ta, tb)
    tm = _tile(M, 2304, 16) if not ta else _tile(M, 1024, LANE)
    tn = _tile(N, 512, LANE)
    tk = _tile(K, 1408, LANE) if not ta else _tile(K, 1100, 16)
    nk = K // tk
    dims = (((0,) if ta else (1,), (1,) if tb else (0,)), ((), ()))

    def body(a_ref, b_ref, o_ref, acc_ref):
        k = pl.program_id(2)
        r = lax.dot_general(a_ref[...].astype(_MXU), b_ref[...].astype(_MXU), dims, preferred_element_type=F32)

        @pl.when(k == 0)
        def _():
            acc_ref[...] = r

        @pl.when(k > 0)
        def _():
            acc_ref[...] += r

        @pl.when(k == nk - 1)
        def _():
            o_ref[...] = acc_ref[...].astype(o_ref.dtype)

    a_spec = pl.BlockSpec((tk, tm), lambda j, i, k: (k, i)) if ta else pl.BlockSpec((tm, tk), lambda j, i, k: (i, k))
    b_spec = pl.BlockSpec((tn, tk), lambda j, i, k: (j, k)) if tb else pl.BlockSpec((tk, tn), lambda j, i, k: (k, j))
    return _pcall(
        body, name=name, grid=(N // tn, M // tm, nk), in_specs=[a_spec, b_spec],
        out_specs=pl.BlockSpec((tm, tn), lambda j, i, k: (i, j)),
        out_shape=jax.ShapeDtypeStruct((M, N), out_dtype),
        scratch_shapes=[pltpu.VMEM((tm, tn), F32)],
        compiler_params=_cp(("parallel", "parallel", "arbitrary")),
    )(a, b)


def _rows(fn, rows, pars, out_rows, out_pars, *, name, tm=TM):
    R = rows[0].shape[0]
    nt = R // tm
    nct = NCTX // tm
    n_r, n_p, n_or, n_op = len(rows), len(pars), len(out_rows), len(out_pars)

    def sel(S):
        if S == 1:
            return lambda i: 0
        return lambda i: jnp.where(i < nct, 0, 1)

    def body(*refs):
        r_in = refs[:n_r]
        p_in = refs[n_r:n_r + n_p]
        r_out = refs[n_r + n_p:n_r + n_p + n_or]
        p_out = refs[n_r + n_p + n_or:]
        i = pl.program_id(0)
        vals = [r[...].astype(F32) for r in r_in] + [p[0] for p in p_in]
        outs = fn(*vals)
        for r, v in zip(r_out, outs[:n_or]):
            r[...] = v.astype(r.dtype)
        for (S, _), r, v in zip(out_pars, p_out, outs[n_or:]):
            first = (i == 0) if S == 1 else jnp.logical_or(i == 0, i == nct)

            @pl.when(first)
            def _():
                r[0] = v

            @pl.when(jnp.logical_not(first))
            def _():
                r[0] += v

    def pspec(shape):
        S = shape[0]
        rest = tuple(shape[1:])
        s = sel(S)
        return pl.BlockSpec((1,) + rest, lambda i: (s(i),) + (0,) * len(rest))

    in_specs = [pl.BlockSpec((tm, r.shape[1]), lambda i: (i, 0)) for r in rows] + [pspec(p.shape) for p in pars]
    out_specs = [pl.BlockSpec((tm, w), lambda i: (i, 0)) for w, _ in out_rows] + [pspec((S,) + tuple(sh)) for S, sh in out_pars]
    out_shape = [jax.ShapeDtypeStruct((R, w), dt) for w, dt in out_rows] + \
                [jax.ShapeDtypeStruct((S,) + tuple(sh), F32) for S, sh in out_pars]
    res = _pcall(body, name=name, grid=(nt,), in_specs=in_specs, out_specs=out_specs, out_shape=out_shape,
                 compiler_params=_cp(("arbitrary",)))(*rows, *pars)
    return res


def _ln(z, g, b):
    mu = jnp.mean(z, axis=-1, keepdims=True)
    var = jnp.mean(jnp.square(z - mu), axis=-1, keepdims=True)
    return (z - mu) * lax.rsqrt(var + LN_EPS) * g + b


def _f1(x, sh, sc):
    return x * (1.0 + sc) + sh


def _f2(x, y, gt, g, b, sh, sc):
    x1 = _ln(ALPHA * x + gt * y, g, b)
    return x1, x1 * (1.0 + sc) + sh


def _f3(x1, f, gt, g, b):
    return _ln(ALPHA * x1 + gt * f, g, b)


def _pre_mixer(x, sh, sc, dtype):
    return _rows(lambda x, sh, sc: (_f1(x, sh, sc),), [x], [sh, sc], [(D, dtype)], [], name="pre_mixer")[0]


def _pre_mixer_bwd(x, dhs, dx_prev, sh, sc):
    n = len(dhs)

    def fn(x, *rest):
        dh = rest[0]
        for t in rest[1:n]:
            dh = dh + t
        dxp, sh, sc = rest[n], rest[n + 1], rest[n + 2]
        _, vjp = jax.vjp(_f1, x, sh, sc)
        dx, dsh, dsc = vjp(dh)
        return dxp + dx, dsh, dsc

    return _rows(fn, [x, *dhs, dx_prev], [sh, sc], [(D, F32)], [(2, (1, D)), (2, (1, D))], name="pre_mixer_bwd")


def _post_mixer(x, y, gt, g, b, sh, sc):
    return _rows(_f2, [x, y], [gt, g, b, sh, sc], [(D, F32), (D, _MXU)], [], name="post_mixer")


def _post_mixer_bwd(x, y, dx1, dh2, gt, g, b, sh, sc):
    def fn(x, y, dx1, dh2, gt, g, b, sh, sc):
        _, vjp = jax.vjp(_f2, x, y, gt, g, b, sh, sc)
        return vjp((dx1, dh2))

    return _rows(fn, [x, y, dx1, dh2], [gt, g, b, sh, sc], [(D, F32), (D, F32)],
                 [(2, (1, D)), (1, (1, D)), (1, (1, D)), (2, (1, D)), (2, (1, D))], name="post_mixer_bwd")


def _post_ffn(x1, f, gt, g, b):
    return _rows(lambda *a: (_f3(*a),), [x1, f], [gt, g, b], [(D, F32)], [], name="post_ffn")[0]


def _post_ffn_bwd(x1, f, dx2, gt, g, b):
    def fn(x1, f, dx2, gt, g, b):
        _, vjp = jax.vjp(_f3, x1, f, gt, g, b)
        return vjp(dx2)

    return _rows(fn, [x1, f, dx2], [gt, g, b], [(D, F32), (D, _MXU)],
                 [(2, (1, D)), (1, (1, D)), (1, (1, D))], name="post_ffn_bwd")


def _halo_specs(tm, w, col):
    r8 = tm // SUBLANE
    return [
        pl.BlockSpec((SUBLANE, w), lambda j, i: (jnp.maximum(i * r8 - 1, 0), col(j))),
        pl.BlockSpec((tm, w), lambda j, i: (i, col(j))),
        pl.BlockSpec((SUBLANE, w), lambda j, i: (jnp.minimum((i + 1) * r8, T // SUBLANE - 1), col(j))),
    ]


def _seg_flags(i, tm):
    nct = NCTX // tm
    first = jnp.logical_or(i == 0, i == nct)
    last = jnp.logical_or(i == nct - 1, i == T // tm - 1)
    return first, last


def _shift_rows(cur, prev8, next8, first, last):
    tm = cur.shape[0]
    rid = lax.broadcasted_iota(jnp.int32, cur.shape, 0)
    pr = jnp.where(first, 0.0, prev8[SUBLANE - 1:SUBLANE, :])
    nx = jnp.where(last, 0.0, next8[0:1, :])
    up = jnp.where(rid == 0, pr, pltpu.roll(cur, 1, 0))
    dn = jnp.where(rid == tm - 1, nx, pltpu.roll(cur, tm - 1, 0))
    return up, dn


FFN_TC = 1408
FFN_NCT = FFN_H // FFN_TC


def _conv3(cur, prev8, next8, w3, first, last):
    up, dn = _shift_rows(cur, prev8, next8, first, last)
    return up * w3[0:1] + cur * w3[1:2] + dn * w3[2:3], up, dn


def _ffn_mid(a, cw, cb):
    def body(vp, vc, vn, gp, gc, gn, cwv, cwg, cbv, cbg, o_ref):
        first, last = _seg_flags(pl.program_id(1), TM)
        val = _conv3(vc[...], vp[...], vn[...], cwv[...], first, last)[0] + cbv[...]
        gate = _conv3(gc[...], gp[...], gn[...], cwg[...], first, last)[0] + cbg[...]
        o_ref[...] = (val * jax.nn.silu(gate)).astype(o_ref.dtype)

    specs = _halo_specs(TM, FFN_TC, lambda j: j) + _halo_specs(TM, FFN_TC, lambda j: j + FFN_NCT)
    specs += [pl.BlockSpec((3, FFN_TC), lambda j, i: (0, j)), pl.BlockSpec((3, FFN_TC), lambda j, i: (0, j + FFN_NCT)),
              pl.BlockSpec((1, FFN_TC), lambda j, i: (0, j)), pl.BlockSpec((1, FFN_TC), lambda j, i: (0, j + FFN_NCT))]
    return _pcall(body, name="ffn_mid", grid=(FFN_NCT, NT), in_specs=specs,
                  out_specs=pl.BlockSpec((TM, FFN_TC), lambda j, i: (i, j)),
                  out_shape=jax.ShapeDtypeStruct((T, FFN_H), _MXU),
                  compiler_params=_cp(("parallel", "arbitrary")))(a, a, a, a, a, a, cw, cw, cb, cb)


def _ffn_mid_bwd1(a, du, cw, cb):
    def body(vp, vc, vn, gp, gc, gn, du_ref, cwv, cwg, cbv, cbg, dv_ref, dg_ref, dcwv, dcwg, dcbv, dcbg):
        i = pl.program_id(1)
        first, last = _seg_flags(i, TM)
        v0, vup, vdn = _conv3(vc[...], vp[...], vn[...], cwv[...], first, last)
        g0, gup, gdn = _conv3(gc[...], gp[...], gn[...], cwg[...], first, last)
        val = v0 + cbv[...]
        gate = g0 + cbg[...]
        sg = jax.nn.sigmoid(gate)
        du = du_ref[...]
        dval = du * (gate * sg)
        dgate = du * val * (sg * (1.0 + gate * (1.0 - sg)))
        dv_ref[...] = dval
        dg_ref[...] = dgate

        def acc(ref, v):
            @pl.when(i == 0)
            def _():
                ref[...] = v

            @pl.when(i > 0)
            def _():
                ref[...] += v

        for dref, d, up, cur, dn, bref in ((dcwv, dval, vup, vc[...], vdn, dcbv), (dcwg, dgate, gup, gc[...], gdn, dcbg)):
            acc(dref, jnp.concatenate([jnp.sum(d * up, 0, keepdims=True), jnp.sum(d * cur, 0, keepdims=True),
                                       jnp.sum(d * dn, 0, keepdims=True)], axis=0))
            acc(bref, jnp.sum(d, 0, keepdims=True))

    specs = _halo_specs(TM, FFN_TC, lambda j: j) + _halo_specs(TM, FFN_TC, lambda j: j + FFN_NCT)
    specs += [pl.BlockSpec((TM, FFN_TC), lambda j, i: (i, j))]
    specs += [pl.BlockSpec((3, FFN_TC), lambda j, i: (0, j)), pl.BlockSpec((3, FFN_TC), lambda j, i: (0, j + FFN_NCT)),
              pl.BlockSpec((1, FFN_TC), lambda j, i: (0, j)), pl.BlockSpec((1, FFN_TC), lambda j, i: (0, j + FFN_NCT))]
    out_specs = [pl.BlockSpec((TM, FFN_TC), lambda j, i: (i, j)), pl.BlockSpec((TM, FFN_TC), lambda j, i: (i, j)),
                 pl.BlockSpec((3, FFN_TC), lambda j, i: (0, j)), pl.BlockSpec((3, FFN_TC), lambda j, i: (0, j)),
                 pl.BlockSpec((1, FFN_TC), lambda j, i: (0, j)), pl.BlockSpec((1, FFN_TC), lambda j, i: (0, j))]
    out_shape = [jax.ShapeDtypeStruct((T, FFN_H), F32)] * 2 + [jax.ShapeDtypeStruct((3, FFN_H), F32)] * 2 + \
                [jax.ShapeDtypeStruct((1, FFN_H), F32)] * 2
    dv, dg, dcwv, dcwg, dcbv, dcbg = _pcall(
        body, name="ffn_mid_bwd1", grid=(FFN_NCT, NT), in_specs=specs, out_specs=out_specs, out_shape=out_shape,
        compiler_params=_cp(("parallel", "arbitrary")))(a, a, a, a, a, a, du, cw, cw, cb, cb)
    return dv, dg, jnp.concatenate([dcwv, dcwg], axis=1), jnp.concatenate([dcbv, dcbg], axis=1)


def _ffn_mid_bwd2(dv, dg, cw):
    tc = FFN_TC
    ncol = 2 * FFN_NCT

    def body(pv, cv, nv, pg, cg, ng, cw_ref, o_ref):
        j = pl.program_id(0)
        first, last = _seg_flags(pl.program_id(1), TM)
        w3 = cw_ref[...]
        w3r = jnp.concatenate([w3[2:3], w3[1:2], w3[0:1]], axis=0)
        is_val = j < FFN_NCT
        cur = jnp.where(is_val, cv[...], cg[...])
        p8 = jnp.where(is_val, pv[...], pg[...])
        n8 = jnp.where(is_val, nv[...], ng[...])
        o_ref[...] = _conv3(cur, p8, n8, w3r, first, last)[0].astype(o_ref.dtype)

    half = lambda j: j % FFN_NCT
    specs = _halo_specs(TM, tc, half) + _halo_specs(TM, tc, half) + [pl.BlockSpec((3, tc), lambda j, i: (0, j))]
    return _pcall(body, name="ffn_mid_bwd2", grid=(ncol, NT), in_specs=specs,
                  out_specs=pl.BlockSpec((TM, tc), lambda j, i: (i, j)),
                  out_shape=jax.ShapeDtypeStruct((T, 2 * FFN_H), _MXU),
                  compiler_params=_cp(("parallel", "arbitrary")))(dv, dv, dv, dg, dg, dg, cw)


def _ffn_fwd(h2, w_up, cw, cb, w_down):
    a = _mm(h2, w_up, name="ffn_up")
    u = _ffn_mid(a, cw, cb)
    f = _mm(u, w_down, name="ffn_down")
    return f, (a, u)


def _ffn_bwd(df, h2, a, u, w_up, cw, cb, w_down):
    dw_down = _mm(u, df, ta=True, name="ffn_down_dw")
    du = _mm(df, w_down, tb=True, name="ffn_down_dx")
    dv, dg, dcw, dcb = _ffn_mid_bwd1(a, du, cw, cb)
    da = _ffn_mid_bwd2(dv, dg, cw)
    dw_up = _mm(h2, da, ta=True, name="ffn_up_dw")
    dh2 = _mm(da, w_up, tb=True, name="ffn_up_dx")
    return dh2, dw_up, dcw, dcb, dw_down


def _pool_ext(p_ref, c_ref, n_ref, first, last):
    p8 = jnp.where(first, 0.0, p_ref[...])
    n8 = jnp.where(last, 0.0, n_ref[...])
    return jnp.concatenate([p8, c_ref[...], n8], axis=0)


def _winsum(e, lo, hi):
    n = e.shape[0]
    acc = None
    for o in range(lo, hi + 1):
        t = e if o == 0 else pltpu.roll(e, (-o) % n, 0)
        acc = t if acc is None else acc + t
    return acc


def _pool_cnt(i, w, rows, off):
    nct = NCTX // TM
    seg_len = jnp.where(i < nct, NCTX, SEQ)
    seg_tile = jnp.where(i < nct, i, i - nct)
    pos = lax.broadcasted_iota(jnp.int32, (rows, 1), 0) - off + seg_tile * TM
    lo = jnp.clip(pos - w // 2, 0, seg_len)
    hi = jnp.clip(pos - w // 2 + w, 0, seg_len)
    return jnp.maximum(hi - lo, 1).astype(F32)


def _pool_fwd(h, pw, pb, ps):
    def body(hp, hc, hn, w_ref, b_ref, s_ref, o_ref):
        i = pl.program_id(1)
        first, last = _seg_flags(i, TM)
        e = _pool_ext(hp, hc, hn, first, last)
        outs = []
        for g, w in enumerate(POOL_WINDOWS):
            sl = slice(256 * g, 256 * (g + 1))
            eg = e[:, sl]
            mean = _winsum(eg, -(w // 2), w // 2 - 1)[SUBLANE:SUBLANE + TM] / _pool_cnt(i, w, TM, 0)
            mixed = mean - hc[:, sl]
            outs.append(jnp.dot(mixed.astype(_MXU), w_ref[g].astype(_MXU), preferred_element_type=F32))
        o_ref[...] = (jnp.concatenate(outs, axis=1) + b_ref[...]) * s_ref[...]

    full = lambda *s: pl.BlockSpec(s, lambda j, i: (0,) * len(s))
    return _pcall(body, name="pool_fwd", grid=(1, NT), in_specs=_halo_specs(TM, D, lambda j: 0) + [full(4, 256, 256), full(1, D), full(1, D)],
                  out_specs=pl.BlockSpec((TM, D), lambda j, i: (i, 0)), out_shape=jax.ShapeDtypeStruct((T, D), F32),
                  compiler_params=_cp(("parallel", "arbitrary")))(h, h, h, pw, pb, ps)


def _pool_bwd(h, dy, pw, pb, ps):
    E = TM + 2 * SUBLANE

    def body(hp, hc, hn, dp, dc, dn, w_ref, b_ref, s_ref, dh_ref, dw_ref, db_ref, ds_ref):
        i = pl.program_id(1)
        first, last = _seg_flags(i, TM)
        e = _pool_ext(hp, hc, hn, first, last)
        de = _pool_ext(dp, dc, dn, first, last)
        dys = de * s_ref[...]
        dhs, pre = [], []
        for g, w in enumerate(POOL_WINDOWS):
            sl = slice(256 * g, 256 * (g + 1))
            wg = w_ref[g].astype(_MXU)
            dyg = dys[:, sl].astype(_MXU)
            dmix = lax.dot_general(dyg, wg, (((1,), (1,)), ((), ())), preferred_element_type=F32)
            q = dmix / _pool_cnt(i, w, E, SUBLANE)
            dhs.append(_winsum(q, -(w // 2) + 1, w // 2)[SUBLANE:SUBLANE + TM] - dmix[SUBLANE:SUBLANE + TM])
            mean = _winsum(e[:, sl], -(w // 2), w // 2 - 1)[SUBLANE:SUBLANE + TM] / _pool_cnt(i, w, TM, 0)
            mixed = (mean - hc[:, sl]).astype(_MXU)
            pre.append(jnp.dot(mixed, wg, preferred_element_type=F32))
            dwg = lax.dot_general(mixed, dyg[SUBLANE:SUBLANE + TM], (((0,), (0,)), ((), ())), preferred_element_type=F32)

            @pl.when(i == 0)
            def _():
                dw_ref[g] = dwg

            @pl.when(i > 0)
            def _():
                dw_ref[g] += dwg
        dh_ref[...] = jnp.concatenate(dhs, axis=1)
        db = jnp.sum(dys[SUBLANE:SUBLANE + TM], 0, keepdims=True)
        ds = jnp.sum(dc[...] * (jnp.concatenate(pre, axis=1) + b_ref[...]), 0, keepdims=True)

        @pl.when(i == 0)
        def _():
            db_ref[...] = db
            ds_ref[...] = ds

        @pl.when(i > 0)
        def _():
            db_ref[...] += db
            ds_ref[...] += ds

    full = lambda *s: pl.BlockSpec(s, lambda j, i: (0,) * len(s))
    specs = _halo_specs(TM, D, lambda j: 0) + _halo_specs(TM, D, lambda j: 0) + [full(4, 256, 256), full(1, D), full(1, D)]
    return _pcall(body, name="pool_bwd", grid=(1, NT), in_specs=specs,
                  out_specs=[pl.BlockSpec((TM, D), lambda j, i: (i, 0)), full(4, 256, 256), full(1, D), full(1, D)],
                  out_shape=[jax.ShapeDtypeStruct((T, D), F32), jax.ShapeDtypeStruct((4, 256, 256), F32),
                             jax.ShapeDtypeStruct((1, D), F32), jax.ShapeDtypeStruct((1, D), F32)],
                  compiler_params=_cp(("parallel", "arbitrary")))(h, h, h, dy, dy, dy, pw, pb, ps)


def _rope_tables():
    half = HEAD_DIM // 4
    t = jnp.arange(SEQ)
    freqs = ROPE_BASE ** (-jnp.arange(half, dtype=F32) / half)
    ang_r = (t // GRID_W).astype(F32)[:, None] * freqs[None, :]
    ang_c = (t % GRID_W).astype(F32)[:, None] * freqs[None, :]
    cos = jnp.concatenate([jnp.cos(ang_r), jnp.cos(ang_r), jnp.cos(ang_c), jnp.cos(ang_c)], axis=1)
    sin = jnp.concatenate([-jnp.sin(ang_r), jnp.sin(ang_r), -jnp.sin(ang_c), jnp.sin(ang_c)], axis=1)
    cos = jnp.concatenate([jnp.ones((NCTX, HEAD_DIM), F32), cos], axis=0)
    sin = jnp.concatenate([jnp.zeros((NCTX, HEAD_DIM), F32), sin], axis=0)
    return jnp.tile(cos, (1, 2)), jnp.tile(sin, (1, 2))


QK_W = (NQH + NKVH) * HEAD_DIM
QKV_W = QK_W + NKVH * HEAD_DIM


def _rope(x, cos, sin, sign):
    def body(x_ref, c_ref, s_ref, o_ref):
        c = c_ref[...]
        s = s_ref[...] * sign
        lane = lax.broadcasted_iota(jnp.int32, (TM, LANE), 1)
        lo = (lane % 32) < 16
        for k in range(QK_W // LANE):
            xk = x_ref[:, LANE * k:LANE * (k + 1)]
            partner = jnp.where(lo, pltpu.roll(xk, LANE - 16, 1), pltpu.roll(xk, 16, 1))
            o_ref[:, LANE * k:LANE * (k + 1)] = (xk * c + partner * s).astype(o_ref.dtype)
        o_ref[:, QK_W:] = x_ref[:, QK_W:].astype(o_ref.dtype)

    return _pcall(body, name="rope", grid=(NT,),
                  in_specs=[pl.BlockSpec((TM, QKV_W), lambda i: (i, 0)), pl.BlockSpec((TM, LANE), lambda i: (i, 0)),
                            pl.BlockSpec((TM, LANE), lambda i: (i, 0))],
                  out_specs=pl.BlockSpec((TM, QKV_W), lambda i: (i, 0)),
                  out_shape=jax.ShapeDtypeStruct((T, QKV_W), _MXU), compiler_params=_cp(("parallel",)))(x, cos, sin)


NQB = T // ABLK
KPAD = T + 2 * ABLK
NKEY = NCTX + 3 * ABLK


def _attn_mask(i):
    r = lax.broadcasted_iota(jnp.int32, (ABLK, NKEY), 0)
    c = lax.broadcasted_iota(jnp.int32, (ABLK, NKEY), 1)
    n = i - NCTX // ABLK
    kpos = (n - 1) * ABLK + (c - NCTX)
    qpos = n * ABLK + r
    loc = (c >= NCTX) & (jnp.abs(kpos - qpos) <= WINDOW) & (kpos >= 0) & (kpos < SEQ) & (n >= 0)
    return (c < NCTX) | loc


def _attn_specs():
    qs = pl.BlockSpec((GQA, ABLK, HEAD_DIM), lambda h, i: (h, i, 0))
    kc = pl.BlockSpec((1, NCTX, HEAD_DIM), lambda h, i: (h, 0, 0))
    kl = [pl.BlockSpec((1, ABLK, HEAD_DIM), functools.partial(lambda h, i, d: (h, i + d, 0), d=d)) for d in range(3)]
    sk = pl.BlockSpec((GQA, 1, 1), lambda h, i: (h, 0, 0))
    return qs, kc, kl, sk


def _attn_fwd(q, k, v, sink):
    scale = HEAD_DIM ** -0.5

    def body(q_ref, kc, k0, k1, k2, vc, v0, v1, v2, s_ref, o_ref, l_ref):
        valid = _attn_mask(pl.program_id(1))
        kk = jnp.concatenate([kc[0], k0[0], k1[0], k2[0]], axis=0)
        vv = jnp.concatenate([vc[0], v0[0], v1[0], v2[0]], axis=0)
        for g in range(GQA):
            s = lax.dot_general(q_ref[g], kk, (((1,), (1,)), ((), ())), preferred_element_type=F32) * scale
            s = jnp.where(valid, s, NEG_INF)
            sk = s_ref[g]
            m = jnp.maximum(jnp.max(s, axis=-1, keepdims=True), sk)
            p = jnp.exp(s - m)
            l = jnp.sum(p, axis=-1, keepdims=True) + jnp.exp(sk - m)
            o_ref[g] = jnp.dot((p / l).astype(_MXU), vv, preferred_element_type=F32).astype(o_ref.dtype)
            l_ref[g] = m + jnp.log(l)

    qs, kc, kl, sk = _attn_specs()
    return _pcall(body, name="attn_fwd", grid=(NKVH, NQB), in_specs=[qs, kc, *kl, kc, *kl, sk],
                  out_specs=[qs, pl.BlockSpec((GQA, ABLK, 1), lambda h, i: (h, i, 0))],
                  out_shape=[jax.ShapeDtypeStruct((NQH, T, HEAD_DIM), _MXU), jax.ShapeDtypeStruct((NQH, T, 1), F32)],
                  compiler_params=_cp(("parallel", "arbitrary")))(q, k, k, k, k, v, v, v, v, sink)


def _attn_bwd(q, k, v, sink, lse, do):
    scale = HEAD_DIM ** -0.5

    def body(q_ref, kc, k0, k1, k2, vc, v0, v1, v2, s_ref, l_ref, do_ref, dq_ref, dk_ref, dv_ref, ds_ref):
        i = pl.program_id(1)

        @pl.when(i == 0)
        def _():
            dk_ref[...] = jnp.zeros_like(dk_ref)
            dv_ref[...] = jnp.zeros_like(dv_ref)
            ds_ref[...] = jnp.zeros_like(ds_ref)

        valid = _attn_mask(i)
        kk = jnp.concatenate([kc[0], k0[0], k1[0], k2[0]], axis=0)
        vv = jnp.concatenate([vc[0], v0[0], v1[0], v2[0]], axis=0)
        dkk = jnp.zeros((NKEY, HEAD_DIM), F32)
        dvv = jnp.zeros((NKEY, HEAD_DIM), F32)
        for g in range(GQA):
            qg = q_ref[g]
            dog = do_ref[g]
            s = lax.dot_general(qg, kk, (((1,), (1,)), ((), ())), preferred_element_type=F32) * scale
            s = jnp.where(valid, s, NEG_INF)
            lse_g = l_ref[g]
            p = jnp.exp(s - lse_g)
            psink = jnp.exp(s_ref[g] - lse_g)
            dp = lax.dot_general(dog, vv, (((1,), (1,)), ((), ())), preferred_element_type=F32)
            delta = jnp.sum(p * dp, axis=-1, keepdims=True)
            ds = p * (dp - delta)
            ds_ref[g] += -jnp.sum(psink * delta, axis=0, keepdims=True)
            dsq = (ds * scale).astype(_MXU)
            dq_ref[g] = jnp.dot(dsq, kk, preferred_element_type=F32)
            dkk += lax.dot_general(dsq, qg, (((0,), (0,)), ((), ())), preferred_element_type=F32)
            dvv += lax.dot_general(p.astype(_MXU), dog, (((0,), (0,)), ((), ())), preferred_element_type=F32)
        loc = pl.ds(pl.multiple_of(i * ABLK, ABLK), 3 * ABLK)
        dk_ref[0, 0:NCTX, :] += dkk[:NCTX]
        dv_ref[0, 0:NCTX, :] += dvv[:NCTX]
        dk_ref[0, loc, :] += dkk[NCTX:]
        dv_ref[0, loc, :] += dvv[NCTX:]

    qs, kc, kl, sk = _attn_specs()
    ls = pl.BlockSpec((GQA, ABLK, 1), lambda h, i: (h, i, 0))
    kfull = pl.BlockSpec((1, KPAD, HEAD_DIM), lambda h, i: (h, 0, 0))
    return _pcall(body, name="attn_bwd", grid=(NKVH, NQB), in_specs=[qs, kc, *kl, kc, *kl, sk, ls, qs],
                  out_specs=[qs, kfull, kfull, sk],
                  out_shape=[jax.ShapeDtypeStruct((NQH, T, HEAD_DIM), F32), jax.ShapeDtypeStruct((NKVH, KPAD, HEAD_DIM), F32),
                             jax.ShapeDtypeStruct((NKVH, KPAD, HEAD_DIM), F32), jax.ShapeDtypeStruct((NQH, 1, 1), F32)],
                  compiler_params=_cp(("parallel", "arbitrary")))(q, k, k, k, k, v, v, v, v, sink, lse, do)


def _split_heads(x, nh):
    return x.reshape(T, nh, HEAD_DIM).transpose(1, 0, 2)


def _merge_heads(x):
    return x.transpose(1, 0, 2).reshape(T, -1)


def _pad_keys(x):
    z = jnp.zeros((x.shape[0], ABLK, HEAD_DIM), x.dtype)
    return jnp.concatenate([x[:, :NCTX], z, x[:, NCTX:], z], axis=1)


def _unpad_keys(x):
    return jnp.concatenate([x[:, :NCTX], x[:, NCTX + ABLK:NCTX + ABLK + SEQ]], axis=1)


def _attn_mixer_fwd(h, w_qkv, w_o, sink):
    cos, sin = _rope_tables()
    qkv = _rope(_mm(h, w_qkv, name="attn_qkv"), cos, sin, 1.0)
    q = _split_heads(qkv[:, :NQH * HEAD_DIM], NQH)
    k = _pad_keys(_split_heads(qkv[:, NQH * HEAD_DIM:QK_W], NKVH))
    v = _pad_keys(_split_heads(qkv[:, QK_W:], NKVH))
    sk = sink.reshape(NQH, 1, 1)
    o, lse = _attn_fwd(q, k, v, sk)
    om = _merge_heads(o)
    y = _mm(om, w_o, name="attn_out")
    return y, (q, k, v, sk, lse, om)


def _attn_mixer_bwd(dy, h, saved, w_qkv, w_o):
    q, k, v, sk, lse, om = saved
    cos, sin = _rope_tables()
    dyb = dy.astype(_MXU)
    dw_o = _mm(om, dyb, ta=True, name="attn_out_dw")
    do = _split_heads(_mm(dyb, w_o, tb=True, out_dtype=_MXU, name="attn_out_dx"), NQH)
    dq, dk, dv, dsk = _attn_bwd(q, k, v, sk, lse, do)
    dqkv = jnp.concatenate([_merge_heads(dq), _merge_heads(_unpad_keys(dk)), _merge_heads(_unpad_keys(dv))], axis=1)
    dqkv = _rope(dqkv, cos, sin, -1.0)
    dw_qkv = _mm(h, dqkv, ta=True, name="attn_qkv_dw")
    dh = _mm(dqkv, w_qkv, tb=True, name="attn_qkv_dx")
    return dh, dw_qkv, dw_o, dsk.reshape(1, NQH)


SSM_S = SSM_G * SSM_P
SSM_SL = SSM_S // LANE
SSM_TS = 128
SSM_NTS = T // SSM_TS
SSM_NCT = NCTX // SSM_TS
SSM_JB = 4


def _proj3d(u, w_re, w_im):
    def body(u_ref, wr_ref, wi_ref, or_ref, oi_ref):
        for hlf in range(2):
            ub = u_ref[:, LANE * hlf:LANE * (hlf + 1)].astype(_MXU)
            for w_ref, o_ref in ((wr_ref, or_ref), (wi_ref, oi_ref)):
                r = jnp.dot(ub, w_ref[hlf].astype(_MXU), preferred_element_type=F32)
                for q in range(4):
                    o_ref[:, 4 * hlf + q, :] = r[:, LANE * q:LANE * (q + 1)]

    ws = pl.BlockSpec((2, LANE, 512), lambda i, j: (j, 0, 0))
    os_ = pl.BlockSpec((TM, 8, LANE), lambda i, j: (i, j, 0))
    return _pcall(body, name="ssm_proj", grid=(NT, SSM_JB), in_specs=[pl.BlockSpec((TM, 2 * LANE), lambda i, j: (i, j)), ws, ws],
                  out_specs=[os_, os_], out_shape=[jax.ShapeDtypeStruct((T, SSM_SL, LANE), F32)] * 2,
                  compiler_params=_cp(("parallel", "parallel")))(u, w_re, w_im)


def _readout(s_re, s_im, w_re, w_im):
    def body(sr_ref, si_ref, wr_ref, wi_ref, o_ref):
        for hlf in range(2):
            acc = None
            for s_ref, w_ref in ((sr_ref, wr_ref), (si_ref, wi_ref)):
                x = jnp.concatenate([s_ref[:, 4 * hlf + q, :] for q in range(4)], axis=1).astype(_MXU)
                r = jnp.dot(x, w_ref[hlf].astype(_MXU), preferred_element_type=F32)
                acc = r if acc is None else acc + r
            o_ref[:, LANE * hlf:LANE * (hlf + 1)] = acc

    ss = pl.BlockSpec((TM, 8, LANE), lambda i, j: (i, j, 0))
    ws = pl.BlockSpec((2, 512, LANE), lambda i, j: (j, 0, 0))
    return _pcall(body, name="ssm_readout", grid=(NT, SSM_JB), in_specs=[ss, ss, ws, ws],
                  out_specs=pl.BlockSpec((TM, 2 * LANE), lambda i, j: (i, j)), out_shape=jax.ShapeDtypeStruct((T, D), F32),
                  compiler_params=_cp(("parallel", "parallel")))(s_re, s_im, w_re, w_im)


def _outer3d(s_re, s_im, y):
    def body(sr_ref, si_ref, y_ref, dr_ref, di_ref):
        i = pl.program_id(1)
        for hlf in range(2):
            yb = y_ref[:, LANE * hlf:LANE * (hlf + 1)].astype(_MXU)
            for s_ref, d_ref in ((sr_ref, dr_ref), (si_ref, di_ref)):
                x = jnp.concatenate([s_ref[:, 4 * hlf + q, :] for q in range(4)], axis=1).astype(_MXU)
                r = lax.dot_general(x, yb, (((0,), (0,)), ((), ())), preferred_element_type=F32)

                @pl.when(i == 0)
                def _():
                    d_ref[hlf] = r

                @pl.when(i > 0)
                def _():
                    d_ref[hlf] += r

    ss = pl.BlockSpec((TM, 8, LANE), lambda j, i: (i, j, 0))
    ds = pl.BlockSpec((2, 512, LANE), lambda j, i: (j, 0, 0))
    return _pcall(body, name="ssm_outer", grid=(SSM_JB, NT), in_specs=[ss, ss, pl.BlockSpec((TM, 2 * LANE), lambda j, i: (i, j))],
                  out_specs=[ds, ds], out_shape=[jax.ShapeDtypeStruct((8, 512, LANE), F32)] * 2,
                  compiler_params=_cp(("parallel", "arbitrary")))(s_re, s_im, y)


def _scan_order(order):
    n, c = SSM_NTS, SSM_NCT
    if order == "fwd":
        return (lambda i: i), False
    if order == "fwd_adj":
        return (lambda i: n - 1 - i), True
    if order == "rev":
        return (lambda i: jnp.where(i < c, c - 1 - i, n + c - 1 - i)), True
    if order == "rev_adj":
        return (lambda i: jnp.where(i < n - c, i + c, i - (n - c))), False
    raise ValueError(order)


def _scan(b_re, b_im, lam_re, lam_im, order):
    tile, down = _scan_order(order)

    def body(br_ref, bi_ref, lr_ref, li_ref, sr_ref, si_ref, cr, ci):
        @pl.when(pl.program_id(0) == 0)
        def _():
            cr[...] = jnp.zeros_like(cr)
            ci[...] = jnp.zeros_like(ci)

        lr = lr_ref[...]
        li = li_ref[...]

        def step(n, c):
            t = SSM_TS - 1 - n if down else n
            sr, si = c
            nr = lr * sr - li * si + br_ref[t]
            ni = lr * si + li * sr + bi_ref[t]
            sr_ref[t] = nr
            si_ref[t] = ni
            return nr, ni

        sr, si = lax.fori_loop(0, SSM_TS, step, (cr[...], ci[...]))
        cr[...] = sr
        ci[...] = si

    bs = pl.BlockSpec((SSM_TS, SSM_SL, LANE), lambda i: (tile(i), 0, 0))
    ps = pl.BlockSpec((SSM_SL, LANE), lambda i: (0, 0))
    return _pcall(body, name="ssm_scan_" + order, grid=(SSM_NTS,), in_specs=[bs, bs, ps, ps], out_specs=[bs, bs],
                  out_shape=[jax.ShapeDtypeStruct((T, SSM_SL, LANE), F32)] * 2,
                  scratch_shapes=[pltpu.VMEM((SSM_SL, LANE), F32)] * 2, compiler_params=_cp(("arbitrary",)))(b_re, b_im, lam_re, lam_im)


def _scan_adj(g_re, g_im, s_re, s_im, lam_re, lam_im, order):
    tile, down = _scan_order(order)

    def body(gr_ref, gi_ref, sr_ref, si_ref, lr_ref, li_ref, ar_ref, ai_ref, dlr_ref, dli_ref, cr, ci):
        @pl.when(pl.program_id(0) == 0)
        def _():
            cr[...] = jnp.zeros_like(cr)
            ci[...] = jnp.zeros_like(ci)
            dlr_ref[...] = jnp.zeros_like(dlr_ref)
            dli_ref[...] = jnp.zeros_like(dli_ref)

        lr = lr_ref[...]
        li = li_ref[...]

        def step(n, c):
            t = SSM_TS - 1 - n if down else n
            ar, ai, dr, di = c
            sr = sr_ref[t]
            si = si_ref[t]
            dr = dr + ar * sr + ai * si
            di = di + ai * sr - ar * si
            nr = gr_ref[t] + lr * ar + li * ai
            ni = gi_ref[t] + lr * ai - li * ar
            ar_ref[t] = nr
            ai_ref[t] = ni
            return nr, ni, dr, di

        ar, ai, dr, di = lax.fori_loop(0, SSM_TS, step, (cr[...], ci[...], dlr_ref[...], dli_ref[...]))
        cr[...] = ar
        ci[...] = ai
        dlr_ref[...] = dr
        dli_ref[...] = di

    bs = pl.BlockSpec((SSM_TS, SSM_SL, LANE), lambda i: (tile(i), 0, 0))
    ps = pl.BlockSpec((SSM_SL, LANE), lambda i: (0, 0))
    return _pcall(body, name="ssm_scan_" + order, grid=(SSM_NTS,), in_specs=[bs, bs, bs, bs, ps, ps], out_specs=[bs, bs, ps, ps],
                  out_shape=[jax.ShapeDtypeStruct((T, SSM_SL, LANE), F32)] * 2 + [jax.ShapeDtypeStruct((SSM_SL, LANE), F32)] * 2,
                  scratch_shapes=[pltpu.VMEM((SSM_SL, LANE), F32)] * 2,
                  compiler_params=_cp(("arbitrary",)))(g_re, g_im, s_re, s_im, lam_re, lam_im)


def _block_diag(x):
    x4 = x.reshape(8, 8, SSM_P, SSM_C)
    return jnp.einsum("jgpc,gh->jgphc", x4, jnp.eye(8, dtype=x.dtype)).reshape(8, 8 * SSM_P, 8 * SSM_C)


def _ssm_prep(lam_re, lam_im, log_dt, b_re, b_im, c_re, c_im):
    lam = lax.complex(lam_re, lam_im)
    dt = jnp.exp(log_dt)[:, None]
    lam_bar = jnp.exp(lam * dt)
    b_bar = ((lam_bar - 1.0) / lam)[..., None] * lax.complex(b_re, b_im)
    return (jnp.real(lam_bar).reshape(SSM_SL, LANE), jnp.imag(lam_bar).reshape(SSM_SL, LANE),
            _block_diag(jnp.real(b_bar)), _block_diag(jnp.imag(b_bar)),
            _block_diag(c_re.transpose(0, 2, 1)), _block_diag(-c_im.transpose(0, 2, 1)))


def _ssm_glue(h, yf, yr, d):
    return jax.nn.gelu(d * h + yf + yr)


def _glu(ga, gb):
    return ga * jax.nn.sigmoid(gb)


def _ssm_mixer_fwd(h, sp, w_a, w_b):
    lam_re, lam_im, log_dt, b_re, b_im, c_re, c_im, d_skip = sp
    ys, saved = [], []
    for di, order in enumerate(("fwd", "rev")):
        lr, li, wb_r, wb_i, wc_r, wc_i = _ssm_prep(lam_re[di], lam_im[di], log_dt[di], b_re[di], b_im[di], c_re[di], c_im[di])
        bu_r, bu_i = _proj3d(h, wb_r.transpose(0, 2, 1), wb_i.transpose(0, 2, 1))
        s_r, s_i = _scan(bu_r, bu_i, lr, li, order)
        ys.append(_readout(s_r, s_i, wc_r, wc_i))
        saved.append((s_r, s_i))
    g = _rows(lambda *a: (_ssm_glue(*a),), [h, ys[0], ys[1]], [d_skip], [(D, _MXU)], [], name="ssm_glue")[0]
    ga = _mm(g, w_a, name="ssm_glu_a")
    gb = _mm(g, w_b, name="ssm_glu_b")
    y = _rows(lambda *a: (_glu(*a),), [ga, gb], [], [(D, F32)], [], name="ssm_glu")[0]
    return y, (ys, saved, g, ga, gb)


def _ssm_mixer_bwd(dy, h, saved_all, sp, w_a, w_b):
    lam_re, lam_im, log_dt, b_re, b_im, c_re, c_im, d_skip = sp
    ys, saved, g, ga, gb = saved_all

    def glu_bwd(ga, gb, dy):
        _, vjp = jax.vjp(_glu, ga, gb)
        return vjp(dy)

    dga, dgb = _rows(glu_bwd, [ga, gb, dy], [], [(D, _MXU), (D, _MXU)], [], name="ssm_glu_bwd")
    dw_a = _mm(g, dga, ta=True, name="ssm_glu_a_dw")
    dw_b = _mm(g, dgb, ta=True, name="ssm_glu_b_dw")
    dg_a = _mm(dga, w_a, tb=True, name="ssm_glu_a_dx")
    dg_b = _mm(dgb, w_b, tb=True, name="ssm_glu_b_dx")

    def glue_bwd(h, yf, yr, dg_a, dg_b, d):
        _, vjp = jax.vjp(_ssm_glue, h, yf, yr, d)
        dh, dyl, _, dd = vjp(dg_a + dg_b)
        return dh, dyl, dd

    dh0, dyl, dd = _rows(glue_bwd, [h, ys[0], ys[1], dg_a, dg_b], [d_skip], [(D, F32), (D, F32)], [(1, (1, D))], name="ssm_glue_bwd")
    dhs = [dh0]
    dparams = []
    for di, (order, adj) in enumerate((("fwd", "fwd_adj"), ("rev", "rev_adj"))):
        args = (lam_re[di], lam_im[di], log_dt[di], b_re[di], b_im[di], c_re[di], c_im[di])
        (lr, li, wb_r, wb_i, wc_r, wc_i), prep_vjp = jax.vjp(_ssm_prep, *args)
        s_r, s_i = saved[di]
        dwc_r, dwc_i = _outer3d(s_r, s_i, dyl)
        g_r, g_i = _proj3d(dyl, wc_r.transpose(0, 2, 1), wc_i.transpose(0, 2, 1))
        a_r, a_i, dlr, dli = _scan_adj(g_r, g_i, s_r, s_i, lr, li, adj)
        dwb_r, dwb_i = _outer3d(a_r, a_i, h)
        dhs.append(_readout(a_r, a_i, wb_r, wb_i))
        dparams.append(prep_vjp((dlr, dli, dwb_r, dwb_i, dwc_r, dwc_i)))
    dsp = [jnp.stack([dparams[0][k], dparams[1][k]], axis=0) for k in range(7)]
    return dhs, dsp, dd, dw_a, dw_b


NCH = T // GM_CHUNK


def _gm_specs():
    full = lambda *s: pl.BlockSpec(s, lambda i: (0,) * len(s))
    zu = pl.BlockSpec((GM_CHUNK, GM_HALF), lambda i: (i, 0))
    zv = pl.BlockSpec((GM_CHUNK, GM_HALF), lambda i: (i, 1))
    pars = [full(1, GM_HALF), pl.BlockSpec((1, GM_HALF), lambda i: (0, 1)), full(1, GM_HALF), full(1, GM_HALF),
            full(GM_HEADS, GM_CHUNK, GM_CHUNK), full(GM_HEADS, GM_CHUNK, 1)]
    return zu, zv, pars, full


def _gm_forward(zu_ref, zv_ref, bu_ref, bv_ref, g_ref, b_ref, ws_ref, bs_ref):
    u = jax.nn.gelu(zu_ref[...] + bu_ref[...])
    zv = jax.nn.gelu(zv_ref[...] + bv_ref[...])
    mu = jnp.mean(zv, axis=-1, keepdims=True)
    zc = zv - mu
    rstd = lax.rsqrt(jnp.mean(jnp.square(zc), axis=-1, keepdims=True) + LN_EPS)
    vhat = zc * rstd
    v = (vhat * g_ref[...] + b_ref[...]).astype(_MXU)
    gates = [jnp.dot(ws_ref[hd].astype(_MXU), v[:, GM_HD * hd:GM_HD * (hd + 1)], preferred_element_type=F32) + bs_ref[hd]
             for hd in range(GM_HEADS)]
    return u, vhat, rstd, v, jnp.concatenate(gates, axis=1)


def _gmlp_chunk(zp, b_in, ln_g, ln_b, w_s, b_s):
    def body(zu_ref, zv_ref, bu_ref, bv_ref, g_ref, b_ref, ws_ref, bs_ref, o_ref):
        u, _, _, _, gate = _gm_forward(zu_ref, zv_ref, bu_ref, bv_ref, g_ref, b_ref, ws_ref, bs_ref)
        o_ref[...] = (u * gate).astype(o_ref.dtype)

    zu, zv, pars, _ = _gm_specs()
    return _pcall(body, name="gmlp_chunk", grid=(NCH,), in_specs=[zu, zv, *pars], out_specs=zu,
                  out_shape=jax.ShapeDtypeStruct((T, GM_HALF), _MXU),
                  compiler_params=_cp(("parallel",)))(zp, zp, b_in, b_in, ln_g, ln_b, w_s, b_s)


def _gmlp_chunk_bwd(zp, do, b_in, ln_g, ln_b, w_s, b_s):
    def body(zu_ref, zv_ref, do_ref, bu_ref, bv_ref, g_ref, b_ref, ws_ref, bs_ref,
             dzu_ref, dzv_ref, dbu_ref, dbv_ref, dg_ref, db_ref, dws_ref, dbs_ref):
        i = pl.program_id(0)

        def acc(ref, val, idx=None):
            @pl.when(i == 0)
            def _():
                if idx is None:
                    ref[...] = val
                else:
                    ref[idx] = val

            @pl.when(i > 0)
            def _():
                if idx is None:
                    ref[...] += val
                else:
                    ref[idx] += val

        u, vhat, rstd, v, gate = _gm_forward(zu_ref, zv_ref, bu_ref, bv_ref, g_ref, b_ref, ws_ref, bs_ref)
        do = do_ref[...]
        du = do * gate
        dgate = do * u
        dvs = []
        for hd in range(GM_HEADS):
            sl = slice(GM_HD * hd, GM_HD * (hd + 1))
            dgh = dgate[:, sl]
            dghb = dgh.astype(_MXU)
            dvs.append(lax.dot_general(ws_ref[hd].astype(_MXU), dghb, (((0,), (0,)), ((), ())), preferred_element_type=F32))
            acc(dws_ref, lax.dot_general(dghb, v[:, sl], (((1,), (1,)), ((), ())), preferred_element_type=F32), hd)
            acc(dbs_ref, jnp.sum(dgh, axis=1, keepdims=True), hd)
        dv = jnp.concatenate(dvs, axis=1)
        acc(dg_ref, jnp.sum(dv * vhat, axis=0, keepdims=True))
        acc(db_ref, jnp.sum(dv, axis=0, keepdims=True))
        dvh = dv * g_ref[...]
        dzv = rstd * (dvh - jnp.mean(dvh, axis=-1, keepdims=True) - vhat * jnp.mean(dvh * vhat, axis=-1, keepdims=True))
        dpu = jax.vjp(jax.nn.gelu, zu_ref[...] + bu_ref[...])[1](du)[0]
        dpv = jax.vjp(jax.nn.gelu, zv_ref[...] + bv_ref[...])[1](dzv)[0]
        dzu_ref[...] = dpu.astype(dzu_ref.dtype)
        dzv_ref[...] = dpv.astype(dzv_ref.dtype)
        acc(dbu_ref, jnp.sum(dpu, axis=0, keepdims=True))
        acc(dbv_ref, jnp.sum(dpv, axis=0, keepdims=True))

    zu, zv, pars, full = _gm_specs()
    out_specs = [zu, zu, full(1, GM_HALF), full(1, GM_HALF), full(1, GM_HALF), full(1, GM_HALF),
                 full(GM_HEADS, GM_CHUNK, GM_CHUNK), full(GM_HEADS, GM_CHUNK, 1)]
    out_shape = [jax.ShapeDtypeStruct((T, GM_HALF), _MXU)] * 2 + [jax.ShapeDtypeStruct((1, GM_HALF), F32)] * 4 + \
                [jax.ShapeDtypeStruct((GM_HEADS, GM_CHUNK, GM_CHUNK), F32), jax.ShapeDtypeStruct((GM_HEADS, GM_CHUNK, 1), F32)]
    dzu, dzv, dbu, dbv, dg, db, dws, dbs = _pcall(
        body, name="gmlp_chunk_bwd", grid=(NCH,), in_specs=[zu, zv, zu, *pars], out_specs=out_specs, out_shape=out_shape,
        compiler_params=_cp(("arbitrary",)))(zp, zp, do, b_in, b_in, ln_g, ln_b, w_s, b_s)
    return jnp.concatenate([dzu, dzv], axis=1), jnp.concatenate([dbu, dbv], axis=1), dg, db, dws, dbs


def _gmlp_mixer_fwd(h, w_in, b_in, ln_g, ln_b, w_s, b_s, w_out):
    zp = _mm(h, w_in, name="gmlp_in")
    ug = _gmlp_chunk(zp, b_in, ln_g, ln_b, w_s, b_s[..., None])
    return _mm(ug, w_out, name="gmlp_out"), (zp, ug)


def _gmlp_mixer_bwd(dy, h, saved, w_in, b_in, ln_g, ln_b, w_s, b_s, w_out):
    zp, ug = saved
    dw_out = _mm(ug, dy, ta=True, name="gmlp_out_dw")
    do = _mm(dy, w_out, tb=True, name="gmlp_out_dx")
    dzp, db_in, dg, db, dws, dbs = _gmlp_chunk_bwd(zp, do, b_in, ln_g, ln_b, w_s, b_s[..., None])
    dw_in = _mm(h, dzp, ta=True, name="gmlp_in_dw")
    dh = _mm(dzp, w_in, tb=True, name="gmlp_in_dx")
    return dh, dw_in, db_in, dg, db, dws, dbs[..., 0], dw_out


def _loss_head(x, target):
    nct = NCTX // TM

    def body(x_ref, t_ref, l_ref, dx_ref):
        i = pl.program_id(0)
        err = jnp.where(i >= nct, x_ref[...] - t_ref[...], 0.0)
        dx_ref[...] = err * (1.0 / D)
        part = 0.5 * jnp.sum(jnp.sum(jnp.square(err), axis=-1, keepdims=True) * (1.0 / D), axis=0, keepdims=True)

        @pl.when(i == 0)
        def _():
            l_ref[...] = part

        @pl.when(i > 0)
        def _():
            l_ref[...] += part

    return _pcall(body, name="loss_head", grid=(NT,),
                  in_specs=[pl.BlockSpec((TM, D), lambda i: (i, 0)), pl.BlockSpec((TM, D), lambda i: (jnp.maximum(i - nct, 0), 0))],
                  out_specs=[pl.BlockSpec((1, 1), lambda i: (0, 0)), pl.BlockSpec((TM, D), lambda i: (i, 0))],
                  out_shape=[jax.ShapeDtypeStruct((1, 1), F32), jax.ShapeDtypeStruct((T, D), F32)],
                  compiler_params=_cp(("arbitrary",)))(x, target)


def _as2d(a):
    return a.reshape(-1, a.shape[-1])


def _adamw(w, g, m, v):
    shape = w.shape
    w2, g2, m2, v2 = _as2d(w), _as2d(g), _as2d(m), _as2d(v)
    R, C = w2.shape
    tr = _tile(R, 512, SUBLANE)
    c1 = 1.0 - B1 ** STEP
    c2 = 1.0 - B2 ** STEP

    def body(w_ref, g_ref, m_ref, v_ref, d_ref, nm_ref, nv_ref):
        g = g_ref[...]
        m = B1 * m_ref[...] + (1.0 - B1) * g
        v = B2 * v_ref[...] + (1.0 - B2) * jnp.square(g)
        nm_ref[...] = m
        nv_ref[...] = v
        d_ref[...] = -LR * ((m / c1) / (jnp.sqrt(v / c2) + EPS) + WD * w_ref[...])

    spec = pl.BlockSpec((tr, C), lambda i: (i, 0))
    outs = _pcall(body, name="adamw", grid=(R // tr,), in_specs=[spec] * 4, out_specs=[spec] * 3,
                  out_shape=[jax.ShapeDtypeStruct((R, C), F32)] * 3, compiler_params=_cp(("parallel",)))(w2, g2, m2, v2)
    return tuple(o.reshape(shape) for o in outs)


def _sum_slabs(x):
    _, R, C = x.shape
    tr = _tile(R, 256, SUBLANE)

    def body(x_ref, o_ref):
        acc = x_ref[0]
        for k in range(1, NDEV):
            acc = acc + x_ref[k]
        o_ref[...] = acc

    return _pcall(body, name="sum_slabs", grid=(R // tr,), in_specs=[pl.BlockSpec((NDEV, tr, C), lambda i: (0, i, 0))],
                  out_specs=pl.BlockSpec((tr, C), lambda i: (i, 0)), out_shape=jax.ShapeDtypeStruct((R, C), x.dtype),
                  compiler_params=_cp(("parallel",)))(x)


def _exchange(xs, broadcast, name):
    n = len(xs)

    def body(*refs):
        x_refs, o_refs = refs[:n], refs[n:2 * n]
        send_sems, recv_sems, loc_sems = refs[2 * n:]
        mx, my, mc = lax.axis_index("x"), lax.axis_index("y"), lax.axis_index("c")
        me = 4 * mx + 2 * my + mc
        pending = []
        for a in range(n):
            mine = pltpu.make_async_copy(x_refs[a] if broadcast else x_refs[a].at[me], o_refs[a].at[me], loc_sems.at[a])
            mine.start()
            pending.append(mine)
            for k in range(1, NDEV):
                px = 1 - mx if k & 4 else mx
                py = 1 - my if k & 2 else my
                pc = 1 - mc if k & 1 else mc
                cp = pltpu.make_async_remote_copy(
                    src_ref=x_refs[a] if broadcast else x_refs[a].at[4 * px + 2 * py + pc], dst_ref=o_refs[a].at[me],
                    send_sem=send_sems.at[a, k - 1], recv_sem=recv_sems.at[a, k - 1],
                    device_id=(px, py, pc), device_id_type=pl.DeviceIdType.MESH)
                cp.start()
                pending.append(cp)
        for cp in pending:
            cp.wait()

    hbm = pl.BlockSpec(memory_space=pl.ANY)
    out_shape = [jax.ShapeDtypeStruct((NDEV,) + tuple(x.shape[-2:]), x.dtype) for x in xs]
    return _pcall(body, name=name, in_specs=[hbm] * n, out_specs=[hbm] * n, out_shape=out_shape,
                  scratch_shapes=[pltpu.SemaphoreType.DMA((n, NDEV - 1)), pltpu.SemaphoreType.DMA((n, NDEV - 1)),
                                  pltpu.SemaphoreType.DMA((n,))],
                  compiler_params=pltpu.CompilerParams(has_side_effects=True))(*xs)


SLAB_W = 1024
SLAB_ROWS = 16


def _pack(parts, lead=None):
    if lead is None:
        flat = jnp.concatenate([p.reshape(-1) for p in parts])
        n = flat.shape[0]
        padn = -n % (SLAB_ROWS * SLAB_W)
        return jnp.pad(flat, (0, padn)).reshape(-1, SLAB_W)
    flat = jnp.concatenate([p.reshape(lead, -1) for p in parts], axis=1)
    n = flat.shape[1]
    padn = -n % (SLAB_ROWS * SLAB_W)
    return jnp.pad(flat, ((0, 0), (0, padn))).reshape(lead, -1, SLAB_W)


def _unpack(buf, shapes, lead=None):
    out, off = [], 0
    flat = buf.reshape(-1) if lead is None else buf.reshape(lead, -1)
    for s in shapes:
        n = int(np.prod(s))
        if lead is None:
            out.append(flat[off:off + n].reshape(s))
        else:
            out.append(flat[:, off:off + n].reshape((lead,) + tuple(s)))
        off += n
    return out


def _gathered(blk, ax):
    m = jnp.moveaxis(blk, 0, ax)
    s = list(m.shape)
    return m.reshape(s[:ax] + [s[ax] * s[ax + 1]] + s[ax + 2:])


def _scattered(full, ax):
    s = list(full.shape)
    m = full.reshape(s[:ax] + [NDEV, s[ax] // NDEV] + s[ax + 1:])
    return jnp.moveaxis(m, ax, 0)


_MM_SHARDED = (("ffn_w_up", 2), ("ffn_w_down", 1), ("pool_w", 2), ("attn_w_qkv", 2), ("attn_w_o", 1),
               ("ssm_w_glu_a", 1), ("ssm_w_glu_b", 1), ("gmlp_w_in", 2), ("gmlp_w_out", 1))
_VEC_SHARDED = (("ffn_conv_w", 2), ("ssm_d", 1), ("gmlp_b_in", 1), ("gmlp_ln_g", 1), ("gmlp_ln_b", 1))
_REPLICATED = ("ln1_g", "ln1_b", "ln2_g", "ln2_b", "ffn_conv_b", "pool_b", "pool_scale", "attn_sink",
               "ssm_lambda_re", "ssm_lambda_im", "ssm_log_dt", "ssm_b_re", "ssm_b_im", "ssm_c_re", "ssm_c_im",
               "gmlp_w_s", "gmlp_b_s")
_WEIGHTS = ("c_ctx", "ada_w", "ada_b", "ln1_g", "ln1_b", "ln2_g", "ln2_b", "ffn_w_up", "ffn_conv_w", "ffn_conv_b", "ffn_w_down",
            "pool_w", "pool_b", "pool_scale", "attn_w_qkv", "attn_w_o", "attn_sink", "ssm_lambda_re", "ssm_lambda_im",
            "ssm_log_dt", "ssm_b_re", "ssm_b_im", "ssm_c_re", "ssm_c_im", "ssm_d", "ssm_w_glu_a", "ssm_w_glu_b",
            "gmlp_w_in", "gmlp_b_in", "gmlp_ln_g", "gmlp_ln_b", "gmlp_w_s", "gmlp_b_s", "gmlp_w_out")
N_MODS = 6
ADA_COLS = N_MODS * D // NDEV
PAD_ROWS = 16


def _silu_rows(x):
    return _rows(lambda v: (jax.nn.silu(v),), [x], [], [(x.shape[1], F32)], [], name="silu", tm=x.shape[0])[0]


def _step(x, c, ctx, loss_target, w, m, v):
    mx, my, mc = lax.axis_index("x"), lax.axis_index("y"), lax.axis_index("c")
    me = 4 * mx + 2 * my + mc

    mm_buf = _pack([w[n].astype(_MXU) for n, _ in _MM_SHARDED])
    vec_buf = _pack([w[n] for n, _ in _VEC_SHARDED] + [c])
    mm_all, vec_all = _exchange([mm_buf, vec_buf], True, "gather_weights")
    mm_parts = _unpack(mm_all, [w[n].shape for n, _ in _MM_SHARDED], lead=NDEV)
    vec_parts = _unpack(vec_all, [w[n].shape for n, _ in _VEC_SHARDED] + [c.shape], lead=NDEV)
    full = {n: _gathered(p, ax) for (n, ax), p in zip(_MM_SHARDED, mm_parts)}
    full.update({n: _gathered(p, ax) for (n, ax), p in zip(_VEC_SHARDED, vec_parts[:-1])})
    c_all = vec_parts[-1].reshape(NDEV, D)

    cc = jnp.concatenate([c_all, w["c_ctx"].reshape(1, D), jnp.zeros((PAD_ROWS - NDEV - 1, D), F32)], axis=0)
    silu_cc = _silu_rows(cc)
    ada_b_mine = lax.dynamic_slice(w["ada_b"], (0, me * ADA_COLS), (DEPTH, ADA_COLS))
    mods_mine = jnp.stack([_mm(silu_cc, w["ada_w"][l], name="ada_mods") + ada_b_mine[l][None, :] for l in range(DEPTH)], axis=1)
    per_dev = mods_mine[:NDEV].reshape(NDEV, DEPTH * ADA_COLS)
    cm = jnp.broadcast_to(mods_mine[NDEV].reshape(1, DEPTH * ADA_COLS), (NDEV, DEPTH * ADA_COLS))
    mods_all = _exchange([_pack([per_dev, cm], lead=NDEV)], False, "scatter_mods")[0]
    got = _unpack(mods_all, [(DEPTH, ADA_COLS), (DEPTH, ADA_COLS)], lead=NDEV)
    mods = got[0].transpose(1, 0, 2).reshape(DEPTH, N_MODS * D)
    cmods = got[1].transpose(1, 0, 2).reshape(DEPTH, N_MODS * D)
    P = [[jnp.stack([cmods[l, k * D:(k + 1) * D], mods[l, k * D:(k + 1) * D]]).reshape(2, 1, D) for k in range(N_MODS)]
         for l in range(DEPTH)]
    row = lambda a, l: a[l].reshape(1, 1, D)

    xs = jnp.concatenate([ctx[0], x[0]], axis=0)
    sp = tuple(w[n][0] for n in ("ssm_lambda_re", "ssm_lambda_im", "ssm_log_dt", "ssm_b_re", "ssm_b_im", "ssm_c_re", "ssm_c_im")) + \
        (full["ssm_d"].reshape(1, 1, D),)
    gm = (full["gmlp_w_in"][0], full["gmlp_b_in"], full["gmlp_ln_g"], full["gmlp_ln_b"], w["gmlp_w_s"][0], w["gmlp_b_s"][0],
          full["gmlp_w_out"][0])
    pool_args = (full["pool_w"][0], w["pool_b"], w["pool_scale"])
    saved = []
    for l in range(DEPTH):
        sh1, sc1, gt1, sh2, sc2, gt2 = P[l]
        h1 = _pre_mixer(xs, sh1, sc1, _MXU if l in (1, 3) else F32)
        if l == 0:
            y, ms = _pool_fwd(h1, *pool_args), None
        elif l == 1:
            y, ms = _attn_mixer_fwd(h1, full["attn_w_qkv"][0], full["attn_w_o"][0], w["attn_sink"])
        elif l == 2:
            y, ms = _ssm_mixer_fwd(h1, sp, full["ssm_w_glu_a"][0], full["ssm_w_glu_b"][0])
        else:
            y, ms = _gmlp_mixer_fwd(h1, *gm)
        x1, h2 = _post_mixer(xs, y, gt1, row(w["ln1_g"], l), row(w["ln1_b"], l), sh2, sc2)
        f, (a, u) = _ffn_fwd(h2, full["ffn_w_up"][l], full["ffn_conv_w"][l], w["ffn_conv_b"][l][None, :], full["ffn_w_down"][l])
        x2 = _post_ffn(x1, f, gt2, row(w["ln2_g"], l), row(w["ln2_b"], l))
        saved.append((xs, h1, y, ms, x1, h2, a, u, f))
        xs = x2
    loss, dxs = _loss_head(xs, loss_target[0])

    g = {n: [None] * DEPTH for n in ("ln1_g", "ln1_b", "ln2_g", "ln2_b", "ffn_w_up", "ffn_conv_w", "ffn_conv_b", "ffn_w_down")}
    dP = [None] * DEPTH
    for l in reversed(range(DEPTH)):
        sh1, sc1, gt1, sh2, sc2, gt2 = P[l]
        x0, h1, y, ms, x1, h2, a, u, f = saved[l]
        dx1, df, dgt2, g["ln2_g"][l], g["ln2_b"][l] = _post_ffn_bwd(x1, f, dxs, gt2, row(w["ln2_g"], l), row(w["ln2_b"], l))
        dh2, g["ffn_w_up"][l], g["ffn_conv_w"][l], g["ffn_conv_b"][l], g["ffn_w_down"][l] = _ffn_bwd(
            df, h2, a, u, full["ffn_w_up"][l], full["ffn_conv_w"][l], w["ffn_conv_b"][l][None, :], full["ffn_w_down"][l])
        dx0, dy, dgt1, g["ln1_g"][l], g["ln1_b"][l], dsh2, dsc2 = _post_mixer_bwd(
            x0, y, dx1, dh2, gt1, row(w["ln1_g"], l), row(w["ln1_b"], l), sh2, sc2)
        if l == 0:
            dh, g["pool_w"], g["pool_b"], g["pool_scale"] = _pool_bwd(h1, dy, *pool_args)
            dhs = [dh]
        elif l == 1:
            dh, g["attn_w_qkv"], g["attn_w_o"], g["attn_sink"] = _attn_mixer_bwd(dy, h1, ms, full["attn_w_qkv"][0], full["attn_w_o"][0])
            dhs = [dh]
        elif l == 2:
            dhs, dsp, dd, g["ssm_w_glu_a"], g["ssm_w_glu_b"] = _ssm_mixer_bwd(dy, h1, ms, sp, full["ssm_w_glu_a"][0], full["ssm_w_glu_b"][0])
            for n, d_ in zip(("ssm_lambda_re", "ssm_lambda_im", "ssm_log_dt", "ssm_b_re", "ssm_b_im", "ssm_c_re", "ssm_c_im"), dsp):
                g[n] = d_
            g["ssm_d"] = dd.reshape(1, D)
        else:
            (dh, g["gmlp_w_in"], g["gmlp_b_in"], g["gmlp_ln_g"], g["gmlp_ln_b"], g["gmlp_w_s"], g["gmlp_b_s"],
             g["gmlp_w_out"]) = _gmlp_mixer_bwd(dy, h1, ms, *gm)
            dhs = [dh]
        dxs, dsh1, dsc1 = _pre_mixer_bwd(x0, dhs, dx0, sh1, sc1)
        dP[l] = (dsh1, dsc1, dgt1, dsh2, dsc2, dgt2)
    grad_x = dxs[NCTX:][None]
    dmods = jnp.stack([jnp.concatenate([p[1, 0] for p in dP[l]]) for l in range(DEPTH)])
    dcmods = jnp.stack([jnp.concatenate([p[0, 0] for p in dP[l]]) for l in range(DEPTH)])

    gfull = {n: (jnp.stack(g[n]) if isinstance(g[n], list) else g[n]) for n in g}
    sharded_names = [n for n, _ in _MM_SHARDED] + [n for n, _ in _VEC_SHARDED]
    sharded_axes = dict(_MM_SHARDED + _VEC_SHARDED)

    def as_param(n, a):
        shard = w[n].shape
        ax = sharded_axes.get(n)
        fs = tuple(s * NDEV if i == ax else s for i, s in enumerate(shard))
        return a.reshape(fs)

    rep = jnp.concatenate([as_param(n, gfull[n]).reshape(-1) for n in _REPLICATED])
    n_rep = rep.shape[0]
    rep = jnp.pad(rep, (0, -n_rep % (NDEV * SLAB_W))).reshape(NDEV, -1)
    by_dev = lambda a: a.reshape(DEPTH, NDEV, ADA_COLS).transpose(1, 0, 2)
    parts = [_scattered(as_param(n, gfull[n]), sharded_axes[n]) for n in sharded_names] + [rep, by_dev(dmods), by_dev(dcmods)]
    grads_in = _exchange([_pack(parts, lead=NDEV)], False, "scatter_grads")[0]
    shapes = [w[n].shape for n in sharded_names] + [(rep.shape[1],), (DEPTH, ADA_COLS), (DEPTH, ADA_COLS)]
    red = _unpack(_sum_slabs(grads_in), shapes)
    grads = dict(zip(sharded_names, red[:len(sharded_names)]))
    rep_mine, dcm = red[-3], red[-1]
    dm_all = _unpack(grads_in, shapes, lead=NDEV)[-2]

    e_rows = jnp.concatenate([dm_all, dcm[None], jnp.zeros((PAD_ROWS - NDEV - 1, DEPTH, ADA_COLS), F32)], axis=0)
    grads["ada_w"] = jnp.stack([_mm(silu_cc, e_rows[:, l], ta=True, name="ada_dw") for l in range(DEPTH)])
    ada_b_blk = jnp.sum(e_rows, axis=0)
    dcm_rows = jnp.concatenate([dcm[None], jnp.zeros((PAD_ROWS - 1, DEPTH, ADA_COLS), F32)], axis=0)
    cpart = sum(_mm(dcm_rows[:, l], w["ada_w"][l], tb=True, name="ada_dc")[0] for l in range(DEPTH))

    small_all = _exchange([_pack([rep_mine, ada_b_blk, cpart])], True, "gather_small")[0]
    sm = _unpack(small_all, [rep_mine.shape, (DEPTH, ADA_COLS), (D,)], lead=NDEV)
    rep_full = sm[0].reshape(-1)[:n_rep]
    off = 0
    for n in _REPLICATED:
        k = int(np.prod(w[n].shape))
        grads[n] = rep_full[off:off + k].reshape(w[n].shape)
        off += k
    grads["ada_b"] = sm[1].transpose(1, 0, 2).reshape(DEPTH, N_MODS * D)
    csum = _unpack(_sum_slabs(small_all), [rep_mine.shape, (DEPTH, ADA_COLS), (D,)])[2]

    def dsilu(vv, dd):
        return (jax.vjp(jax.nn.silu, vv)[1](dd)[0],)

    grads["c_ctx"] = _rows(dsilu, [jnp.broadcast_to(w["c_ctx"][None], (SUBLANE, D)), jnp.broadcast_to(csum[None], (SUBLANE, D))],
                           [], [(D, F32)], [], name="dsilu", tm=SUBLANE)[0][0]

    delta, new_m, new_v = {}, {}, {}
    for n in _WEIGHTS:
        delta[n], new_m[n], new_v[n] = _adamw(w[n], grads[n], m[n], v[n])
    loss = lax.psum(loss[0, 0], ("x", "y", "c"))
    return loss, grad_x, grads, delta, new_m, new_v


def kernel(x, c, ctx, c_ctx, ada_w, ada_b, ln1_g, ln1_b, ln2_g, ln2_b, ffn_w_up, ffn_conv_w, ffn_conv_b, ffn_w_down, pool_w, pool_b, pool_scale, attn_w_qkv, attn_w_o, attn_sink, ssm_lambda_re, ssm_lambda_im, ssm_log_dt, ssm_b_re, ssm_b_im, ssm_c_re, ssm_c_im, ssm_d, ssm_w_glu_a, ssm_w_glu_b, gmlp_w_in, gmlp_b_in, gmlp_ln_g, gmlp_ln_b, gmlp_w_s, gmlp_b_s, gmlp_w_out, loss_target, m_c_ctx, m_ada_w, m_ada_b, m_ln1_g, m_ln1_b, m_ln2_g, m_ln2_b, m_ffn_w_up, m_ffn_conv_w, m_ffn_conv_b, m_ffn_w_down, m_pool_w, m_pool_b, m_pool_scale, m_attn_w_qkv, m_attn_w_o, m_attn_sink, m_ssm_lambda_re, m_ssm_lambda_im, m_ssm_log_dt, m_ssm_b_re, m_ssm_b_im, m_ssm_c_re, m_ssm_c_im, m_ssm_d, m_ssm_w_glu_a, m_ssm_w_glu_b, m_gmlp_w_in, m_gmlp_b_in, m_gmlp_ln_g, m_gmlp_ln_b, m_gmlp_w_s, m_gmlp_b_s, m_gmlp_w_out, v_c_ctx, v_ada_w, v_ada_b, v_ln1_g, v_ln1_b, v_ln2_g, v_ln2_b, v_ffn_w_up, v_ffn_conv_w, v_ffn_conv_b, v_ffn_w_down, v_pool_w, v_pool_b, v_pool_scale, v_attn_w_qkv, v_attn_w_o, v_attn_sink, v_ssm_lambda_re, v_ssm_lambda_im, v_ssm_log_dt, v_ssm_b_re, v_ssm_b_im, v_ssm_c_re, v_ssm_c_im, v_ssm_d, v_ssm_w_glu_a, v_ssm_w_glu_b, v_gmlp_w_in, v_gmlp_b_in, v_gmlp_ln_g, v_gmlp_ln_b, v_gmlp_w_s, v_gmlp_b_s, v_gmlp_w_out):
    args = dict(locals())
    w = {n: args[n] for n in _WEIGHTS}
    m = {n: args["m_" + n] for n in _WEIGHTS}
    v = {n: args["v_" + n] for n in _WEIGHTS}
    loss, grad_x, grads, delta, new_m, new_v = _step(x, c, ctx, loss_target, w, m, v)
    return (loss, grad_x, *[grads[n] for n in _WEIGHTS], *[delta[n] for n in _WEIGHTS],
            *[new_m[n] for n in _WEIGHTS], *[new_v[n] for n in _WEIGHTS])
```

```python
import functools
import math

import jax
import jax.numpy as jnp
import numpy as np
from jax import lax
from jax.experimental import pallas as pl
from jax.experimental.pallas import tpu as pltpu

D = 1024
SEQ = 4096
NCTX = 256
T = NCTX + SEQ
DEPTH = 4
NDEV = 8
GRID_W = 64
ALPHA = (2.0 * DEPTH) ** 0.25
LN_EPS = 1e-5
FFN_H = 2816
HEAD_DIM = 64
NQH, NKVH, GQA = 16, 4, 4
WINDOW = 128
ABLK = 128
NEG_INF = -1e30
ROPE_BASE = 10000.0
POOL_WINDOWS = (2, 4, 8, 16)
SSM_G, SSM_P, SSM_C = 64, 64, 16
GM_HALF = 2048
GM_HEADS = 8
GM_HD = GM_HALF // GM_HEADS
GM_CHUNK = 128
B1, B2, LR, EPS, WD, STEP = 0.9, 0.999, 0.001, 1e-8, 0.01, 10

LANE = 128
SUBLANE = 8
VMEM_LIMIT = 56 * 1024 * 1024
TM = 256
NT = T // TM

_MXU = jnp.bfloat16
F32 = jnp.float32


def _pcall(body, **kw):
    return pl.pallas_call(body, **kw)


def _cp(sem):
    return pltpu.CompilerParams(dimension_semantics=sem, vmem_limit_bytes=VMEM_LIMIT)


def _tile(dim, pref, align):
    best = None
    for t in range(align, min(dim, pref) + 1, align):
        if dim % t == 0:
            best = t
    return dim if best is None else best


def _mm(a, b, *, ta=False, tb=False, out_dtype=F32, name):
    if ta:
        a = a.astype(_MXU).T
    M, K = a.shape
    if tb:
        N, K2 = b.shape
    else:
        K2, N = b.shape
    assert K == K2, (a.shape, b.shape, ta, tb)
    tm = _tile(M, 2304, 16)
    tn = _tile(N, 512, LANE)
    tk = _tile(K, 2304, LANE)
    nk = K // tk
    dims = (((1,), (1,) if tb else (0,)), ((), ()))

    def body(a_ref, b_ref, o_ref, acc_ref):
        k = pl.program_id(2)
        r = lax.dot_general(a_ref[...].astype(_MXU), b_ref[...].astype(_MXU), dims, preferred_element_type=F32)

        @pl.when(k == 0)
        def _():
            acc_ref[...] = r

        @pl.when(k > 0)
        def _():
            acc_ref[...] += r

        @pl.when(k == nk - 1)
        def _():
            o_ref[...] = acc_ref[...].astype(o_ref.dtype)

    a_spec = pl.BlockSpec((tm, tk), lambda j, i, k: (i, k))
    b_spec = pl.BlockSpec((tn, tk), lambda j, i, k: (j, k)) if tb else pl.BlockSpec((tk, tn), lambda j, i, k: (k, j))
    return _pcall(
        body, name=name, grid=(N // tn, M // tm, nk), in_specs=[a_spec, b_spec],
        out_specs=pl.BlockSpec((tm, tn), lambda j, i, k: (i, j)),
        out_shape=jax.ShapeDtypeStruct((M, N), out_dtype),
        scratch_shapes=[pltpu.VMEM((tm, tn), F32)],
        compiler_params=_cp(("parallel", "parallel", "arbitrary")),
    )(a, b)


def _rows(fn, rows, pars, out_rows, out_pars, *, name, tm=TM):
    R = rows[0].shape[0]
    nt = R // tm
    nct = NCTX // tm
    n_r, n_p, n_or, n_op = len(rows), len(pars), len(out_rows), len(out_pars)

    def sel(S):
        if S == 1:
            return lambda i: 0
        return lambda i: jnp.where(i < nct, 0, 1)

    def body(*refs):
        r_in = refs[:n_r]
        p_in = refs[n_r:n_r + n_p]
        r_out = refs[n_r + n_p:n_r + n_p + n_or]
        p_out = refs[n_r + n_p + n_or:]
        i = pl.program_id(0)
        vals = [r[...].astype(F32) for r in r_in] + [p[0] for p in p_in]
        outs = fn(*vals)
        for r, v in zip(r_out, outs[:n_or]):
            r[...] = v.astype(r.dtype)
        for (S, _), r, v in zip(out_pars, p_out, outs[n_or:]):
            first = (i == 0) if S == 1 else jnp.logical_or(i == 0, i == nct)

            @pl.when(first)
            def _():
                r[0] = v

            @pl.when(jnp.logical_not(first))
            def _():
                r[0] += v

    def pspec(shape):
        S = shape[0]
        rest = tuple(shape[1:])
        s = sel(S)
        return pl.BlockSpec((1,) + rest, lambda i: (s(i),) + (0,) * len(rest))

    in_specs = [pl.BlockSpec((tm, r.shape[1]), lambda i: (i, 0)) for r in rows] + [pspec(p.shape) for p in pars]
    out_specs = [pl.BlockSpec((tm, w), lambda i: (i, 0)) for w, _ in out_rows] + [pspec((S,) + tuple(sh)) for S, sh in out_pars]
    out_shape = [jax.ShapeDtypeStruct((R, w), dt) for w, dt in out_rows] + \
                [jax.ShapeDtypeStruct((S,) + tuple(sh), F32) for S, sh in out_pars]
    res = _pcall(body, name=name, grid=(nt,), in_specs=in_specs, out_specs=out_specs, out_shape=out_shape,
                 compiler_params=_cp(("arbitrary",)))(*rows, *pars)
    return res


def _ln(z, g, b):
    mu = jnp.mean(z, axis=-1, keepdims=True)
    var = jnp.mean(jnp.square(z - mu), axis=-1, keepdims=True)
    return (z - mu) * lax.rsqrt(var + LN_EPS) * g + b


def _f1(x, sh, sc):
    return x * (1.0 + sc) + sh


def _f2(x, y, gt, g, b, sh, sc):
    x1 = _ln(ALPHA * x + gt * y, g, b)
    return x1, x1 * (1.0 + sc) + sh


def _f3(x1, f, gt, g, b):
    return _ln(ALPHA * x1 + gt * f, g, b)


def _pre_mixer(x, sh, sc, dtype):
    return _rows(lambda x, sh, sc: (_f1(x, sh, sc),), [x], [sh, sc], [(D, dtype)], [], name="pre_mixer")[0]


def _pre_mixer_bwd(x, dhs, dx_prev, sh, sc):
    n = len(dhs)

    def fn(x, *rest):
        dh = rest[0]
        for t in rest[1:n]:
            dh = dh + t
        dxp, sh, sc = rest[n], rest[n + 1], rest[n + 2]
        _, vjp = jax.vjp(_f1, x, sh, sc)
        dx, dsh, dsc = vjp(dh)
        return dxp + dx, dsh, dsc

    return _rows(fn, [x, *dhs, dx_prev], [sh, sc], [(D, F32)], [(2, (1, D)), (2, (1, D))], name="pre_mixer_bwd")


def _post_mixer(x, y, gt, g, b, sh, sc):
    return _rows(_f2, [x, y], [gt, g, b, sh, sc], [(D, F32), (D, _MXU)], [], name="post_mixer")


def _post_mixer_bwd(x, y, dx1, dh2, gt, g, b, sh, sc):
    def fn(x, y, dx1, dh2, gt, g, b, sh, sc):
        _, vjp = jax.vjp(_f2, x, y, gt, g, b, sh, sc)
        return vjp((dx1, dh2))

    return _rows(fn, [x, y, dx1, dh2], [gt, g, b, sh, sc], [(D, F32), (D, F32)],
                 [(2, (1, D)), (1, (1, D)), (1, (1, D)), (2, (1, D)), (2, (1, D))], name="post_mixer_bwd")


def _post_ffn(x1, f, gt, g, b):
    return _rows(lambda *a: (_f3(*a),), [x1, f], [gt, g, b], [(D, F32)], [], name="post_ffn")[0]


def _post_ffn_bwd(x1, f, dx2, gt, g, b):
    def fn(x1, f, dx2, gt, g, b):
        _, vjp = jax.vjp(_f3, x1, f, gt, g, b)
        return vjp(dx2)

    return _rows(fn, [x1, f, dx2], [gt, g, b], [(D, F32), (D, _MXU)],
                 [(2, (1, D)), (1, (1, D)), (1, (1, D))], name="post_ffn_bwd")


def _halo_specs(tm, w, col, active=None):
    r8 = tm // SUBLANE
    act = (lambda j, r: r) if active is None else (lambda j, r: jnp.where(active(j), r, 0))
    return [
        pl.BlockSpec((SUBLANE, w), lambda j, i: (act(j, jnp.maximum(i * r8 - 1, 0)), col(j))),
        pl.BlockSpec((tm, w), lambda j, i: (act(j, i), col(j))),
        pl.BlockSpec((SUBLANE, w), lambda j, i: (act(j, jnp.minimum((i + 1) * r8, T // SUBLANE - 1)), col(j))),
    ]


def _seg_flags(i, tm):
    nct = NCTX // tm
    first = jnp.logical_or(i == 0, i == nct)
    last = jnp.logical_or(i == nct - 1, i == T // tm - 1)
    return first, last


def _shift_rows(cur, prev8, next8, first, last):
    tm = cur.shape[0]
    rid = lax.broadcasted_iota(jnp.int32, cur.shape, 0)
    pr = jnp.where(first, 0.0, prev8[SUBLANE - 1:SUBLANE, :])
    nx = jnp.where(last, 0.0, next8[0:1, :])
    up = jnp.where(rid == 0, pr, pltpu.roll(cur, 1, 0))
    dn = jnp.where(rid == tm - 1, nx, pltpu.roll(cur, tm - 1, 0))
    return up, dn


FFN_TC = 1408
FFN_NCT = FFN_H // FFN_TC


def _conv3(cur, prev8, next8, w3, first, last):
    up, dn = _shift_rows(cur, prev8, next8, first, last)
    return up * w3[0:1] + cur * w3[1:2] + dn * w3[2:3], up, dn


def _ffn_mid(a, cw, cb):
    def body(vp, vc, vn, gp, gc, gn, cwv, cwg, cbv, cbg, o_ref):
        first, last = _seg_flags(pl.program_id(1), TM)
        val = _conv3(vc[...], vp[...], vn[...], cwv[...], first, last)[0] + cbv[...]
        gate = _conv3(gc[...], gp[...], gn[...], cwg[...], first, last)[0] + cbg[...]
        o_ref[...] = (val * jax.nn.silu(gate)).astype(o_ref.dtype)

    specs = _halo_specs(TM, FFN_TC, lambda j: j) + _halo_specs(TM, FFN_TC, lambda j: j + FFN_NCT)
    specs += [pl.BlockSpec((3, FFN_TC), lambda j, i: (0, j)), pl.BlockSpec((3, FFN_TC), lambda j, i: (0, j + FFN_NCT)),
              pl.BlockSpec((1, FFN_TC), lambda j, i: (0, j)), pl.BlockSpec((1, FFN_TC), lambda j, i: (0, j + FFN_NCT))]
    return _pcall(body, name="ffn_mid", grid=(FFN_NCT, NT), in_specs=specs,
                  out_specs=pl.BlockSpec((TM, FFN_TC), lambda j, i: (i, j)),
                  out_shape=jax.ShapeDtypeStruct((T, FFN_H), _MXU),
                  compiler_params=_cp(("parallel", "arbitrary")))(a, a, a, a, a, a, cw, cw, cb, cb)


def _ffn_mid_bwd1(a, du, cw, cb):
    def body(vp, vc, vn, gp, gc, gn, du_ref, cwv, cwg, cbv, cbg, dv_ref, dg_ref, dcwv, dcwg, dcbv, dcbg):
        i = pl.program_id(1)
        first, last = _seg_flags(i, TM)
        v0, vup, vdn = _conv3(vc[...], vp[...], vn[...], cwv[...], first, last)
        g0, gup, gdn = _conv3(gc[...], gp[...], gn[...], cwg[...], first, last)
        val = v0 + cbv[...]
        gate = g0 + cbg[...]
        sg = jax.nn.sigmoid(gate)
        du = du_ref[...]
        dval = du * (gate * sg)
        dgate = du * val * (sg * (1.0 + gate * (1.0 - sg)))
        dv_ref[...] = dval
        dg_ref[...] = dgate

        def acc(ref, v):
            @pl.when(i == 0)
            def _():
                ref[...] = v

            @pl.when(i > 0)
            def _():
                ref[...] += v

        for dref, d, up, cur, dn, bref in ((dcwv, dval, vup, vc[...], vdn, dcbv), (dcwg, dgate, gup, gc[...], gdn, dcbg)):
            acc(dref, jnp.concatenate([jnp.sum(d * up, 0, keepdims=True), jnp.sum(d * cur, 0, keepdims=True),
                                       jnp.sum(d * dn, 0, keepdims=True)], axis=0))
            acc(bref, jnp.sum(d, 0, keepdims=True))

    specs = _halo_specs(TM, FFN_TC, lambda j: j) + _halo_specs(TM, FFN_TC, lambda j: j + FFN_NCT)
    specs += [pl.BlockSpec((TM, FFN_TC), lambda j, i: (i, j))]
    specs += [pl.BlockSpec((3, FFN_TC), lambda j, i: (0, j)), pl.BlockSpec((3, FFN_TC), lambda j, i: (0, j + FFN_NCT)),
              pl.BlockSpec((1, FFN_TC), lambda j, i: (0, j)), pl.BlockSpec((1, FFN_TC), lambda j, i: (0, j + FFN_NCT))]
    out_specs = [pl.BlockSpec((TM, FFN_TC), lambda j, i: (i, j)), pl.BlockSpec((TM, FFN_TC), lambda j, i: (i, j)),
                 pl.BlockSpec((3, FFN_TC), lambda j, i: (0, j)), pl.BlockSpec((3, FFN_TC), lambda j, i: (0, j)),
                 pl.BlockSpec((1, FFN_TC), lambda j, i: (0, j)), pl.BlockSpec((1, FFN_TC), lambda j, i: (0, j))]
    out_shape = [jax.ShapeDtypeStruct((T, FFN_H), F32)] * 2 + [jax.ShapeDtypeStruct((3, FFN_H), F32)] * 2 + \
                [jax.ShapeDtypeStruct((1, FFN_H), F32)] * 2
    dv, dg, dcwv, dcwg, dcbv, dcbg = _pcall(
        body, name="ffn_mid_bwd1", grid=(FFN_NCT, NT), in_specs=specs, out_specs=out_specs, out_shape=out_shape,
        compiler_params=_cp(("parallel", "arbitrary")))(a, a, a, a, a, a, du, cw, cw, cb, cb)
    return dv, dg, jnp.concatenate([dcwv, dcwg], axis=1), jnp.concatenate([dcbv, dcbg], axis=1)


def _ffn_mid_bwd2(dv, dg, cw):
    tc = FFN_TC
    ncol = 2 * FFN_NCT

    def body(pv, cv, nv, pg, cg, ng, cw_ref, o_ref):
        j = pl.program_id(0)
        first, last = _seg_flags(pl.program_id(1), TM)
        w3 = cw_ref[...]
        w3r = jnp.concatenate([w3[2:3], w3[1:2], w3[0:1]], axis=0)
        is_val = j < FFN_NCT
        cur = jnp.where(is_val, cv[...], cg[...])
        p8 = jnp.where(is_val, pv[...], pg[...])
        n8 = jnp.where(is_val, nv[...], ng[...])
        o_ref[...] = _conv3(cur, p8, n8, w3r, first, last)[0].astype(o_ref.dtype)

    is_v = lambda j: j < FFN_NCT
    specs = _halo_specs(TM, tc, lambda j: jnp.minimum(j, FFN_NCT - 1), is_v) + \
        _halo_specs(TM, tc, lambda j: jnp.maximum(j - FFN_NCT, 0), lambda j: jnp.logical_not(is_v(j))) + \
        [pl.BlockSpec((3, tc), lambda j, i: (0, j))]
    return _pcall(body, name="ffn_mid_bwd2", grid=(ncol, NT), in_specs=specs,
                  out_specs=pl.BlockSpec((TM, tc), lambda j, i: (i, j)),
                  out_shape=jax.ShapeDtypeStruct((T, 2 * FFN_H), _MXU),
                  compiler_params=_cp(("parallel", "arbitrary")))(dv, dv, dv, dg, dg, dg, cw)


def _ffn_fwd(h2, w_up, cw, cb, w_down):
    a = _mm(h2, w_up, name="ffn_up")
    u = _ffn_mid(a, cw, cb)
    f = _mm(u, w_down, name="ffn_down")
    return f, (a, u)


def _ffn_bwd(df, h2, a, u, w_up, cw, cb, w_down):
    dw_down = _mm(u, df, ta=True, name="ffn_down_dw")
    du = _mm(df, w_down, tb=True, name="ffn_down_dx")
    dv, dg, dcw, dcb = _ffn_mid_bwd1(a, du, cw, cb)
    da = _ffn_mid_bwd2(dv, dg, cw)
    dw_up = _mm(h2, da, ta=True, name="ffn_up_dw")
    dh2 = _mm(da, w_up, tb=True, name="ffn_up_dx")
    return dh2, dw_up, dcw, dcb, dw_down


def _pool_ext(p_ref, c_ref, n_ref, first, last):
    p8 = jnp.where(first, 0.0, p_ref[...])
    n8 = jnp.where(last, 0.0, n_ref[...])
    return jnp.concatenate([p8, c_ref[...], n8], axis=0)


def _winsum(e, lo, hi):
    n = e.shape[0]
    acc = None
    for o in range(lo, hi + 1):
        t = e if o == 0 else pltpu.roll(e, (-o) % n, 0)
        acc = t if acc is None else acc + t
    return acc


def _pool_cnt(i, w, rows, off):
    nct = NCTX // TM
    seg_len = jnp.where(i < nct, NCTX, SEQ)
    seg_tile = jnp.where(i < nct, i, i - nct)
    pos = lax.broadcasted_iota(jnp.int32, (rows, 1), 0) - off + seg_tile * TM
    lo = jnp.clip(pos - w // 2, 0, seg_len)
    hi = jnp.clip(pos - w // 2 + w, 0, seg_len)
    return jnp.maximum(hi - lo, 1).astype(F32)


def _pool_fwd(h, pw, pb, ps):
    def body(hp, hc, hn, w_ref, b_ref, s_ref, o_ref):
        i = pl.program_id(1)
        first, last = _seg_flags(i, TM)
        e = _pool_ext(hp, hc, hn, first, last)
        outs = []
        for g, w in enumerate(POOL_WINDOWS):
            sl = slice(256 * g, 256 * (g + 1))
            eg = e[:, sl]
            mean = _winsum(eg, -(w // 2), w // 2 - 1)[SUBLANE:SUBLANE + TM] / _pool_cnt(i, w, TM, 0)
            mixed = mean - hc[:, sl]
            outs.append(jnp.dot(mixed.astype(_MXU), w_ref[g].astype(_MXU), preferred_element_type=F32))
        o_ref[...] = (jnp.concatenate(outs, axis=1) + b_ref[...]) * s_ref[...]

    full = lambda *s: pl.BlockSpec(s, lambda j, i: (0,) * len(s))
    return _pcall(body, name="pool_fwd", grid=(1, NT), in_specs=_halo_specs(TM, D, lambda j: 0) + [full(4, 256, 256), full(1, D), full(1, D)],
                  out_specs=pl.BlockSpec((TM, D), lambda j, i: (i, 0)), out_shape=jax.ShapeDtypeStruct((T, D), F32),
                  compiler_params=_cp(("parallel", "arbitrary")))(h, h, h, pw, pb, ps)


def _pool_bwd(h, dy, pw, pb, ps):
    E = TM + 2 * SUBLANE

    def body(hp, hc, hn, dp, dc, dn, w_ref, b_ref, s_ref, dh_ref, dw_ref, db_ref, ds_ref):
        i = pl.program_id(1)
        first, last = _seg_flags(i, TM)
        e = _pool_ext(hp, hc, hn, first, last)
        de = _pool_ext(dp, dc, dn, first, last)
        dys = de * s_ref[...]
        dhs, pre = [], []
        for g, w in enumerate(POOL_WINDOWS):
            sl = slice(256 * g, 256 * (g + 1))
            wg = w_ref[g].astype(_MXU)
            dyg = dys[:, sl].astype(_MXU)
            dmix = lax.dot_general(dyg, wg, (((1,), (1,)), ((), ())), preferred_element_type=F32)
            q = dmix / _pool_cnt(i, w, E, SUBLANE)
            dhs.append(_winsum(q, -(w // 2) + 1, w // 2)[SUBLANE:SUBLANE + TM] - dmix[SUBLANE:SUBLANE + TM])
            mean = _winsum(e[:, sl], -(w // 2), w // 2 - 1)[SUBLANE:SUBLANE + TM] / _pool_cnt(i, w, TM, 0)
            mixed = (mean - hc[:, sl]).astype(_MXU)
            pre.append(jnp.dot(mixed, wg, preferred_element_type=F32))
            dwg = lax.dot_general(mixed, dyg[SUBLANE:SUBLANE + TM], (((0,), (0,)), ((), ())), preferred_element_type=F32)

            @pl.when(i == 0)
            def _():
                dw_ref[g] = dwg

            @pl.when(i > 0)
            def _():
                dw_ref[g] += dwg
        dh_ref[...] = jnp.concatenate(dhs, axis=1)
        db = jnp.sum(dys[SUBLANE:SUBLANE + TM], 0, keepdims=True)
        ds = jnp.sum(dc[...] * (jnp.concatenate(pre, axis=1) + b_ref[...]), 0, keepdims=True)

        @pl.when(i == 0)
        def _():
            db_ref[...] = db
            ds_ref[...] = ds

        @pl.when(i > 0)
        def _():
            db_ref[...] += db
            ds_ref[...] += ds

    full = lambda *s: pl.BlockSpec(s, lambda j, i: (0,) * len(s))
    specs = _halo_specs(TM, D, lambda j: 0) + _halo_specs(TM, D, lambda j: 0) + [full(4, 256, 256), full(1, D), full(1, D)]
    return _pcall(body, name="pool_bwd", grid=(1, NT), in_specs=specs,
                  out_specs=[pl.BlockSpec((TM, D), lambda j, i: (i, 0)), full(4, 256, 256), full(1, D), full(1, D)],
                  out_shape=[jax.ShapeDtypeStruct((T, D), F32), jax.ShapeDtypeStruct((4, 256, 256), F32),
                             jax.ShapeDtypeStruct((1, D), F32), jax.ShapeDtypeStruct((1, D), F32)],
                  compiler_params=_cp(("parallel", "arbitrary")))(h, h, h, dy, dy, dy, pw, pb, ps)


def _rope_tables():
    half = HEAD_DIM // 4
    t = jnp.arange(SEQ)
    freqs = ROPE_BASE ** (-jnp.arange(half, dtype=F32) / half)
    ang_r = (t // GRID_W).astype(F32)[:, None] * freqs[None, :]
    ang_c = (t % GRID_W).astype(F32)[:, None] * freqs[None, :]
    cos = jnp.concatenate([jnp.cos(ang_r), jnp.cos(ang_r), jnp.cos(ang_c), jnp.cos(ang_c)], axis=1)
    sin = jnp.concatenate([-jnp.sin(ang_r), jnp.sin(ang_r), -jnp.sin(ang_c), jnp.sin(ang_c)], axis=1)
    cos = jnp.concatenate([jnp.ones((NCTX, HEAD_DIM), F32), cos], axis=0)
    sin = jnp.concatenate([jnp.zeros((NCTX, HEAD_DIM), F32), sin], axis=0)
    return jnp.tile(cos, (1, 2)), jnp.tile(sin, (1, 2))


QK_W = (NQH + NKVH) * HEAD_DIM
QKV_W = QK_W + NKVH * HEAD_DIM


def _rope(x, cos, sin, sign):
    def body(x_ref, c_ref, s_ref, o_ref):
        c = c_ref[...]
        s = s_ref[...] * sign
        lane = lax.broadcasted_iota(jnp.int32, (TM, LANE), 1)
        lo = (lane % 32) < 16
        for k in range(QK_W // LANE):
            xk = x_ref[:, LANE * k:LANE * (k + 1)]
            partner = jnp.where(lo, pltpu.roll(xk, LANE - 16, 1), pltpu.roll(xk, 16, 1))
            o_ref[:, LANE * k:LANE * (k + 1)] = (xk * c + partner * s).astype(o_ref.dtype)
        o_ref[:, QK_W:] = x_ref[:, QK_W:].astype(o_ref.dtype)

    return _pcall(body, name="rope", grid=(NT,),
                  in_specs=[pl.BlockSpec((TM, QKV_W), lambda i: (i, 0)), pl.BlockSpec((TM, LANE), lambda i: (i, 0)),
                            pl.BlockSpec((TM, LANE), lambda i: (i, 0))],
                  out_specs=pl.BlockSpec((TM, QKV_W), lambda i: (i, 0)),
                  out_shape=jax.ShapeDtypeStruct((T, QKV_W), _MXU), compiler_params=_cp(("parallel",)))(x, cos, sin)


NQB = T // ABLK
KPAD = T + 2 * ABLK
NKEY = NCTX + 3 * ABLK


def _attn_mask(i):
    r = lax.broadcasted_iota(jnp.int32, (ABLK, NKEY), 0)
    c = lax.broadcasted_iota(jnp.int32, (ABLK, NKEY), 1)
    n = i - NCTX // ABLK
    kpos = (n - 1) * ABLK + (c - NCTX)
    qpos = n * ABLK + r
    loc = (c >= NCTX) & (jnp.abs(kpos - qpos) <= WINDOW) & (kpos >= 0) & (kpos < SEQ) & (n >= 0)
    return (c < NCTX) | loc


def _attn_specs():
    qs = pl.BlockSpec((GQA, ABLK, HEAD_DIM), lambda h, i: (h, i, 0))
    kc = pl.BlockSpec((1, NCTX, HEAD_DIM), lambda h, i: (h, 0, 0))
    kl = [pl.BlockSpec((1, ABLK, HEAD_DIM), functools.partial(lambda h, i, d: (h, i + d, 0), d=d)) for d in range(3)]
    sk = pl.BlockSpec((GQA, 1, 1), lambda h, i: (h, 0, 0))
    return qs, kc, kl, sk


def _attn_fwd(q, k, v, sink):
    scale = HEAD_DIM ** -0.5

    def body(q_ref, kc, k0, k1, k2, vc, v0, v1, v2, s_ref, o_ref, l_ref):
        valid = _attn_mask(pl.program_id(1))
        kk = jnp.concatenate([kc[0], k0[0], k1[0], k2[0]], axis=0)
        vv = jnp.concatenate([vc[0], v0[0], v1[0], v2[0]], axis=0)
        for g in range(GQA):
            s = lax.dot_general(q_ref[g], kk, (((1,), (1,)), ((), ())), preferred_element_type=F32) * scale
            s = jnp.where(valid, s, NEG_INF)
            sk = s_ref[g]
            m = jnp.maximum(jnp.max(s, axis=-1, keepdims=True), sk)
            p = jnp.exp(s - m)
            l = jnp.sum(p, axis=-1, keepdims=True) + jnp.exp(sk - m)
            o_ref[g] = jnp.dot((p / l).astype(_MXU), vv, preferred_element_type=F32).astype(o_ref.dtype)
            l_ref[g] = m + jnp.log(l)

    qs, kc, kl, sk = _attn_specs()
    return _pcall(body, name="attn_fwd", grid=(NKVH, NQB), in_specs=[qs, kc, *kl, kc, *kl, sk],
                  out_specs=[qs, pl.BlockSpec((GQA, ABLK, 1), lambda h, i: (h, i, 0))],
                  out_shape=[jax.ShapeDtypeStruct((NQH, T, HEAD_DIM), _MXU), jax.ShapeDtypeStruct((NQH, T, 1), F32)],
                  compiler_params=_cp(("parallel", "arbitrary")))(q, k, k, k, k, v, v, v, v, sink)


def _attn_bwd(q, k, v, sink, lse, do):
    scale = HEAD_DIM ** -0.5

    def body(q_ref, kc, k0, k1, k2, vc, v0, v1, v2, s_ref, l_ref, do_ref, dq_ref, dk_ref, dv_ref, ds_ref):
        i = pl.program_id(1)

        @pl.when(i == 0)
        def _():
            dk_ref[...] = jnp.zeros_like(dk_ref)
            dv_ref[...] = jnp.zeros_like(dv_ref)
            ds_ref[...] = jnp.zeros_like(ds_ref)

        valid = _attn_mask(i)
        kk = jnp.concatenate([kc[0], k0[0], k1[0], k2[0]], axis=0)
        vv = jnp.concatenate([vc[0], v0[0], v1[0], v2[0]], axis=0)
        dkk = jnp.zeros((NKEY, HEAD_DIM), F32)
        dvv = jnp.zeros((NKEY, HEAD_DIM), F32)
        for g in range(GQA):
            qg = q_ref[g]
            dog = do_ref[g]
            s = lax.dot_general(qg, kk, (((1,), (1,)), ((), ())), preferred_element_type=F32) * scale
            s = jnp.where(valid, s, NEG_INF)
            lse_g = l_ref[g]
            p = jnp.exp(s - lse_g)
            psink = jnp.exp(s_ref[g] - lse_g)
            dp = lax.dot_general(dog, vv, (((1,), (1,)), ((), ())), preferred_element_type=F32)
            delta = jnp.sum(p * dp, axis=-1, keepdims=True)
            ds = p * (dp - delta)
            ds_ref[g] += -jnp.sum(psink * delta, axis=0, keepdims=True)
            dsq = (ds * scale).astype(_MXU)
            dq_ref[g] = jnp.dot(dsq, kk, preferred_element_type=F32)
            dkk += lax.dot_general(dsq, qg, (((0,), (0,)), ((), ())), preferred_element_type=F32)
            dvv += lax.dot_general(p.astype(_MXU), dog, (((0,), (0,)), ((), ())), preferred_element_type=F32)
        loc = pl.ds(pl.multiple_of(i * ABLK, ABLK), 3 * ABLK)
        dk_ref[0, 0:NCTX, :] += dkk[:NCTX]
        dv_ref[0, 0:NCTX, :] += dvv[:NCTX]
        dk_ref[0, loc, :] += dkk[NCTX:]
        dv_ref[0, loc, :] += dvv[NCTX:]

    qs, kc, kl, sk = _attn_specs()
    ls = pl.BlockSpec((GQA, ABLK, 1), lambda h, i: (h, i, 0))
    kfull = pl.BlockSpec((1, KPAD, HEAD_DIM), lambda h, i: (h, 0, 0))
    return _pcall(body, name="attn_bwd", grid=(NKVH, NQB), in_specs=[qs, kc, *kl, kc, *kl, sk, ls, qs],
                  out_specs=[qs, kfull, kfull, sk],
                  out_shape=[jax.ShapeDtypeStruct((NQH, T, HEAD_DIM), F32), jax.ShapeDtypeStruct((NKVH, KPAD, HEAD_DIM), F32),
                             jax.ShapeDtypeStruct((NKVH, KPAD, HEAD_DIM), F32), jax.ShapeDtypeStruct((NQH, 1, 1), F32)],
                  compiler_params=_cp(("parallel", "arbitrary")))(q, k, k, k, k, v, v, v, v, sink, lse, do)


def _split_heads(x, nh):
    return x.reshape(T, nh, HEAD_DIM).transpose(1, 0, 2)


def _merge_heads(x):
    return x.transpose(1, 0, 2).reshape(T, -1)


def _pad_keys(x):
    z = jnp.zeros((x.shape[0], ABLK, HEAD_DIM), x.dtype)
    return jnp.concatenate([x[:, :NCTX], z, x[:, NCTX:], z], axis=1)


def _unpad_keys(x):
    return jnp.concatenate([x[:, :NCTX], x[:, NCTX + ABLK:NCTX + ABLK + SEQ]], axis=1)


def _attn_mixer_fwd(h, w_qkv, w_o, sink):
    cos, sin = _rope_tables()
    qkv = _rope(_mm(h, w_qkv, name="attn_qkv"), cos, sin, 1.0)
    q = _split_heads(qkv[:, :NQH * HEAD_DIM], NQH)
    k = _pad_keys(_split_heads(qkv[:, NQH * HEAD_DIM:QK_W], NKVH))
    v = _pad_keys(_split_heads(qkv[:, QK_W:], NKVH))
    sk = sink.reshape(NQH, 1, 1)
    o, lse = _attn_fwd(q, k, v, sk)
    om = _merge_heads(o)
    y = _mm(om, w_o, name="attn_out")
    return y, (q, k, v, sk, lse, om)


def _attn_mixer_bwd(dy, h, saved, w_qkv, w_o):
    q, k, v, sk, lse, om = saved
    cos, sin = _rope_tables()
    dyb = dy.astype(_MXU)
    dw_o = _mm(om, dyb, ta=True, name="attn_out_dw")
    do = _split_heads(_mm(dyb, w_o, tb=True, out_dtype=_MXU, name="attn_out_dx"), NQH)
    dq, dk, dv, dsk = _attn_bwd(q, k, v, sk, lse, do)
    dqkv = jnp.concatenate([_merge_heads(dq), _merge_heads(_unpad_keys(dk)), _merge_heads(_unpad_keys(dv))], axis=1)
    dqkv = _rope(dqkv, cos, sin, -1.0)
    dw_qkv = _mm(h, dqkv, ta=True, name="attn_qkv_dw")
    dh = _mm(dqkv, w_qkv, tb=True, name="attn_qkv_dx")
    return dh, dw_qkv, dw_o, dsk.reshape(1, NQH)


SSM_S = SSM_G * SSM_P
SSM_SL = SSM_S // LANE
SSM_TS = 128
SSM_NTS = T // SSM_TS
SSM_NCT = NCTX // SSM_TS
SSM_JB = 4
SSM_NTR = 4
SSM_TR = T // SSM_NTR


def _proj3d(u, w_re, w_im):
    def body(u_ref, wr_ref, wi_ref, or_ref, oi_ref):
        for hlf in range(2):
            ub = u_ref[:, LANE * hlf:LANE * (hlf + 1)].astype(_MXU)
            for w_ref, o_ref in ((wr_ref, or_ref), (wi_ref, oi_ref)):
                r = jnp.dot(ub, w_ref[hlf].astype(_MXU), preferred_element_type=F32)
                for q in range(4):
                    o_ref[:, 4 * hlf + q, :] = r[:, LANE * q:LANE * (q + 1)]

    ws = pl.BlockSpec((2, LANE, 512), lambda i, j: (j, 0, 0))
    os_ = pl.BlockSpec((SSM_TR, 8, LANE), lambda i, j: (i, j, 0))
    return _pcall(body, name="ssm_proj", grid=(SSM_NTR, SSM_JB), in_specs=[pl.BlockSpec((SSM_TR, 2 * LANE), lambda i, j: (i, j)), ws, ws],
                  out_specs=[os_, os_], out_shape=[jax.ShapeDtypeStruct((T, SSM_SL, LANE), F32)] * 2,
                  compiler_params=_cp(("parallel", "parallel")))(u, w_re, w_im)


def _readout(s_re, s_im, w_re, w_im):
    def body(sr_ref, si_ref, wr_ref, wi_ref, o_ref):
        for hlf in range(2):
            acc = None
            for s_ref, w_ref in ((sr_ref, wr_ref), (si_ref, wi_ref)):
                x = jnp.concatenate([s_ref[:, 4 * hlf + q, :] for q in range(4)], axis=1).astype(_MXU)
                r = jnp.dot(x, w_ref[hlf].astype(_MXU), preferred_element_type=F32)
                acc = r if acc is None else acc + r
            o_ref[:, LANE * hlf:LANE * (hlf + 1)] = acc

    ss = pl.BlockSpec((SSM_TR, 8, LANE), lambda i, j: (i, j, 0))
    ws = pl.BlockSpec((2, 512, LANE), lambda i, j: (j, 0, 0))
    return _pcall(body, name="ssm_readout", grid=(SSM_NTR, SSM_JB), in_specs=[ss, ss, ws, ws],
                  out_specs=pl.BlockSpec((SSM_TR, 2 * LANE), lambda i, j: (i, j)), out_shape=jax.ShapeDtypeStruct((T, D), F32),
                  compiler_params=_cp(("parallel", "parallel")))(s_re, s_im, w_re, w_im)


def _outer3d(s_re, s_im, y):
    def body(sr_ref, si_ref, y_ref, dr_ref, di_ref):
        i = pl.program_id(1)
        for hlf in range(2):
            yb = y_ref[:, LANE * hlf:LANE * (hlf + 1)].astype(_MXU)
            for s_ref, d_ref in ((sr_ref, dr_ref), (si_ref, di_ref)):
                x = jnp.concatenate([s_ref[:, 4 * hlf + q, :] for q in range(4)], axis=1).astype(_MXU)
                r = lax.dot_general(yb, x, (((0,), (0,)), ((), ())), preferred_element_type=F32)

                @pl.when(i == 0)
                def _():
                    d_ref[hlf] = r

                @pl.when(i > 0)
                def _():
                    d_ref[hlf] += r

    ss = pl.BlockSpec((SSM_TR, 8, LANE), lambda j, i: (i, j, 0))
    ds = pl.BlockSpec((2, LANE, 512), lambda j, i: (j, 0, 0))
    return _pcall(body, name="ssm_outer", grid=(SSM_JB, SSM_NTR), in_specs=[ss, ss, pl.BlockSpec((SSM_TR, 2 * LANE), lambda j, i: (i, j))],
                  out_specs=[ds, ds], out_shape=[jax.ShapeDtypeStruct((8, LANE, 512), F32)] * 2,
                  compiler_params=_cp(("parallel", "arbitrary")))(s_re, s_im, y)


def _scan_order(order):
    n, c = SSM_NTS, SSM_NCT
    if order == "fwd":
        return (lambda i: i), False
    if order == "fwd_adj":
        return (lambda i: n - 1 - i), True
    if order == "rev":
        return (lambda i: jnp.where(i < c, c - 1 - i, n + c - 1 - i)), True
    if order == "rev_adj":
        return (lambda i: jnp.where(i < n - c, i + c, i - (n - c))), False
    raise ValueError(order)


def _scan(b_re, b_im, lam_re, lam_im, order):
    tile, down = _scan_order(order)

    def body(br_ref, bi_ref, lr_ref, li_ref, sr_ref, si_ref, cr, ci):
        @pl.when(pl.program_id(0) == 0)
        def _():
            cr[...] = jnp.zeros_like(cr)
            ci[...] = jnp.zeros_like(ci)

        lr = lr_ref[...]
        li = li_ref[...]

        def step(n, c):
            t = SSM_TS - 1 - n if down else n
            sr, si = c
            nr = lr * sr - li * si + br_ref[t]
            ni = lr * si + li * sr + bi_ref[t]
            sr_ref[t] = nr
            si_ref[t] = ni
            return nr, ni

        sr, si = lax.fori_loop(0, SSM_TS, step, (cr[...], ci[...]))
        cr[...] = sr
        ci[...] = si

    bs = pl.BlockSpec((SSM_TS, SSM_SL, LANE), lambda i: (tile(i), 0, 0))
    ps = pl.BlockSpec((SSM_SL, LANE), lambda i: (0, 0))
    return _pcall(body, name="ssm_scan_" + order, grid=(SSM_NTS,), in_specs=[bs, bs, ps, ps], out_specs=[bs, bs],
                  out_shape=[jax.ShapeDtypeStruct((T, SSM_SL, LANE), F32)] * 2,
                  scratch_shapes=[pltpu.VMEM((SSM_SL, LANE), F32)] * 2, compiler_params=_cp(("arbitrary",)))(b_re, b_im, lam_re, lam_im)


def _scan_adj(g_re, g_im, s_re, s_im, lam_re, lam_im, order):
    tile, down = _scan_order(order)

    def body(gr_ref, gi_ref, sr_ref, si_ref, lr_ref, li_ref, ar_ref, ai_ref, dlr_ref, dli_ref, cr, ci):
        @pl.when(pl.program_id(0) == 0)
        def _():
            cr[...] = jnp.zeros_like(cr)
            ci[...] = jnp.zeros_like(ci)
            dlr_ref[...] = jnp.zeros_like(dlr_ref)
            dli_ref[...] = jnp.zeros_like(dli_ref)

        lr = lr_ref[...]
        li = li_ref[...]

        def step(n, c):
            t = SSM_TS - 1 - n if down else n
            ar, ai, dr, di = c
            sr = sr_ref[t]
            si = si_ref[t]
            dr = dr + ar * sr + ai * si
            di = di + ai * sr - ar * si
            nr = gr_ref[t] + lr * ar + li * ai
            ni = gi_ref[t] + lr * ai - li * ar
            ar_ref[t] = nr
            ai_ref[t] = ni
            return nr, ni, dr, di

        ar, ai, dr, di = lax.fori_loop(0, SSM_TS, step, (cr[...], ci[...], dlr_ref[...], dli_ref[...]))
        cr[...] = ar
        ci[...] = ai
        dlr_ref[...] = dr
        dli_ref[...] = di

    bs = pl.BlockSpec((SSM_TS, SSM_SL, LANE), lambda i: (tile(i), 0, 0))
    ps = pl.BlockSpec((SSM_SL, LANE), lambda i: (0, 0))
    return _pcall(body, name="ssm_scan_" + order, grid=(SSM_NTS,), in_specs=[bs, bs, bs, bs, ps, ps], out_specs=[bs, bs, ps, ps],
                  out_shape=[jax.ShapeDtypeStruct((T, SSM_SL, LANE), F32)] * 2 + [jax.ShapeDtypeStruct((SSM_SL, LANE), F32)] * 2,
                  scratch_shapes=[pltpu.VMEM((SSM_SL, LANE), F32)] * 2,
                  compiler_params=_cp(("arbitrary",)))(g_re, g_im, s_re, s_im, lam_re, lam_im)


def _block_diag(x):
    x4 = x.reshape(8, 8, SSM_P, SSM_C)
    return jnp.einsum("jgpc,gh->jgphc", x4, jnp.eye(8, dtype=x.dtype)).reshape(8, 8 * SSM_P, 8 * SSM_C)


def _ssm_prep(lam_re, lam_im, log_dt, b_re, b_im, c_re, c_im):
    lam = lax.complex(lam_re, lam_im)
    dt = jnp.exp(log_dt)[:, None]
    lam_bar = jnp.exp(lam * dt)
    b_bar = ((lam_bar - 1.0) / lam)[..., None] * lax.complex(b_re, b_im)
    return (jnp.real(lam_bar).reshape(SSM_SL, LANE), jnp.imag(lam_bar).reshape(SSM_SL, LANE),
            _block_diag(jnp.real(b_bar)), _block_diag(jnp.imag(b_bar)),
            _block_diag(c_re.transpose(0, 2, 1)), _block_diag(-c_im.transpose(0, 2, 1)))


def _ssm_glue(h, yf, yr, d):
    return jax.nn.gelu(d * h + yf + yr)


def _glu(ga, gb):
    return ga * jax.nn.sigmoid(gb)


def _ssm_mixer_fwd(h, sp, w_a, w_b):
    lam_re, lam_im, log_dt, b_re, b_im, c_re, c_im, d_skip = sp
    ys, saved = [], []
    for di, order in enumerate(("fwd", "rev")):
        lr, li, wb_r, wb_i, wc_r, wc_i = _ssm_prep(lam_re[di], lam_im[di], log_dt[di], b_re[di], b_im[di], c_re[di], c_im[di])
        bu_r, bu_i = _proj3d(h, wb_r.transpose(0, 2, 1), wb_i.transpose(0, 2, 1))
        s_r, s_i = _scan(bu_r, bu_i, lr, li, order)
        ys.append(_readout(s_r, s_i, wc_r, wc_i))
        saved.append((s_r, s_i))
    g = _rows(lambda *a: (_ssm_glue(*a),), [h, ys[0], ys[1]], [d_skip], [(D, _MXU)], [], name="ssm_glue")[0]
    ga = _mm(g, w_a, name="ssm_glu_a")
    gb = _mm(g, w_b, name="ssm_glu_b")
    y = _rows(lambda *a: (_glu(*a),), [ga, gb], [], [(D, F32)], [], name="ssm_glu")[0]
    return y, (ys, saved, g, ga, gb)


def _ssm_mixer_bwd(dy, h, saved_all, sp, w_a, w_b):
    lam_re, lam_im, log_dt, b_re, b_im, c_re, c_im, d_skip = sp
    ys, saved, g, ga, gb = saved_all

    def glu_bwd(ga, gb, dy):
        _, vjp = jax.vjp(_glu, ga, gb)
        return vjp(dy)

    dga, dgb = _rows(glu_bwd, [ga, gb, dy], [], [(D, _MXU), (D, _MXU)], [], name="ssm_glu_bwd")
    dw_a = _mm(g, dga, ta=True, name="ssm_glu_a_dw")
    dw_b = _mm(g, dgb, ta=True, name="ssm_glu_b_dw")
    dg_a = _mm(dga, w_a, tb=True, name="ssm_glu_a_dx")
    dg_b = _mm(dgb, w_b, tb=True, name="ssm_glu_b_dx")

    def glue_bwd(h, yf, yr, dg_a, dg_b, d):
        _, vjp = jax.vjp(_ssm_glue, h, yf, yr, d)
        dh, dyl, _, dd = vjp(dg_a + dg_b)
        return dh, dyl, dd

    dh0, dyl, dd = _rows(glue_bwd, [h, ys[0], ys[1], dg_a, dg_b], [d_skip], [(D, F32), (D, F32)], [(1, (1, D))], name="ssm_glue_bwd")
    dhs = [dh0]
    dparams = []
    for di, (order, adj) in enumerate((("fwd", "fwd_adj"), ("rev", "rev_adj"))):
        args = (lam_re[di], lam_im[di], log_dt[di], b_re[di], b_im[di], c_re[di], c_im[di])
        (lr, li, wb_r, wb_i, wc_r, wc_i), prep_vjp = jax.vjp(_ssm_prep, *args)
        s_r, s_i = saved[di]
        dwc_r, dwc_i = _outer3d(s_r, s_i, dyl)
        g_r, g_i = _proj3d(dyl, wc_r.transpose(0, 2, 1), wc_i.transpose(0, 2, 1))
        a_r, a_i, dlr, dli = _scan_adj(g_r, g_i, s_r, s_i, lr, li, adj)
        dwb_r, dwb_i = _outer3d(a_r, a_i, h)
        dhs.append(_readout(a_r, a_i, wb_r, wb_i))
        dparams.append(prep_vjp((dlr, dli) + tuple(d.transpose(0, 2, 1) for d in (dwb_r, dwb_i, dwc_r, dwc_i))))
    dsp = [jnp.stack([dparams[0][k], dparams[1][k]], axis=0) for k in range(7)]
    return dhs, dsp, dd, dw_a, dw_b


NCH = T // GM_CHUNK


def _gm_specs():
    full = lambda *s: pl.BlockSpec(s, lambda i: (0,) * len(s))
    zu = pl.BlockSpec((GM_CHUNK, GM_HALF), lambda i: (i, 0))
    zv = pl.BlockSpec((GM_CHUNK, GM_HALF), lambda i: (i, 1))
    pars = [full(1, GM_HALF), pl.BlockSpec((1, GM_HALF), lambda i: (0, 1)), full(1, GM_HALF), full(1, GM_HALF),
            full(GM_HEADS, GM_CHUNK, GM_CHUNK), full(GM_HEADS, GM_CHUNK, 1)]
    return zu, zv, pars, full


def _gm_forward(zu_ref, zv_ref, bu_ref, bv_ref, g_ref, b_ref, ws_ref, bs_ref):
    u = jax.nn.gelu(zu_ref[...] + bu_ref[...])
    zv = jax.nn.gelu(zv_ref[...] + bv_ref[...])
    mu = jnp.mean(zv, axis=-1, keepdims=True)
    zc = zv - mu
    rstd = lax.rsqrt(jnp.mean(jnp.square(zc), axis=-1, keepdims=True) + LN_EPS)
    vhat = zc * rstd
    v = (vhat * g_ref[...] + b_ref[...]).astype(_MXU)
    gates = [jnp.dot(ws_ref[hd].astype(_MXU), v[:, GM_HD * hd:GM_HD * (hd + 1)], preferred_element_type=F32) + bs_ref[hd]
             for hd in range(GM_HEADS)]
    return u, vhat, rstd, v, jnp.concatenate(gates, axis=1)


def _gmlp_chunk(zp, b_in, ln_g, ln_b, w_s, b_s):
    def body(zu_ref, zv_ref, bu_ref, bv_ref, g_ref, b_ref, ws_ref, bs_ref, o_ref):
        u, _, _, _, gate = _gm_forward(zu_ref, zv_ref, bu_ref, bv_ref, g_ref, b_ref, ws_ref, bs_ref)
        o_ref[...] = (u * gate).astype(o_ref.dtype)

    zu, zv, pars, _ = _gm_specs()
    return _pcall(body, name="gmlp_chunk", grid=(NCH,), in_specs=[zu, zv, *pars], out_specs=zu,
                  out_shape=jax.ShapeDtypeStruct((T, GM_HALF), _MXU),
                  compiler_params=_cp(("parallel",)))(zp, zp, b_in, b_in, ln_g, ln_b, w_s, b_s)


def _gmlp_chunk_bwd(zp, do, b_in, ln_g, ln_b, w_s, b_s):
    def body(zu_ref, zv_ref, do_ref, bu_ref, bv_ref, g_ref, b_ref, ws_ref, bs_ref,
             dzu_ref, dzv_ref, dbu_ref, dbv_ref, dg_ref, db_ref, dws_ref, dbs_ref):
        i = pl.program_id(0)

        def acc(ref, val, idx=None):
            @pl.when(i == 0)
            def _():
                if idx is None:
                    ref[...] = val
                else:
                    ref[idx] = val

            @pl.when(i > 0)
            def _():
                if idx is None:
                    ref[...] += val
                else:
                    ref[idx] += val

        u, vhat, rstd, v, gate = _gm_forward(zu_ref, zv_ref, bu_ref, bv_ref, g_ref, b_ref, ws_ref, bs_ref)
        do = do_ref[...]
        du = do * gate
        dgate = do * u
        dvs = []
        for hd in range(GM_HEADS):
            sl = slice(GM_HD * hd, GM_HD * (hd + 1))
            dgh = dgate[:, sl]
            dghb = dgh.astype(_MXU)
            dvs.append(lax.dot_general(ws_ref[hd].astype(_MXU), dghb, (((0,), (0,)), ((), ())), preferred_element_type=F32))
            acc(dws_ref, lax.dot_general(dghb, v[:, sl], (((1,), (1,)), ((), ())), preferred_element_type=F32), hd)
            acc(dbs_ref, jnp.sum(dgh, axis=1, keepdims=True), hd)
        dv = jnp.concatenate(dvs, axis=1)
        acc(dg_ref, jnp.sum(dv * vhat, axis=0, keepdims=True))
        acc(db_ref, jnp.sum(dv, axis=0, keepdims=True))
        dvh = dv * g_ref[...]
        dzv = rstd * (dvh - jnp.mean(dvh, axis=-1, keepdims=True) - vhat * jnp.mean(dvh * vhat, axis=-1, keepdims=True))
        dpu = jax.vjp(jax.nn.gelu, zu_ref[...] + bu_ref[...])[1](du)[0]
        dpv = jax.vjp(jax.nn.gelu, zv_ref[...] + bv_ref[...])[1](dzv)[0]
        dzu_ref[...] = dpu.astype(dzu_ref.dtype)
        dzv_ref[...] = dpv.astype(dzv_ref.dtype)
        acc(dbu_ref, jnp.sum(dpu, axis=0, keepdims=True))
        acc(dbv_ref, jnp.sum(dpv, axis=0, keepdims=True))

    zu, zv, pars, full = _gm_specs()
    out_specs = [zu, zu, full(1, GM_HALF), full(1, GM_HALF), full(1, GM_HALF), full(1, GM_HALF),
                 full(GM_HEADS, GM_CHUNK, GM_CHUNK), full(GM_HEADS, GM_CHUNK, 1)]
    out_shape = [jax.ShapeDtypeStruct((T, GM_HALF), _MXU)] * 2 + [jax.ShapeDtypeStruct((1, GM_HALF), F32)] * 4 + \
                [jax.ShapeDtypeStruct((GM_HEADS, GM_CHUNK, GM_CHUNK), F32), jax.ShapeDtypeStruct((GM_HEADS, GM_CHUNK, 1), F32)]
    dzu, dzv, dbu, dbv, dg, db, dws, dbs = _pcall(
        body, name="gmlp_chunk_bwd", grid=(NCH,), in_specs=[zu, zv, zu, *pars], out_specs=out_specs, out_shape=out_shape,
        compiler_params=_cp(("arbitrary",)))(zp, zp, do, b_in, b_in, ln_g, ln_b, w_s, b_s)
    return jnp.concatenate([dzu, dzv], axis=1), jnp.concatenate([dbu, dbv], axis=1), dg, db, dws, dbs


def _gmlp_mixer_fwd(h, w_in, b_in, ln_g, ln_b, w_s, b_s, w_out):
    zp = _mm(h, w_in, name="gmlp_in")
    ug = _gmlp_chunk(zp, b_in, ln_g, ln_b, w_s, b_s[..., None])
    return _mm(ug, w_out, name="gmlp_out"), (zp, ug)


def _gmlp_mixer_bwd(dy, h, saved, w_in, b_in, ln_g, ln_b, w_s, b_s, w_out):
    zp, ug = saved
    dw_out = _mm(ug, dy, ta=True, name="gmlp_out_dw")
    do = _mm(dy, w_out, tb=True, name="gmlp_out_dx")
    dzp, db_in, dg, db, dws, dbs = _gmlp_chunk_bwd(zp, do, b_in, ln_g, ln_b, w_s, b_s[..., None])
    dw_in = _mm(h, dzp, ta=True, name="gmlp_in_dw")
    dh = _mm(dzp, w_in, tb=True, name="gmlp_in_dx")
    return dh, dw_in, db_in, dg, db, dws, dbs[..., 0], dw_out


def _loss_head(x, target):
    nct = NCTX // TM

    def body(x_ref, t_ref, l_ref, dx_ref):
        i = pl.program_id(0)
        err = jnp.where(i >= nct, x_ref[...] - t_ref[...], 0.0)
        dx_ref[...] = err * (1.0 / D)
        part = 0.5 * jnp.sum(jnp.sum(jnp.square(err), axis=-1, keepdims=True) * (1.0 / D), axis=0, keepdims=True)

        @pl.when(i == 0)
        def _():
            l_ref[...] = part

        @pl.when(i > 0)
        def _():
            l_ref[...] += part

    return _pcall(body, name="loss_head", grid=(NT,),
                  in_specs=[pl.BlockSpec((TM, D), lambda i: (i, 0)), pl.BlockSpec((TM, D), lambda i: (jnp.maximum(i - nct, 0), 0))],
                  out_specs=[pl.BlockSpec((1, 1), lambda i: (0, 0)), pl.BlockSpec((TM, D), lambda i: (i, 0))],
                  out_shape=[jax.ShapeDtypeStruct((1, 1), F32), jax.ShapeDtypeStruct((T, D), F32)],
                  compiler_params=_cp(("arbitrary",)))(x, target)


def _as2d(a):
    return a.reshape(-1, a.shape[-1])


def _adamw(w, g, m, v):
    shape = w.shape
    w2, g2, m2, v2 = _as2d(w), _as2d(g), _as2d(m), _as2d(v)
    R, C = w2.shape
    tr = _tile(R, 512, SUBLANE)
    c1 = 1.0 - B1 ** STEP
    c2 = 1.0 - B2 ** STEP

    def body(w_ref, g_ref, m_ref, v_ref, d_ref, nm_ref, nv_ref):
        g = g_ref[...]
        m = B1 * m_ref[...] + (1.0 - B1) * g
        v = B2 * v_ref[...] + (1.0 - B2) * jnp.square(g)
        nm_ref[...] = m
        nv_ref[...] = v
        d_ref[...] = -LR * ((m / c1) / (jnp.sqrt(v / c2) + EPS) + WD * w_ref[...])

    spec = pl.BlockSpec((tr, C), lambda i: (i, 0))
    outs = _pcall(body, name="adamw", grid=(R // tr,), in_specs=[spec] * 4, out_specs=[spec] * 3,
                  out_shape=[jax.ShapeDtypeStruct((R, C), F32)] * 3, compiler_params=_cp(("parallel",)))(w2, g2, m2, v2)
    return tuple(o.reshape(shape) for o in outs)


def _sum_slabs(x):
    _, R, C = x.shape
    tr = _tile(R, 256, 16)

    def body(x_ref, o_ref):
        acc = x_ref[0].astype(F32)
        for k in range(1, NDEV):
            acc = acc + x_ref[k].astype(F32)
        o_ref[...] = acc

    return _pcall(body, name="sum_slabs", grid=(R // tr,), in_specs=[pl.BlockSpec((NDEV, tr, C), lambda i: (0, i, 0))],
                  out_specs=pl.BlockSpec((tr, C), lambda i: (i, 0)), out_shape=jax.ShapeDtypeStruct((R, C), F32),
                  compiler_params=_cp(("parallel",)))(x)


def _exchange(xs, broadcast, name):
    n = len(xs)

    def body(*refs):
        x_refs, o_refs = refs[:n], refs[n:2 * n]
        send_sems, recv_sems, loc_sems = refs[2 * n:]
        mx, my, mc = lax.axis_index("x"), lax.axis_index("y"), lax.axis_index("c")
        me = 4 * mx + 2 * my + mc
        pending = []
        for a in range(n):
            mine = pltpu.make_async_copy(x_refs[a] if broadcast else x_refs[a].at[me], o_refs[a].at[me], loc_sems.at[a])
            mine.start()
            pending.append(mine)
            for k in range(1, NDEV):
                px = 1 - mx if k & 4 else mx
                py = 1 - my if k & 2 else my
                pc = 1 - mc if k & 1 else mc
                cp = pltpu.make_async_remote_copy(
                    src_ref=x_refs[a] if broadcast else x_refs[a].at[4 * px + 2 * py + pc], dst_ref=o_refs[a].at[me],
                    send_sem=send_sems.at[a, k - 1], recv_sem=recv_sems.at[a, k - 1],
                    device_id=(px, py, pc), device_id_type=pl.DeviceIdType.MESH)
                cp.start()
                pending.append(cp)
        for cp in pending:
            cp.wait()

    hbm = pl.BlockSpec(memory_space=pl.ANY)
    out_shape = [jax.ShapeDtypeStruct(((NDEV,) + tuple(x.shape)) if broadcast else tuple(x.shape), x.dtype) for x in xs]
    return _pcall(body, name=name, in_specs=[hbm] * n, out_specs=[hbm] * n, out_shape=out_shape,
                  scratch_shapes=[pltpu.SemaphoreType.DMA((n, NDEV - 1)), pltpu.SemaphoreType.DMA((n, NDEV - 1)),
                                  pltpu.SemaphoreType.DMA((n,))],
                  compiler_params=pltpu.CompilerParams(has_side_effects=True))(*xs)


SLAB_W = 1024
SLAB_ROWS = 16


def _pack(parts, lead=None):
    if lead is None:
        flat = jnp.concatenate([p.reshape(-1) for p in parts])
        n = flat.shape[0]
        padn = -n % (SLAB_ROWS * SLAB_W)
        return jnp.pad(flat, (0, padn)).reshape(-1, SLAB_W)
    flat = jnp.concatenate([p.reshape(lead, -1) for p in parts], axis=1)
    n = flat.shape[1]
    padn = -n % (SLAB_ROWS * SLAB_W)
    return jnp.pad(flat, ((0, 0), (0, padn))).reshape(lead, -1, SLAB_W)


def _unpack(buf, shapes, lead=None):
    out, off = [], 0
    flat = buf.reshape(-1) if lead is None else buf.reshape(lead, -1)
    for s in shapes:
        n = int(np.prod(s))
        if lead is None:
            out.append(flat[off:off + n].reshape(s))
        else:
            out.append(flat[:, off:off + n].reshape((lead,) + tuple(s)))
        off += n
    return out


def _gathered(blk, ax):
    m = jnp.moveaxis(blk, 0, ax)
    s = list(m.shape)
    return m.reshape(s[:ax] + [s[ax] * s[ax + 1]] + s[ax + 2:])


def _scattered(full, ax):
    s = list(full.shape)
    m = full.reshape(s[:ax] + [NDEV, s[ax] // NDEV] + s[ax + 1:])
    return jnp.moveaxis(m, ax, 0)


_MM_SHARDED = (("ffn_w_up", 2), ("ffn_w_down", 1), ("pool_w", 2), ("attn_w_qkv", 2), ("attn_w_o", 1),
               ("ssm_w_glu_a", 1), ("ssm_w_glu_b", 1), ("gmlp_w_in", 2), ("gmlp_w_out", 1))
_VEC_SHARDED = (("ffn_conv_w", 2), ("ssm_d", 1), ("gmlp_b_in", 1), ("gmlp_ln_g", 1), ("gmlp_ln_b", 1))
_REPLICATED = ("ln1_g", "ln1_b", "ln2_g", "ln2_b", "ffn_conv_b", "pool_b", "pool_scale", "attn_sink",
               "ssm_lambda_re", "ssm_lambda_im", "ssm_log_dt", "ssm_b_re", "ssm_b_im", "ssm_c_re", "ssm_c_im",
               "gmlp_w_s", "gmlp_b_s")
_WEIGHTS = ("c_ctx", "ada_w", "ada_b", "ln1_g", "ln1_b", "ln2_g", "ln2_b", "ffn_w_up", "ffn_conv_w", "ffn_conv_b", "ffn_w_down",
            "pool_w", "pool_b", "pool_scale", "attn_w_qkv", "attn_w_o", "attn_sink", "ssm_lambda_re", "ssm_lambda_im",
            "ssm_log_dt", "ssm_b_re", "ssm_b_im", "ssm_c_re", "ssm_c_im", "ssm_d", "ssm_w_glu_a", "ssm_w_glu_b",
            "gmlp_w_in", "gmlp_b_in", "gmlp_ln_g", "gmlp_ln_b", "gmlp_w_s", "gmlp_b_s", "gmlp_w_out")
N_MODS = 6
ADA_COLS = N_MODS * D // NDEV
PAD_ROWS = 16


def _silu_rows(x):
    return _rows(lambda v: (jax.nn.silu(v),), [x], [], [(x.shape[1], F32)], [], name="silu", tm=x.shape[0])[0]


def _step(x, c, ctx, loss_target, w, m, v):
    mx, my, mc = lax.axis_index("x"), lax.axis_index("y"), lax.axis_index("c")
    me = 4 * mx + 2 * my + mc

    vec_buf = _pack([w[n] for n, _ in _VEC_SHARDED] + [c])
    *mm_parts, vec_all = _exchange([w[n].astype(_MXU) for n, _ in _MM_SHARDED] + [vec_buf], True, "gather_weights")
    vec_parts = _unpack(vec_all, [w[n].shape for n, _ in _VEC_SHARDED] + [c.shape], lead=NDEV)
    full = {n: _gathered(p, ax) for (n, ax), p in zip(_MM_SHARDED, mm_parts)}
    full.update({n: _gathered(p, ax) for (n, ax), p in zip(_VEC_SHARDED, vec_parts[:-1])})
    c_all = vec_parts[-1].reshape(NDEV, D)

    cc = jnp.concatenate([c_all, w["c_ctx"].reshape(1, D), jnp.zeros((PAD_ROWS - NDEV - 1, D), F32)], axis=0)
    silu_cc = _silu_rows(cc)
    ada_b_mine = lax.dynamic_slice(w["ada_b"], (0, me * ADA_COLS), (DEPTH, ADA_COLS))
    mods_mine = jnp.stack([_mm(silu_cc, w["ada_w"][l], name="ada_mods") + ada_b_mine[l][None, :] for l in range(DEPTH)], axis=1)
    per_dev = mods_mine[:NDEV].reshape(NDEV, DEPTH * ADA_COLS)
    cm = jnp.broadcast_to(mods_mine[NDEV].reshape(1, DEPTH * ADA_COLS), (NDEV, DEPTH * ADA_COLS))
    mods_all = _exchange([_pack([per_dev, cm], lead=NDEV)], False, "scatter_mods")[0]
    got = _unpack(mods_all, [(DEPTH, ADA_COLS), (DEPTH, ADA_COLS)], lead=NDEV)
    mods = got[0].transpose(1, 0, 2).reshape(DEPTH, N_MODS * D)
    cmods = got[1].transpose(1, 0, 2).reshape(DEPTH, N_MODS * D)
    P = [[jnp.stack([cmods[l, k * D:(k + 1) * D], mods[l, k * D:(k + 1) * D]]).reshape(2, 1, D) for k in range(N_MODS)]
         for l in range(DEPTH)]
    row = lambda a, l: a[l].reshape(1, 1, D)

    xs = jnp.concatenate([ctx[0], x[0]], axis=0)
    sp = tuple(w[n][0] for n in ("ssm_lambda_re", "ssm_lambda_im", "ssm_log_dt", "ssm_b_re", "ssm_b_im", "ssm_c_re", "ssm_c_im")) + \
        (full["ssm_d"].reshape(1, 1, D),)
    gm = (full["gmlp_w_in"][0], full["gmlp_b_in"], full["gmlp_ln_g"], full["gmlp_ln_b"], w["gmlp_w_s"][0], w["gmlp_b_s"][0],
          full["gmlp_w_out"][0])
    pool_args = (full["pool_w"][0], w["pool_b"], w["pool_scale"])
    saved = []
    for l in range(DEPTH):
        sh1, sc1, gt1, sh2, sc2, gt2 = P[l]
        h1 = _pre_mixer(xs, sh1, sc1, _MXU if l in (1, 3) else F32)
        if l == 0:
            y, ms = _pool_fwd(h1, *pool_args), None
        elif l == 1:
            y, ms = _attn_mixer_fwd(h1, full["attn_w_qkv"][0], full["attn_w_o"][0], w["attn_sink"])
        elif l == 2:
            y, ms = _ssm_mixer_fwd(h1, sp, full["ssm_w_glu_a"][0], full["ssm_w_glu_b"][0])
        else:
            y, ms = _gmlp_mixer_fwd(h1, *gm)
        x1, h2 = _post_mixer(xs, y, gt1, row(w["ln1_g"], l), row(w["ln1_b"], l), sh2, sc2)
        f, (a, u) = _ffn_fwd(h2, full["ffn_w_up"][l], full["ffn_conv_w"][l], w["ffn_conv_b"][l][None, :], full["ffn_w_down"][l])
        x2 = _post_ffn(x1, f, gt2, row(w["ln2_g"], l), row(w["ln2_b"], l))
        saved.append((xs, h1, y, ms, x1, h2, a, u, f))
        xs = x2
    loss, dxs = _loss_head(xs, loss_target[0])

    g = {n: [None] * DEPTH for n in ("ln1_g", "ln1_b", "ln2_g", "ln2_b", "ffn_w_up", "ffn_conv_w", "ffn_conv_b", "ffn_w_down")}
    dP = [None] * DEPTH
    for l in reversed(range(DEPTH)):
        sh1, sc1, gt1, sh2, sc2, gt2 = P[l]
        x0, h1, y, ms, x1, h2, a, u, f = saved[l]
        dx1, df, dgt2, g["ln2_g"][l], g["ln2_b"][l] = _post_ffn_bwd(x1, f, dxs, gt2, row(w["ln2_g"], l), row(w["ln2_b"], l))
        dh2, g["ffn_w_up"][l], g["ffn_conv_w"][l], g["ffn_conv_b"][l], g["ffn_w_down"][l] = _ffn_bwd(
            df, h2, a, u, full["ffn_w_up"][l], full["ffn_conv_w"][l], w["ffn_conv_b"][l][None, :], full["ffn_w_down"][l])
        dx0, dy, dgt1, g["ln1_g"][l], g["ln1_b"][l], dsh2, dsc2 = _post_mixer_bwd(
            x0, y, dx1, dh2, gt1, row(w["ln1_g"], l), row(w["ln1_b"], l), sh2, sc2)
        if l == 0:
            dh, g["pool_w"], g["pool_b"], g["pool_scale"] = _pool_bwd(h1, dy, *pool_args)
            dhs = [dh]
        elif l == 1:
            dh, g["attn_w_qkv"], g["attn_w_o"], g["attn_sink"] = _attn_mixer_bwd(dy, h1, ms, full["attn_w_qkv"][0], full["attn_w_o"][0])
            dhs = [dh]
        elif l == 2:
            dhs, dsp, dd, g["ssm_w_glu_a"], g["ssm_w_glu_b"] = _ssm_mixer_bwd(dy, h1, ms, sp, full["ssm_w_glu_a"][0], full["ssm_w_glu_b"][0])
            for n, d_ in zip(("ssm_lambda_re", "ssm_lambda_im", "ssm_log_dt", "ssm_b_re", "ssm_b_im", "ssm_c_re", "ssm_c_im"), dsp):
                g[n] = d_
            g["ssm_d"] = dd.reshape(1, D)
        else:
            (dh, g["gmlp_w_in"], g["gmlp_b_in"], g["gmlp_ln_g"], g["gmlp_ln_b"], g["gmlp_w_s"], g["gmlp_b_s"],
             g["gmlp_w_out"]) = _gmlp_mixer_bwd(dy, h1, ms, *gm)
            dhs = [dh]
        dxs, dsh1, dsc1 = _pre_mixer_bwd(x0, dhs, dx0, sh1, sc1)
        dP[l] = (dsh1, dsc1, dgt1, dsh2, dsc2, dgt2)
    grad_x = dxs[NCTX:][None]
    dmods = jnp.stack([jnp.concatenate([p[1, 0] for p in dP[l]]) for l in range(DEPTH)])
    dcmods = jnp.stack([jnp.concatenate([p[0, 0] for p in dP[l]]) for l in range(DEPTH)])

    gfull = {n: (jnp.stack(g[n]) if isinstance(g[n], list) else g[n]) for n in g}
    sharded_names = [n for n, _ in _MM_SHARDED] + [n for n, _ in _VEC_SHARDED]
    sharded_axes = dict(_MM_SHARDED + _VEC_SHARDED)

    def as_param(n, a):
        shard = w[n].shape
        ax = sharded_axes.get(n)
        fs = tuple(s * NDEV if i == ax else s for i, s in enumerate(shard))
        return a.reshape(fs)

    rep = jnp.concatenate([as_param(n, gfull[n]).reshape(-1) for n in _REPLICATED])
    n_rep = rep.shape[0]
    rep = jnp.pad(rep, (0, -n_rep % (NDEV * SLAB_W))).reshape(NDEV, -1)
    by_dev = lambda a: a.reshape(DEPTH, NDEV, ADA_COLS).transpose(1, 0, 2)
    big = [_scattered(as_param(n, gfull[n]), ax).astype(_MXU) for n, ax in _MM_SHARDED]
    parts = [_scattered(as_param(n, gfull[n]), ax) for n, ax in _VEC_SHARDED] + [rep, by_dev(dmods), by_dev(dcmods)]
    *big_in, grads_in = _exchange(big + [_pack(parts, lead=NDEV)], False, "scatter_grads")
    grads = {n: _sum_slabs(b.reshape(NDEV, -1, b.shape[-1])).reshape(w[n].shape) for (n, _), b in zip(_MM_SHARDED, big_in)}
    shapes = [w[n].shape for n, _ in _VEC_SHARDED] + [(rep.shape[1],), (DEPTH, ADA_COLS), (DEPTH, ADA_COLS)]
    red = _unpack(_sum_slabs(grads_in), shapes)
    grads.update({n: r for (n, _), r in zip(_VEC_SHARDED, red)})
    rep_mine, dcm = red[-3], red[-1]
    dm_all = _unpack(grads_in, shapes, lead=NDEV)[-2]

    e_rows = jnp.concatenate([dm_all, dcm[None], jnp.zeros((PAD_ROWS - NDEV - 1, DEPTH, ADA_COLS), F32)], axis=0)
    grads["ada_w"] = jnp.stack([_mm(silu_cc, e_rows[:, l], ta=True, name="ada_dw") for l in range(DEPTH)])
    ada_b_blk = jnp.sum(e_rows, axis=0)
    dcm_rows = jnp.concatenate([dcm[None], jnp.zeros((PAD_ROWS - 1, DEPTH, ADA_COLS), F32)], axis=0)
    cpart = sum(_mm(dcm_rows[:, l], w["ada_w"][l], tb=True, name="ada_dc")[0] for l in range(DEPTH))

    small_all = _exchange([_pack([rep_mine, ada_b_blk, cpart])], True, "gather_small")[0]
    sm = _unpack(small_all, [rep_mine.shape, (DEPTH, ADA_COLS), (D,)], lead=NDEV)
    rep_full = sm[0].reshape(-1)[:n_rep]
    off = 0
    for n in _REPLICATED:
        k = int(np.prod(w[n].shape))
        grads[n] = rep_full[off:off + k].reshape(w[n].shape)
        off += k
    grads["ada_b"] = sm[1].transpose(1, 0, 2).reshape(DEPTH, N_MODS * D)
    csum = _unpack(_sum_slabs(small_all), [rep_mine.shape, (DEPTH, ADA_COLS), (D,)])[2]

    def dsilu(vv, dd):
        return (jax.vjp(jax.nn.silu, vv)[1](dd)[0],)

    grads["c_ctx"] = _rows(dsilu, [jnp.broadcast_to(w["c_ctx"][None], (SUBLANE, D)), jnp.broadcast_to(csum[None], (SUBLANE, D))],
                           [], [(D, F32)], [], name="dsilu", tm=SUBLANE)[0][0]

    delta, new_m, new_v = {}, {}, {}
    for n in _WEIGHTS:
        delta[n], new_m[n], new_v[n] = _adamw(w[n], grads[n], m[n], v[n])
    loss = lax.psum(loss[0, 0], ("x", "y", "c"))
    return loss, grad_x, grads, delta, new_m, new_v


def kernel(x, c, ctx, c_ctx, ada_w, ada_b, ln1_g, ln1_b, ln2_g, ln2_b, ffn_w_up, ffn_conv_w, ffn_conv_b, ffn_w_down, pool_w, pool_b, pool_scale, attn_w_qkv, attn_w_o, attn_sink, ssm_lambda_re, ssm_lambda_im, ssm_log_dt, ssm_b_re, ssm_b_im, ssm_c_re, ssm_c_im, ssm_d, ssm_w_glu_a, ssm_w_glu_b, gmlp_w_in, gmlp_b_in, gmlp_ln_g, gmlp_ln_b, gmlp_w_s, gmlp_b_s, gmlp_w_out, loss_target, m_c_ctx, m_ada_w, m_ada_b, m_ln1_g, m_ln1_b, m_ln2_g, m_ln2_b, m_ffn_w_up, m_ffn_conv_w, m_ffn_conv_b, m_ffn_w_down, m_pool_w, m_pool_b, m_pool_scale, m_attn_w_qkv, m_attn_w_o, m_attn_sink, m_ssm_lambda_re, m_ssm_lambda_im, m_ssm_log_dt, m_ssm_b_re, m_ssm_b_im, m_ssm_c_re, m_ssm_c_im, m_ssm_d, m_ssm_w_glu_a, m_ssm_w_glu_b, m_gmlp_w_in, m_gmlp_b_in, m_gmlp_ln_g, m_gmlp_ln_b, m_gmlp_w_s, m_gmlp_b_s, m_gmlp_w_out, v_c_ctx, v_ada_w, v_ada_b, v_ln1_g, v_ln1_b, v_ln2_g, v_ln2_b, v_ffn_w_up, v_ffn_conv_w, v_ffn_conv_b, v_ffn_w_down, v_pool_w, v_pool_b, v_pool_scale, v_attn_w_qkv, v_attn_w_o, v_attn_sink, v_ssm_lambda_re, v_ssm_lambda_im, v_ssm_log_dt, v_ssm_b_re, v_ssm_b_im, v_ssm_c_re, v_ssm_c_im, v_ssm_d, v_ssm_w_glu_a, v_ssm_w_glu_b, v_gmlp_w_in, v_gmlp_b_in, v_gmlp_ln_g, v_gmlp_ln_b, v_gmlp_w_s, v_gmlp_b_s, v_gmlp_w_out):
    args = dict(locals())
    w = {n: args[n] for n in _WEIGHTS}
    m = {n: args["m_" + n] for n in _WEIGHTS}
    v = {n: args["v_" + n] for n in _WEIGHTS}
    loss, grad_x, grads, delta, new_m, new_v = _step(x, c, ctx, loss_target, w, m, v)
    return (loss, grad_x, *[grads[n] for n in _WEIGHTS], *[delta[n] for n in _WEIGHTS],
            *[new_m[n] for n in _WEIGHTS], *[new_v[n] for n in _WEIGHTS])
```

```python
import functools
import math

import jax
import jax.numpy as jnp
import numpy as np
from jax import lax
from jax.experimental import pallas as pl
from jax.experimental.pallas import tpu as pltpu

D = 1024
SEQ = 4096
NCTX = 256
T = NCTX + SEQ
DEPTH = 4
NDEV = 8
GRID_W = 64
ALPHA = (2.0 * DEPTH) ** 0.25
LN_EPS = 1e-5
FFN_H = 2816
HEAD_DIM = 64
NQH, NKVH, GQA = 16, 4, 4
WINDOW = 128
ABLK = 128
NEG_INF = -1e30
ROPE_BASE = 10000.0
POOL_WINDOWS = (2, 4, 8, 16)
SSM_G, SSM_P, SSM_C = 64, 64, 16
GM_HALF = 2048
GM_HEADS = 8
GM_HD = GM_HALF // GM_HEADS
GM_CHUNK = 128
B1, B2, LR, EPS, WD, STEP = 0.9, 0.999, 0.001, 1e-8, 0.01, 10

LANE = 128
SUBLANE = 8
VMEM_LIMIT = 56 * 1024 * 1024
TM = 256
NT = T // TM

_MXU = jnp.bfloat16
F32 = jnp.float32


def _pcall(body, **kw):
    return pl.pallas_call(body, **kw)


def _cp(sem):
    return pltpu.CompilerParams(dimension_semantics=sem, vmem_limit_bytes=VMEM_LIMIT)


def _tile(dim, pref, align):
    best = None
    for t in range(align, min(dim, pref) + 1, align):
        if dim % t == 0:
            best = t
    return dim if best is None else best


def _mm(a, b, *, ta=False, tb=False, out_dtype=F32, name):
    if ta:
        a = a.astype(_MXU).T
    M, K = a.shape
    if tb:
        N, K2 = b.shape
    else:
        K2, N = b.shape
    assert K == K2, (a.shape, b.shape, ta, tb)
    tm = _tile(M, 2304, 16)
    tn = _tile(N, 512, LANE)
    tk = _tile(K, 2304, LANE)
    nk = K // tk
    dims = (((1,), (1,) if tb else (0,)), ((), ()))

    def body(a_ref, b_ref, o_ref, acc_ref):
        k = pl.program_id(2)
        r = lax.dot_general(a_ref[...].astype(_MXU), b_ref[...].astype(_MXU), dims, preferred_element_type=F32)

        @pl.when(k == 0)
        def _():
            acc_ref[...] = r

        @pl.when(k > 0)
        def _():
            acc_ref[...] += r

        @pl.when(k == nk - 1)
        def _():
            o_ref[...] = acc_ref[...].astype(o_ref.dtype)

    a_spec = pl.BlockSpec((tm, tk), lambda j, i, k: (i, k))
    b_spec = pl.BlockSpec((tn, tk), lambda j, i, k: (j, k)) if tb else pl.BlockSpec((tk, tn), lambda j, i, k: (k, j))
    return _pcall(
        body, name=name, grid=(N // tn, M // tm, nk), in_specs=[a_spec, b_spec],
        out_specs=pl.BlockSpec((tm, tn), lambda j, i, k: (i, j)),
        out_shape=jax.ShapeDtypeStruct((M, N), out_dtype),
        scratch_shapes=[pltpu.VMEM((tm, tn), F32)],
        compiler_params=_cp(("parallel", "parallel", "arbitrary")),
    )(a, b)


def _rows(fn, rows, pars, out_rows, out_pars, *, name, tm=TM):
    R = rows[0].shape[0]
    nt = R // tm
    nct = NCTX // tm
    n_r, n_p, n_or, n_op = len(rows), len(pars), len(out_rows), len(out_pars)

    def sel(S):
        if S == 1:
            return lambda i: 0
        return lambda i: jnp.where(i < nct, 0, 1)

    def body(*refs):
        r_in = refs[:n_r]
        p_in = refs[n_r:n_r + n_p]
        r_out = refs[n_r + n_p:n_r + n_p + n_or]
        p_out = refs[n_r + n_p + n_or:]
        i = pl.program_id(0)
        vals = [r[...].astype(F32) for r in r_in] + [p[0] for p in p_in]
        outs = fn(*vals)
        for r, v in zip(r_out, outs[:n_or]):
            r[...] = v.astype(r.dtype)
        for (S, _), r, v in zip(out_pars, p_out, outs[n_or:]):
            first = (i == 0) if S == 1 else jnp.logical_or(i == 0, i == nct)

            @pl.when(first)
            def _():
                r[0] = v

            @pl.when(jnp.logical_not(first))
            def _():
                r[0] += v

    def pspec(shape):
        S = shape[0]
        rest = tuple(shape[1:])
        s = sel(S)
        return pl.BlockSpec((1,) + rest, lambda i: (s(i),) + (0,) * len(rest))

    in_specs = [pl.BlockSpec((tm, r.shape[1]), lambda i: (i, 0)) for r in rows] + [pspec(p.shape) for p in pars]
    out_specs = [pl.BlockSpec((tm, w), lambda i: (i, 0)) for w, _ in out_rows] + [pspec((S,) + tuple(sh)) for S, sh in out_pars]
    out_shape = [jax.ShapeDtypeStruct((R, w), dt) for w, dt in out_rows] + \
                [jax.ShapeDtypeStruct((S,) + tuple(sh), F32) for S, sh in out_pars]
    res = _pcall(body, name=name, grid=(nt,), in_specs=in_specs, out_specs=out_specs, out_shape=out_shape,
                 compiler_params=_cp(("arbitrary",)))(*rows, *pars)
    return res


def _ln(z, g, b):
    mu = jnp.mean(z, axis=-1, keepdims=True)
    var = jnp.mean(jnp.square(z - mu), axis=-1, keepdims=True)
    return (z - mu) * lax.rsqrt(var + LN_EPS) * g + b


def _f1(x, sh, sc):
    return x * (1.0 + sc) + sh


def _f2(x, y, gt, g, b, sh, sc):
    x1 = _ln(ALPHA * x + gt * y, g, b)
    return x1, x1 * (1.0 + sc) + sh


def _f3(x1, f, gt, g, b):
    return _ln(ALPHA * x1 + gt * f, g, b)


def _pre_mixer(x, sh, sc, dtype):
    return _rows(lambda x, sh, sc: (_f1(x, sh, sc),), [x], [sh, sc], [(D, dtype)], [], name="pre_mixer")[0]


def _pre_mixer_bwd(x, dhs, dx_prev, sh, sc):
    n = len(dhs)

    def fn(x, *rest):
        dh = rest[0]
        for t in rest[1:n]:
            dh = dh + t
        dxp, sh, sc = rest[n], rest[n + 1], rest[n + 2]
        _, vjp = jax.vjp(_f1, x, sh, sc)
        dx, dsh, dsc = vjp(dh)
        return dxp + dx, dsh, dsc

    return _rows(fn, [x, *dhs, dx_prev], [sh, sc], [(D, F32)], [(2, (1, D)), (2, (1, D))], name="pre_mixer_bwd")


def _post_mixer(x, y, gt, g, b, sh, sc):
    return _rows(_f2, [x, y], [gt, g, b, sh, sc], [(D, F32), (D, _MXU)], [], name="post_mixer")


def _post_mixer_bwd(x, y, dx1, dh2, gt, g, b, sh, sc):
    def fn(x, y, dx1, dh2a, dh2b, gt, g, b, sh, sc):
        _, vjp = jax.vjp(_f2, x, y, gt, g, b, sh, sc)
        return vjp((dx1, dh2a + dh2b))

    return _rows(fn, [x, y, dx1, *dh2], [gt, g, b, sh, sc], [(D, F32), (D, F32)],
                 [(2, (1, D)), (1, (1, D)), (1, (1, D)), (2, (1, D)), (2, (1, D))], name="post_mixer_bwd")


def _post_ffn(x1, f, gt, g, b):
    return _rows(lambda *a: (_f3(*a),), [x1, f], [gt, g, b], [(D, F32)], [], name="post_ffn")[0]


def _post_ffn_bwd(x1, f, dx2, gt, g, b):
    def fn(x1, f, dx2, gt, g, b):
        _, vjp = jax.vjp(_f3, x1, f, gt, g, b)
        return vjp(dx2)

    return _rows(fn, [x1, f, dx2], [gt, g, b], [(D, F32), (D, _MXU)],
                 [(2, (1, D)), (1, (1, D)), (1, (1, D))], name="post_ffn_bwd")


def _halo_specs(tm, w, col, active=None):
    r8 = tm // SUBLANE
    act = (lambda j, r: r) if active is None else (lambda j, r: jnp.where(active(j), r, 0))
    return [
        pl.BlockSpec((SUBLANE, w), lambda j, i: (act(j, jnp.maximum(i * r8 - 1, 0)), col(j))),
        pl.BlockSpec((tm, w), lambda j, i: (act(j, i), col(j))),
        pl.BlockSpec((SUBLANE, w), lambda j, i: (act(j, jnp.minimum((i + 1) * r8, T // SUBLANE - 1)), col(j))),
    ]


def _seg_flags(i, tm):
    nct = NCTX // tm
    first = jnp.logical_or(i == 0, i == nct)
    last = jnp.logical_or(i == nct - 1, i == T // tm - 1)
    return first, last


def _shift_rows(cur, prev8, next8, first, last):
    tm = cur.shape[0]
    rid = lax.broadcasted_iota(jnp.int32, cur.shape, 0)
    pr = jnp.where(first, 0.0, prev8[SUBLANE - 1:SUBLANE, :])
    nx = jnp.where(last, 0.0, next8[0:1, :])
    up = jnp.where(rid == 0, pr, pltpu.roll(cur, 1, 0))
    dn = jnp.where(rid == tm - 1, nx, pltpu.roll(cur, tm - 1, 0))
    return up, dn


FFN_TC = 1408
FFN_NCT = FFN_H // FFN_TC


def _conv3(cur, prev8, next8, w3, first, last):
    up, dn = _shift_rows(cur, prev8, next8, first, last)
    return up * w3[0:1] + cur * w3[1:2] + dn * w3[2:3], up, dn


def _ffn_mid(a, cw, cb):
    def body(vp, vc, vn, gp, gc, gn, cwv, cwg, cbv, cbg, o_ref):
        first, last = _seg_flags(pl.program_id(1), TM)
        val = _conv3(vc[...], vp[...], vn[...], cwv[...], first, last)[0] + cbv[...]
        gate = _conv3(gc[...], gp[...], gn[...], cwg[...], first, last)[0] + cbg[...]
        o_ref[...] = (val * jax.nn.silu(gate)).astype(o_ref.dtype)

    specs = _halo_specs(TM, FFN_TC, lambda j: j) + _halo_specs(TM, FFN_TC, lambda j: j + FFN_NCT)
    specs += [pl.BlockSpec((3, FFN_TC), lambda j, i: (0, j)), pl.BlockSpec((3, FFN_TC), lambda j, i: (0, j + FFN_NCT)),
              pl.BlockSpec((1, FFN_TC), lambda j, i: (0, j)), pl.BlockSpec((1, FFN_TC), lambda j, i: (0, j + FFN_NCT))]
    return _pcall(body, name="ffn_mid", grid=(FFN_NCT, NT), in_specs=specs,
                  out_specs=pl.BlockSpec((TM, FFN_TC), lambda j, i: (i, j)),
                  out_shape=jax.ShapeDtypeStruct((T, FFN_H), _MXU),
                  compiler_params=_cp(("parallel", "arbitrary")))(a, a, a, a, a, a, cw, cw, cb, cb)


def _ext_rows(p_ref, c_ref, n_ref, first, last):
    p8 = jnp.where(first, 0.0, p_ref[...])
    n8 = jnp.where(last, 0.0, n_ref[...])
    return jnp.concatenate([p8, c_ref[...], n8], axis=0)


def _ffn_mid_bwd(a, du, cw, cb):
    E = TM + 2 * SUBLANE
    ctr = slice(SUBLANE, SUBLANE + TM)

    def body(vp, vc, vn, gp, gc, gn, dp, dc, dn_, cwv, cwg, cbv, cbg, dav_ref, dag_ref, dcwv, dcwg, dcbv, dcbg):
        i = pl.program_id(1)
        first, last = _seg_flags(i, TM)
        ev = _ext_rows(vp, vc, vn, first, last)
        eg = _ext_rows(gp, gc, gn, first, last)
        edu = _ext_rows(dp, dc, dn_, first, last)
        wv, wg = cwv[...], cwg[...]
        vup, vdn = pltpu.roll(ev, 1, 0), pltpu.roll(ev, E - 1, 0)
        gup, gdn = pltpu.roll(eg, 1, 0), pltpu.roll(eg, E - 1, 0)
        val = vup * wv[0:1] + ev * wv[1:2] + vdn * wv[2:3] + cbv[...]
        gate = gup * wg[0:1] + eg * wg[1:2] + gdn * wg[2:3] + cbg[...]
        sg = jax.nn.sigmoid(gate)
        dval = edu * (gate * sg)
        dgate = edu * val * (sg * (1.0 + gate * (1.0 - sg)))

        def conv_t(d, w3):
            return (pltpu.roll(d, E - 1, 0) * w3[0:1] + d * w3[1:2] + pltpu.roll(d, 1, 0) * w3[2:3])[ctr]

        dav_ref[...] = conv_t(dval, wv).astype(dav_ref.dtype)
        dag_ref[...] = conv_t(dgate, wg).astype(dag_ref.dtype)

        def acc(ref, v):
            @pl.when(i == 0)
            def _():
                ref[...] = v

            @pl.when(i > 0)
            def _():
                ref[...] += v

        for dref, d, up, cur, dn, bref in ((dcwv, dval[ctr], vup[ctr], ev[ctr], vdn[ctr], dcbv),
                                           (dcwg, dgate[ctr], gup[ctr], eg[ctr], gdn[ctr], dcbg)):
            acc(dref, jnp.concatenate([jnp.sum(d * up, 0, keepdims=True), jnp.sum(d * cur, 0, keepdims=True),
                                       jnp.sum(d * dn, 0, keepdims=True)], axis=0))
            acc(bref, jnp.sum(d, 0, keepdims=True))

    specs = _halo_specs(TM, FFN_TC, lambda j: j) + _halo_specs(TM, FFN_TC, lambda j: j + FFN_NCT) + _halo_specs(TM, FFN_TC, lambda j: j)
    specs += [pl.BlockSpec((3, FFN_TC), lambda j, i: (0, j)), pl.BlockSpec((3, FFN_TC), lambda j, i: (0, j + FFN_NCT)),
              pl.BlockSpec((1, FFN_TC), lambda j, i: (0, j)), pl.BlockSpec((1, FFN_TC), lambda j, i: (0, j + FFN_NCT))]
    out_specs = [pl.BlockSpec((TM, FFN_TC), lambda j, i: (i, j)), pl.BlockSpec((TM, FFN_TC), lambda j, i: (i, j)),
                 pl.BlockSpec((3, FFN_TC), lambda j, i: (0, j)), pl.BlockSpec((3, FFN_TC), lambda j, i: (0, j)),
                 pl.BlockSpec((1, FFN_TC), lambda j, i: (0, j)), pl.BlockSpec((1, FFN_TC), lambda j, i: (0, j))]
    out_shape = [jax.ShapeDtypeStruct((T, FFN_H), _MXU)] * 2 + [jax.ShapeDtypeStruct((3, FFN_H), F32)] * 2 + \
                [jax.ShapeDtypeStruct((1, FFN_H), F32)] * 2
    dav, dag, dcwv, dcwg, dcbv, dcbg = _pcall(
        body, name="ffn_mid_bwd", grid=(FFN_NCT, NT), in_specs=specs, out_specs=out_specs, out_shape=out_shape,
        compiler_params=_cp(("parallel", "arbitrary")))(a, a, a, a, a, a, du, du, du, cw, cw, cb, cb)
    return dav, dag, jnp.concatenate([dcwv, dcwg], axis=1), jnp.concatenate([dcbv, dcbg], axis=1)


def _ffn_fwd(h2, w_up, cw, cb, w_down):
    a = _mm(h2, w_up, name="ffn_up")
    u = _ffn_mid(a, cw, cb)
    f = _mm(u, w_down, name="ffn_down")
    return f, (a, u)


def _ffn_bwd(df, h2, a, u, w_up, cw, cb, w_down):
    dw_down = _mm(u, df, ta=True, name="ffn_down_dw")
    du = _mm(df, w_down, tb=True, name="ffn_down_dx")
    dav, dag, dcw, dcb = _ffn_mid_bwd(a, du, cw, cb)
    dw_up = jnp.concatenate([_mm(h2, dav, ta=True, name="ffn_up_dw"), _mm(h2, dag, ta=True, name="ffn_up_dw")], axis=1)
    dh2 = [_mm(dav, w_up[:, :FFN_H], tb=True, name="ffn_up_dx"), _mm(dag, w_up[:, FFN_H:], tb=True, name="ffn_up_dx")]
    return dh2, dw_up, dcw, dcb, dw_down


def _winsum(e, lo, hi):
    n = e.shape[0]
    acc = None
    for o in range(lo, hi + 1):
        t = e if o == 0 else pltpu.roll(e, (-o) % n, 0)
        acc = t if acc is None else acc + t
    return acc


def _pool_cnt(i, w, rows, off):
    nct = NCTX // TM
    seg_len = jnp.where(i < nct, NCTX, SEQ)
    seg_tile = jnp.where(i < nct, i, i - nct)
    pos = lax.broadcasted_iota(jnp.int32, (rows, 1), 0) - off + seg_tile * TM
    lo = jnp.clip(pos - w // 2, 0, seg_len)
    hi = jnp.clip(pos - w // 2 + w, 0, seg_len)
    return jnp.maximum(hi - lo, 1).astype(F32)


def _pool_fwd(h, pw, pb, ps):
    def body(hp, hc, hn, w_ref, b_ref, s_ref, o_ref):
        i = pl.program_id(1)
        first, last = _seg_flags(i, TM)
        e = _ext_rows(hp, hc, hn, first, last)
        outs = []
        for g, w in enumerate(POOL_WINDOWS):
            sl = slice(256 * g, 256 * (g + 1))
            eg = e[:, sl]
            mean = _winsum(eg, -(w // 2), w // 2 - 1)[SUBLANE:SUBLANE + TM] / _pool_cnt(i, w, TM, 0)
            mixed = mean - hc[:, sl]
            outs.append(jnp.dot(mixed.astype(_MXU), w_ref[g].astype(_MXU), preferred_element_type=F32))
        o_ref[...] = (jnp.concatenate(outs, axis=1) + b_ref[...]) * s_ref[...]

    full = lambda *s: pl.BlockSpec(s, lambda j, i: (0,) * len(s))
    return _pcall(body, name="pool_fwd", grid=(1, NT), in_specs=_halo_specs(TM, D, lambda j: 0) + [full(4, 256, 256), full(1, D), full(1, D)],
                  out_specs=pl.BlockSpec((TM, D), lambda j, i: (i, 0)), out_shape=jax.ShapeDtypeStruct((T, D), F32),
                  compiler_params=_cp(("parallel", "arbitrary")))(h, h, h, pw, pb, ps)


def _pool_bwd(h, dy, pw, pb, ps):
    E = TM + 2 * SUBLANE

    def body(hp, hc, hn, dp, dc, dn, w_ref, b_ref, s_ref, dh_ref, dw_ref, db_ref, ds_ref):
        i = pl.program_id(1)
        first, last = _seg_flags(i, TM)
        e = _ext_rows(hp, hc, hn, first, last)
        de = _ext_rows(dp, dc, dn, first, last)
        dys = de * s_ref[...]
        dhs, pre = [], []
        for g, w in enumerate(POOL_WINDOWS):
            sl = slice(256 * g, 256 * (g + 1))
            wg = w_ref[g].astype(_MXU)
            dyg = dys[:, sl].astype(_MXU)
            dmix = lax.dot_general(dyg, wg, (((1,), (1,)), ((), ())), preferred_element_type=F32)
            q = dmix / _pool_cnt(i, w, E, SUBLANE)
            dhs.append(_winsum(q, -(w // 2) + 1, w // 2)[SUBLANE:SUBLANE + TM] - dmix[SUBLANE:SUBLANE + TM])
            mean = _winsum(e[:, sl], -(w // 2), w // 2 - 1)[SUBLANE:SUBLANE + TM] / _pool_cnt(i, w, TM, 0)
            mixed = (mean - hc[:, sl]).astype(_MXU)
            pre.append(jnp.dot(mixed, wg, preferred_element_type=F32))
            dwg = lax.dot_general(mixed, dyg[SUBLANE:SUBLANE + TM], (((0,), (0,)), ((), ())), preferred_element_type=F32)

            @pl.when(i == 0)
            def _():
                dw_ref[g] = dwg

            @pl.when(i > 0)
            def _():
                dw_ref[g] += dwg
        dh_ref[...] = jnp.concatenate(dhs, axis=1)
        db = jnp.sum(dys[SUBLANE:SUBLANE + TM], 0, keepdims=True)
        ds = jnp.sum(dc[...] * (jnp.concatenate(pre, axis=1) + b_ref[...]), 0, keepdims=True)

        @pl.when(i == 0)
        def _():
            db_ref[...] = db
            ds_ref[...] = ds

        @pl.when(i > 0)
        def _():
            db_ref[...] += db
            ds_ref[...] += ds

    full = lambda *s: pl.BlockSpec(s, lambda j, i: (0,) * len(s))
    specs = _halo_specs(TM, D, lambda j: 0) + _halo_specs(TM, D, lambda j: 0) + [full(4, 256, 256), full(1, D), full(1, D)]
    return _pcall(body, name="pool_bwd", grid=(1, NT), in_specs=specs,
                  out_specs=[pl.BlockSpec((TM, D), lambda j, i: (i, 0)), full(4, 256, 256), full(1, D), full(1, D)],
                  out_shape=[jax.ShapeDtypeStruct((T, D), F32), jax.ShapeDtypeStruct((4, 256, 256), F32),
                             jax.ShapeDtypeStruct((1, D), F32), jax.ShapeDtypeStruct((1, D), F32)],
                  compiler_params=_cp(("parallel", "arbitrary")))(h, h, h, dy, dy, dy, pw, pb, ps)


def _rope_tables():
    half = HEAD_DIM // 4
    t = jnp.arange(SEQ)
    freqs = ROPE_BASE ** (-jnp.arange(half, dtype=F32) / half)
    ang_r = (t // GRID_W).astype(F32)[:, None] * freqs[None, :]
    ang_c = (t % GRID_W).astype(F32)[:, None] * freqs[None, :]
    cos = jnp.concatenate([jnp.cos(ang_r), jnp.cos(ang_r), jnp.cos(ang_c), jnp.cos(ang_c)], axis=1)
    sin = jnp.concatenate([-jnp.sin(ang_r), jnp.sin(ang_r), -jnp.sin(ang_c), jnp.sin(ang_c)], axis=1)
    cos = jnp.concatenate([jnp.ones((NCTX, HEAD_DIM), F32), cos], axis=0)
    sin = jnp.concatenate([jnp.zeros((NCTX, HEAD_DIM), F32), sin], axis=0)
    return jnp.tile(cos, (1, 2)), jnp.tile(sin, (1, 2))


QK_W = (NQH + NKVH) * HEAD_DIM
QKV_W = QK_W + NKVH * HEAD_DIM


def _rope(x, cos, sin, sign):
    def body(x_ref, c_ref, s_ref, o_ref):
        c = c_ref[...]
        s = s_ref[...] * sign
        lane = lax.broadcasted_iota(jnp.int32, (TM, LANE), 1)
        lo = (lane % 32) < 16
        for k in range(QK_W // LANE):
            xk = x_ref[:, LANE * k:LANE * (k + 1)]
            partner = jnp.where(lo, pltpu.roll(xk, LANE - 16, 1), pltpu.roll(xk, 16, 1))
            o_ref[:, LANE * k:LANE * (k + 1)] = (xk * c + partner * s).astype(o_ref.dtype)
        o_ref[:, QK_W:] = x_ref[:, QK_W:].astype(o_ref.dtype)

    return _pcall(body, name="rope", grid=(NT,),
                  in_specs=[pl.BlockSpec((TM, QKV_W), lambda i: (i, 0)), pl.BlockSpec((TM, LANE), lambda i: (i, 0)),
                            pl.BlockSpec((TM, LANE), lambda i: (i, 0))],
                  out_specs=pl.BlockSpec((TM, QKV_W), lambda i: (i, 0)),
                  out_shape=jax.ShapeDtypeStruct((T, QKV_W), _MXU), compiler_params=_cp(("parallel",)))(x, cos, sin)


NQB = T // ABLK
KPAD = T + 2 * ABLK
NKEY = NCTX + 3 * ABLK


def _stack_heads(ref):
    return jnp.concatenate([ref[g] for g in range(GQA)], axis=0)


def _sink_rows(s_ref):
    return jnp.concatenate([jnp.broadcast_to(s_ref[g], (ABLK, 1)) for g in range(GQA)], axis=0)


def _attn_mask(i):
    r = lax.broadcasted_iota(jnp.int32, (GQA * ABLK, NKEY), 0) % ABLK
    c = lax.broadcasted_iota(jnp.int32, (GQA * ABLK, NKEY), 1)
    n = i - NCTX // ABLK
    kpos = (n - 1) * ABLK + (c - NCTX)
    qpos = n * ABLK + r
    loc = (c >= NCTX) & (jnp.abs(kpos - qpos) <= WINDOW) & (kpos >= 0) & (kpos < SEQ) & (n >= 0)
    return (c < NCTX) | loc


def _attn_specs():
    qs = pl.BlockSpec((GQA, ABLK, HEAD_DIM), lambda h, i: (h, i, 0))
    kc = pl.BlockSpec((1, NCTX, HEAD_DIM), lambda h, i: (h, 0, 0))
    kl = [pl.BlockSpec((1, ABLK, HEAD_DIM), functools.partial(lambda h, i, d: (h, i + d, 0), d=d)) for d in range(3)]
    sk = pl.BlockSpec((GQA, 1, 1), lambda h, i: (h, 0, 0))
    return qs, kc, kl, sk


def _attn_fwd(q, k, v, sink):
    scale = HEAD_DIM ** -0.5

    def body(q_ref, kc, k0, k1, k2, vc, v0, v1, v2, s_ref, o_ref, l_ref):
        valid = _attn_mask(pl.program_id(1))
        kk = jnp.concatenate([kc[0], k0[0], k1[0], k2[0]], axis=0)
        vv = jnp.concatenate([vc[0], v0[0], v1[0], v2[0]], axis=0)
        s = lax.dot_general(_stack_heads(q_ref), kk, (((1,), (1,)), ((), ())), preferred_element_type=F32) * scale
        s = jnp.where(valid, s, NEG_INF)
        sk = _sink_rows(s_ref)
        m = jnp.maximum(jnp.max(s, axis=-1, keepdims=True), sk)
        p = jnp.exp(s - m)
        l = jnp.sum(p, axis=-1, keepdims=True) + jnp.exp(sk - m)
        o = jnp.dot((p / l).astype(_MXU), vv, preferred_element_type=F32).astype(o_ref.dtype)
        lse = m + jnp.log(l)
        for g in range(GQA):
            o_ref[g] = o[ABLK * g:ABLK * (g + 1)]
            l_ref[g] = lse[ABLK * g:ABLK * (g + 1)]

    qs, kc, kl, sk = _attn_specs()
    return _pcall(body, name="attn_fwd", grid=(NKVH, NQB), in_specs=[qs, kc, *kl, kc, *kl, sk],
                  out_specs=[qs, pl.BlockSpec((GQA, ABLK, 1), lambda h, i: (h, i, 0))],
                  out_shape=[jax.ShapeDtypeStruct((NQH, T, HEAD_DIM), _MXU), jax.ShapeDtypeStruct((NQH, T, 1), F32)],
                  compiler_params=_cp(("parallel", "arbitrary")))(q, k, k, k, k, v, v, v, v, sink)


def _attn_bwd(q, k, v, sink, lse, do):
    scale = HEAD_DIM ** -0.5

    def body(q_ref, kc, k0, k1, k2, vc, v0, v1, v2, s_ref, l_ref, do_ref, dq_ref, dk_ref, dv_ref, ds_ref):
        i = pl.program_id(1)

        @pl.when(i == 0)
        def _():
            dk_ref[...] = jnp.zeros_like(dk_ref)
            dv_ref[...] = jnp.zeros_like(dv_ref)
            ds_ref[...] = jnp.zeros_like(ds_ref)

        valid = _attn_mask(i)
        kk = jnp.concatenate([kc[0], k0[0], k1[0], k2[0]], axis=0)
        vv = jnp.concatenate([vc[0], v0[0], v1[0], v2[0]], axis=0)
        qst = _stack_heads(q_ref)
        dos = _stack_heads(do_ref)
        lse = _stack_heads(l_ref)
        s = lax.dot_general(qst, kk, (((1,), (1,)), ((), ())), preferred_element_type=F32) * scale
        s = jnp.where(valid, s, NEG_INF)
        p = jnp.exp(s - lse)
        psink = jnp.exp(_sink_rows(s_ref) - lse)
        dp = lax.dot_general(dos, vv, (((1,), (1,)), ((), ())), preferred_element_type=F32)
        delta = jnp.sum(p * dp, axis=-1, keepdims=True)
        ds = p * (dp - delta)
        dsk = psink * delta
        dsq = (ds * scale).astype(_MXU)
        dq = jnp.dot(dsq, kk, preferred_element_type=F32)
        for g in range(GQA):
            dq_ref[g] = dq[ABLK * g:ABLK * (g + 1)]
            ds_ref[g] += -jnp.sum(dsk[ABLK * g:ABLK * (g + 1)], axis=0, keepdims=True)
        dkk = lax.dot_general(dsq, qst, (((0,), (0,)), ((), ())), preferred_element_type=F32)
        dvv = lax.dot_general(p.astype(_MXU), dos, (((0,), (0,)), ((), ())), preferred_element_type=F32)
        loc = pl.ds(pl.multiple_of(i * ABLK, ABLK), 3 * ABLK)
        dk_ref[0, 0:NCTX, :] += dkk[:NCTX]
        dv_ref[0, 0:NCTX, :] += dvv[:NCTX]
        dk_ref[0, loc, :] += dkk[NCTX:]
        dv_ref[0, loc, :] += dvv[NCTX:]

    qs, kc, kl, sk = _attn_specs()
    ls = pl.BlockSpec((GQA, ABLK, 1), lambda h, i: (h, i, 0))
    kfull = pl.BlockSpec((1, KPAD, HEAD_DIM), lambda h, i: (h, 0, 0))
    return _pcall(body, name="attn_bwd", grid=(NKVH, NQB), in_specs=[qs, kc, *kl, kc, *kl, sk, ls, qs],
                  out_specs=[qs, kfull, kfull, sk],
                  out_shape=[jax.ShapeDtypeStruct((NQH, T, HEAD_DIM), F32), jax.ShapeDtypeStruct((NKVH, KPAD, HEAD_DIM), F32),
                             jax.ShapeDtypeStruct((NKVH, KPAD, HEAD_DIM), F32), jax.ShapeDtypeStruct((NQH, 1, 1), F32)],
                  compiler_params=_cp(("parallel", "arbitrary")))(q, k, k, k, k, v, v, v, v, sink, lse, do)


def _split_heads(x, nh):
    return x.reshape(T, nh, HEAD_DIM).transpose(1, 0, 2)


def _merge_heads(x):
    return x.transpose(1, 0, 2).reshape(T, -1)


def _pad_keys(x):
    z = jnp.zeros((x.shape[0], ABLK, HEAD_DIM), x.dtype)
    return jnp.concatenate([x[:, :NCTX], z, x[:, NCTX:], z], axis=1)


def _unpad_keys(x):
    return jnp.concatenate([x[:, :NCTX], x[:, NCTX + ABLK:NCTX + ABLK + SEQ]], axis=1)


def _attn_mixer_fwd(h, w_qkv, w_o, sink):
    cos, sin = _rope_tables()
    qkv = _rope(_mm(h, w_qkv, name="attn_qkv"), cos, sin, 1.0)
    q = _split_heads(qkv[:, :NQH * HEAD_DIM], NQH)
    k = _pad_keys(_split_heads(qkv[:, NQH * HEAD_DIM:QK_W], NKVH))
    v = _pad_keys(_split_heads(qkv[:, QK_W:], NKVH))
    sk = sink.reshape(NQH, 1, 1)
    o, lse = _attn_fwd(q, k, v, sk)
    om = _merge_heads(o)
    y = _mm(om, w_o, name="attn_out")
    return y, (q, k, v, sk, lse, om)


def _attn_mixer_bwd(dy, h, saved, w_qkv, w_o):
    q, k, v, sk, lse, om = saved
    cos, sin = _rope_tables()
    dyb = dy.astype(_MXU)
    dw_o = _mm(om, dyb, ta=True, name="attn_out_dw")
    do = _split_heads(_mm(dyb, w_o, tb=True, out_dtype=_MXU, name="attn_out_dx"), NQH)
    dq, dk, dv, dsk = _attn_bwd(q, k, v, sk, lse, do)
    dqkv = jnp.concatenate([_merge_heads(dq), _merge_heads(_unpad_keys(dk)), _merge_heads(_unpad_keys(dv))], axis=1)
    dqkv = _rope(dqkv, cos, sin, -1.0)
    dw_qkv = _mm(h, dqkv, ta=True, name="attn_qkv_dw")
    dh = _mm(dqkv, w_qkv, tb=True, name="attn_qkv_dx")
    return dh, dw_qkv, dw_o, dsk.reshape(1, NQH)


SSM_S = SSM_G * SSM_P
SSM_SL = SSM_S // LANE
SSM_TS = 128
SSM_NTS = T // SSM_TS
SSM_NCT = NCTX // SSM_TS
SSM_JB = 4
SSM_NTR = 4
SSM_TR = T // SSM_NTR


def _proj3d(u, w_re, w_im):
    def body(u_ref, wr_ref, wi_ref, or_ref, oi_ref):
        for hlf in range(2):
            ub = u_ref[:, LANE * hlf:LANE * (hlf + 1)].astype(_MXU)
            for w_ref, o_ref in ((wr_ref, or_ref), (wi_ref, oi_ref)):
                r = jnp.dot(ub, w_ref[hlf].astype(_MXU), preferred_element_type=F32)
                for q in range(4):
                    o_ref[:, 4 * hlf + q, :] = r[:, LANE * q:LANE * (q + 1)]

    ws = pl.BlockSpec((2, LANE, 512), lambda i, j: (j, 0, 0))
    os_ = pl.BlockSpec((SSM_TR, 8, LANE), lambda i, j: (i, j, 0))
    return _pcall(body, name="ssm_proj", grid=(SSM_NTR, SSM_JB), in_specs=[pl.BlockSpec((SSM_TR, 2 * LANE), lambda i, j: (i, j)), ws, ws],
                  out_specs=[os_, os_], out_shape=[jax.ShapeDtypeStruct((T, SSM_SL, LANE), F32)] * 2,
                  compiler_params=_cp(("parallel", "parallel")))(u, w_re, w_im)


def _readout(s_re, s_im, w_re, w_im):
    def body(sr_ref, si_ref, wr_ref, wi_ref, o_ref):
        for hlf in range(2):
            acc = None
            for s_ref, w_ref in ((sr_ref, wr_ref), (si_ref, wi_ref)):
                x = jnp.concatenate([s_ref[:, 4 * hlf + q, :] for q in range(4)], axis=1).astype(_MXU)
                r = jnp.dot(x, w_ref[hlf].astype(_MXU), preferred_element_type=F32)
                acc = r if acc is None else acc + r
            o_ref[:, LANE * hlf:LANE * (hlf + 1)] = acc

    ss = pl.BlockSpec((SSM_TR, 8, LANE), lambda i, j: (i, j, 0))
    ws = pl.BlockSpec((2, 512, LANE), lambda i, j: (j, 0, 0))
    return _pcall(body, name="ssm_readout", grid=(SSM_NTR, SSM_JB), in_specs=[ss, ss, ws, ws],
                  out_specs=pl.BlockSpec((SSM_TR, 2 * LANE), lambda i, j: (i, j)), out_shape=jax.ShapeDtypeStruct((T, D), F32),
                  compiler_params=_cp(("parallel", "parallel")))(s_re, s_im, w_re, w_im)


def _outer3d(s_re, s_im, y):
    def body(sr_ref, si_ref, y_ref, dr_ref, di_ref):
        i = pl.program_id(1)
        for hlf in range(2):
            yb = y_ref[:, LANE * hlf:LANE * (hlf + 1)].astype(_MXU)
            for s_ref, d_ref in ((sr_ref, dr_ref), (si_ref, di_ref)):
                x = jnp.concatenate([s_ref[:, 4 * hlf + q, :] for q in range(4)], axis=1).astype(_MXU)
                r = lax.dot_general(yb, x, (((0,), (0,)), ((), ())), preferred_element_type=F32)

                @pl.when(i == 0)
                def _():
                    d_ref[hlf] = r

                @pl.when(i > 0)
                def _():
                    d_ref[hlf] += r

    ss = pl.BlockSpec((SSM_TR, 8, LANE), lambda j, i: (i, j, 0))
    ds = pl.BlockSpec((2, LANE, 512), lambda j, i: (j, 0, 0))
    return _pcall(body, name="ssm_outer", grid=(SSM_JB, SSM_NTR), in_specs=[ss, ss, pl.BlockSpec((SSM_TR, 2 * LANE), lambda j, i: (i, j))],
                  out_specs=[ds, ds], out_shape=[jax.ShapeDtypeStruct((8, LANE, 512), F32)] * 2,
                  compiler_params=_cp(("parallel", "arbitrary")))(s_re, s_im, y)


def _scan_order(order):
    n, c = SSM_NTS, SSM_NCT
    if order == "fwd":
        return (lambda i: i), False
    if order == "fwd_adj":
        return (lambda i: n - 1 - i), True
    if order == "rev":
        return (lambda i: jnp.where(i < c, c - 1 - i, n + c - 1 - i)), True
    if order == "rev_adj":
        return (lambda i: jnp.where(i < n - c, i + c, i - (n - c))), False
    raise ValueError(order)


def _scan(b_re, b_im, lam_re, lam_im, order):
    tile, down = _scan_order(order)

    def body(br_ref, bi_ref, lr_ref, li_ref, sr_ref, si_ref, cr, ci):
        @pl.when(pl.program_id(0) == 0)
        def _():
            cr[...] = jnp.zeros_like(cr)
            ci[...] = jnp.zeros_like(ci)

        lr = lr_ref[...]
        li = li_ref[...]

        def step(n, c):
            t = SSM_TS - 1 - n if down else n
            sr, si = c
            nr = lr * sr - li * si + br_ref[t]
            ni = lr * si + li * sr + bi_ref[t]
            sr_ref[t] = nr
            si_ref[t] = ni
            return nr, ni

        sr, si = lax.fori_loop(0, SSM_TS, step, (cr[...], ci[...]))
        cr[...] = sr
        ci[...] = si

    bs = pl.BlockSpec((SSM_TS, SSM_SL, LANE), lambda i: (tile(i), 0, 0))
    ps = pl.BlockSpec((SSM_SL, LANE), lambda i: (0, 0))
    return _pcall(body, name="ssm_scan_" + order, grid=(SSM_NTS,), in_specs=[bs, bs, ps, ps], out_specs=[bs, bs],
                  out_shape=[jax.ShapeDtypeStruct((T, SSM_SL, LANE), F32)] * 2,
                  scratch_shapes=[pltpu.VMEM((SSM_SL, LANE), F32)] * 2, compiler_params=_cp(("arbitrary",)))(b_re, b_im, lam_re, lam_im)


def _scan_adj(g_re, g_im, s_re, s_im, lam_re, lam_im, order):
    tile, down = _scan_order(order)

    def body(gr_ref, gi_ref, sr_ref, si_ref, lr_ref, li_ref, ar_ref, ai_ref, dlr_ref, dli_ref, cr, ci):
        @pl.when(pl.program_id(0) == 0)
        def _():
            cr[...] = jnp.zeros_like(cr)
            ci[...] = jnp.zeros_like(ci)
            dlr_ref[...] = jnp.zeros_like(dlr_ref)
            dli_ref[...] = jnp.zeros_like(dli_ref)

        lr = lr_ref[...]
        li = li_ref[...]

        def step(n, c):
            t = SSM_TS - 1 - n if down else n
            ar, ai, dr, di = c
            sr = sr_ref[t]
            si = si_ref[t]
            dr = dr + ar * sr + ai * si
            di = di + ai * sr - ar * si
            nr = gr_ref[t] + lr * ar + li * ai
            ni = gi_ref[t] + lr * ai - li * ar
            ar_ref[t] = nr
            ai_ref[t] = ni
            return nr, ni, dr, di

        ar, ai, dr, di = lax.fori_loop(0, SSM_TS, step, (cr[...], ci[...], dlr_ref[...], dli_ref[...]))
        cr[...] = ar
        ci[...] = ai
        dlr_ref[...] = dr
        dli_ref[...] = di

    bs = pl.BlockSpec((SSM_TS, SSM_SL, LANE), lambda i: (tile(i), 0, 0))
    ps = pl.BlockSpec((SSM_SL, LANE), lambda i: (0, 0))
    return _pcall(body, name="ssm_scan_" + order, grid=(SSM_NTS,), in_specs=[bs, bs, bs, bs, ps, ps], out_specs=[bs, bs, ps, ps],
                  out_shape=[jax.ShapeDtypeStruct((T, SSM_SL, LANE), F32)] * 2 + [jax.ShapeDtypeStruct((SSM_SL, LANE), F32)] * 2,
                  scratch_shapes=[pltpu.VMEM((SSM_SL, LANE), F32)] * 2,
                  compiler_params=_cp(("arbitrary",)))(g_re, g_im, s_re, s_im, lam_re, lam_im)


def _block_diag(x):
    x4 = x.reshape(8, 8, SSM_P, SSM_C)
    return jnp.einsum("jgpc,gh->jgphc", x4, jnp.eye(8, dtype=x.dtype)).reshape(8, 8 * SSM_P, 8 * SSM_C)


def _ssm_prep(lam_re, lam_im, log_dt, b_re, b_im, c_re, c_im):
    lam = lax.complex(lam_re, lam_im)
    dt = jnp.exp(log_dt)[:, None]
    lam_bar = jnp.exp(lam * dt)
    b_bar = ((lam_bar - 1.0) / lam)[..., None] * lax.complex(b_re, b_im)
    return (jnp.real(lam_bar).reshape(SSM_SL, LANE), jnp.imag(lam_bar).reshape(SSM_SL, LANE),
            _block_diag(jnp.real(b_bar)), _block_diag(jnp.imag(b_bar)),
            _block_diag(c_re.transpose(0, 2, 1)), _block_diag(-c_im.transpose(0, 2, 1)))


def _ssm_glue(h, yf, yr, d):
    return jax.nn.gelu(d * h + yf + yr)


def _glu(ga, gb):
    return ga * jax.nn.sigmoid(gb)


def _ssm_mixer_fwd(h, sp, w_a, w_b):
    lam_re, lam_im, log_dt, b_re, b_im, c_re, c_im, d_skip = sp
    ys, saved = [], []
    for di, order in enumerate(("fwd", "rev")):
        lr, li, wb_r, wb_i, wc_r, wc_i = _ssm_prep(lam_re[di], lam_im[di], log_dt[di], b_re[di], b_im[di], c_re[di], c_im[di])
        bu_r, bu_i = _proj3d(h, wb_r.transpose(0, 2, 1), wb_i.transpose(0, 2, 1))
        s_r, s_i = _scan(bu_r, bu_i, lr, li, order)
        ys.append(_readout(s_r, s_i, wc_r, wc_i))
        saved.append((s_r, s_i))
    g = _rows(lambda *a: (_ssm_glue(*a),), [h, ys[0], ys[1]], [d_skip], [(D, _MXU)], [], name="ssm_glue")[0]
    ga = _mm(g, w_a, name="ssm_glu_a")
    gb = _mm(g, w_b, name="ssm_glu_b")
    y = _rows(lambda *a: (_glu(*a),), [ga, gb], [], [(D, F32)], [], name="ssm_glu")[0]
    return y, (ys, saved, g, ga, gb)


def _ssm_mixer_bwd(dy, h, saved_all, sp, w_a, w_b):
    lam_re, lam_im, log_dt, b_re, b_im, c_re, c_im, d_skip = sp
    ys, saved, g, ga, gb = saved_all

    def glu_bwd(ga, gb, dy):
        _, vjp = jax.vjp(_glu, ga, gb)
        return vjp(dy)

    dga, dgb = _rows(glu_bwd, [ga, gb, dy], [], [(D, _MXU), (D, _MXU)], [], name="ssm_glu_bwd")
    dw_a = _mm(g, dga, ta=True, name="ssm_glu_a_dw")
    dw_b = _mm(g, dgb, ta=True, name="ssm_glu_b_dw")
    dg_a = _mm(dga, w_a, tb=True, name="ssm_glu_a_dx")
    dg_b = _mm(dgb, w_b, tb=True, name="ssm_glu_b_dx")

    def glue_bwd(h, yf, yr, dg_a, dg_b, d):
        _, vjp = jax.vjp(_ssm_glue, h, yf, yr, d)
        dh, dyl, _, dd = vjp(dg_a + dg_b)
        return dh, dyl, dd

    dh0, dyl, dd = _rows(glue_bwd, [h, ys[0], ys[1], dg_a, dg_b], [d_skip], [(D, F32), (D, F32)], [(1, (1, D))], name="ssm_glue_bwd")
    dhs = [dh0]
    dparams = []
    for di, (order, adj) in enumerate((("fwd", "fwd_adj"), ("rev", "rev_adj"))):
        args = (lam_re[di], lam_im[di], log_dt[di], b_re[di], b_im[di], c_re[di], c_im[di])
        (lr, li, wb_r, wb_i, wc_r, wc_i), prep_vjp = jax.vjp(_ssm_prep, *args)
        s_r, s_i = saved[di]
        dwc_r, dwc_i = _outer3d(s_r, s_i, dyl)
        g_r, g_i = _proj3d(dyl, wc_r.transpose(0, 2, 1), wc_i.transpose(0, 2, 1))
        a_r, a_i, dlr, dli = _scan_adj(g_r, g_i, s_r, s_i, lr, li, adj)
        dwb_r, dwb_i = _outer3d(a_r, a_i, h)
        dhs.append(_readout(a_r, a_i, wb_r, wb_i))
        dparams.append(prep_vjp((dlr, dli) + tuple(d.transpose(0, 2, 1) for d in (dwb_r, dwb_i, dwc_r, dwc_i))))
    dsp = [jnp.stack([dparams[0][k], dparams[1][k]], axis=0) for k in range(7)]
    return dhs, dsp, dd, dw_a, dw_b


NCH = T // GM_CHUNK


def _gm_specs():
    full = lambda *s: pl.BlockSpec(s, lambda i: (0,) * len(s))
    zu = pl.BlockSpec((GM_CHUNK, GM_HALF), lambda i: (i, 0))
    zv = pl.BlockSpec((GM_CHUNK, GM_HALF), lambda i: (i, 1))
    pars = [full(1, GM_HALF), pl.BlockSpec((1, GM_HALF), lambda i: (0, 1)), full(1, GM_HALF), full(1, GM_HALF),
            full(GM_HEADS, GM_CHUNK, GM_CHUNK), full(GM_HEADS, GM_CHUNK, 1)]
    return zu, zv, pars, full


def _gm_forward(zu_ref, zv_ref, bu_ref, bv_ref, g_ref, b_ref, ws_ref, bs_ref):
    u = jax.nn.gelu(zu_ref[...] + bu_ref[...])
    zv = jax.nn.gelu(zv_ref[...] + bv_ref[...])
    mu = jnp.mean(zv, axis=-1, keepdims=True)
    zc = zv - mu
    rstd = lax.rsqrt(jnp.mean(jnp.square(zc), axis=-1, keepdims=True) + LN_EPS)
    vhat = zc * rstd
    v = (vhat * g_ref[...] + b_ref[...]).astype(_MXU)
    gates = [jnp.dot(ws_ref[hd].astype(_MXU), v[:, GM_HD * hd:GM_HD * (hd + 1)], preferred_element_type=F32) + bs_ref[hd]
             for hd in range(GM_HEADS)]
    return u, vhat, rstd, v, jnp.concatenate(gates, axis=1)


def _gmlp_chunk(zp, b_in, ln_g, ln_b, w_s, b_s):
    def body(zu_ref, zv_ref, bu_ref, bv_ref, g_ref, b_ref, ws_ref, bs_ref, o_ref):
        u, _, _, _, gate = _gm_forward(zu_ref, zv_ref, bu_ref, bv_ref, g_ref, b_ref, ws_ref, bs_ref)
        o_ref[...] = (u * gate).astype(o_ref.dtype)

    zu, zv, pars, _ = _gm_specs()
    return _pcall(body, name="gmlp_chunk", grid=(NCH,), in_specs=[zu, zv, *pars], out_specs=zu,
                  out_shape=jax.ShapeDtypeStruct((T, GM_HALF), _MXU),
                  compiler_params=_cp(("parallel",)))(zp, zp, b_in, b_in, ln_g, ln_b, w_s, b_s)


def _gmlp_chunk_bwd(zp, do, b_in, ln_g, ln_b, w_s, b_s):
    def body(zu_ref, zv_ref, do_ref, bu_ref, bv_ref, g_ref, b_ref, ws_ref, bs_ref,
             dzu_ref, dzv_ref, dbu_ref, dbv_ref, dg_ref, db_ref, dws_ref, dbs_ref):
        i = pl.program_id(0)

        def acc(ref, val, idx=None):
            @pl.when(i == 0)
            def _():
                if idx is None:
                    ref[...] = val
                else:
                    ref[idx] = val

            @pl.when(i > 0)
            def _():
                if idx is None:
                    ref[...] += val
                else:
                    ref[idx] += val

        u, vhat, rstd, v, gate = _gm_forward(zu_ref, zv_ref, bu_ref, bv_ref, g_ref, b_ref, ws_ref, bs_ref)
        do = do_ref[...]
        du = do * gate
        dgate = do * u
        dvs = []
        for hd in range(GM_HEADS):
            sl = slice(GM_HD * hd, GM_HD * (hd + 1))
            dgh = dgate[:, sl]
            dghb = dgh.astype(_MXU)
            dvs.append(lax.dot_general(ws_ref[hd].astype(_MXU), dghb, (((0,), (0,)), ((), ())), preferred_element_type=F32))
            acc(dws_ref, lax.dot_general(dghb, v[:, sl], (((1,), (1,)), ((), ())), preferred_element_type=F32), hd)
            acc(dbs_ref, jnp.sum(dgh, axis=1, keepdims=True), hd)
        dv = jnp.concatenate(dvs, axis=1)
        acc(dg_ref, jnp.sum(dv * vhat, axis=0, keepdims=True))
        acc(db_ref, jnp.sum(dv, axis=0, keepdims=True))
        dvh = dv * g_ref[...]
        dzv = rstd * (dvh - jnp.mean(dvh, axis=-1, keepdims=True) - vhat * jnp.mean(dvh * vhat, axis=-1, keepdims=True))
        dpu = jax.vjp(jax.nn.gelu, zu_ref[...] + bu_ref[...])[1](du)[0]
        dpv = jax.vjp(jax.nn.gelu, zv_ref[...] + bv_ref[...])[1](dzv)[0]
        dzu_ref[...] = dpu.astype(dzu_ref.dtype)
        dzv_ref[...] = dpv.astype(dzv_ref.dtype)
        acc(dbu_ref, jnp.sum(dpu, axis=0, keepdims=True))
        acc(dbv_ref, jnp.sum(dpv, axis=0, keepdims=True))

    zu, zv, pars, full = _gm_specs()
    out_specs = [zu, zu, full(1, GM_HALF), full(1, GM_HALF), full(1, GM_HALF), full(1, GM_HALF),
                 full(GM_HEADS, GM_CHUNK, GM_CHUNK), full(GM_HEADS, GM_CHUNK, 1)]
    out_shape = [jax.ShapeDtypeStruct((T, GM_HALF), _MXU)] * 2 + [jax.ShapeDtypeStruct((1, GM_HALF), F32)] * 4 + \
                [jax.ShapeDtypeStruct((GM_HEADS, GM_CHUNK, GM_CHUNK), F32), jax.ShapeDtypeStruct((GM_HEADS, GM_CHUNK, 1), F32)]
    dzu, dzv, dbu, dbv, dg, db, dws, dbs = _pcall(
        body, name="gmlp_chunk_bwd", grid=(NCH,), in_specs=[zu, zv, zu, *pars], out_specs=out_specs, out_shape=out_shape,
        compiler_params=_cp(("arbitrary",)))(zp, zp, do, b_in, b_in, ln_g, ln_b, w_s, b_s)
    return jnp.concatenate([dzu, dzv], axis=1), jnp.concatenate([dbu, dbv], axis=1), dg, db, dws, dbs


def _gmlp_mixer_fwd(h, w_in, b_in, ln_g, ln_b, w_s, b_s, w_out):
    zp = _mm(h, w_in, name="gmlp_in")
    ug = _gmlp_chunk(zp, b_in, ln_g, ln_b, w_s, b_s[..., None])
    return _mm(ug, w_out, name="gmlp_out"), (zp, ug)


def _gmlp_mixer_bwd(dy, h, saved, w_in, b_in, ln_g, ln_b, w_s, b_s, w_out):
    zp, ug = saved
    dw_out = _mm(ug, dy, ta=True, name="gmlp_out_dw")
    do = _mm(dy, w_out, tb=True, name="gmlp_out_dx")
    dzp, db_in, dg, db, dws, dbs = _gmlp_chunk_bwd(zp, do, b_in, ln_g, ln_b, w_s, b_s[..., None])
    dw_in = _mm(h, dzp, ta=True, name="gmlp_in_dw")
    dh = _mm(dzp, w_in, tb=True, name="gmlp_in_dx")
    return dh, dw_in, db_in, dg, db, dws, dbs[..., 0], dw_out


def _loss_head(x, target):
    nct = NCTX // TM

    def body(x_ref, t_ref, l_ref, dx_ref):
        i = pl.program_id(0)
        err = jnp.where(i >= nct, x_ref[...] - t_ref[...], 0.0)
        dx_ref[...] = err * (1.0 / D)
        part = 0.5 * jnp.sum(jnp.sum(jnp.square(err), axis=-1, keepdims=True) * (1.0 / D), axis=0, keepdims=True)

        @pl.when(i == 0)
        def _():
            l_ref[...] = part

        @pl.when(i > 0)
        def _():
            l_ref[...] += part

    return _pcall(body, name="loss_head", grid=(NT,),
                  in_specs=[pl.BlockSpec((TM, D), lambda i: (i, 0)), pl.BlockSpec((TM, D), lambda i: (jnp.maximum(i - nct, 0), 0))],
                  out_specs=[pl.BlockSpec((1, 1), lambda i: (0, 0)), pl.BlockSpec((TM, D), lambda i: (i, 0))],
                  out_shape=[jax.ShapeDtypeStruct((1, 1), F32), jax.ShapeDtypeStruct((T, D), F32)],
                  compiler_params=_cp(("arbitrary",)))(x, target)


def _as2d(a):
    return a.reshape(-1, a.shape[-1])


def _adamw(w, g, m, v):
    shape = w.shape
    w2, g2, m2, v2 = _as2d(w), _as2d(g), _as2d(m), _as2d(v)
    R, C = w2.shape
    tr = _tile(R, 512, SUBLANE)
    c1 = 1.0 - B1 ** STEP
    c2 = 1.0 - B2 ** STEP

    def body(w_ref, g_ref, m_ref, v_ref, d_ref, nm_ref, nv_ref):
        g = g_ref[...]
        m = B1 * m_ref[...] + (1.0 - B1) * g
        v = B2 * v_ref[...] + (1.0 - B2) * jnp.square(g)
        nm_ref[...] = m
        nv_ref[...] = v
        d_ref[...] = -LR * ((m / c1) / (jnp.sqrt(v / c2) + EPS) + WD * w_ref[...])

    spec = pl.BlockSpec((tr, C), lambda i: (i, 0))
    outs = _pcall(body, name="adamw", grid=(R // tr,), in_specs=[spec] * 4, out_specs=[spec] * 3,
                  out_shape=[jax.ShapeDtypeStruct((R, C), F32)] * 3, compiler_params=_cp(("parallel",)))(w2, g2, m2, v2)
    return tuple(o.reshape(shape) for o in outs)


def _sum_slabs(x):
    _, R, C = x.shape
    tr = _tile(R, 256, 16)

    def body(x_ref, o_ref):
        acc = x_ref[0].astype(F32)
        for k in range(1, NDEV):
            acc = acc + x_ref[k].astype(F32)
        o_ref[...] = acc

    return _pcall(body, name="sum_slabs", grid=(R // tr,), in_specs=[pl.BlockSpec((NDEV, tr, C), lambda i: (0, i, 0))],
                  out_specs=pl.BlockSpec((tr, C), lambda i: (i, 0)), out_shape=jax.ShapeDtypeStruct((R, C), F32),
                  compiler_params=_cp(("parallel",)))(x)


def _comm_call(body, xs, out_shape, name):
    n = len(xs)
    hbm = pl.BlockSpec(memory_space=pl.ANY)
    return _pcall(body, name=name, in_specs=[hbm] * n, out_specs=[hbm] * n, out_shape=out_shape,
                  scratch_shapes=[pltpu.SemaphoreType.DMA((n, NDEV - 1)), pltpu.SemaphoreType.DMA((n, NDEV - 1)),
                                  pltpu.SemaphoreType.DMA((n,))],
                  compiler_params=pltpu.CompilerParams(has_side_effects=True))(*xs)


def _exchange(xs, name):
    n = len(xs)

    def body(*refs):
        x_refs, o_refs = refs[:n], refs[n:2 * n]
        send_sems, recv_sems, loc_sems = refs[2 * n:]
        mx, my, mc = lax.axis_index("x"), lax.axis_index("y"), lax.axis_index("c")
        me = 4 * mx + 2 * my + mc
        pending = []
        for a in range(n):
            mine = pltpu.make_async_copy(x_refs[a].at[me], o_refs[a].at[me], loc_sems.at[a])
            mine.start()
            pending.append(mine)
            for k in range(1, NDEV):
                px = 1 - mx if k & 4 else mx
                py = 1 - my if k & 2 else my
                pc = 1 - mc if k & 1 else mc
                cp = pltpu.make_async_remote_copy(
                    src_ref=x_refs[a].at[4 * px + 2 * py + pc], dst_ref=o_refs[a].at[me],
                    send_sem=send_sems.at[a, k - 1], recv_sem=recv_sems.at[a, k - 1],
                    device_id=(px, py, pc), device_id_type=pl.DeviceIdType.MESH)
                cp.start()
                pending.append(cp)
        for cp in pending:
            cp.wait()

    return _comm_call(body, xs, [jax.ShapeDtypeStruct(tuple(x.shape), x.dtype) for x in xs], name)


def _gather(xs, name):
    n = len(xs)

    def body(*refs):
        x_refs, o_refs = refs[:n], refs[n:2 * n]
        send_sems, recv_sems, loc_sems = refs[2 * n:]
        mx, my, mc = lax.axis_index("x"), lax.axis_index("y"), lax.axis_index("c")
        me = 4 * mx + 2 * my + mc
        sibling = (mx, my, 1 - mc)
        chips = [(1 - mx, my), (mx, 1 - my), (1 - mx, 1 - my)]
        slot = lambda px, py, pc: 4 * px + 2 * py + pc

        def copy(a, k, s, to, from_input=False):
            return pltpu.make_async_remote_copy(
                src_ref=x_refs[a] if from_input else o_refs[a].at[s], dst_ref=o_refs[a].at[s],
                send_sem=send_sems.at[a, k], recv_sem=recv_sems.at[a, k], device_id=to, device_id_type=pl.DeviceIdType.MESH)

        sends, mines = [], []
        for a in range(n):
            mine = pltpu.make_async_copy(x_refs[a], o_refs[a].at[me], loc_sems.at[a])
            mine.start()
            mines.append(mine)
            first = [copy(a, 0, me, sibling, True)] + [copy(a, 1 + j, me, (cx, cy, mc), True) for j, (cx, cy) in enumerate(chips)]
            for cp in first:
                cp.start()
            sends += first
        for a in range(n):
            for j, (cx, cy) in enumerate(chips):
                s = slot(cx, cy, mc)
                copy(a, 1 + j, s, sibling).wait_recv()
                passed = copy(a, 4 + j, s, sibling)
                passed.start()
                sends.append(passed)
        for a in range(n):
            copy(a, 0, slot(*sibling), sibling).wait_recv()
            for j, (cx, cy) in enumerate(chips):
                copy(a, 4 + j, slot(cx, cy, 1 - mc), sibling).wait_recv()
        for cp in sends:
            cp.wait_send()
        for mine in mines:
            mine.wait()

    return _comm_call(body, xs, [jax.ShapeDtypeStruct((NDEV,) + tuple(x.shape), x.dtype) for x in xs], name)


SLAB_W = 1024
SLAB_ROWS = 16


def _pack(parts, lead=None):
    if lead is None:
        flat = jnp.concatenate([p.reshape(-1) for p in parts])
        n = flat.shape[0]
        padn = -n % (SLAB_ROWS * SLAB_W)
        return jnp.pad(flat, (0, padn)).reshape(-1, SLAB_W)
    flat = jnp.concatenate([p.reshape(lead, -1) for p in parts], axis=1)
    n = flat.shape[1]
    padn = -n % (SLAB_ROWS * SLAB_W)
    return jnp.pad(flat, ((0, 0), (0, padn))).reshape(lead, -1, SLAB_W)


def _unpack(buf, shapes, lead=None):
    out, off = [], 0
    flat = buf.reshape(-1) if lead is None else buf.reshape(lead, -1)
    for s in shapes:
        n = int(np.prod(s))
        if lead is None:
            out.append(flat[off:off + n].reshape(s))
        else:
            out.append(flat[:, off:off + n].reshape((lead,) + tuple(s)))
        off += n
    return out


def _gathered(blk, ax):
    m = jnp.moveaxis(blk, 0, ax)
    s = list(m.shape)
    return m.reshape(s[:ax] + [s[ax] * s[ax + 1]] + s[ax + 2:])


def _scattered(full, ax):
    s = list(full.shape)
    m = full.reshape(s[:ax] + [NDEV, s[ax] // NDEV] + s[ax + 1:])
    return jnp.moveaxis(m, ax, 0)


_MM_SHARDED = (("ffn_w_up", 2), ("ffn_w_down", 1), ("pool_w", 2), ("attn_w_qkv", 2), ("attn_w_o", 1),
               ("ssm_w_glu_a", 1), ("ssm_w_glu_b", 1), ("gmlp_w_in", 2), ("gmlp_w_out", 1))
_VEC_SHARDED = (("ffn_conv_w", 2), ("ssm_d", 1), ("gmlp_b_in", 1), ("gmlp_ln_g", 1), ("gmlp_ln_b", 1))
_REPLICATED = ("ln1_g", "ln1_b", "ln2_g", "ln2_b", "ffn_conv_b", "pool_b", "pool_scale", "attn_sink",
               "ssm_lambda_re", "ssm_lambda_im", "ssm_log_dt", "ssm_b_re", "ssm_b_im", "ssm_c_re", "ssm_c_im",
               "gmlp_w_s", "gmlp_b_s")
_WEIGHTS = ("c_ctx", "ada_w", "ada_b", "ln1_g", "ln1_b", "ln2_g", "ln2_b", "ffn_w_up", "ffn_conv_w", "ffn_conv_b", "ffn_w_down",
            "pool_w", "pool_b", "pool_scale", "attn_w_qkv", "attn_w_o", "attn_sink", "ssm_lambda_re", "ssm_lambda_im",
            "ssm_log_dt", "ssm_b_re", "ssm_b_im", "ssm_c_re", "ssm_c_im", "ssm_d", "ssm_w_glu_a", "ssm_w_glu_b",
            "gmlp_w_in", "gmlp_b_in", "gmlp_ln_g", "gmlp_ln_b", "gmlp_w_s", "gmlp_b_s", "gmlp_w_out")
N_MODS = 6
ADA_COLS = N_MODS * D // NDEV
PAD_ROWS = 16


def _silu_rows(x):
    return _rows(lambda v: (jax.nn.silu(v),), [x], [], [(x.shape[1], F32)], [], name="silu", tm=x.shape[0])[0]


def _step(x, c, ctx, loss_target, w, m, v):
    mx, my, mc = lax.axis_index("x"), lax.axis_index("y"), lax.axis_index("c")
    me = 4 * mx + 2 * my + mc

    vec_buf = _pack([w[n] for n, _ in _VEC_SHARDED] + [c])
    *mm_parts, vec_all = _gather([w[n].astype(_MXU) for n, _ in _MM_SHARDED] + [vec_buf], "gather_weights")
    vec_parts = _unpack(vec_all, [w[n].shape for n, _ in _VEC_SHARDED] + [c.shape], lead=NDEV)
    full = {n: _gathered(p, ax) for (n, ax), p in zip(_MM_SHARDED, mm_parts)}
    full.update({n: _gathered(p, ax) for (n, ax), p in zip(_VEC_SHARDED, vec_parts[:-1])})
    c_all = vec_parts[-1].reshape(NDEV, D)

    cc = jnp.concatenate([c_all, w["c_ctx"].reshape(1, D), jnp.zeros((PAD_ROWS - NDEV - 1, D), F32)], axis=0)
    silu_cc = _silu_rows(cc)
    ada_b_mine = lax.dynamic_slice(w["ada_b"], (0, me * ADA_COLS), (DEPTH, ADA_COLS))
    mods_mine = jnp.stack([_mm(silu_cc, w["ada_w"][l], name="ada_mods") + ada_b_mine[l][None, :] for l in range(DEPTH)], axis=1)
    per_dev = mods_mine[:NDEV].reshape(NDEV, DEPTH * ADA_COLS)
    cm = jnp.broadcast_to(mods_mine[NDEV].reshape(1, DEPTH * ADA_COLS), (NDEV, DEPTH * ADA_COLS))
    mods_all = _exchange([_pack([per_dev, cm], lead=NDEV)], "scatter_mods")[0]
    got = _unpack(mods_all, [(DEPTH, ADA_COLS), (DEPTH, ADA_COLS)], lead=NDEV)
    mods = got[0].transpose(1, 0, 2).reshape(DEPTH, N_MODS * D)
    cmods = got[1].transpose(1, 0, 2).reshape(DEPTH, N_MODS * D)
    P = [[jnp.stack([cmods[l, k * D:(k + 1) * D], mods[l, k * D:(k + 1) * D]]).reshape(2, 1, D) for k in range(N_MODS)]
         for l in range(DEPTH)]
    row = lambda a, l: a[l].reshape(1, 1, D)

    xs = jnp.concatenate([ctx[0], x[0]], axis=0)
    sp = tuple(w[n][0] for n in ("ssm_lambda_re", "ssm_lambda_im", "ssm_log_dt", "ssm_b_re", "ssm_b_im", "ssm_c_re", "ssm_c_im")) + \
        (full["ssm_d"].reshape(1, 1, D),)
    gm = (full["gmlp_w_in"][0], full["gmlp_b_in"], full["gmlp_ln_g"], full["gmlp_ln_b"], w["gmlp_w_s"][0], w["gmlp_b_s"][0],
          full["gmlp_w_out"][0])
    pool_args = (full["pool_w"][0], w["pool_b"], w["pool_scale"])
    saved = []
    for l in range(DEPTH):
        sh1, sc1, gt1, sh2, sc2, gt2 = P[l]
        h1 = _pre_mixer(xs, sh1, sc1, _MXU if l in (1, 3) else F32)
        if l == 0:
            y, ms = _pool_fwd(h1, *pool_args), None
        elif l == 1:
            y, ms = _attn_mixer_fwd(h1, full["attn_w_qkv"][0], full["attn_w_o"][0], w["attn_sink"])
        elif l == 2:
            y, ms = _ssm_mixer_fwd(h1, sp, full["ssm_w_glu_a"][0], full["ssm_w_glu_b"][0])
        else:
            y, ms = _gmlp_mixer_fwd(h1, *gm)
        x1, h2 = _post_mixer(xs, y, gt1, row(w["ln1_g"], l), row(w["ln1_b"], l), sh2, sc2)
        f, (a, u) = _ffn_fwd(h2, full["ffn_w_up"][l], full["ffn_conv_w"][l], w["ffn_conv_b"][l][None, :], full["ffn_w_down"][l])
        x2 = _post_ffn(x1, f, gt2, row(w["ln2_g"], l), row(w["ln2_b"], l))
        saved.append((xs, h1, y, ms, x1, h2, a, u, f))
        xs = x2
    loss, dxs = _loss_head(xs, loss_target[0])

    g = {n: [None] * DEPTH for n in ("ln1_g", "ln1_b", "ln2_g", "ln2_b", "ffn_w_up", "ffn_conv_w", "ffn_conv_b", "ffn_w_down")}
    dP = [None] * DEPTH
    for l in reversed(range(DEPTH)):
        sh1, sc1, gt1, sh2, sc2, gt2 = P[l]
        x0, h1, y, ms, x1, h2, a, u, f = saved[l]
        dx1, df, dgt2, g["ln2_g"][l], g["ln2_b"][l] = _post_ffn_bwd(x1, f, dxs, gt2, row(w["ln2_g"], l), row(w["ln2_b"], l))
        dh2, g["ffn_w_up"][l], g["ffn_conv_w"][l], g["ffn_conv_b"][l], g["ffn_w_down"][l] = _ffn_bwd(
            df, h2, a, u, full["ffn_w_up"][l], full["ffn_conv_w"][l], w["ffn_conv_b"][l][None, :], full["ffn_w_down"][l])
        dx0, dy, dgt1, g["ln1_g"][l], g["ln1_b"][l], dsh2, dsc2 = _post_mixer_bwd(
            x0, y, dx1, dh2, gt1, row(w["ln1_g"], l), row(w["ln1_b"], l), sh2, sc2)
        if l == 0:
            dh, g["pool_w"], g["pool_b"], g["pool_scale"] = _pool_bwd(h1, dy, *pool_args)
            dhs = [dh]
        elif l == 1:
            dh, g["attn_w_qkv"], g["attn_w_o"], g["attn_sink"] = _attn_mixer_bwd(dy, h1, ms, full["attn_w_qkv"][0], full["attn_w_o"][0])
            dhs = [dh]
        elif l == 2:
            dhs, dsp, dd, g["ssm_w_glu_a"], g["ssm_w_glu_b"] = _ssm_mixer_bwd(dy, h1, ms, sp, full["ssm_w_glu_a"][0], full["ssm_w_glu_b"][0])
            for n, d_ in zip(("ssm_lambda_re", "ssm_lambda_im", "ssm_log_dt", "ssm_b_re", "ssm_b_im", "ssm_c_re", "ssm_c_im"), dsp):
                g[n] = d_
            g["ssm_d"] = dd.reshape(1, D)
        else:
            (dh, g["gmlp_w_in"], g["gmlp_b_in"], g["gmlp_ln_g"], g["gmlp_ln_b"], g["gmlp_w_s"], g["gmlp_b_s"],
             g["gmlp_w_out"]) = _gmlp_mixer_bwd(dy, h1, ms, *gm)
            dhs = [dh]
        dxs, dsh1, dsc1 = _pre_mixer_bwd(x0, dhs, dx0, sh1, sc1)
        dP[l] = (dsh1, dsc1, dgt1, dsh2, dsc2, dgt2)
    grad_x = dxs[NCTX:][None]
    dmods = jnp.stack([jnp.concatenate([p[1, 0] for p in dP[l]]) for l in range(DEPTH)])
    dcmods = jnp.stack([jnp.concatenate([p[0, 0] for p in dP[l]]) for l in range(DEPTH)])

    gfull = {n: (jnp.stack(g[n]) if isinstance(g[n], list) else g[n]) for n in g}
    sharded_names = [n for n, _ in _MM_SHARDED] + [n for n, _ in _VEC_SHARDED]
    sharded_axes = dict(_MM_SHARDED + _VEC_SHARDED)

    def as_param(n, a):
        shard = w[n].shape
        ax = sharded_axes.get(n)
        fs = tuple(s * NDEV if i == ax else s for i, s in enumerate(shard))
        return a.reshape(fs)

    rep = jnp.concatenate([as_param(n, gfull[n]).reshape(-1) for n in _REPLICATED])
    n_rep = rep.shape[0]
    rep = jnp.pad(rep, (0, -n_rep % (NDEV * SLAB_W))).reshape(NDEV, -1)
    by_dev = lambda a: a.reshape(DEPTH, NDEV, ADA_COLS).transpose(1, 0, 2)
    big = [_scattered(as_param(n, gfull[n]), ax).astype(_MXU) for n, ax in _MM_SHARDED]
    parts = [_scattered(as_param(n, gfull[n]), ax) for n, ax in _VEC_SHARDED] + [rep, by_dev(dmods), by_dev(dcmods)]
    *big_in, grads_in = _exchange(big + [_pack(parts, lead=NDEV)], "scatter_grads")
    grads = {n: _sum_slabs(b.reshape(NDEV, -1, b.shape[-1])).reshape(w[n].shape) for (n, _), b in zip(_MM_SHARDED, big_in)}
    shapes = [w[n].shape for n, _ in _VEC_SHARDED] + [(rep.shape[1],), (DEPTH, ADA_COLS), (DEPTH, ADA_COLS)]
    red = _unpack(_sum_slabs(grads_in), shapes)
    grads.update({n: r for (n, _), r in zip(_VEC_SHARDED, red)})
    rep_mine, dcm = red[-3], red[-1]
    dm_all = _unpack(grads_in, shapes, lead=NDEV)[-2]

    e_rows = jnp.concatenate([dm_all, dcm[None], jnp.zeros((PAD_ROWS - NDEV - 1, DEPTH, ADA_COLS), F32)], axis=0)
    grads["ada_w"] = jnp.stack([_mm(silu_cc, e_rows[:, l], ta=True, name="ada_dw") for l in range(DEPTH)])
    ada_b_blk = jnp.sum(e_rows, axis=0)
    dcm_rows = jnp.concatenate([dcm[None], jnp.zeros((PAD_ROWS - 1, DEPTH, ADA_COLS), F32)], axis=0)
    cpart = sum(_mm(dcm_rows[:, l], w["ada_w"][l], tb=True, name="ada_dc")[0] for l in range(DEPTH))

    small_all = _gather([_pack([rep_mine, ada_b_blk, cpart])], "gather_small")[0]
    sm = _unpack(small_all, [rep_mine.shape, (DEPTH, ADA_COLS), (D,)], lead=NDEV)
    rep_full = sm[0].reshape(-1)[:n_rep]
    off = 0
    for n in _REPLICATED:
        k = int(np.prod(w[n].shape))
        grads[n] = rep_full[off:off + k].reshape(w[n].shape)
        off += k
    grads["ada_b"] = sm[1].transpose(1, 0, 2).reshape(DEPTH, N_MODS * D)
    csum = _unpack(_sum_slabs(small_all), [rep_mine.shape, (DEPTH, ADA_COLS), (D,)])[2]

    def dsilu(vv, dd):
        return (jax.vjp(jax.nn.silu, vv)[1](dd)[0],)

    grads["c_ctx"] = _rows(dsilu, [jnp.broadcast_to(w["c_ctx"][None], (SUBLANE, D)), jnp.broadcast_to(csum[None], (SUBLANE, D))],
                           [], [(D, F32)], [], name="dsilu", tm=SUBLANE)[0][0]

    delta, new_m, new_v = {}, {}, {}
    for n in _WEIGHTS:
        delta[n], new_m[n], new_v[n] = _adamw(w[n], grads[n], m[n], v[n])
    loss = lax.psum(loss[0, 0], ("x", "y", "c"))
    return loss, grad_x, grads, delta, new_m, new_v


def kernel(x, c, ctx, c_ctx, ada_w, ada_b, ln1_g, ln1_b, ln2_g, ln2_b, ffn_w_up, ffn_conv_w, ffn_conv_b, ffn_w_down, pool_w, pool_b, pool_scale, attn_w_qkv, attn_w_o, attn_sink, ssm_lambda_re, ssm_lambda_im, ssm_log_dt, ssm_b_re, ssm_b_im, ssm_c_re, ssm_c_im, ssm_d, ssm_w_glu_a, ssm_w_glu_b, gmlp_w_in, gmlp_b_in, gmlp_ln_g, gmlp_ln_b, gmlp_w_s, gmlp_b_s, gmlp_w_out, loss_target, m_c_ctx, m_ada_w, m_ada_b, m_ln1_g, m_ln1_b, m_ln2_g, m_ln2_b, m_ffn_w_up, m_ffn_conv_w, m_ffn_conv_b, m_ffn_w_down, m_pool_w, m_pool_b, m_pool_scale, m_attn_w_qkv, m_attn_w_o, m_attn_sink, m_ssm_lambda_re, m_ssm_lambda_im, m_ssm_log_dt, m_ssm_b_re, m_ssm_b_im, m_ssm_c_re, m_ssm_c_im, m_ssm_d, m_ssm_w_glu_a, m_ssm_w_glu_b, m_gmlp_w_in, m_gmlp_b_in, m_gmlp_ln_g, m_gmlp_ln_b, m_gmlp_w_s, m_gmlp_b_s, m_gmlp_w_out, v_c_ctx, v_ada_w, v_ada_b, v_ln1_g, v_ln1_b, v_ln2_g, v_ln2_b, v_ffn_w_up, v_ffn_conv_w, v_ffn_conv_b, v_ffn_w_down, v_pool_w, v_pool_b, v_pool_scale, v_attn_w_qkv, v_attn_w_o, v_attn_sink, v_ssm_lambda_re, v_ssm_lambda_im, v_ssm_log_dt, v_ssm_b_re, v_ssm_b_im, v_ssm_c_re, v_ssm_c_im, v_ssm_d, v_ssm_w_glu_a, v_ssm_w_glu_b, v_gmlp_w_in, v_gmlp_b_in, v_gmlp_ln_g, v_gmlp_ln_b, v_gmlp_w_s, v_gmlp_b_s, v_gmlp_w_out):
    args = dict(locals())
    w = {n: args[n] for n in _WEIGHTS}
    m = {n: args["m_" + n] for n in _WEIGHTS}
    v = {n: args["v_" + n] for n in _WEIGHTS}
    loss, grad_x, grads, delta, new_m, new_v = _step(x, c, ctx, loss_target, w, m, v)
    return (loss, grad_x, *[grads[n] for n in _WEIGHTS], *[delta[n] for n in _WEIGHTS],
            *[new_m[n] for n in _WEIGHTS], *[new_v[n] for n in _WEIGHTS])
```

```python
import functools
import math

import jax
import jax.numpy as jnp
import numpy as np
from jax import lax
from jax.experimental import pallas as pl
from jax.experimental.pallas import tpu as pltpu

D = 1024
SEQ = 4096
NCTX = 256
T = NCTX + SEQ
DEPTH = 4
NDEV = 8
GRID_W = 64
ALPHA = (2.0 * DEPTH) ** 0.25
LN_EPS = 1e-5
FFN_H = 2816
HEAD_DIM = 64
NQH, NKVH, GQA = 16, 4, 4
WINDOW = 128
ABLK = 128
NEG_INF = -1e30
ROPE_BASE = 10000.0
POOL_WINDOWS = (2, 4, 8, 16)
SSM_G, SSM_P, SSM_C = 64, 64, 16
GM_HALF = 2048
GM_HEADS = 8
GM_HD = GM_HALF // GM_HEADS
GM_CHUNK = 128
B1, B2, LR, EPS, WD, STEP = 0.9, 0.999, 0.001, 1e-8, 0.01, 10

LANE = 128
SUBLANE = 8
VMEM_LIMIT = 56 * 1024 * 1024
MM_OUT_TILE_BYTES = 6 * 1024 * 1024
TM = 256
NT = T // TM

_MXU = jnp.bfloat16
F32 = jnp.float32


def _pcall(body, **kw):
    return pl.pallas_call(body, **kw)


def _cp(sem):
    return pltpu.CompilerParams(dimension_semantics=sem, vmem_limit_bytes=VMEM_LIMIT)


def _tile(dim, pref, align):
    best = None
    for t in range(align, min(dim, pref) + 1, align):
        if dim % t == 0:
            best = t
    return dim if best is None else best


def _mm(a, b, *, ta=False, tb=False, out_dtype=F32, name):
    if ta:
        a = a.astype(_MXU).T
    M, K = a.shape
    if tb:
        N, K2 = b.shape
    else:
        K2, N = b.shape
    assert K == K2, (a.shape, b.shape, ta, tb)
    tm = _tile(M, 2304, 16)
    tn = _tile(N, max(512, MM_OUT_TILE_BYTES // (4 * tm)), LANE)
    tk = _tile(K, 2304, LANE)
    nk = K // tk
    dims = (((1,), (1,) if tb else (0,)), ((), ()))

    def body(a_ref, b_ref, o_ref, acc_ref):
        k = pl.program_id(2)
        r = lax.dot_general(a_ref[...].astype(_MXU), b_ref[...].astype(_MXU), dims, preferred_element_type=F32)
        if nk == 1:
            o_ref[...] = r.astype(o_ref.dtype)
            return

        @pl.when(k == 0)
        def _():
            acc_ref[...] = r

        @pl.when(k > 0)
        def _():
            acc_ref[...] += r

        @pl.when(k == nk - 1)
        def _():
            o_ref[...] = acc_ref[...].astype(o_ref.dtype)

    a_spec = pl.BlockSpec((tm, tk), lambda j, i, k: (i, k))
    b_spec = pl.BlockSpec((tn, tk), lambda j, i, k: (j, k)) if tb else pl.BlockSpec((tk, tn), lambda j, i, k: (k, j))
    return _pcall(
        body, name=name, grid=(N // tn, M // tm, nk), in_specs=[a_spec, b_spec],
        out_specs=pl.BlockSpec((tm, tn), lambda j, i, k: (i, j)),
        out_shape=jax.ShapeDtypeStruct((M, N), out_dtype),
        scratch_shapes=[pltpu.VMEM((tm, tn), F32)],
        compiler_params=_cp(("parallel", "parallel", "arbitrary")),
    )(a, b)


def _rows(fn, rows, pars, out_rows, out_pars, *, name, tm=TM):
    R = rows[0].shape[0]
    nt = R // tm
    nct = NCTX // tm
    n_r, n_p, n_or, n_op = len(rows), len(pars), len(out_rows), len(out_pars)

    def sel(S):
        if S == 1:
            return lambda i: 0
        return lambda i: jnp.where(i < nct, 0, 1)

    def body(*refs):
        r_in = refs[:n_r]
        p_in = refs[n_r:n_r + n_p]
        r_out = refs[n_r + n_p:n_r + n_p + n_or]
        p_out = refs[n_r + n_p + n_or:]
        i = pl.program_id(0)
        vals = [r[...].astype(F32) for r in r_in] + [p[0] for p in p_in]
        outs = fn(*vals)
        for r, v in zip(r_out, outs[:n_or]):
            r[...] = v.astype(r.dtype)
        for (S, _), r, v in zip(out_pars, p_out, outs[n_or:]):
            first = (i == 0) if S == 1 else jnp.logical_or(i == 0, i == nct)

            @pl.when(first)
            def _():
                r[0] = v

            @pl.when(jnp.logical_not(first))
            def _():
                r[0] += v

    def pspec(shape):
        S = shape[0]
        rest = tuple(shape[1:])
        s = sel(S)
        return pl.BlockSpec((1,) + rest, lambda i: (s(i),) + (0,) * len(rest))

    in_specs = [pl.BlockSpec((tm, r.shape[1]), lambda i: (i, 0)) for r in rows] + [pspec(p.shape) for p in pars]
    out_specs = [pl.BlockSpec((tm, w), lambda i: (i, 0)) for w, _ in out_rows] + [pspec((S,) + tuple(sh)) for S, sh in out_pars]
    out_shape = [jax.ShapeDtypeStruct((R, w), dt) for w, dt in out_rows] + \
                [jax.ShapeDtypeStruct((S,) + tuple(sh), F32) for S, sh in out_pars]
    res = _pcall(body, name=name, grid=(nt,), in_specs=in_specs, out_specs=out_specs, out_shape=out_shape,
                 compiler_params=_cp(("arbitrary",)))(*rows, *pars)
    return res


def _ln(z, g, b):
    mu = jnp.mean(z, axis=-1, keepdims=True)
    var = jnp.mean(jnp.square(z - mu), axis=-1, keepdims=True)
    return (z - mu) * lax.rsqrt(var + LN_EPS) * g + b


def _f1(x, sh, sc):
    return x * (1.0 + sc) + sh


def _f2(x, y, gt, g, b, sh, sc):
    x1 = _ln(ALPHA * x + gt * y, g, b)
    return x1, x1 * (1.0 + sc) + sh


def _f3(x1, f, gt, g, b):
    return _ln(ALPHA * x1 + gt * f, g, b)


def _pre_mixer(x, sh, sc, dtype):
    return _rows(lambda x, sh, sc: (_f1(x, sh, sc),), [x], [sh, sc], [(D, dtype)], [], name="pre_mixer")[0]


def _pre_mixer_bwd(x, dhs, dx_prev, sh, sc):
    n = len(dhs)

    def fn(x, *rest):
        dh = rest[0]
        for t in rest[1:n]:
            dh = dh + t
        dxp, sh, sc = rest[n], rest[n + 1], rest[n + 2]
        _, vjp = jax.vjp(_f1, x, sh, sc)
        dx, dsh, dsc = vjp(dh)
        return dxp + dx, dsh, dsc

    return _rows(fn, [x, *dhs, dx_prev], [sh, sc], [(D, F32)], [(2, (1, D)), (2, (1, D))], name="pre_mixer_bwd")


def _post_mixer(x, y, gt, g, b, sh, sc):
    return _rows(_f2, [x, y], [gt, g, b, sh, sc], [(D, F32), (D, _MXU)], [], name="post_mixer")


def _post_mixer_bwd(x, y, dx1, dh2, gt, g, b, sh, sc):
    def fn(x, y, dx1, dh2a, dh2b, gt, g, b, sh, sc):
        _, vjp = jax.vjp(_f2, x, y, gt, g, b, sh, sc)
        return vjp((dx1, dh2a + dh2b))

    return _rows(fn, [x, y, dx1, *dh2], [gt, g, b, sh, sc], [(D, F32), (D, F32)],
                 [(2, (1, D)), (1, (1, D)), (1, (1, D)), (2, (1, D)), (2, (1, D))], name="post_mixer_bwd")


def _post_ffn(x1, f, gt, g, b):
    return _rows(lambda *a: (_f3(*a),), [x1, f], [gt, g, b], [(D, F32)], [], name="post_ffn")[0]


def _post_ffn_bwd(x1, f, dx2, gt, g, b):
    def fn(x1, f, dx2, gt, g, b):
        _, vjp = jax.vjp(_f3, x1, f, gt, g, b)
        return vjp(dx2)

    return _rows(fn, [x1, f, dx2], [gt, g, b], [(D, F32), (D, _MXU)],
                 [(2, (1, D)), (1, (1, D)), (1, (1, D))], name="post_ffn_bwd")


def _halo_specs(tm, w, col, active=None):
    r8 = tm // SUBLANE
    act = (lambda j, r: r) if active is None else (lambda j, r: jnp.where(active(j), r, 0))
    return [
        pl.BlockSpec((SUBLANE, w), lambda j, i: (act(j, jnp.maximum(i * r8 - 1, 0)), col(j))),
        pl.BlockSpec((tm, w), lambda j, i: (act(j, i), col(j))),
        pl.BlockSpec((SUBLANE, w), lambda j, i: (act(j, jnp.minimum((i + 1) * r8, T // SUBLANE - 1)), col(j))),
    ]


def _seg_flags(i, tm):
    nct = NCTX // tm
    first = jnp.logical_or(i == 0, i == nct)
    last = jnp.logical_or(i == nct - 1, i == T // tm - 1)
    return first, last


def _shift_rows(cur, prev8, next8, first, last):
    tm = cur.shape[0]
    rid = lax.broadcasted_iota(jnp.int32, cur.shape, 0)
    pr = jnp.where(first, 0.0, prev8[SUBLANE - 1:SUBLANE, :])
    nx = jnp.where(last, 0.0, next8[0:1, :])
    up = jnp.where(rid == 0, pr, pltpu.roll(cur, 1, 0))
    dn = jnp.where(rid == tm - 1, nx, pltpu.roll(cur, tm - 1, 0))
    return up, dn


FFN_TC = 1408
FFN_NCT = FFN_H // FFN_TC


def _conv3(cur, prev8, next8, w3, first, last):
    up, dn = _shift_rows(cur, prev8, next8, first, last)
    return up * w3[0:1] + cur * w3[1:2] + dn * w3[2:3], up, dn


def _ffn_mid(a, cw, cb):
    def body(vp, vc, vn, gp, gc, gn, cwv, cwg, cbv, cbg, o_ref):
        first, last = _seg_flags(pl.program_id(1), TM)
        val = _conv3(vc[...], vp[...], vn[...], cwv[...], first, last)[0] + cbv[...]
        gate = _conv3(gc[...], gp[...], gn[...], cwg[...], first, last)[0] + cbg[...]
        o_ref[...] = (val * jax.nn.silu(gate)).astype(o_ref.dtype)

    specs = _halo_specs(TM, FFN_TC, lambda j: j) + _halo_specs(TM, FFN_TC, lambda j: j + FFN_NCT)
    specs += [pl.BlockSpec((3, FFN_TC), lambda j, i: (0, j)), pl.BlockSpec((3, FFN_TC), lambda j, i: (0, j + FFN_NCT)),
              pl.BlockSpec((1, FFN_TC), lambda j, i: (0, j)), pl.BlockSpec((1, FFN_TC), lambda j, i: (0, j + FFN_NCT))]
    return _pcall(body, name="ffn_mid", grid=(FFN_NCT, NT), in_specs=specs,
                  out_specs=pl.BlockSpec((TM, FFN_TC), lambda j, i: (i, j)),
                  out_shape=jax.ShapeDtypeStruct((T, FFN_H), _MXU),
                  compiler_params=_cp(("parallel", "arbitrary")))(a, a, a, a, a, a, cw, cw, cb, cb)


def _ext_rows(p_ref, c_ref, n_ref, first, last):
    p8 = jnp.where(first, 0.0, p_ref[...])
    n8 = jnp.where(last, 0.0, n_ref[...])
    return jnp.concatenate([p8, c_ref[...], n8], axis=0)


def _ffn_mid_bwd(a, du, cw, cb):
    E = TM + 2 * SUBLANE
    ctr = slice(SUBLANE, SUBLANE + TM)

    def body(vp, vc, vn, gp, gc, gn, dp, dc, dn_, cwv, cwg, cbv, cbg, dav_ref, dag_ref, dcwv, dcwg, dcbv, dcbg):
        i = pl.program_id(1)
        first, last = _seg_flags(i, TM)
        ev = _ext_rows(vp, vc, vn, first, last)
        eg = _ext_rows(gp, gc, gn, first, last)
        edu = _ext_rows(dp, dc, dn_, first, last)
        wv, wg = cwv[...], cwg[...]
        vup, vdn = pltpu.roll(ev, 1, 0), pltpu.roll(ev, E - 1, 0)
        gup, gdn = pltpu.roll(eg, 1, 0), pltpu.roll(eg, E - 1, 0)
        val = vup * wv[0:1] + ev * wv[1:2] + vdn * wv[2:3] + cbv[...]
        gate = gup * wg[0:1] + eg * wg[1:2] + gdn * wg[2:3] + cbg[...]
        sg = jax.nn.sigmoid(gate)
        dval = edu * (gate * sg)
        dgate = edu * val * (sg * (1.0 + gate * (1.0 - sg)))

        def conv_t(d, w3):
            return (pltpu.roll(d, E - 1, 0) * w3[0:1] + d * w3[1:2] + pltpu.roll(d, 1, 0) * w3[2:3])[ctr]

        dav_ref[...] = conv_t(dval, wv).astype(dav_ref.dtype)
        dag_ref[...] = conv_t(dgate, wg).astype(dag_ref.dtype)

        def acc(ref, v):
            @pl.when(i == 0)
            def _():
                ref[...] = v

            @pl.when(i > 0)
            def _():
                ref[...] += v

        for dref, d, up, cur, dn, bref in ((dcwv, dval[ctr], vup[ctr], ev[ctr], vdn[ctr], dcbv),
                                           (dcwg, dgate[ctr], gup[ctr], eg[ctr], gdn[ctr], dcbg)):
            acc(dref, jnp.concatenate([jnp.sum(d * up, 0, keepdims=True), jnp.sum(d * cur, 0, keepdims=True),
                                       jnp.sum(d * dn, 0, keepdims=True)], axis=0))
            acc(bref, jnp.sum(d, 0, keepdims=True))

    specs = _halo_specs(TM, FFN_TC, lambda j: j) + _halo_specs(TM, FFN_TC, lambda j: j + FFN_NCT) + _halo_specs(TM, FFN_TC, lambda j: j)
    specs += [pl.BlockSpec((3, FFN_TC), lambda j, i: (0, j)), pl.BlockSpec((3, FFN_TC), lambda j, i: (0, j + FFN_NCT)),
              pl.BlockSpec((1, FFN_TC), lambda j, i: (0, j)), pl.BlockSpec((1, FFN_TC), lambda j, i: (0, j + FFN_NCT))]
    out_specs = [pl.BlockSpec((TM, FFN_TC), lambda j, i: (i, j)), pl.BlockSpec((TM, FFN_TC), lambda j, i: (i, j)),
                 pl.BlockSpec((3, FFN_TC), lambda j, i: (0, j)), pl.BlockSpec((3, FFN_TC), lambda j, i: (0, j)),
                 pl.BlockSpec((1, FFN_TC), lambda j, i: (0, j)), pl.BlockSpec((1, FFN_TC), lambda j, i: (0, j))]
    out_shape = [jax.ShapeDtypeStruct((T, FFN_H), _MXU)] * 2 + [jax.ShapeDtypeStruct((3, FFN_H), F32)] * 2 + \
                [jax.ShapeDtypeStruct((1, FFN_H), F32)] * 2
    dav, dag, dcwv, dcwg, dcbv, dcbg = _pcall(
        body, name="ffn_mid_bwd", grid=(FFN_NCT, NT), in_specs=specs, out_specs=out_specs, out_shape=out_shape,
        compiler_params=_cp(("parallel", "arbitrary")))(a, a, a, a, a, a, du, du, du, cw, cw, cb, cb)
    return dav, dag, jnp.concatenate([dcwv, dcwg], axis=1), jnp.concatenate([dcbv, dcbg], axis=1)


def _ffn_fwd(h2, w_up, cw, cb, w_down):
    a = _mm(h2, w_up, name="ffn_up")
    u = _ffn_mid(a, cw, cb)
    f = _mm(u, w_down, name="ffn_down")
    return f, (a, u)


def _ffn_bwd(df, h2, a, u, w_up, cw, cb, w_down):
    dw_down = _mm(u, df, ta=True, name="ffn_down_dw")
    du = _mm(df, w_down, tb=True, name="ffn_down_dx")
    dav, dag, dcw, dcb = _ffn_mid_bwd(a, du, cw, cb)
    dw_up = jnp.concatenate([_mm(h2, dav, ta=True, name="ffn_up_dw"), _mm(h2, dag, ta=True, name="ffn_up_dw")], axis=1)
    dh2 = [_mm(dav, w_up[:, :FFN_H], tb=True, name="ffn_up_dx"), _mm(dag, w_up[:, FFN_H:], tb=True, name="ffn_up_dx")]
    return dh2, dw_up, dcw, dcb, dw_down


def _winsum(e, lo, hi):
    n = e.shape[0]
    acc = None
    for o in range(lo, hi + 1):
        t = e if o == 0 else pltpu.roll(e, (-o) % n, 0)
        acc = t if acc is None else acc + t
    return acc


def _pool_cnt(i, w, rows, off):
    nct = NCTX // TM
    seg_len = jnp.where(i < nct, NCTX, SEQ)
    seg_tile = jnp.where(i < nct, i, i - nct)
    pos = lax.broadcasted_iota(jnp.int32, (rows, 1), 0) - off + seg_tile * TM
    lo = jnp.clip(pos - w // 2, 0, seg_len)
    hi = jnp.clip(pos - w // 2 + w, 0, seg_len)
    return jnp.maximum(hi - lo, 1).astype(F32)


def _pool_fwd(h, pw, pb, ps):
    def body(hp, hc, hn, w_ref, b_ref, s_ref, o_ref):
        i = pl.program_id(1)
        first, last = _seg_flags(i, TM)
        e = _ext_rows(hp, hc, hn, first, last)
        outs = []
        for g, w in enumerate(POOL_WINDOWS):
            sl = slice(256 * g, 256 * (g + 1))
            eg = e[:, sl]
            mean = _winsum(eg, -(w // 2), w // 2 - 1)[SUBLANE:SUBLANE + TM] / _pool_cnt(i, w, TM, 0)
            mixed = mean - hc[:, sl]
            outs.append(jnp.dot(mixed.astype(_MXU), w_ref[g].astype(_MXU), preferred_element_type=F32))
        o_ref[...] = (jnp.concatenate(outs, axis=1) + b_ref[...]) * s_ref[...]

    full = lambda *s: pl.BlockSpec(s, lambda j, i: (0,) * len(s))
    return _pcall(body, name="pool_fwd", grid=(1, NT), in_specs=_halo_specs(TM, D, lambda j: 0) + [full(4, 256, 256), full(1, D), full(1, D)],
                  out_specs=pl.BlockSpec((TM, D), lambda j, i: (i, 0)), out_shape=jax.ShapeDtypeStruct((T, D), F32),
                  compiler_params=_cp(("parallel", "arbitrary")))(h, h, h, pw, pb, ps)


def _pool_bwd(h, dy, pw, pb, ps):
    E = TM + 2 * SUBLANE

    def body(hp, hc, hn, dp, dc, dn, w_ref, b_ref, s_ref, dh_ref, dw_ref, db_ref, ds_ref):
        i = pl.program_id(1)
        first, last = _seg_flags(i, TM)
        e = _ext_rows(hp, hc, hn, first, last)
        de = _ext_rows(dp, dc, dn, first, last)
        dys = de * s_ref[...]
        dhs, pre = [], []
        for g, w in enumerate(POOL_WINDOWS):
            sl = slice(256 * g, 256 * (g + 1))
            wg = w_ref[g].astype(_MXU)
            dyg = dys[:, sl].astype(_MXU)
            dmix = lax.dot_general(dyg, wg, (((1,), (1,)), ((), ())), preferred_element_type=F32)
            q = dmix / _pool_cnt(i, w, E, SUBLANE)
            dhs.append(_winsum(q, -(w // 2) + 1, w // 2)[SUBLANE:SUBLANE + TM] - dmix[SUBLANE:SUBLANE + TM])
            mean = _winsum(e[:, sl], -(w // 2), w // 2 - 1)[SUBLANE:SUBLANE + TM] / _pool_cnt(i, w, TM, 0)
            mixed = (mean - hc[:, sl]).astype(_MXU)
            pre.append(jnp.dot(mixed, wg, preferred_element_type=F32))
            dwg = lax.dot_general(mixed, dyg[SUBLANE:SUBLANE + TM], (((0,), (0,)), ((), ())), preferred_element_type=F32)

            @pl.when(i == 0)
            def _():
                dw_ref[g] = dwg

            @pl.when(i > 0)
            def _():
                dw_ref[g] += dwg
        dh_ref[...] = jnp.concatenate(dhs, axis=1)
        db = jnp.sum(dys[SUBLANE:SUBLANE + TM], 0, keepdims=True)
        ds = jnp.sum(dc[...] * (jnp.concatenate(pre, axis=1) + b_ref[...]), 0, keepdims=True)

        @pl.when(i == 0)
        def _():
            db_ref[...] = db
            ds_ref[...] = ds

        @pl.when(i > 0)
        def _():
            db_ref[...] += db
            ds_ref[...] += ds

    full = lambda *s: pl.BlockSpec(s, lambda j, i: (0,) * len(s))
    specs = _halo_specs(TM, D, lambda j: 0) + _halo_specs(TM, D, lambda j: 0) + [full(4, 256, 256), full(1, D), full(1, D)]
    return _pcall(body, name="pool_bwd", grid=(1, NT), in_specs=specs,
                  out_specs=[pl.BlockSpec((TM, D), lambda j, i: (i, 0)), full(4, 256, 256), full(1, D), full(1, D)],
                  out_shape=[jax.ShapeDtypeStruct((T, D), F32), jax.ShapeDtypeStruct((4, 256, 256), F32),
                             jax.ShapeDtypeStruct((1, D), F32), jax.ShapeDtypeStruct((1, D), F32)],
                  compiler_params=_cp(("parallel", "arbitrary")))(h, h, h, dy, dy, dy, pw, pb, ps)


def _rope_tables():
    half = HEAD_DIM // 4
    t = jnp.arange(SEQ)
    freqs = ROPE_BASE ** (-jnp.arange(half, dtype=F32) / half)
    ang_r = (t // GRID_W).astype(F32)[:, None] * freqs[None, :]
    ang_c = (t % GRID_W).astype(F32)[:, None] * freqs[None, :]
    cos = jnp.concatenate([jnp.cos(ang_r), jnp.cos(ang_r), jnp.cos(ang_c), jnp.cos(ang_c)], axis=1)
    sin = jnp.concatenate([-jnp.sin(ang_r), jnp.sin(ang_r), -jnp.sin(ang_c), jnp.sin(ang_c)], axis=1)
    cos = jnp.concatenate([jnp.ones((NCTX, HEAD_DIM), F32), cos], axis=0)
    sin = jnp.concatenate([jnp.zeros((NCTX, HEAD_DIM), F32), sin], axis=0)
    return jnp.tile(cos, (1, 2)), jnp.tile(sin, (1, 2))


QK_W = (NQH + NKVH) * HEAD_DIM
QKV_W = QK_W + NKVH * HEAD_DIM


def _rope(x, cos, sin, sign):
    def body(x_ref, c_ref, s_ref, o_ref):
        c = c_ref[...]
        s = s_ref[...] * sign
        lane = lax.broadcasted_iota(jnp.int32, (TM, LANE), 1)
        lo = (lane % 32) < 16
        for k in range(QK_W // LANE):
            xk = x_ref[:, LANE * k:LANE * (k + 1)]
            partner = jnp.where(lo, pltpu.roll(xk, LANE - 16, 1), pltpu.roll(xk, 16, 1))
            o_ref[:, LANE * k:LANE * (k + 1)] = (xk * c + partner * s).astype(o_ref.dtype)
        o_ref[:, QK_W:] = x_ref[:, QK_W:].astype(o_ref.dtype)

    return _pcall(body, name="rope", grid=(NT,),
                  in_specs=[pl.BlockSpec((TM, QKV_W), lambda i: (i, 0)), pl.BlockSpec((TM, LANE), lambda i: (i, 0)),
                            pl.BlockSpec((TM, LANE), lambda i: (i, 0))],
                  out_specs=pl.BlockSpec((TM, QKV_W), lambda i: (i, 0)),
                  out_shape=jax.ShapeDtypeStruct((T, QKV_W), _MXU), compiler_params=_cp(("parallel",)))(x, cos, sin)


NQB = T // ABLK
KPAD = T + 2 * ABLK
NKEY = NCTX + 3 * ABLK


def _stack_heads(ref):
    return jnp.concatenate([ref[g] for g in range(GQA)], axis=0)


def _sink_rows(s_ref):
    return jnp.concatenate([jnp.broadcast_to(s_ref[g], (ABLK, 1)) for g in range(GQA)], axis=0)


def _attn_mask(i):
    r = lax.broadcasted_iota(jnp.int32, (GQA * ABLK, NKEY), 0) % ABLK
    c = lax.broadcasted_iota(jnp.int32, (GQA * ABLK, NKEY), 1)
    n = i - NCTX // ABLK
    kpos = (n - 1) * ABLK + (c - NCTX)
    qpos = n * ABLK + r
    loc = (c >= NCTX) & (jnp.abs(kpos - qpos) <= WINDOW) & (kpos >= 0) & (kpos < SEQ) & (n >= 0)
    return (c < NCTX) | loc


def _attn_specs():
    qs = pl.BlockSpec((GQA, ABLK, HEAD_DIM), lambda h, i: (h, i, 0))
    kc = pl.BlockSpec((1, NCTX, HEAD_DIM), lambda h, i: (h, 0, 0))
    kl = [pl.BlockSpec((1, ABLK, HEAD_DIM), functools.partial(lambda h, i, d: (h, i + d, 0), d=d)) for d in range(3)]
    sk = pl.BlockSpec((GQA, 1, 1), lambda h, i: (h, 0, 0))
    return qs, kc, kl, sk


def _attn_fwd(q, k, v, sink):
    scale = HEAD_DIM ** -0.5

    def body(q_ref, kc, k0, k1, k2, vc, v0, v1, v2, s_ref, o_ref, l_ref):
        valid = _attn_mask(pl.program_id(1))
        kk = jnp.concatenate([kc[0], k0[0], k1[0], k2[0]], axis=0)
        vv = jnp.concatenate([vc[0], v0[0], v1[0], v2[0]], axis=0)
        s = lax.dot_general(_stack_heads(q_ref), kk, (((1,), (1,)), ((), ())), preferred_element_type=F32) * scale
        s = jnp.where(valid, s, NEG_INF)
        sk = _sink_rows(s_ref)
        m = jnp.maximum(jnp.max(s, axis=-1, keepdims=True), sk)
        p = jnp.exp(s - m)
        l = jnp.sum(p, axis=-1, keepdims=True) + jnp.exp(sk - m)
        o = jnp.dot((p / l).astype(_MXU), vv, preferred_element_type=F32).astype(o_ref.dtype)
        lse = m + jnp.log(l)
        for g in range(GQA):
            o_ref[g] = o[ABLK * g:ABLK * (g + 1)]
            l_ref[g] = lse[ABLK * g:ABLK * (g + 1)]

    qs, kc, kl, sk = _attn_specs()
    return _pcall(body, name="attn_fwd", grid=(NKVH, NQB), in_specs=[qs, kc, *kl, kc, *kl, sk],
                  out_specs=[qs, pl.BlockSpec((GQA, ABLK, 1), lambda h, i: (h, i, 0))],
                  out_shape=[jax.ShapeDtypeStruct((NQH, T, HEAD_DIM), _MXU), jax.ShapeDtypeStruct((NQH, T, 1), F32)],
                  compiler_params=_cp(("parallel", "arbitrary")))(q, k, k, k, k, v, v, v, v, sink)


def _attn_bwd(q, k, v, sink, lse, do):
    scale = HEAD_DIM ** -0.5

    def body(q_ref, kc, k0, k1, k2, vc, v0, v1, v2, s_ref, l_ref, do_ref, dq_ref, dk_ref, dv_ref, ds_ref):
        i = pl.program_id(1)

        @pl.when(i == 0)
        def _():
            dk_ref[...] = jnp.zeros_like(dk_ref)
            dv_ref[...] = jnp.zeros_like(dv_ref)
            ds_ref[...] = jnp.zeros_like(ds_ref)

        valid = _attn_mask(i)
        kk = jnp.concatenate([kc[0], k0[0], k1[0], k2[0]], axis=0)
        vv = jnp.concatenate([vc[0], v0[0], v1[0], v2[0]], axis=0)
        qst = _stack_heads(q_ref)
        dos = _stack_heads(do_ref)
        lse = _stack_heads(l_ref)
        s = lax.dot_general(qst, kk, (((1,), (1,)), ((), ())), preferred_element_type=F32) * scale
        s = jnp.where(valid, s, NEG_INF)
        p = jnp.exp(s - lse)
        psink = jnp.exp(_sink_rows(s_ref) - lse)
        dp = lax.dot_general(dos, vv, (((1,), (1,)), ((), ())), preferred_element_type=F32)
        delta = jnp.sum(p * dp, axis=-1, keepdims=True)
        ds = p * (dp - delta)
        dsk = psink * delta
        dsq = (ds * scale).astype(_MXU)
        dq = jnp.dot(dsq, kk, preferred_element_type=F32)
        for g in range(GQA):
            dq_ref[g] = dq[ABLK * g:ABLK * (g + 1)]
            ds_ref[g] += -jnp.sum(dsk[ABLK * g:ABLK * (g + 1)], axis=0, keepdims=True)
        dkk = lax.dot_general(dsq, qst, (((0,), (0,)), ((), ())), preferred_element_type=F32)
        dvv = lax.dot_general(p.astype(_MXU), dos, (((0,), (0,)), ((), ())), preferred_element_type=F32)
        loc = pl.ds(pl.multiple_of(i * ABLK, ABLK), 3 * ABLK)
        dk_ref[0, 0:NCTX, :] += dkk[:NCTX]
        dv_ref[0, 0:NCTX, :] += dvv[:NCTX]
        dk_ref[0, loc, :] += dkk[NCTX:]
        dv_ref[0, loc, :] += dvv[NCTX:]

    qs, kc, kl, sk = _attn_specs()
    ls = pl.BlockSpec((GQA, ABLK, 1), lambda h, i: (h, i, 0))
    kfull = pl.BlockSpec((1, KPAD, HEAD_DIM), lambda h, i: (h, 0, 0))
    return _pcall(body, name="attn_bwd", grid=(NKVH, NQB), in_specs=[qs, kc, *kl, kc, *kl, sk, ls, qs],
                  out_specs=[qs, kfull, kfull, sk],
                  out_shape=[jax.ShapeDtypeStruct((NQH, T, HEAD_DIM), F32), jax.ShapeDtypeStruct((NKVH, KPAD, HEAD_DIM), F32),
                             jax.ShapeDtypeStruct((NKVH, KPAD, HEAD_DIM), F32), jax.ShapeDtypeStruct((NQH, 1, 1), F32)],
                  compiler_params=_cp(("parallel", "arbitrary")))(q, k, k, k, k, v, v, v, v, sink, lse, do)


def _split_heads(x, nh):
    return x.reshape(T, nh, HEAD_DIM).transpose(1, 0, 2)


def _merge_heads(x):
    return x.transpose(1, 0, 2).reshape(T, -1)


def _pad_keys(x):
    z = jnp.zeros((x.shape[0], ABLK, HEAD_DIM), x.dtype)
    return jnp.concatenate([x[:, :NCTX], z, x[:, NCTX:], z], axis=1)


def _unpad_keys(x):
    return jnp.concatenate([x[:, :NCTX], x[:, NCTX + ABLK:NCTX + ABLK + SEQ]], axis=1)


def _attn_mixer_fwd(h, w_qkv, w_o, sink):
    cos, sin = _rope_tables()
    qkv = _rope(_mm(h, w_qkv, name="attn_qkv"), cos, sin, 1.0)
    q = _split_heads(qkv[:, :NQH * HEAD_DIM], NQH)
    k = _pad_keys(_split_heads(qkv[:, NQH * HEAD_DIM:QK_W], NKVH))
    v = _pad_keys(_split_heads(qkv[:, QK_W:], NKVH))
    sk = sink.reshape(NQH, 1, 1)
    o, lse = _attn_fwd(q, k, v, sk)
    om = _merge_heads(o)
    y = _mm(om, w_o, name="attn_out")
    return y, (q, k, v, sk, lse, om)


def _attn_mixer_bwd(dy, h, saved, w_qkv, w_o):
    q, k, v, sk, lse, om = saved
    cos, sin = _rope_tables()
    dyb = dy.astype(_MXU)
    dw_o = _mm(om, dyb, ta=True, name="attn_out_dw")
    do = _split_heads(_mm(dyb, w_o, tb=True, out_dtype=_MXU, name="attn_out_dx"), NQH)
    dq, dk, dv, dsk = _attn_bwd(q, k, v, sk, lse, do)
    dqkv = jnp.concatenate([_merge_heads(dq), _merge_heads(_unpad_keys(dk)), _merge_heads(_unpad_keys(dv))], axis=1)
    dqkv = _rope(dqkv, cos, sin, -1.0)
    dw_qkv = _mm(h, dqkv, ta=True, name="attn_qkv_dw")
    dh = _mm(dqkv, w_qkv, tb=True, name="attn_qkv_dx")
    return dh, dw_qkv, dw_o, dsk.reshape(1, NQH)


SSM_S = SSM_G * SSM_P
SSM_SL = SSM_S // LANE
SSM_TS = 128
SSM_NTS = T // SSM_TS
SSM_NCT = NCTX // SSM_TS
SSM_JB = 4
SSM_NTR = 4
SSM_TR = T // SSM_NTR


def _proj3d(u, w_re, w_im):
    def body(u_ref, wr_ref, wi_ref, or_ref, oi_ref):
        for hlf in range(2):
            ub = u_ref[:, LANE * hlf:LANE * (hlf + 1)].astype(_MXU)
            for w_ref, o_ref in ((wr_ref, or_ref), (wi_ref, oi_ref)):
                r = jnp.dot(ub, w_ref[hlf].astype(_MXU), preferred_element_type=F32)
                for q in range(4):
                    o_ref[:, 4 * hlf + q, :] = r[:, LANE * q:LANE * (q + 1)]

    ws = pl.BlockSpec((2, LANE, 512), lambda i, j: (j, 0, 0))
    os_ = pl.BlockSpec((SSM_TR, 8, LANE), lambda i, j: (i, j, 0))
    return _pcall(body, name="ssm_proj", grid=(SSM_NTR, SSM_JB), in_specs=[pl.BlockSpec((SSM_TR, 2 * LANE), lambda i, j: (i, j)), ws, ws],
                  out_specs=[os_, os_], out_shape=[jax.ShapeDtypeStruct((T, SSM_SL, LANE), F32)] * 2,
                  compiler_params=_cp(("parallel", "parallel")))(u, w_re, w_im)


def _readout(s_re, s_im, w_re, w_im):
    def body(sr_ref, si_ref, wr_ref, wi_ref, o_ref):
        for hlf in range(2):
            acc = None
            for s_ref, w_ref in ((sr_ref, wr_ref), (si_ref, wi_ref)):
                x = jnp.concatenate([s_ref[:, 4 * hlf + q, :] for q in range(4)], axis=1).astype(_MXU)
                r = jnp.dot(x, w_ref[hlf].astype(_MXU), preferred_element_type=F32)
                acc = r if acc is None else acc + r
            o_ref[:, LANE * hlf:LANE * (hlf + 1)] = acc

    ss = pl.BlockSpec((SSM_TR, 8, LANE), lambda i, j: (i, j, 0))
    ws = pl.BlockSpec((2, 512, LANE), lambda i, j: (j, 0, 0))
    return _pcall(body, name="ssm_readout", grid=(SSM_NTR, SSM_JB), in_specs=[ss, ss, ws, ws],
                  out_specs=pl.BlockSpec((SSM_TR, 2 * LANE), lambda i, j: (i, j)), out_shape=jax.ShapeDtypeStruct((T, D), F32),
                  compiler_params=_cp(("parallel", "parallel")))(s_re, s_im, w_re, w_im)


def _outer3d(s_re, s_im, y):
    def body(sr_ref, si_ref, y_ref, dr_ref, di_ref):
        i = pl.program_id(1)
        for hlf in range(2):
            yb = y_ref[:, LANE * hlf:LANE * (hlf + 1)].astype(_MXU)
            for s_ref, d_ref in ((sr_ref, dr_ref), (si_ref, di_ref)):
                x = jnp.concatenate([s_ref[:, 4 * hlf + q, :] for q in range(4)], axis=1).astype(_MXU)
                r = lax.dot_general(yb, x, (((0,), (0,)), ((), ())), preferred_element_type=F32)

                @pl.when(i == 0)
                def _():
                    d_ref[hlf] = r

                @pl.when(i > 0)
                def _():
                    d_ref[hlf] += r

    ss = pl.BlockSpec((SSM_TR, 8, LANE), lambda j, i: (i, j, 0))
    ds = pl.BlockSpec((2, LANE, 512), lambda j, i: (j, 0, 0))
    return _pcall(body, name="ssm_outer", grid=(SSM_JB, SSM_NTR), in_specs=[ss, ss, pl.BlockSpec((SSM_TR, 2 * LANE), lambda j, i: (i, j))],
                  out_specs=[ds, ds], out_shape=[jax.ShapeDtypeStruct((8, LANE, 512), F32)] * 2,
                  compiler_params=_cp(("parallel", "arbitrary")))(s_re, s_im, y)


def _scan_order(order):
    n, c = SSM_NTS, SSM_NCT
    if order == "fwd":
        return (lambda i: i), False
    if order == "fwd_adj":
        return (lambda i: n - 1 - i), True
    if order == "rev":
        return (lambda i: jnp.where(i < c, c - 1 - i, n + c - 1 - i)), True
    if order == "rev_adj":
        return (lambda i: jnp.where(i < n - c, i + c, i - (n - c))), False
    raise ValueError(order)


def _scan(b_re, b_im, lam_re, lam_im, order):
    tile, down = _scan_order(order)

    def body(br_ref, bi_ref, lr_ref, li_ref, sr_ref, si_ref, cr, ci):
        @pl.when(pl.program_id(0) == 0)
        def _():
            cr[...] = jnp.zeros_like(cr)
            ci[...] = jnp.zeros_like(ci)

        lr = lr_ref[...]
        li = li_ref[...]

        def step(n, c):
            t = SSM_TS - 1 - n if down else n
            sr, si = c
            nr = lr * sr - li * si + br_ref[t]
            ni = lr * si + li * sr + bi_ref[t]
            sr_ref[t] = nr
            si_ref[t] = ni
            return nr, ni

        sr, si = lax.fori_loop(0, SSM_TS, step, (cr[...], ci[...]))
        cr[...] = sr
        ci[...] = si

    bs = pl.BlockSpec((SSM_TS, SSM_SL, LANE), lambda i: (tile(i), 0, 0))
    ps = pl.BlockSpec((SSM_SL, LANE), lambda i: (0, 0))
    return _pcall(body, name="ssm_scan_" + order, grid=(SSM_NTS,), in_specs=[bs, bs, ps, ps], out_specs=[bs, bs],
                  out_shape=[jax.ShapeDtypeStruct((T, SSM_SL, LANE), F32)] * 2,
                  scratch_shapes=[pltpu.VMEM((SSM_SL, LANE), F32)] * 2, compiler_params=_cp(("arbitrary",)))(b_re, b_im, lam_re, lam_im)


def _scan_adj(g_re, g_im, s_re, s_im, lam_re, lam_im, order):
    tile, down = _scan_order(order)

    def body(gr_ref, gi_ref, sr_ref, si_ref, lr_ref, li_ref, ar_ref, ai_ref, dlr_ref, dli_ref, cr, ci):
        @pl.when(pl.program_id(0) == 0)
        def _():
            cr[...] = jnp.zeros_like(cr)
            ci[...] = jnp.zeros_like(ci)
            dlr_ref[...] = jnp.zeros_like(dlr_ref)
            dli_ref[...] = jnp.zeros_like(dli_ref)

        lr = lr_ref[...]
        li = li_ref[...]

        def step(n, c):
            t = SSM_TS - 1 - n if down else n
            ar, ai, dr, di = c
            sr = sr_ref[t]
            si = si_ref[t]
            dr = dr + ar * sr + ai * si
            di = di + ai * sr - ar * si
            nr = gr_ref[t] + lr * ar + li * ai
            ni = gi_ref[t] + lr * ai - li * ar
            ar_ref[t] = nr
            ai_ref[t] = ni
            return nr, ni, dr, di

        ar, ai, dr, di = lax.fori_loop(0, SSM_TS, step, (cr[...], ci[...], dlr_ref[...], dli_ref[...]))
        cr[...] = ar
        ci[...] = ai
        dlr_ref[...] = dr
        dli_ref[...] = di

    bs = pl.BlockSpec((SSM_TS, SSM_SL, LANE), lambda i: (tile(i), 0, 0))
    ps = pl.BlockSpec((SSM_SL, LANE), lambda i: (0, 0))
    return _pcall(body, name="ssm_scan_" + order, grid=(SSM_NTS,), in_specs=[bs, bs, bs, bs, ps, ps], out_specs=[bs, bs, ps, ps],
                  out_shape=[jax.ShapeDtypeStruct((T, SSM_SL, LANE), F32)] * 2 + [jax.ShapeDtypeStruct((SSM_SL, LANE), F32)] * 2,
                  scratch_shapes=[pltpu.VMEM((SSM_SL, LANE), F32)] * 2,
                  compiler_params=_cp(("arbitrary",)))(g_re, g_im, s_re, s_im, lam_re, lam_im)


def _block_diag(x):
    x4 = x.reshape(8, 8, SSM_P, SSM_C)
    return jnp.einsum("jgpc,gh->jgphc", x4, jnp.eye(8, dtype=x.dtype)).reshape(8, 8 * SSM_P, 8 * SSM_C)


def _ssm_prep(lam_re, lam_im, log_dt, b_re, b_im, c_re, c_im):
    lam = lax.complex(lam_re, lam_im)
    dt = jnp.exp(log_dt)[:, None]
    lam_bar = jnp.exp(lam * dt)
    b_bar = ((lam_bar - 1.0) / lam)[..., None] * lax.complex(b_re, b_im)
    return (jnp.real(lam_bar).reshape(SSM_SL, LANE), jnp.imag(lam_bar).reshape(SSM_SL, LANE),
            _block_diag(jnp.real(b_bar)), _block_diag(jnp.imag(b_bar)),
            _block_diag(c_re.transpose(0, 2, 1)), _block_diag(-c_im.transpose(0, 2, 1)))


def _ssm_glue(h, yf, yr, d):
    return jax.nn.gelu(d * h + yf + yr)


def _glu(ga, gb):
    return ga * jax.nn.sigmoid(gb)


def _ssm_mixer_fwd(h, sp, w_a, w_b):
    lam_re, lam_im, log_dt, b_re, b_im, c_re, c_im, d_skip = sp
    ys, saved = [], []
    for di, order in enumerate(("fwd", "rev")):
        lr, li, wb_r, wb_i, wc_r, wc_i = _ssm_prep(lam_re[di], lam_im[di], log_dt[di], b_re[di], b_im[di], c_re[di], c_im[di])
        bu_r, bu_i = _proj3d(h, wb_r.transpose(0, 2, 1), wb_i.transpose(0, 2, 1))
        s_r, s_i = _scan(bu_r, bu_i, lr, li, order)
        ys.append(_readout(s_r, s_i, wc_r, wc_i))
        saved.append((s_r, s_i))
    g = _rows(lambda *a: (_ssm_glue(*a),), [h, ys[0], ys[1]], [d_skip], [(D, _MXU)], [], name="ssm_glue")[0]
    ga = _mm(g, w_a, name="ssm_glu_a")
    gb = _mm(g, w_b, name="ssm_glu_b")
    y = _rows(lambda *a: (_glu(*a),), [ga, gb], [], [(D, F32)], [], name="ssm_glu")[0]
    return y, (ys, saved, g, ga, gb)


def _ssm_mixer_bwd(dy, h, saved_all, sp, w_a, w_b):
    lam_re, lam_im, log_dt, b_re, b_im, c_re, c_im, d_skip = sp
    ys, saved, g, ga, gb = saved_all

    def glu_bwd(ga, gb, dy):
        _, vjp = jax.vjp(_glu, ga, gb)
        return vjp(dy)

    dga, dgb = _rows(glu_bwd, [ga, gb, dy], [], [(D, _MXU), (D, _MXU)], [], name="ssm_glu_bwd")
    dw_a = _mm(g, dga, ta=True, name="ssm_glu_a_dw")
    dw_b = _mm(g, dgb, ta=True, name="ssm_glu_b_dw")
    dg_a = _mm(dga, w_a, tb=True, name="ssm_glu_a_dx")
    dg_b = _mm(dgb, w_b, tb=True, name="ssm_glu_b_dx")

    def glue_bwd(h, yf, yr, dg_a, dg_b, d):
        _, vjp = jax.vjp(_ssm_glue, h, yf, yr, d)
        dh, dyl, _, dd = vjp(dg_a + dg_b)
        return dh, dyl, dd

    dh0, dyl, dd = _rows(glue_bwd, [h, ys[0], ys[1], dg_a, dg_b], [d_skip], [(D, F32), (D, F32)], [(1, (1, D))], name="ssm_glue_bwd")
    dhs = [dh0]
    dparams = []
    for di, (order, adj) in enumerate((("fwd", "fwd_adj"), ("rev", "rev_adj"))):
        args = (lam_re[di], lam_im[di], log_dt[di], b_re[di], b_im[di], c_re[di], c_im[di])
        (lr, li, wb_r, wb_i, wc_r, wc_i), prep_vjp = jax.vjp(_ssm_prep, *args)
        s_r, s_i = saved[di]
        dwc_r, dwc_i = _outer3d(s_r, s_i, dyl)
        g_r, g_i = _proj3d(dyl, wc_r.transpose(0, 2, 1), wc_i.transpose(0, 2, 1))
        a_r, a_i, dlr, dli = _scan_adj(g_r, g_i, s_r, s_i, lr, li, adj)
        dwb_r, dwb_i = _outer3d(a_r, a_i, h)
        dhs.append(_readout(a_r, a_i, wb_r, wb_i))
        dparams.append(prep_vjp((dlr, dli) + tuple(d.transpose(0, 2, 1) for d in (dwb_r, dwb_i, dwc_r, dwc_i))))
    dsp = [jnp.stack([dparams[0][k], dparams[1][k]], axis=0) for k in range(7)]
    return dhs, dsp, dd, dw_a, dw_b


NCH = T // GM_CHUNK


def _gm_specs():
    full = lambda *s: pl.BlockSpec(s, lambda i: (0,) * len(s))
    zu = pl.BlockSpec((GM_CHUNK, GM_HALF), lambda i: (i, 0))
    zv = pl.BlockSpec((GM_CHUNK, GM_HALF), lambda i: (i, 1))
    pars = [full(1, GM_HALF), pl.BlockSpec((1, GM_HALF), lambda i: (0, 1)), full(1, GM_HALF), full(1, GM_HALF),
            full(GM_HEADS, GM_CHUNK, GM_CHUNK), full(GM_HEADS, GM_CHUNK, 1)]
    return zu, zv, pars, full


def _gm_forward(zu_ref, zv_ref, bu_ref, bv_ref, g_ref, b_ref, ws_ref, bs_ref):
    u = jax.nn.gelu(zu_ref[...] + bu_ref[...])
    zv = jax.nn.gelu(zv_ref[...] + bv_ref[...])
    mu = jnp.mean(zv, axis=-1, keepdims=True)
    zc = zv - mu
    rstd = lax.rsqrt(jnp.mean(jnp.square(zc), axis=-1, keepdims=True) + LN_EPS)
    vhat = zc * rstd
    v = (vhat * g_ref[...] + b_ref[...]).astype(_MXU)
    gates = [jnp.dot(ws_ref[hd].astype(_MXU), v[:, GM_HD * hd:GM_HD * (hd + 1)], preferred_element_type=F32) + bs_ref[hd]
             for hd in range(GM_HEADS)]
    return u, vhat, rstd, v, jnp.concatenate(gates, axis=1)


def _gmlp_chunk(zp, b_in, ln_g, ln_b, w_s, b_s):
    def body(zu_ref, zv_ref, bu_ref, bv_ref, g_ref, b_ref, ws_ref, bs_ref, o_ref):
        u, _, _, _, gate = _gm_forward(zu_ref, zv_ref, bu_ref, bv_ref, g_ref, b_ref, ws_ref, bs_ref)
        o_ref[...] = (u * gate).astype(o_ref.dtype)

    zu, zv, pars, _ = _gm_specs()
    return _pcall(body, name="gmlp_chunk", grid=(NCH,), in_specs=[zu, zv, *pars], out_specs=zu,
                  out_shape=jax.ShapeDtypeStruct((T, GM_HALF), _MXU),
                  compiler_params=_cp(("parallel",)))(zp, zp, b_in, b_in, ln_g, ln_b, w_s, b_s)


def _gmlp_chunk_bwd(zp, do, b_in, ln_g, ln_b, w_s, b_s):
    def body(zu_ref, zv_ref, do_ref, bu_ref, bv_ref, g_ref, b_ref, ws_ref, bs_ref,
             dzu_ref, dzv_ref, dbu_ref, dbv_ref, dg_ref, db_ref, dws_ref, dbs_ref):
        i = pl.program_id(0)

        def acc(ref, val, idx=None):
            @pl.when(i == 0)
            def _():
                if idx is None:
                    ref[...] = val
                else:
                    ref[idx] = val

            @pl.when(i > 0)
            def _():
                if idx is None:
                    ref[...] += val
                else:
                    ref[idx] += val

        u, vhat, rstd, v, gate = _gm_forward(zu_ref, zv_ref, bu_ref, bv_ref, g_ref, b_ref, ws_ref, bs_ref)
        do = do_ref[...]
        du = do * gate
        dgate = do * u
        dvs = []
        for hd in range(GM_HEADS):
            sl = slice(GM_HD * hd, GM_HD * (hd + 1))
            dgh = dgate[:, sl]
            dghb = dgh.astype(_MXU)
            dvs.append(lax.dot_general(ws_ref[hd].astype(_MXU), dghb, (((0,), (0,)), ((), ())), preferred_element_type=F32))
            acc(dws_ref, lax.dot_general(dghb, v[:, sl], (((1,), (1,)), ((), ())), preferred_element_type=F32), hd)
            acc(dbs_ref, jnp.sum(dgh, axis=1, keepdims=True), hd)
        dv = jnp.concatenate(dvs, axis=1)
        acc(dg_ref, jnp.sum(dv * vhat, axis=0, keepdims=True))
        acc(db_ref, jnp.sum(dv, axis=0, keepdims=True))
        dvh = dv * g_ref[...]
        dzv = rstd * (dvh - jnp.mean(dvh, axis=-1, keepdims=True) - vhat * jnp.mean(dvh * vhat, axis=-1, keepdims=True))
        dpu = jax.vjp(jax.nn.gelu, zu_ref[...] + bu_ref[...])[1](du)[0]
        dpv = jax.vjp(jax.nn.gelu, zv_ref[...] + bv_ref[...])[1](dzv)[0]
        dzu_ref[...] = dpu.astype(dzu_ref.dtype)
        dzv_ref[...] = dpv.astype(dzv_ref.dtype)
        acc(dbu_ref, jnp.sum(dpu, axis=0, keepdims=True))
        acc(dbv_ref, jnp.sum(dpv, axis=0, keepdims=True))

    zu, zv, pars, full = _gm_specs()
    out_specs = [zu, zu, full(1, GM_HALF), full(1, GM_HALF), full(1, GM_HALF), full(1, GM_HALF),
                 full(GM_HEADS, GM_CHUNK, GM_CHUNK), full(GM_HEADS, GM_CHUNK, 1)]
    out_shape = [jax.ShapeDtypeStruct((T, GM_HALF), _MXU)] * 2 + [jax.ShapeDtypeStruct((1, GM_HALF), F32)] * 4 + \
                [jax.ShapeDtypeStruct((GM_HEADS, GM_CHUNK, GM_CHUNK), F32), jax.ShapeDtypeStruct((GM_HEADS, GM_CHUNK, 1), F32)]
    dzu, dzv, dbu, dbv, dg, db, dws, dbs = _pcall(
        body, name="gmlp_chunk_bwd", grid=(NCH,), in_specs=[zu, zv, zu, *pars], out_specs=out_specs, out_shape=out_shape,
        compiler_params=_cp(("arbitrary",)))(zp, zp, do, b_in, b_in, ln_g, ln_b, w_s, b_s)
    return jnp.concatenate([dzu, dzv], axis=1), jnp.concatenate([dbu, dbv], axis=1), dg, db, dws, dbs


def _gmlp_mixer_fwd(h, w_in, b_in, ln_g, ln_b, w_s, b_s, w_out):
    zp = _mm(h, w_in, name="gmlp_in")
    ug = _gmlp_chunk(zp, b_in, ln_g, ln_b, w_s, b_s[..., None])
    return _mm(ug, w_out, name="gmlp_out"), (zp, ug)


def _gmlp_mixer_bwd(dy, h, saved, w_in, b_in, ln_g, ln_b, w_s, b_s, w_out):
    zp, ug = saved
    dw_out = _mm(ug, dy, ta=True, name="gmlp_out_dw")
    do = _mm(dy, w_out, tb=True, name="gmlp_out_dx")
    dzp, db_in, dg, db, dws, dbs = _gmlp_chunk_bwd(zp, do, b_in, ln_g, ln_b, w_s, b_s[..., None])
    dw_in = _mm(h, dzp, ta=True, name="gmlp_in_dw")
    dh = _mm(dzp, w_in, tb=True, name="gmlp_in_dx")
    return dh, dw_in, db_in, dg, db, dws, dbs[..., 0], dw_out


def _loss_head(x, target):
    nct = NCTX // TM

    def body(x_ref, t_ref, l_ref, dx_ref):
        i = pl.program_id(0)
        err = jnp.where(i >= nct, x_ref[...] - t_ref[...], 0.0)
        dx_ref[...] = err * (1.0 / D)
        part = 0.5 * jnp.sum(jnp.sum(jnp.square(err), axis=-1, keepdims=True) * (1.0 / D), axis=0, keepdims=True)

        @pl.when(i == 0)
        def _():
            l_ref[...] = part

        @pl.when(i > 0)
        def _():
            l_ref[...] += part

    return _pcall(body, name="loss_head", grid=(NT,),
                  in_specs=[pl.BlockSpec((TM, D), lambda i: (i, 0)), pl.BlockSpec((TM, D), lambda i: (jnp.maximum(i - nct, 0), 0))],
                  out_specs=[pl.BlockSpec((1, 1), lambda i: (0, 0)), pl.BlockSpec((TM, D), lambda i: (i, 0))],
                  out_shape=[jax.ShapeDtypeStruct((1, 1), F32), jax.ShapeDtypeStruct((T, D), F32)],
                  compiler_params=_cp(("arbitrary",)))(x, target)


def _as2d(a):
    return a.reshape(-1, a.shape[-1])


def _adamw(w, g, m, v):
    shape = w.shape
    w2, g2, m2, v2 = _as2d(w), _as2d(g), _as2d(m), _as2d(v)
    R, C = w2.shape
    tr = _tile(R, 512, SUBLANE)
    c1 = 1.0 - B1 ** STEP
    c2 = 1.0 - B2 ** STEP

    def body(w_ref, g_ref, m_ref, v_ref, d_ref, nm_ref, nv_ref):
        g = g_ref[...]
        m = B1 * m_ref[...] + (1.0 - B1) * g
        v = B2 * v_ref[...] + (1.0 - B2) * jnp.square(g)
        nm_ref[...] = m
        nv_ref[...] = v
        d_ref[...] = -LR * ((m / c1) / (jnp.sqrt(v / c2) + EPS) + WD * w_ref[...])

    spec = pl.BlockSpec((tr, C), lambda i: (i, 0))
    outs = _pcall(body, name="adamw", grid=(R // tr,), in_specs=[spec] * 4, out_specs=[spec] * 3,
                  out_shape=[jax.ShapeDtypeStruct((R, C), F32)] * 3, compiler_params=_cp(("parallel",)))(w2, g2, m2, v2)
    return tuple(o.reshape(shape) for o in outs)


def _sum_slabs(x):
    n = x.shape[0]
    x = x.reshape(n, -1, x.shape[-1])
    _, R, C = x.shape
    tr = _tile(R, 256, 16)

    def body(x_ref, o_ref):
        acc = x_ref[0].astype(F32)
        for k in range(1, n):
            acc = acc + x_ref[k].astype(F32)
        o_ref[...] = acc

    return _pcall(body, name="sum_slabs", grid=(R // tr,), in_specs=[pl.BlockSpec((n, tr, C), lambda i: (0, i, 0))],
                  out_specs=pl.BlockSpec((tr, C), lambda i: (i, 0)), out_shape=jax.ShapeDtypeStruct((R, C), F32),
                  compiler_params=_cp(("parallel",)))(x)


def _comm_call(body, xs, out_shape, name):
    n = len(xs)
    hbm = pl.BlockSpec(memory_space=pl.ANY)
    return _pcall(body, name=name, in_specs=[hbm] * n, out_specs=[hbm] * n, out_shape=out_shape,
                  scratch_shapes=[pltpu.SemaphoreType.DMA((n, NDEV - 1)), pltpu.SemaphoreType.DMA((n, NDEV - 1)),
                                  pltpu.SemaphoreType.DMA((n,))],
                  compiler_params=pltpu.CompilerParams(has_side_effects=True))(*xs)


def _exchange(xs, name):
    n = len(xs)

    def body(*refs):
        x_refs, o_refs = refs[:n], refs[n:2 * n]
        send_sems, recv_sems, loc_sems = refs[2 * n:]
        mx, my, mc = lax.axis_index("x"), lax.axis_index("y"), lax.axis_index("c")
        me = 4 * mx + 2 * my + mc
        pending = []
        for a in range(n):
            mine = pltpu.make_async_copy(x_refs[a].at[me], o_refs[a].at[me], loc_sems.at[a])
            mine.start()
            pending.append(mine)
            for k in range(1, NDEV):
                px = 1 - mx if k & 4 else mx
                py = 1 - my if k & 2 else my
                pc = 1 - mc if k & 1 else mc
                cp = pltpu.make_async_remote_copy(
                    src_ref=x_refs[a].at[4 * px + 2 * py + pc], dst_ref=o_refs[a].at[me],
                    send_sem=send_sems.at[a, k - 1], recv_sem=recv_sems.at[a, k - 1],
                    device_id=(px, py, pc), device_id_type=pl.DeviceIdType.MESH)
                cp.start()
                pending.append(cp)
        for cp in pending:
            cp.wait()

    return _comm_call(body, xs, [jax.ShapeDtypeStruct(tuple(x.shape), x.dtype) for x in xs], name)


NCHIP = NDEV // 2


def _sibling_exchange(xs, name):
    n = len(xs)

    def body(*refs):
        x_refs, o_refs = refs[:n], refs[n:2 * n]
        send_sems, recv_sems, _ = refs[2 * n:]
        mx, my, mc = lax.axis_index("x"), lax.axis_index("y"), lax.axis_index("c")
        pending = []
        for a in range(n):
            for b in range(NCHIP):
                cp = pltpu.make_async_remote_copy(
                    src_ref=x_refs[a].at[2 * b + (1 - mc)], dst_ref=o_refs[a].at[b],
                    send_sem=send_sems.at[a, b], recv_sem=recv_sems.at[a, b],
                    device_id=(mx, my, 1 - mc), device_id_type=pl.DeviceIdType.MESH)
                cp.start()
                pending.append(cp)
        for cp in pending:
            cp.wait()

    return _comm_call(body, xs, [jax.ShapeDtypeStruct((NCHIP,) + tuple(x.shape[1:]), x.dtype) for x in xs], name)


def _chip_exchange(xs, name):
    n = len(xs)

    def body(*refs):
        x_refs, o_refs = refs[:n], refs[n:2 * n]
        send_sems, recv_sems, loc_sems = refs[2 * n:]
        mx, my, mc = lax.axis_index("x"), lax.axis_index("y"), lax.axis_index("c")
        chip = 2 * mx + my
        pending = []
        for a in range(n):
            mine = pltpu.make_async_copy(x_refs[a].at[chip], o_refs[a].at[chip], loc_sems.at[a])
            mine.start()
            pending.append(mine)
            for k in range(1, NCHIP):
                px = 1 - mx if k & 2 else mx
                py = 1 - my if k & 1 else my
                cp = pltpu.make_async_remote_copy(
                    src_ref=x_refs[a].at[2 * px + py], dst_ref=o_refs[a].at[chip],
                    send_sem=send_sems.at[a, k - 1], recv_sem=recv_sems.at[a, k - 1],
                    device_id=(px, py, mc), device_id_type=pl.DeviceIdType.MESH)
                cp.start()
                pending.append(cp)
        for cp in pending:
            cp.wait()

    return _comm_call(body, xs, [jax.ShapeDtypeStruct(tuple(x.shape), x.dtype) for x in xs], name)


def _pair_sum(x, y):
    _, _, R, C = x.shape
    tr = _tile(R, 128, 16)

    def body(x_ref, y_ref, o_ref):
        mc = lax.axis_index("c")
        mine = jnp.where(mc == 0, x_ref[:, 0].astype(F32), x_ref[:, 1].astype(F32))
        o_ref[...] = (mine + y_ref[...].astype(F32)).astype(o_ref.dtype)

    return _pcall(body, name="pair_sum", grid=(R // tr,),
                  in_specs=[pl.BlockSpec((NCHIP, 2, tr, C), lambda i: (0, 0, i, 0)), pl.BlockSpec((NCHIP, tr, C), lambda i: (0, i, 0))],
                  out_specs=pl.BlockSpec((NCHIP, tr, C), lambda i: (0, i, 0)), out_shape=jax.ShapeDtypeStruct((NCHIP, R, C), x.dtype),
                  compiler_params=_cp(("parallel",)))(x, y)


def _reduce_scatter(xs, name):
    from_sibling = _sibling_exchange(xs, name + "_d2d")
    pair = [_pair_sum(x.reshape(NCHIP, 2, -1, x.shape[-1]), y.reshape(NCHIP, -1, x.shape[-1])) for x, y in zip(xs, from_sibling)]
    got = _chip_exchange(pair, name + "_ici")
    return [_sum_slabs(g).reshape(x.shape[1:]) for g, x in zip(got, xs)]


def _gather(xs, name):
    n = len(xs)

    def body(*refs):
        x_refs, o_refs = refs[:n], refs[n:2 * n]
        send_sems, recv_sems, loc_sems = refs[2 * n:]
        mx, my, mc = lax.axis_index("x"), lax.axis_index("y"), lax.axis_index("c")
        me = 4 * mx + 2 * my + mc
        sibling = (mx, my, 1 - mc)
        chips = [(1 - mx, my), (mx, 1 - my), (1 - mx, 1 - my)]
        slot = lambda px, py, pc: 4 * px + 2 * py + pc

        def copy(a, k, s, to, from_input=False):
            return pltpu.make_async_remote_copy(
                src_ref=x_refs[a] if from_input else o_refs[a].at[s], dst_ref=o_refs[a].at[s],
                send_sem=send_sems.at[a, k], recv_sem=recv_sems.at[a, k], device_id=to, device_id_type=pl.DeviceIdType.MESH)

        sends, mines = [], []
        for a in range(n):
            mine = pltpu.make_async_copy(x_refs[a], o_refs[a].at[me], loc_sems.at[a])
            mine.start()
            mines.append(mine)
            first = [copy(a, 0, me, sibling, True)] + [copy(a, 1 + j, me, (cx, cy, mc), True) for j, (cx, cy) in enumerate(chips)]
            for cp in first:
                cp.start()
            sends += first
        for a in range(n):
            for j, (cx, cy) in enumerate(chips):
                s = slot(cx, cy, mc)
                copy(a, 1 + j, s, sibling).wait_recv()
                passed = copy(a, 4 + j, s, sibling)
                passed.start()
                sends.append(passed)
        for a in range(n):
            copy(a, 0, slot(*sibling), sibling).wait_recv()
            for j, (cx, cy) in enumerate(chips):
                copy(a, 4 + j, slot(cx, cy, 1 - mc), sibling).wait_recv()
        for cp in sends:
            cp.wait_send()
        for mine in mines:
            mine.wait()

    return _comm_call(body, xs, [jax.ShapeDtypeStruct((NDEV,) + tuple(x.shape), x.dtype) for x in xs], name)


SLAB_W = 1024
SLAB_ROWS = 16


def _pack(parts, lead=None):
    if lead is None:
        flat = jnp.concatenate([p.reshape(-1) for p in parts])
        n = flat.shape[0]
        padn = -n % (SLAB_ROWS * SLAB_W)
        return jnp.pad(flat, (0, padn)).reshape(-1, SLAB_W)
    flat = jnp.concatenate([p.reshape(lead, -1) for p in parts], axis=1)
    n = flat.shape[1]
    padn = -n % (SLAB_ROWS * SLAB_W)
    return jnp.pad(flat, ((0, 0), (0, padn))).reshape(lead, -1, SLAB_W)


def _unpack(buf, shapes, lead=None):
    out, off = [], 0
    flat = buf.reshape(-1) if lead is None else buf.reshape(lead, -1)
    for s in shapes:
        n = int(np.prod(s))
        if lead is None:
            out.append(flat[off:off + n].reshape(s))
        else:
            out.append(flat[:, off:off + n].reshape((lead,) + tuple(s)))
        off += n
    return out


def _gathered(blk, ax):
    m = jnp.moveaxis(blk, 0, ax)
    s = list(m.shape)
    return m.reshape(s[:ax] + [s[ax] * s[ax + 1]] + s[ax + 2:])


def _scattered(full, ax):
    s = list(full.shape)
    m = full.reshape(s[:ax] + [NDEV, s[ax] // NDEV] + s[ax + 1:])
    return jnp.moveaxis(m, ax, 0)


_MM_SHARDED = (("ffn_w_up", 2), ("ffn_w_down", 1), ("pool_w", 2), ("attn_w_qkv", 2), ("attn_w_o", 1),
               ("ssm_w_glu_a", 1), ("ssm_w_glu_b", 1), ("gmlp_w_in", 2), ("gmlp_w_out", 1))
_VEC_SHARDED = (("ffn_conv_w", 2), ("ssm_d", 1), ("gmlp_b_in", 1), ("gmlp_ln_g", 1), ("gmlp_ln_b", 1))
_REPLICATED = ("ln1_g", "ln1_b", "ln2_g", "ln2_b", "ffn_conv_b", "pool_b", "pool_scale", "attn_sink",
               "ssm_lambda_re", "ssm_lambda_im", "ssm_log_dt", "ssm_b_re", "ssm_b_im", "ssm_c_re", "ssm_c_im",
               "gmlp_w_s", "gmlp_b_s")
_WEIGHTS = ("c_ctx", "ada_w", "ada_b", "ln1_g", "ln1_b", "ln2_g", "ln2_b", "ffn_w_up", "ffn_conv_w", "ffn_conv_b", "ffn_w_down",
            "pool_w", "pool_b", "pool_scale", "attn_w_qkv", "attn_w_o", "attn_sink", "ssm_lambda_re", "ssm_lambda_im",
            "ssm_log_dt", "ssm_b_re", "ssm_b_im", "ssm_c_re", "ssm_c_im", "ssm_d", "ssm_w_glu_a", "ssm_w_glu_b",
            "gmlp_w_in", "gmlp_b_in", "gmlp_ln_g", "gmlp_ln_b", "gmlp_w_s", "gmlp_b_s", "gmlp_w_out")
N_MODS = 6
ADA_COLS = N_MODS * D // NDEV
PAD_ROWS = 16


def _silu_rows(x):
    return _rows(lambda v: (jax.nn.silu(v),), [x], [], [(x.shape[1], F32)], [], name="silu", tm=x.shape[0])[0]


def _step(x, c, ctx, loss_target, w, m, v):
    mx, my, mc = lax.axis_index("x"), lax.axis_index("y"), lax.axis_index("c")
    me = 4 * mx + 2 * my + mc

    vec_buf = _pack([w[n] for n, _ in _VEC_SHARDED] + [c])
    *mm_parts, vec_all = _gather([w[n].astype(_MXU) for n, _ in _MM_SHARDED] + [vec_buf], "gather_weights")
    vec_parts = _unpack(vec_all, [w[n].shape for n, _ in _VEC_SHARDED] + [c.shape], lead=NDEV)
    full = {n: _gathered(p, ax) for (n, ax), p in zip(_MM_SHARDED, mm_parts)}
    full.update({n: _gathered(p, ax) for (n, ax), p in zip(_VEC_SHARDED, vec_parts[:-1])})
    c_all = vec_parts[-1].reshape(NDEV, D)

    cc = jnp.concatenate([c_all, w["c_ctx"].reshape(1, D), jnp.zeros((PAD_ROWS - NDEV - 1, D), F32)], axis=0)
    silu_cc = _silu_rows(cc)
    ada_b_mine = lax.dynamic_slice(w["ada_b"], (0, me * ADA_COLS), (DEPTH, ADA_COLS))
    mods_mine = jnp.stack([_mm(silu_cc, w["ada_w"][l], name="ada_mods") + ada_b_mine[l][None, :] for l in range(DEPTH)], axis=1)
    per_dev = mods_mine[:NDEV].reshape(NDEV, DEPTH * ADA_COLS)
    cm = jnp.broadcast_to(mods_mine[NDEV].reshape(1, DEPTH * ADA_COLS), (NDEV, DEPTH * ADA_COLS))
    mods_all = _exchange([_pack([per_dev, cm], lead=NDEV)], "scatter_mods")[0]
    got = _unpack(mods_all, [(DEPTH, ADA_COLS), (DEPTH, ADA_COLS)], lead=NDEV)
    mods = got[0].transpose(1, 0, 2).reshape(DEPTH, N_MODS * D)
    cmods = got[1].transpose(1, 0, 2).reshape(DEPTH, N_MODS * D)
    P = [[jnp.stack([cmods[l, k * D:(k + 1) * D], mods[l, k * D:(k + 1) * D]]).reshape(2, 1, D) for k in range(N_MODS)]
         for l in range(DEPTH)]
    row = lambda a, l: a[l].reshape(1, 1, D)

    xs = jnp.concatenate([ctx[0], x[0]], axis=0)
    sp = tuple(w[n][0] for n in ("ssm_lambda_re", "ssm_lambda_im", "ssm_log_dt", "ssm_b_re", "ssm_b_im", "ssm_c_re", "ssm_c_im")) + \
        (full["ssm_d"].reshape(1, 1, D),)
    gm = (full["gmlp_w_in"][0], full["gmlp_b_in"], full["gmlp_ln_g"], full["gmlp_ln_b"], w["gmlp_w_s"][0], w["gmlp_b_s"][0],
          full["gmlp_w_out"][0])
    pool_args = (full["pool_w"][0], w["pool_b"], w["pool_scale"])
    saved = []
    for l in range(DEPTH):
        sh1, sc1, gt1, sh2, sc2, gt2 = P[l]
        h1 = _pre_mixer(xs, sh1, sc1, _MXU if l in (1, 3) else F32)
        if l == 0:
            y, ms = _pool_fwd(h1, *pool_args), None
        elif l == 1:
            y, ms = _attn_mixer_fwd(h1, full["attn_w_qkv"][0], full["attn_w_o"][0], w["attn_sink"])
        elif l == 2:
            y, ms = _ssm_mixer_fwd(h1, sp, full["ssm_w_glu_a"][0], full["ssm_w_glu_b"][0])
        else:
            y, ms = _gmlp_mixer_fwd(h1, *gm)
        x1, h2 = _post_mixer(xs, y, gt1, row(w["ln1_g"], l), row(w["ln1_b"], l), sh2, sc2)
        f, (a, u) = _ffn_fwd(h2, full["ffn_w_up"][l], full["ffn_conv_w"][l], w["ffn_conv_b"][l][None, :], full["ffn_w_down"][l])
        x2 = _post_ffn(x1, f, gt2, row(w["ln2_g"], l), row(w["ln2_b"], l))
        saved.append((xs, h1, y, ms, x1, h2, a, u, f))
        xs = x2
    loss, dxs = _loss_head(xs, loss_target[0])

    g = {n: [None] * DEPTH for n in ("ln1_g", "ln1_b", "ln2_g", "ln2_b", "ffn_w_up", "ffn_conv_w", "ffn_conv_b", "ffn_w_down")}
    dP = [None] * DEPTH
    for l in reversed(range(DEPTH)):
        sh1, sc1, gt1, sh2, sc2, gt2 = P[l]
        x0, h1, y, ms, x1, h2, a, u, f = saved[l]
        dx1, df, dgt2, g["ln2_g"][l], g["ln2_b"][l] = _post_ffn_bwd(x1, f, dxs, gt2, row(w["ln2_g"], l), row(w["ln2_b"], l))
        dh2, g["ffn_w_up"][l], g["ffn_conv_w"][l], g["ffn_conv_b"][l], g["ffn_w_down"][l] = _ffn_bwd(
            df, h2, a, u, full["ffn_w_up"][l], full["ffn_conv_w"][l], w["ffn_conv_b"][l][None, :], full["ffn_w_down"][l])
        dx0, dy, dgt1, g["ln1_g"][l], g["ln1_b"][l], dsh2, dsc2 = _post_mixer_bwd(
            x0, y, dx1, dh2, gt1, row(w["ln1_g"], l), row(w["ln1_b"], l), sh2, sc2)
        if l == 0:
            dh, g["pool_w"], g["pool_b"], g["pool_scale"] = _pool_bwd(h1, dy, *pool_args)
            dhs = [dh]
        elif l == 1:
            dh, g["attn_w_qkv"], g["attn_w_o"], g["attn_sink"] = _attn_mixer_bwd(dy, h1, ms, full["attn_w_qkv"][0], full["attn_w_o"][0])
            dhs = [dh]
        elif l == 2:
            dhs, dsp, dd, g["ssm_w_glu_a"], g["ssm_w_glu_b"] = _ssm_mixer_bwd(dy, h1, ms, sp, full["ssm_w_glu_a"][0], full["ssm_w_glu_b"][0])
            for n, d_ in zip(("ssm_lambda_re", "ssm_lambda_im", "ssm_log_dt", "ssm_b_re", "ssm_b_im", "ssm_c_re", "ssm_c_im"), dsp):
                g[n] = d_
            g["ssm_d"] = dd.reshape(1, D)
        else:
            (dh, g["gmlp_w_in"], g["gmlp_b_in"], g["gmlp_ln_g"], g["gmlp_ln_b"], g["gmlp_w_s"], g["gmlp_b_s"],
             g["gmlp_w_out"]) = _gmlp_mixer_bwd(dy, h1, ms, *gm)
            dhs = [dh]
        dxs, dsh1, dsc1 = _pre_mixer_bwd(x0, dhs, dx0, sh1, sc1)
        dP[l] = (dsh1, dsc1, dgt1, dsh2, dsc2, dgt2)
    grad_x = dxs[NCTX:][None]
    dmods = jnp.stack([jnp.concatenate([p[1, 0] for p in dP[l]]) for l in range(DEPTH)])
    dcmods = jnp.stack([jnp.concatenate([p[0, 0] for p in dP[l]]) for l in range(DEPTH)])

    gfull = {n: (jnp.stack(g[n]) if isinstance(g[n], list) else g[n]) for n in g}
    sharded_names = [n for n, _ in _MM_SHARDED] + [n for n, _ in _VEC_SHARDED]
    sharded_axes = dict(_MM_SHARDED + _VEC_SHARDED)

    def as_param(n, a):
        shard = w[n].shape
        ax = sharded_axes.get(n)
        fs = tuple(s * NDEV if i == ax else s for i, s in enumerate(shard))
        return a.reshape(fs)

    rep = jnp.concatenate([as_param(n, gfull[n]).reshape(-1) for n in _REPLICATED])
    n_rep = rep.shape[0]
    rep = jnp.pad(rep, (0, -n_rep % (NDEV * SLAB_W))).reshape(NDEV, -1)
    by_dev = lambda a: a.reshape(DEPTH, NDEV, ADA_COLS).transpose(1, 0, 2)
    big = [_scattered(as_param(n, gfull[n]), ax).astype(_MXU) for n, ax in _MM_SHARDED]
    parts = [_scattered(as_param(n, gfull[n]), ax) for n, ax in _VEC_SHARDED] + [rep, by_dev(dmods), by_dev(dcmods)]
    grads = {n: r for (n, _), r in zip(_MM_SHARDED, _reduce_scatter(big, "scatter_grads"))}
    grads_in = _exchange([_pack(parts, lead=NDEV)], "scatter_small")[0]
    shapes = [w[n].shape for n, _ in _VEC_SHARDED] + [(rep.shape[1],), (DEPTH, ADA_COLS), (DEPTH, ADA_COLS)]
    red = _unpack(_sum_slabs(grads_in), shapes)
    grads.update({n: r for (n, _), r in zip(_VEC_SHARDED, red)})
    rep_mine, dcm = red[-3], red[-1]
    dm_all = _unpack(grads_in, shapes, lead=NDEV)[-2]

    e_rows = jnp.concatenate([dm_all, dcm[None], jnp.zeros((PAD_ROWS - NDEV - 1, DEPTH, ADA_COLS), F32)], axis=0)
    grads["ada_w"] = jnp.stack([_mm(silu_cc, e_rows[:, l], ta=True, name="ada_dw") for l in range(DEPTH)])
    ada_b_blk = jnp.sum(e_rows, axis=0)
    dcm_rows = jnp.concatenate([dcm[None], jnp.zeros((PAD_ROWS - 1, DEPTH, ADA_COLS), F32)], axis=0)
    cpart = sum(_mm(dcm_rows[:, l], w["ada_w"][l], tb=True, name="ada_dc")[0] for l in range(DEPTH))

    small_all = _gather([_pack([rep_mine, ada_b_blk, cpart])], "gather_small")[0]
    sm = _unpack(small_all, [rep_mine.shape, (DEPTH, ADA_COLS), (D,)], lead=NDEV)
    rep_full = sm[0].reshape(-1)[:n_rep]
    off = 0
    for n in _REPLICATED:
        k = int(np.prod(w[n].shape))
        grads[n] = rep_full[off:off + k].reshape(w[n].shape)
        off += k
    grads["ada_b"] = sm[1].transpose(1, 0, 2).reshape(DEPTH, N_MODS * D)
    csum = _unpack(_sum_slabs(small_all), [rep_mine.shape, (DEPTH, ADA_COLS), (D,)])[2]

    def dsilu(vv, dd):
        return (jax.vjp(jax.nn.silu, vv)[1](dd)[0],)

    grads["c_ctx"] = _rows(dsilu, [jnp.broadcast_to(w["c_ctx"][None], (SUBLANE, D)), jnp.broadcast_to(csum[None], (SUBLANE, D))],
                           [], [(D, F32)], [], name="dsilu", tm=SUBLANE)[0][0]

    delta, new_m, new_v = {}, {}, {}
    for n in _WEIGHTS:
        delta[n], new_m[n], new_v[n] = _adamw(w[n], grads[n], m[n], v[n])
    loss = lax.psum(loss[0, 0], ("x", "y", "c"))
    return loss, grad_x, grads, delta, new_m, new_v


def kernel(x, c, ctx, c_ctx, ada_w, ada_b, ln1_g, ln1_b, ln2_g, ln2_b, ffn_w_up, ffn_conv_w, ffn_conv_b, ffn_w_down, pool_w, pool_b, pool_scale, attn_w_qkv, attn_w_o, attn_sink, ssm_lambda_re, ssm_lambda_im, ssm_log_dt, ssm_b_re, ssm_b_im, ssm_c_re, ssm_c_im, ssm_d, ssm_w_glu_a, ssm_w_glu_b, gmlp_w_in, gmlp_b_in, gmlp_ln_g, gmlp_ln_b, gmlp_w_s, gmlp_b_s, gmlp_w_out, loss_target, m_c_ctx, m_ada_w, m_ada_b, m_ln1_g, m_ln1_b, m_ln2_g, m_ln2_b, m_ffn_w_up, m_ffn_conv_w, m_ffn_conv_b, m_ffn_w_down, m_pool_w, m_pool_b, m_pool_scale, m_attn_w_qkv, m_attn_w_o, m_attn_sink, m_ssm_lambda_re, m_ssm_lambda_im, m_ssm_log_dt, m_ssm_b_re, m_ssm_b_im, m_ssm_c_re, m_ssm_c_im, m_ssm_d, m_ssm_w_glu_a, m_ssm_w_glu_b, m_gmlp_w_in, m_gmlp_b_in, m_gmlp_ln_g, m_gmlp_ln_b, m_gmlp_w_s, m_gmlp_b_s, m_gmlp_w_out, v_c_ctx, v_ada_w, v_ada_b, v_ln1_g, v_ln1_b, v_ln2_g, v_ln2_b, v_ffn_w_up, v_ffn_conv_w, v_ffn_conv_b, v_ffn_w_down, v_pool_w, v_pool_b, v_pool_scale, v_attn_w_qkv, v_attn_w_o, v_attn_sink, v_ssm_lambda_re, v_ssm_lambda_im, v_ssm_log_dt, v_ssm_b_re, v_ssm_b_im, v_ssm_c_re, v_ssm_c_im, v_ssm_d, v_ssm_w_glu_a, v_ssm_w_glu_b, v_gmlp_w_in, v_gmlp_b_in, v_gmlp_ln_g, v_gmlp_ln_b, v_gmlp_w_s, v_gmlp_b_s, v_gmlp_w_out):
    args = dict(locals())
    w = {n: args[n] for n in _WEIGHTS}
    m = {n: args["m_" + n] for n in _WEIGHTS}
    v = {n: args["v_" + n] for n in _WEIGHTS}
    loss, grad_x, grads, delta, new_m, new_v = _step(x, c, ctx, loss_target, w, m, v)
    return (loss, grad_x, *[grads[n] for n in _WEIGHTS], *[delta[n] for n in _WEIGHTS],
            *[new_m[n] for n in _WEIGHTS], *[new_v[n] for n in _WEIGHTS])
```

```python
import functools
import math

import jax
import jax.numpy as jnp
import numpy as np
from jax import lax
from jax.experimental import pallas as pl
from jax.experimental.pallas import tpu as pltpu

D = 1024
SEQ = 4096
NCTX = 256
T = NCTX + SEQ
DEPTH = 4
NDEV = 8
GRID_W = 64
ALPHA = (2.0 * DEPTH) ** 0.25
LN_EPS = 1e-5
FFN_H = 2816
HEAD_DIM = 64
NQH, NKVH, GQA = 16, 4, 4
WINDOW = 128
ABLK = 128
NEG_INF = -1e30
ROPE_BASE = 10000.0
POOL_WINDOWS = (2, 4, 8, 16)
SSM_G, SSM_P, SSM_C = 64, 64, 16
GM_HALF = 2048
GM_HEADS = 8
GM_HD = GM_HALF // GM_HEADS
GM_CHUNK = 128
B1, B2, LR, EPS, WD, STEP = 0.9, 0.999, 0.001, 1e-8, 0.01, 10

LANE = 128
SUBLANE = 8
VMEM_LIMIT = 56 * 1024 * 1024
MM_OUT_TILE_BYTES = 6 * 1024 * 1024
TM = 256
NT = T // TM

_MXU = jnp.bfloat16
F32 = jnp.float32


def _pcall(body, **kw):
    return pl.pallas_call(body, **kw)


def _cp(sem):
    return pltpu.CompilerParams(dimension_semantics=sem, vmem_limit_bytes=VMEM_LIMIT)


def _tile(dim, pref, align):
    best = None
    for t in range(align, min(dim, pref) + 1, align):
        if dim % t == 0:
            best = t
    return dim if best is None else best


def _mm(a, b, *, ta=False, tb=False, out_dtype=F32, name):
    if ta:
        a = a.astype(_MXU).T
    M, K = a.shape
    if tb:
        N, K2 = b.shape
    else:
        K2, N = b.shape
    assert K == K2, (a.shape, b.shape, ta, tb)
    tm = _tile(M, 2304, 16)
    tn = _tile(N, max(512, MM_OUT_TILE_BYTES // (4 * tm)), LANE)
    tk = _tile(K, 2304, LANE)
    nk = K // tk
    dims = (((1,), (1,) if tb else (0,)), ((), ()))

    def body(a_ref, b_ref, o_ref, acc_ref):
        k = pl.program_id(2)
        r = lax.dot_general(a_ref[...].astype(_MXU), b_ref[...].astype(_MXU), dims, preferred_element_type=F32)
        if nk == 1:
            o_ref[...] = r.astype(o_ref.dtype)
            return

        @pl.when(k == 0)
        def _():
            acc_ref[...] = r

        @pl.when(k > 0)
        def _():
            acc_ref[...] += r

        @pl.when(k == nk - 1)
        def _():
            o_ref[...] = acc_ref[...].astype(o_ref.dtype)

    a_spec = pl.BlockSpec((tm, tk), lambda j, i, k: (i, k))
    b_spec = pl.BlockSpec((tn, tk), lambda j, i, k: (j, k)) if tb else pl.BlockSpec((tk, tn), lambda j, i, k: (k, j))
    return _pcall(
        body, name=name, grid=(N // tn, M // tm, nk), in_specs=[a_spec, b_spec],
        out_specs=pl.BlockSpec((tm, tn), lambda j, i, k: (i, j)),
        out_shape=jax.ShapeDtypeStruct((M, N), out_dtype),
        scratch_shapes=[pltpu.VMEM((tm, tn), F32)],
        compiler_params=_cp(("parallel", "parallel", "arbitrary")),
    )(a, b)


def _rows(fn, rows, pars, out_rows, out_pars, *, name, tm=TM):
    R = rows[0].shape[0]
    nt = R // tm
    nct = NCTX // tm
    n_r, n_p, n_or, n_op = len(rows), len(pars), len(out_rows), len(out_pars)

    def sel(S):
        if S == 1:
            return lambda i: 0
        return lambda i: jnp.where(i < nct, 0, 1)

    def body(*refs):
        r_in = refs[:n_r]
        p_in = refs[n_r:n_r + n_p]
        r_out = refs[n_r + n_p:n_r + n_p + n_or]
        p_out = refs[n_r + n_p + n_or:]
        i = pl.program_id(0)
        vals = [r[...].astype(F32) for r in r_in] + [p[0] for p in p_in]
        outs = fn(*vals)
        for r, v in zip(r_out, outs[:n_or]):
            r[...] = v.astype(r.dtype)
        for (S, _), r, v in zip(out_pars, p_out, outs[n_or:]):
            first = (i == 0) if S == 1 else jnp.logical_or(i == 0, i == nct)

            @pl.when(first)
            def _():
                r[0] = v

            @pl.when(jnp.logical_not(first))
            def _():
                r[0] += v

    def pspec(shape):
        S = shape[0]
        rest = tuple(shape[1:])
        s = sel(S)
        return pl.BlockSpec((1,) + rest, lambda i: (s(i),) + (0,) * len(rest))

    in_specs = [pl.BlockSpec((tm, r.shape[1]), lambda i: (i, 0)) for r in rows] + [pspec(p.shape) for p in pars]
    out_specs = [pl.BlockSpec((tm, w), lambda i: (i, 0)) for w, _ in out_rows] + [pspec((S,) + tuple(sh)) for S, sh in out_pars]
    out_shape = [jax.ShapeDtypeStruct((R, w), dt) for w, dt in out_rows] + \
                [jax.ShapeDtypeStruct((S,) + tuple(sh), F32) for S, sh in out_pars]
    res = _pcall(body, name=name, grid=(nt,), in_specs=in_specs, out_specs=out_specs, out_shape=out_shape,
                 compiler_params=_cp(("arbitrary",)))(*rows, *pars)
    return res


def _ln(z, g, b):
    mu = jnp.mean(z, axis=-1, keepdims=True)
    var = jnp.mean(jnp.square(z - mu), axis=-1, keepdims=True)
    return (z - mu) * lax.rsqrt(var + LN_EPS) * g + b


def _f1(x, sh, sc):
    return x * (1.0 + sc) + sh


def _f2(x, y, gt, g, b, sh, sc):
    x1 = _ln(ALPHA * x + gt * y, g, b)
    return x1, x1 * (1.0 + sc) + sh


def _f3(x1, f, gt, g, b):
    return _ln(ALPHA * x1 + gt * f, g, b)


def _pre_mixer(x, sh, sc, dtype):
    return _rows(lambda x, sh, sc: (_f1(x, sh, sc),), [x], [sh, sc], [(D, dtype)], [], name="pre_mixer")[0]


def _pre_mixer_bwd(x, dhs, dx_prev, sh, sc):
    n = len(dhs)

    def fn(x, *rest):
        dh = rest[0]
        for t in rest[1:n]:
            dh = dh + t
        dxp, sh, sc = rest[n], rest[n + 1], rest[n + 2]
        _, vjp = jax.vjp(_f1, x, sh, sc)
        dx, dsh, dsc = vjp(dh)
        return dxp + dx, dsh, dsc

    return _rows(fn, [x, *dhs, dx_prev], [sh, sc], [(D, F32)], [(2, (1, D)), (2, (1, D))], name="pre_mixer_bwd")


def _post_mixer(x, y, gt, g, b, sh, sc):
    return _rows(_f2, [x, y], [gt, g, b, sh, sc], [(D, F32), (D, _MXU)], [], name="post_mixer")


def _post_mixer_bwd(x, y, dx1, dh2, gt, g, b, sh, sc):
    def fn(x, y, dx1, dh2a, dh2b, gt, g, b, sh, sc):
        _, vjp = jax.vjp(_f2, x, y, gt, g, b, sh, sc)
        return vjp((dx1, dh2a + dh2b))

    return _rows(fn, [x, y, dx1, *dh2], [gt, g, b, sh, sc], [(D, F32), (D, F32)],
                 [(2, (1, D)), (1, (1, D)), (1, (1, D)), (2, (1, D)), (2, (1, D))], name="post_mixer_bwd")


def _post_ffn(x1, f, gt, g, b):
    return _rows(lambda *a: (_f3(*a),), [x1, f], [gt, g, b], [(D, F32)], [], name="post_ffn")[0]


def _post_ffn_bwd(x1, f, dx2, gt, g, b):
    def fn(x1, f, dx2, gt, g, b):
        _, vjp = jax.vjp(_f3, x1, f, gt, g, b)
        return vjp(dx2)

    return _rows(fn, [x1, f, dx2], [gt, g, b], [(D, F32), (D, _MXU)],
                 [(2, (1, D)), (1, (1, D)), (1, (1, D))], name="post_ffn_bwd")


def _halo_specs(tm, w, col, active=None):
    r8 = tm // SUBLANE
    act = (lambda j, r: r) if active is None else (lambda j, r: jnp.where(active(j), r, 0))
    return [
        pl.BlockSpec((SUBLANE, w), lambda j, i: (act(j, jnp.maximum(i * r8 - 1, 0)), col(j))),
        pl.BlockSpec((tm, w), lambda j, i: (act(j, i), col(j))),
        pl.BlockSpec((SUBLANE, w), lambda j, i: (act(j, jnp.minimum((i + 1) * r8, T // SUBLANE - 1)), col(j))),
    ]


def _seg_flags(i, tm):
    nct = NCTX // tm
    first = jnp.logical_or(i == 0, i == nct)
    last = jnp.logical_or(i == nct - 1, i == T // tm - 1)
    return first, last


def _shift_rows(cur, prev8, next8, first, last):
    tm = cur.shape[0]
    rid = lax.broadcasted_iota(jnp.int32, cur.shape, 0)
    pr = jnp.where(first, 0.0, prev8[SUBLANE - 1:SUBLANE, :])
    nx = jnp.where(last, 0.0, next8[0:1, :])
    up = jnp.where(rid == 0, pr, pltpu.roll(cur, 1, 0))
    dn = jnp.where(rid == tm - 1, nx, pltpu.roll(cur, tm - 1, 0))
    return up, dn


FFN_TC = 1408
FFN_NCT = FFN_H // FFN_TC


def _conv3(cur, prev8, next8, w3, first, last):
    up, dn = _shift_rows(cur, prev8, next8, first, last)
    return up * w3[0:1] + cur * w3[1:2] + dn * w3[2:3], up, dn


def _ffn_mid(a, cw, cb):
    def body(vp, vc, vn, gp, gc, gn, cwv, cwg, cbv, cbg, o_ref):
        first, last = _seg_flags(pl.program_id(1), TM)
        val = _conv3(vc[...], vp[...], vn[...], cwv[...], first, last)[0] + cbv[...]
        gate = _conv3(gc[...], gp[...], gn[...], cwg[...], first, last)[0] + cbg[...]
        o_ref[...] = (val * jax.nn.silu(gate)).astype(o_ref.dtype)

    specs = _halo_specs(TM, FFN_TC, lambda j: j) + _halo_specs(TM, FFN_TC, lambda j: j + FFN_NCT)
    specs += [pl.BlockSpec((3, FFN_TC), lambda j, i: (0, j)), pl.BlockSpec((3, FFN_TC), lambda j, i: (0, j + FFN_NCT)),
              pl.BlockSpec((1, FFN_TC), lambda j, i: (0, j)), pl.BlockSpec((1, FFN_TC), lambda j, i: (0, j + FFN_NCT))]
    return _pcall(body, name="ffn_mid", grid=(FFN_NCT, NT), in_specs=specs,
                  out_specs=pl.BlockSpec((TM, FFN_TC), lambda j, i: (i, j)),
                  out_shape=jax.ShapeDtypeStruct((T, FFN_H), _MXU),
                  compiler_params=_cp(("parallel", "arbitrary")))(a, a, a, a, a, a, cw, cw, cb, cb)


def _ext_rows(p_ref, c_ref, n_ref, first, last):
    p8 = jnp.where(first, 0.0, p_ref[...])
    n8 = jnp.where(last, 0.0, n_ref[...])
    return jnp.concatenate([p8, c_ref[...], n8], axis=0)


def _ffn_mid_bwd(a, du, cw, cb):
    E = TM + 2 * SUBLANE
    ctr = slice(SUBLANE, SUBLANE + TM)

    def body(vp, vc, vn, gp, gc, gn, dp, dc, dn_, cwv, cwg, cbv, cbg, dav_ref, dag_ref, dcwv, dcwg, dcbv, dcbg):
        i = pl.program_id(1)
        first, last = _seg_flags(i, TM)
        ev = _ext_rows(vp, vc, vn, first, last)
        eg = _ext_rows(gp, gc, gn, first, last)
        edu = _ext_rows(dp, dc, dn_, first, last)
        wv, wg = cwv[...], cwg[...]
        vup, vdn = pltpu.roll(ev, 1, 0), pltpu.roll(ev, E - 1, 0)
        gup, gdn = pltpu.roll(eg, 1, 0), pltpu.roll(eg, E - 1, 0)
        val = vup * wv[0:1] + ev * wv[1:2] + vdn * wv[2:3] + cbv[...]
        gate = gup * wg[0:1] + eg * wg[1:2] + gdn * wg[2:3] + cbg[...]
        sg = jax.nn.sigmoid(gate)
        dval = edu * (gate * sg)
        dgate = edu * val * (sg * (1.0 + gate * (1.0 - sg)))

        def conv_t(d, w3):
            return (pltpu.roll(d, E - 1, 0) * w3[0:1] + d * w3[1:2] + pltpu.roll(d, 1, 0) * w3[2:3])[ctr]

        dav_ref[...] = conv_t(dval, wv).astype(dav_ref.dtype)
        dag_ref[...] = conv_t(dgate, wg).astype(dag_ref.dtype)

        def acc(ref, v):
            @pl.when(i == 0)
            def _():
                ref[...] = v

            @pl.when(i > 0)
            def _():
                ref[...] += v

        for dref, d, up, cur, dn, bref in ((dcwv, dval[ctr], vup[ctr], ev[ctr], vdn[ctr], dcbv),
                                           (dcwg, dgate[ctr], gup[ctr], eg[ctr], gdn[ctr], dcbg)):
            acc(dref, jnp.concatenate([jnp.sum(d * up, 0, keepdims=True), jnp.sum(d * cur, 0, keepdims=True),
                                       jnp.sum(d * dn, 0, keepdims=True)], axis=0))
            acc(bref, jnp.sum(d, 0, keepdims=True))

    specs = _halo_specs(TM, FFN_TC, lambda j: j) + _halo_specs(TM, FFN_TC, lambda j: j + FFN_NCT) + _halo_specs(TM, FFN_TC, lambda j: j)
    specs += [pl.BlockSpec((3, FFN_TC), lambda j, i: (0, j)), pl.BlockSpec((3, FFN_TC), lambda j, i: (0, j + FFN_NCT)),
              pl.BlockSpec((1, FFN_TC), lambda j, i: (0, j)), pl.BlockSpec((1, FFN_TC), lambda j, i: (0, j + FFN_NCT))]
    out_specs = [pl.BlockSpec((TM, FFN_TC), lambda j, i: (i, j)), pl.BlockSpec((TM, FFN_TC), lambda j, i: (i, j)),
                 pl.BlockSpec((3, FFN_TC), lambda j, i: (0, j)), pl.BlockSpec((3, FFN_TC), lambda j, i: (0, j)),
                 pl.BlockSpec((1, FFN_TC), lambda j, i: (0, j)), pl.BlockSpec((1, FFN_TC), lambda j, i: (0, j))]
    out_shape = [jax.ShapeDtypeStruct((T, FFN_H), _MXU)] * 2 + [jax.ShapeDtypeStruct((3, FFN_H), F32)] * 2 + \
                [jax.ShapeDtypeStruct((1, FFN_H), F32)] * 2
    dav, dag, dcwv, dcwg, dcbv, dcbg = _pcall(
        body, name="ffn_mid_bwd", grid=(FFN_NCT, NT), in_specs=specs, out_specs=out_specs, out_shape=out_shape,
        compiler_params=_cp(("parallel", "arbitrary")))(a, a, a, a, a, a, du, du, du, cw, cw, cb, cb)
    return dav, dag, jnp.concatenate([dcwv, dcwg], axis=1), jnp.concatenate([dcbv, dcbg], axis=1)


def _ffn_fwd(h2, w_up, cw, cb, w_down):
    a = _mm(h2, w_up, name="ffn_up")
    u = _ffn_mid(a, cw, cb)
    f = _mm(u, w_down, name="ffn_down")
    return f, (a, u)


def _ffn_bwd(df, h2, a, u, w_up, cw, cb, w_down):
    dw_down = _mm(u, df, ta=True, name="ffn_down_dw")
    du = _mm(df, w_down, tb=True, name="ffn_down_dx")
    dav, dag, dcw, dcb = _ffn_mid_bwd(a, du, cw, cb)
    dw_up = jnp.concatenate([_mm(h2, dav, ta=True, name="ffn_up_dw"), _mm(h2, dag, ta=True, name="ffn_up_dw")], axis=1)
    dh2 = [_mm(dav, w_up[:, :FFN_H], tb=True, name="ffn_up_dx"), _mm(dag, w_up[:, FFN_H:], tb=True, name="ffn_up_dx")]
    return dh2, dw_up, dcw, dcb, dw_down


def _winsum(e, lo, hi):
    n = e.shape[0]
    acc = None
    for o in range(lo, hi + 1):
        t = e if o == 0 else pltpu.roll(e, (-o) % n, 0)
        acc = t if acc is None else acc + t
    return acc


def _pool_cnt(i, w, rows, off):
    nct = NCTX // TM
    seg_len = jnp.where(i < nct, NCTX, SEQ)
    seg_tile = jnp.where(i < nct, i, i - nct)
    pos = lax.broadcasted_iota(jnp.int32, (rows, 1), 0) - off + seg_tile * TM
    lo = jnp.clip(pos - w // 2, 0, seg_len)
    hi = jnp.clip(pos - w // 2 + w, 0, seg_len)
    return jnp.maximum(hi - lo, 1).astype(F32)


def _pool_fwd(h, pw, pb, ps):
    def body(hp, hc, hn, w_ref, b_ref, s_ref, o_ref):
        i = pl.program_id(1)
        first, last = _seg_flags(i, TM)
        e = _ext_rows(hp, hc, hn, first, last)
        outs = []
        for g, w in enumerate(POOL_WINDOWS):
            sl = slice(256 * g, 256 * (g + 1))
            eg = e[:, sl]
            mean = _winsum(eg, -(w // 2), w // 2 - 1)[SUBLANE:SUBLANE + TM] / _pool_cnt(i, w, TM, 0)
            mixed = mean - hc[:, sl]
            outs.append(jnp.dot(mixed.astype(_MXU), w_ref[g].astype(_MXU), preferred_element_type=F32))
        o_ref[...] = (jnp.concatenate(outs, axis=1) + b_ref[...]) * s_ref[...]

    full = lambda *s: pl.BlockSpec(s, lambda j, i: (0,) * len(s))
    return _pcall(body, name="pool_fwd", grid=(1, NT), in_specs=_halo_specs(TM, D, lambda j: 0) + [full(4, 256, 256), full(1, D), full(1, D)],
                  out_specs=pl.BlockSpec((TM, D), lambda j, i: (i, 0)), out_shape=jax.ShapeDtypeStruct((T, D), F32),
                  compiler_params=_cp(("parallel", "arbitrary")))(h, h, h, pw, pb, ps)


def _pool_bwd(h, dy, pw, pb, ps):
    E = TM + 2 * SUBLANE

    def body(hp, hc, hn, dp, dc, dn, w_ref, b_ref, s_ref, dh_ref, dw_ref, db_ref, ds_ref):
        i = pl.program_id(1)
        first, last = _seg_flags(i, TM)
        e = _ext_rows(hp, hc, hn, first, last)
        de = _ext_rows(dp, dc, dn, first, last)
        dys = de * s_ref[...]
        dhs, pre = [], []
        for g, w in enumerate(POOL_WINDOWS):
            sl = slice(256 * g, 256 * (g + 1))
            wg = w_ref[g].astype(_MXU)
            dyg = dys[:, sl].astype(_MXU)
            dmix = lax.dot_general(dyg, wg, (((1,), (1,)), ((), ())), preferred_element_type=F32)
            q = dmix / _pool_cnt(i, w, E, SUBLANE)
            dhs.append(_winsum(q, -(w // 2) + 1, w // 2)[SUBLANE:SUBLANE + TM] - dmix[SUBLANE:SUBLANE + TM])
            mean = _winsum(e[:, sl], -(w // 2), w // 2 - 1)[SUBLANE:SUBLANE + TM] / _pool_cnt(i, w, TM, 0)
            mixed = (mean - hc[:, sl]).astype(_MXU)
            pre.append(jnp.dot(mixed, wg, preferred_element_type=F32))
            dwg = lax.dot_general(mixed, dyg[SUBLANE:SUBLANE + TM], (((0,), (0,)), ((), ())), preferred_element_type=F32)

            @pl.when(i == 0)
            def _():
                dw_ref[g] = dwg

            @pl.when(i > 0)
            def _():
                dw_ref[g] += dwg
        dh_ref[...] = jnp.concatenate(dhs, axis=1)
        db = jnp.sum(dys[SUBLANE:SUBLANE + TM], 0, keepdims=True)
        ds = jnp.sum(dc[...] * (jnp.concatenate(pre, axis=1) + b_ref[...]), 0, keepdims=True)

        @pl.when(i == 0)
        def _():
            db_ref[...] = db
            ds_ref[...] = ds

        @pl.when(i > 0)
        def _():
            db_ref[...] += db
            ds_ref[...] += ds

    full = lambda *s: pl.BlockSpec(s, lambda j, i: (0,) * len(s))
    specs = _halo_specs(TM, D, lambda j: 0) + _halo_specs(TM, D, lambda j: 0) + [full(4, 256, 256), full(1, D), full(1, D)]
    return _pcall(body, name="pool_bwd", grid=(1, NT), in_specs=specs,
                  out_specs=[pl.BlockSpec((TM, D), lambda j, i: (i, 0)), full(4, 256, 256), full(1, D), full(1, D)],
                  out_shape=[jax.ShapeDtypeStruct((T, D), F32), jax.ShapeDtypeStruct((4, 256, 256), F32),
                             jax.ShapeDtypeStruct((1, D), F32), jax.ShapeDtypeStruct((1, D), F32)],
                  compiler_params=_cp(("parallel", "arbitrary")))(h, h, h, dy, dy, dy, pw, pb, ps)


def _rope_tables():
    half = HEAD_DIM // 4
    t = jnp.arange(SEQ)
    freqs = ROPE_BASE ** (-jnp.arange(half, dtype=F32) / half)
    ang_r = (t // GRID_W).astype(F32)[:, None] * freqs[None, :]
    ang_c = (t % GRID_W).astype(F32)[:, None] * freqs[None, :]
    cos = jnp.concatenate([jnp.cos(ang_r), jnp.cos(ang_r), jnp.cos(ang_c), jnp.cos(ang_c)], axis=1)
    sin = jnp.concatenate([-jnp.sin(ang_r), jnp.sin(ang_r), -jnp.sin(ang_c), jnp.sin(ang_c)], axis=1)
    cos = jnp.concatenate([jnp.ones((NCTX, HEAD_DIM), F32), cos], axis=0)
    sin = jnp.concatenate([jnp.zeros((NCTX, HEAD_DIM), F32), sin], axis=0)
    return jnp.tile(cos, (1, 2)), jnp.tile(sin, (1, 2))


QK_W = (NQH + NKVH) * HEAD_DIM
QKV_W = QK_W + NKVH * HEAD_DIM


def _rope(x, cos, sin, sign):
    def body(x_ref, c_ref, s_ref, o_ref):
        c = c_ref[...]
        s = s_ref[...] * sign
        lane = lax.broadcasted_iota(jnp.int32, (TM, LANE), 1)
        lo = (lane % 32) < 16
        for k in range(QK_W // LANE):
            xk = x_ref[:, LANE * k:LANE * (k + 1)]
            partner = jnp.where(lo, pltpu.roll(xk, LANE - 16, 1), pltpu.roll(xk, 16, 1))
            o_ref[:, LANE * k:LANE * (k + 1)] = (xk * c + partner * s).astype(o_ref.dtype)
        o_ref[:, QK_W:] = x_ref[:, QK_W:].astype(o_ref.dtype)

    return _pcall(body, name="rope", grid=(NT,),
                  in_specs=[pl.BlockSpec((TM, QKV_W), lambda i: (i, 0)), pl.BlockSpec((TM, LANE), lambda i: (i, 0)),
                            pl.BlockSpec((TM, LANE), lambda i: (i, 0))],
                  out_specs=pl.BlockSpec((TM, QKV_W), lambda i: (i, 0)),
                  out_shape=jax.ShapeDtypeStruct((T, QKV_W), _MXU), compiler_params=_cp(("parallel",)))(x, cos, sin)


NQB = T // ABLK
KPAD = T + 2 * ABLK
NKEY = NCTX + 3 * ABLK


def _stack_heads(ref):
    return jnp.concatenate([ref[g] for g in range(GQA)], axis=0)


def _sink_rows(s_ref):
    return jnp.concatenate([jnp.broadcast_to(s_ref[g], (ABLK, 1)) for g in range(GQA)], axis=0)


def _attn_mask(i):
    r = lax.broadcasted_iota(jnp.int32, (GQA * ABLK, NKEY), 0) % ABLK
    c = lax.broadcasted_iota(jnp.int32, (GQA * ABLK, NKEY), 1)
    n = i - NCTX // ABLK
    kpos = (n - 1) * ABLK + (c - NCTX)
    qpos = n * ABLK + r
    loc = (c >= NCTX) & (jnp.abs(kpos - qpos) <= WINDOW) & (kpos >= 0) & (kpos < SEQ) & (n >= 0)
    return (c < NCTX) | loc


def _attn_specs():
    qs = pl.BlockSpec((GQA, ABLK, HEAD_DIM), lambda h, i: (h, i, 0))
    kc = pl.BlockSpec((1, NCTX, HEAD_DIM), lambda h, i: (h, 0, 0))
    kl = [pl.BlockSpec((1, ABLK, HEAD_DIM), functools.partial(lambda h, i, d: (h, i + d, 0), d=d)) for d in range(3)]
    sk = pl.BlockSpec((GQA, 1, 1), lambda h, i: (h, 0, 0))
    return qs, kc, kl, sk


def _attn_fwd(q, k, v, sink):
    scale = HEAD_DIM ** -0.5

    def body(q_ref, kc, k0, k1, k2, vc, v0, v1, v2, s_ref, o_ref, l_ref):
        valid = _attn_mask(pl.program_id(1))
        kk = jnp.concatenate([kc[0], k0[0], k1[0], k2[0]], axis=0)
        vv = jnp.concatenate([vc[0], v0[0], v1[0], v2[0]], axis=0)
        s = lax.dot_general(_stack_heads(q_ref), kk, (((1,), (1,)), ((), ())), preferred_element_type=F32) * scale
        s = jnp.where(valid, s, NEG_INF)
        sk = _sink_rows(s_ref)
        m = jnp.maximum(jnp.max(s, axis=-1, keepdims=True), sk)
        p = jnp.exp(s - m)
        l = jnp.sum(p, axis=-1, keepdims=True) + jnp.exp(sk - m)
        o = jnp.dot((p / l).astype(_MXU), vv, preferred_element_type=F32).astype(o_ref.dtype)
        lse = m + jnp.log(l)
        for g in range(GQA):
            o_ref[g] = o[ABLK * g:ABLK * (g + 1)]
            l_ref[g] = lse[ABLK * g:ABLK * (g + 1)]

    qs, kc, kl, sk = _attn_specs()
    return _pcall(body, name="attn_fwd", grid=(NKVH, NQB), in_specs=[qs, kc, *kl, kc, *kl, sk],
                  out_specs=[qs, pl.BlockSpec((GQA, ABLK, 1), lambda h, i: (h, i, 0))],
                  out_shape=[jax.ShapeDtypeStruct((NQH, T, HEAD_DIM), _MXU), jax.ShapeDtypeStruct((NQH, T, 1), F32)],
                  compiler_params=_cp(("parallel", "arbitrary")))(q, k, k, k, k, v, v, v, v, sink)


def _attn_bwd(q, k, v, sink, lse, do):
    scale = HEAD_DIM ** -0.5

    def body(q_ref, kc, k0, k1, k2, vc, v0, v1, v2, s_ref, l_ref, do_ref, dq_ref, dk_ref, dv_ref, ds_ref):
        i = pl.program_id(1)

        @pl.when(i == 0)
        def _():
            dk_ref[...] = jnp.zeros_like(dk_ref)
            dv_ref[...] = jnp.zeros_like(dv_ref)
            ds_ref[...] = jnp.zeros_like(ds_ref)

        valid = _attn_mask(i)
        kk = jnp.concatenate([kc[0], k0[0], k1[0], k2[0]], axis=0)
        vv = jnp.concatenate([vc[0], v0[0], v1[0], v2[0]], axis=0)
        qst = _stack_heads(q_ref)
        dos = _stack_heads(do_ref)
        lse = _stack_heads(l_ref)
        s = lax.dot_general(qst, kk, (((1,), (1,)), ((), ())), preferred_element_type=F32) * scale
        s = jnp.where(valid, s, NEG_INF)
        p = jnp.exp(s - lse)
        psink = jnp.exp(_sink_rows(s_ref) - lse)
        dp = lax.dot_general(dos, vv, (((1,), (1,)), ((), ())), preferred_element_type=F32)
        delta = jnp.sum(p * dp, axis=-1, keepdims=True)
        ds = p * (dp - delta)
        dsk = psink * delta
        dsq = (ds * scale).astype(_MXU)
        dq = jnp.dot(dsq, kk, preferred_element_type=F32)
        for g in range(GQA):
            dq_ref[g] = dq[ABLK * g:ABLK * (g + 1)]
            ds_ref[g] += -jnp.sum(dsk[ABLK * g:ABLK * (g + 1)], axis=0, keepdims=True)
        dkk = lax.dot_general(dsq, qst, (((0,), (0,)), ((), ())), preferred_element_type=F32)
        dvv = lax.dot_general(p.astype(_MXU), dos, (((0,), (0,)), ((), ())), preferred_element_type=F32)
        loc = pl.ds(pl.multiple_of(i * ABLK, ABLK), 3 * ABLK)
        dk_ref[0, 0:NCTX, :] += dkk[:NCTX]
        dv_ref[0, 0:NCTX, :] += dvv[:NCTX]
        dk_ref[0, loc, :] += dkk[NCTX:]
        dv_ref[0, loc, :] += dvv[NCTX:]

    qs, kc, kl, sk = _attn_specs()
    ls = pl.BlockSpec((GQA, ABLK, 1), lambda h, i: (h, i, 0))
    kfull = pl.BlockSpec((1, KPAD, HEAD_DIM), lambda h, i: (h, 0, 0))
    return _pcall(body, name="attn_bwd", grid=(NKVH, NQB), in_specs=[qs, kc, *kl, kc, *kl, sk, ls, qs],
                  out_specs=[qs, kfull, kfull, sk],
                  out_shape=[jax.ShapeDtypeStruct((NQH, T, HEAD_DIM), F32), jax.ShapeDtypeStruct((NKVH, KPAD, HEAD_DIM), F32),
                             jax.ShapeDtypeStruct((NKVH, KPAD, HEAD_DIM), F32), jax.ShapeDtypeStruct((NQH, 1, 1), F32)],
                  compiler_params=_cp(("parallel", "arbitrary")))(q, k, k, k, k, v, v, v, v, sink, lse, do)


def _split_heads(x, nh):
    return x.reshape(T, nh, HEAD_DIM).transpose(1, 0, 2)


def _merge_heads(x):
    return x.transpose(1, 0, 2).reshape(T, -1)


def _pad_keys(x):
    z = jnp.zeros((x.shape[0], ABLK, HEAD_DIM), x.dtype)
    return jnp.concatenate([x[:, :NCTX], z, x[:, NCTX:], z], axis=1)


def _unpad_keys(x):
    return jnp.concatenate([x[:, :NCTX], x[:, NCTX + ABLK:NCTX + ABLK + SEQ]], axis=1)


def _attn_mixer_fwd(h, w_qkv, w_o, sink):
    cos, sin = _rope_tables()
    qkv = _rope(_mm(h, w_qkv, name="attn_qkv"), cos, sin, 1.0)
    q = _split_heads(qkv[:, :NQH * HEAD_DIM], NQH)
    k = _pad_keys(_split_heads(qkv[:, NQH * HEAD_DIM:QK_W], NKVH))
    v = _pad_keys(_split_heads(qkv[:, QK_W:], NKVH))
    sk = sink.reshape(NQH, 1, 1)
    o, lse = _attn_fwd(q, k, v, sk)
    om = _merge_heads(o)
    y = _mm(om, w_o, name="attn_out")
    return y, (q, k, v, sk, lse, om)


def _attn_mixer_bwd(dy, h, saved, w_qkv, w_o):
    q, k, v, sk, lse, om = saved
    cos, sin = _rope_tables()
    dyb = dy.astype(_MXU)
    dw_o = _mm(om, dyb, ta=True, name="attn_out_dw")
    do = _split_heads(_mm(dyb, w_o, tb=True, out_dtype=_MXU, name="attn_out_dx"), NQH)
    dq, dk, dv, dsk = _attn_bwd(q, k, v, sk, lse, do)
    dqkv = jnp.concatenate([_merge_heads(dq), _merge_heads(_unpad_keys(dk)), _merge_heads(_unpad_keys(dv))], axis=1)
    dqkv = _rope(dqkv, cos, sin, -1.0)
    dw_qkv = _mm(h, dqkv, ta=True, name="attn_qkv_dw")
    dh = _mm(dqkv, w_qkv, tb=True, name="attn_qkv_dx")
    return dh, dw_qkv, dw_o, dsk.reshape(1, NQH)


SSM_S = SSM_G * SSM_P
SSM_SL = SSM_S // LANE
SSM_TS = 128
SSM_NTS = T // SSM_TS
SSM_NCT = NCTX // SSM_TS
SSM_JB = 4
SSM_NTR = 4
SSM_TR = T // SSM_NTR


def _to_slabs(cols):
    return jnp.swapaxes(jnp.stack(cols, axis=0), 0, 1)


def _from_slabs(s_ref):
    x = jnp.swapaxes(s_ref[...], 0, 1)
    return [jnp.concatenate([x[4 * hlf + q] for q in range(4)], axis=1).astype(_MXU) for hlf in range(2)]


def _proj3d(u, w_re, w_im):
    def body(u_ref, wr_ref, wi_ref, or_ref, oi_ref):
        for w_ref, o_ref in ((wr_ref, or_ref), (wi_ref, oi_ref)):
            cols = []
            for hlf in range(2):
                ub = u_ref[:, LANE * hlf:LANE * (hlf + 1)].astype(_MXU)
                r = jnp.dot(ub, w_ref[hlf].astype(_MXU), preferred_element_type=F32)
                cols += [r[:, LANE * q:LANE * (q + 1)] for q in range(4)]
            o_ref[...] = _to_slabs(cols)

    ws = pl.BlockSpec((2, LANE, 512), lambda i, j: (j, 0, 0))
    os_ = pl.BlockSpec((SSM_TR, 8, LANE), lambda i, j: (i, j, 0))
    return _pcall(body, name="ssm_proj", grid=(SSM_NTR, SSM_JB), in_specs=[pl.BlockSpec((SSM_TR, 2 * LANE), lambda i, j: (i, j)), ws, ws],
                  out_specs=[os_, os_], out_shape=[jax.ShapeDtypeStruct((T, SSM_SL, LANE), F32)] * 2,
                  compiler_params=_cp(("parallel", "parallel")))(u, w_re, w_im)


def _readout(s_re, s_im, w_re, w_im):
    def body(sr_ref, si_ref, wr_ref, wi_ref, o_ref):
        xs = [_from_slabs(sr_ref), _from_slabs(si_ref)]
        for hlf in range(2):
            acc = None
            for x, w_ref in zip(xs, (wr_ref, wi_ref)):
                r = jnp.dot(x[hlf], w_ref[hlf].astype(_MXU), preferred_element_type=F32)
                acc = r if acc is None else acc + r
            o_ref[:, LANE * hlf:LANE * (hlf + 1)] = acc

    ss = pl.BlockSpec((SSM_TR, 8, LANE), lambda i, j: (i, j, 0))
    ws = pl.BlockSpec((2, 512, LANE), lambda i, j: (j, 0, 0))
    return _pcall(body, name="ssm_readout", grid=(SSM_NTR, SSM_JB), in_specs=[ss, ss, ws, ws],
                  out_specs=pl.BlockSpec((SSM_TR, 2 * LANE), lambda i, j: (i, j)), out_shape=jax.ShapeDtypeStruct((T, D), F32),
                  compiler_params=_cp(("parallel", "parallel")))(s_re, s_im, w_re, w_im)


def _outer3d(s_re, s_im, y):
    def body(sr_ref, si_ref, y_ref, dr_ref, di_ref):
        i = pl.program_id(1)
        xs = [_from_slabs(sr_ref), _from_slabs(si_ref)]
        for hlf in range(2):
            yb = y_ref[:, LANE * hlf:LANE * (hlf + 1)].astype(_MXU)
            for x, d_ref in zip(xs, (dr_ref, di_ref)):
                r = lax.dot_general(yb, x[hlf], (((0,), (0,)), ((), ())), preferred_element_type=F32)

                @pl.when(i == 0)
                def _():
                    d_ref[hlf] = r

                @pl.when(i > 0)
                def _():
                    d_ref[hlf] += r

    ss = pl.BlockSpec((SSM_TR, 8, LANE), lambda j, i: (i, j, 0))
    ds = pl.BlockSpec((2, LANE, 512), lambda j, i: (j, 0, 0))
    return _pcall(body, name="ssm_outer", grid=(SSM_JB, SSM_NTR), in_specs=[ss, ss, pl.BlockSpec((SSM_TR, 2 * LANE), lambda j, i: (i, j))],
                  out_specs=[ds, ds], out_shape=[jax.ShapeDtypeStruct((8, LANE, 512), F32)] * 2,
                  compiler_params=_cp(("parallel", "arbitrary")))(s_re, s_im, y)


def _scan_order(order):
    n, c = SSM_NTS, SSM_NCT
    if order == "fwd":
        return (lambda i: i), False
    if order == "fwd_adj":
        return (lambda i: n - 1 - i), True
    if order == "rev":
        return (lambda i: jnp.where(i < c, c - 1 - i, n + c - 1 - i)), True
    if order == "rev_adj":
        return (lambda i: jnp.where(i < n - c, i + c, i - (n - c))), False
    raise ValueError(order)


def _scan(b_re, b_im, lam_re, lam_im, order):
    tile, down = _scan_order(order)

    def body(br_ref, bi_ref, lr_ref, li_ref, sr_ref, si_ref, cr, ci):
        @pl.when(pl.program_id(0) == 0)
        def _():
            cr[...] = jnp.zeros_like(cr)
            ci[...] = jnp.zeros_like(ci)

        lr = lr_ref[...]
        li = li_ref[...]

        def step(n, c):
            t = SSM_TS - 1 - n if down else n
            sr, si = c
            nr = lr * sr - li * si + br_ref[t]
            ni = lr * si + li * sr + bi_ref[t]
            sr_ref[t] = nr
            si_ref[t] = ni
            return nr, ni

        sr, si = lax.fori_loop(0, SSM_TS, step, (cr[...], ci[...]))
        cr[...] = sr
        ci[...] = si

    bs = pl.BlockSpec((SSM_TS, SSM_SL, LANE), lambda i: (tile(i), 0, 0))
    ps = pl.BlockSpec((SSM_SL, LANE), lambda i: (0, 0))
    return _pcall(body, name="ssm_scan_" + order, grid=(SSM_NTS,), in_specs=[bs, bs, ps, ps], out_specs=[bs, bs],
                  out_shape=[jax.ShapeDtypeStruct((T, SSM_SL, LANE), F32)] * 2,
                  scratch_shapes=[pltpu.VMEM((SSM_SL, LANE), F32)] * 2, compiler_params=_cp(("arbitrary",)))(b_re, b_im, lam_re, lam_im)


def _scan_adj(g_re, g_im, s_re, s_im, lam_re, lam_im, order):
    tile, down = _scan_order(order)

    def body(gr_ref, gi_ref, sr_ref, si_ref, lr_ref, li_ref, ar_ref, ai_ref, dlr_ref, dli_ref, cr, ci):
        @pl.when(pl.program_id(0) == 0)
        def _():
            cr[...] = jnp.zeros_like(cr)
            ci[...] = jnp.zeros_like(ci)
            dlr_ref[...] = jnp.zeros_like(dlr_ref)
            dli_ref[...] = jnp.zeros_like(dli_ref)

        lr = lr_ref[...]
        li = li_ref[...]

        def step(n, c):
            t = SSM_TS - 1 - n if down else n
            ar, ai, dr, di = c
            sr = sr_ref[t]
            si = si_ref[t]
            dr = dr + ar * sr + ai * si
            di = di + ai * sr - ar * si
            nr = gr_ref[t] + lr * ar + li * ai
            ni = gi_ref[t] + lr * ai - li * ar
            ar_ref[t] = nr
            ai_ref[t] = ni
            return nr, ni, dr, di

        ar, ai, dr, di = lax.fori_loop(0, SSM_TS, step, (cr[...], ci[...], dlr_ref[...], dli_ref[...]))
        cr[...] = ar
        ci[...] = ai
        dlr_ref[...] = dr
        dli_ref[...] = di

    bs = pl.BlockSpec((SSM_TS, SSM_SL, LANE), lambda i: (tile(i), 0, 0))
    ps = pl.BlockSpec((SSM_SL, LANE), lambda i: (0, 0))
    return _pcall(body, name="ssm_scan_" + order, grid=(SSM_NTS,), in_specs=[bs, bs, bs, bs, ps, ps], out_specs=[bs, bs, ps, ps],
                  out_shape=[jax.ShapeDtypeStruct((T, SSM_SL, LANE), F32)] * 2 + [jax.ShapeDtypeStruct((SSM_SL, LANE), F32)] * 2,
                  scratch_shapes=[pltpu.VMEM((SSM_SL, LANE), F32)] * 2,
                  compiler_params=_cp(("arbitrary",)))(g_re, g_im, s_re, s_im, lam_re, lam_im)


def _block_diag(x):
    x4 = x.reshape(8, 8, SSM_P, SSM_C)
    return jnp.einsum("jgpc,gh->jgphc", x4, jnp.eye(8, dtype=x.dtype)).reshape(8, 8 * SSM_P, 8 * SSM_C)


def _ssm_prep(lam_re, lam_im, log_dt, b_re, b_im, c_re, c_im):
    lam = lax.complex(lam_re, lam_im)
    dt = jnp.exp(log_dt)[:, None]
    lam_bar = jnp.exp(lam * dt)
    b_bar = ((lam_bar - 1.0) / lam)[..., None] * lax.complex(b_re, b_im)
    return (jnp.real(lam_bar).reshape(SSM_SL, LANE), jnp.imag(lam_bar).reshape(SSM_SL, LANE),
            _block_diag(jnp.real(b_bar)), _block_diag(jnp.imag(b_bar)),
            _block_diag(c_re.transpose(0, 2, 1)), _block_diag(-c_im.transpose(0, 2, 1)))


def _ssm_glue(h, yf, yr, d):
    return jax.nn.gelu(d * h + yf + yr)


def _glu(ga, gb):
    return ga * jax.nn.sigmoid(gb)


def _ssm_mixer_fwd(h, sp, w_a, w_b):
    lam_re, lam_im, log_dt, b_re, b_im, c_re, c_im, d_skip = sp
    ys, saved = [], []
    for di, order in enumerate(("fwd", "rev")):
        lr, li, wb_r, wb_i, wc_r, wc_i = _ssm_prep(lam_re[di], lam_im[di], log_dt[di], b_re[di], b_im[di], c_re[di], c_im[di])
        bu_r, bu_i = _proj3d(h, wb_r.transpose(0, 2, 1), wb_i.transpose(0, 2, 1))
        s_r, s_i = _scan(bu_r, bu_i, lr, li, order)
        ys.append(_readout(s_r, s_i, wc_r, wc_i))
        saved.append((s_r, s_i))
    g = _rows(lambda *a: (_ssm_glue(*a),), [h, ys[0], ys[1]], [d_skip], [(D, _MXU)], [], name="ssm_glue")[0]
    ga = _mm(g, w_a, name="ssm_glu_a")
    gb = _mm(g, w_b, name="ssm_glu_b")
    y = _rows(lambda *a: (_glu(*a),), [ga, gb], [], [(D, F32)], [], name="ssm_glu")[0]
    return y, (ys, saved, g, ga, gb)


def _ssm_mixer_bwd(dy, h, saved_all, sp, w_a, w_b):
    lam_re, lam_im, log_dt, b_re, b_im, c_re, c_im, d_skip = sp
    ys, saved, g, ga, gb = saved_all

    def glu_bwd(ga, gb, dy):
        _, vjp = jax.vjp(_glu, ga, gb)
        return vjp(dy)

    dga, dgb = _rows(glu_bwd, [ga, gb, dy], [], [(D, _MXU), (D, _MXU)], [], name="ssm_glu_bwd")
    dw_a = _mm(g, dga, ta=True, name="ssm_glu_a_dw")
    dw_b = _mm(g, dgb, ta=True, name="ssm_glu_b_dw")
    dg_a = _mm(dga, w_a, tb=True, name="ssm_glu_a_dx")
    dg_b = _mm(dgb, w_b, tb=True, name="ssm_glu_b_dx")

    def glue_bwd(h, yf, yr, dg_a, dg_b, d):
        _, vjp = jax.vjp(_ssm_glue, h, yf, yr, d)
        dh, dyl, _, dd = vjp(dg_a + dg_b)
        return dh, dyl, dd

    dh0, dyl, dd = _rows(glue_bwd, [h, ys[0], ys[1], dg_a, dg_b], [d_skip], [(D, F32), (D, F32)], [(1, (1, D))], name="ssm_glue_bwd")
    dhs = [dh0]
    dparams = []
    for di, (order, adj) in enumerate((("fwd", "fwd_adj"), ("rev", "rev_adj"))):
        args = (lam_re[di], lam_im[di], log_dt[di], b_re[di], b_im[di], c_re[di], c_im[di])
        (lr, li, wb_r, wb_i, wc_r, wc_i), prep_vjp = jax.vjp(_ssm_prep, *args)
        s_r, s_i = saved[di]
        dwc_r, dwc_i = _outer3d(s_r, s_i, dyl)
        g_r, g_i = _proj3d(dyl, wc_r.transpose(0, 2, 1), wc_i.transpose(0, 2, 1))
        a_r, a_i, dlr, dli = _scan_adj(g_r, g_i, s_r, s_i, lr, li, adj)
        dwb_r, dwb_i = _outer3d(a_r, a_i, h)
        dhs.append(_readout(a_r, a_i, wb_r, wb_i))
        dparams.append(prep_vjp((dlr, dli) + tuple(d.transpose(0, 2, 1) for d in (dwb_r, dwb_i, dwc_r, dwc_i))))
    dsp = [jnp.stack([dparams[0][k], dparams[1][k]], axis=0) for k in range(7)]
    return dhs, dsp, dd, dw_a, dw_b


NCH = T // GM_CHUNK


def _gm_specs():
    full = lambda *s: pl.BlockSpec(s, lambda i: (0,) * len(s))
    zu = pl.BlockSpec((GM_CHUNK, GM_HALF), lambda i: (i, 0))
    zv = pl.BlockSpec((GM_CHUNK, GM_HALF), lambda i: (i, 1))
    pars = [full(1, GM_HALF), pl.BlockSpec((1, GM_HALF), lambda i: (0, 1)), full(1, GM_HALF), full(1, GM_HALF),
            full(GM_HEADS, GM_CHUNK, GM_CHUNK), full(GM_HEADS, GM_CHUNK, 1)]
    return zu, zv, pars, full


def _gm_forward(zu_ref, zv_ref, bu_ref, bv_ref, g_ref, b_ref, ws_ref, bs_ref):
    u = jax.nn.gelu(zu_ref[...] + bu_ref[...])
    zv = jax.nn.gelu(zv_ref[...] + bv_ref[...])
    mu = jnp.mean(zv, axis=-1, keepdims=True)
    zc = zv - mu
    rstd = lax.rsqrt(jnp.mean(jnp.square(zc), axis=-1, keepdims=True) + LN_EPS)
    vhat = zc * rstd
    v = (vhat * g_ref[...] + b_ref[...]).astype(_MXU)
    gates = [jnp.dot(ws_ref[hd].astype(_MXU), v[:, GM_HD * hd:GM_HD * (hd + 1)], preferred_element_type=F32) + bs_ref[hd]
             for hd in range(GM_HEADS)]
    return u, vhat, rstd, v, jnp.concatenate(gates, axis=1)


def _gmlp_chunk(zp, b_in, ln_g, ln_b, w_s, b_s):
    def body(zu_ref, zv_ref, bu_ref, bv_ref, g_ref, b_ref, ws_ref, bs_ref, o_ref):
        u, _, _, _, gate = _gm_forward(zu_ref, zv_ref, bu_ref, bv_ref, g_ref, b_ref, ws_ref, bs_ref)
        o_ref[...] = (u * gate).astype(o_ref.dtype)

    zu, zv, pars, _ = _gm_specs()
    return _pcall(body, name="gmlp_chunk", grid=(NCH,), in_specs=[zu, zv, *pars], out_specs=zu,
                  out_shape=jax.ShapeDtypeStruct((T, GM_HALF), _MXU),
                  compiler_params=_cp(("parallel",)))(zp, zp, b_in, b_in, ln_g, ln_b, w_s, b_s)


def _gmlp_chunk_bwd(zp, do, b_in, ln_g, ln_b, w_s, b_s):
    def body(zu_ref, zv_ref, do_ref, bu_ref, bv_ref, g_ref, b_ref, ws_ref, bs_ref,
             dzu_ref, dzv_ref, dbu_ref, dbv_ref, dg_ref, db_ref, dws_ref, dbs_ref):
        i = pl.program_id(0)

        def acc(ref, val, idx=None):
            @pl.when(i == 0)
            def _():
                if idx is None:
                    ref[...] = val
                else:
                    ref[idx] = val

            @pl.when(i > 0)
            def _():
                if idx is None:
                    ref[...] += val
                else:
                    ref[idx] += val

        u, vhat, rstd, v, gate = _gm_forward(zu_ref, zv_ref, bu_ref, bv_ref, g_ref, b_ref, ws_ref, bs_ref)
        do = do_ref[...]
        du = do * gate
        dgate = do * u
        dvs = []
        for hd in range(GM_HEADS):
            sl = slice(GM_HD * hd, GM_HD * (hd + 1))
            dgh = dgate[:, sl]
            dghb = dgh.astype(_MXU)
            dvs.append(lax.dot_general(ws_ref[hd].astype(_MXU), dghb, (((0,), (0,)), ((), ())), preferred_element_type=F32))
            acc(dws_ref, lax.dot_general(dghb, v[:, sl], (((1,), (1,)), ((), ())), preferred_element_type=F32), hd)
            acc(dbs_ref, jnp.sum(dgh, axis=1, keepdims=True), hd)
        dv = jnp.concatenate(dvs, axis=1)
        acc(dg_ref, jnp.sum(dv * vhat, axis=0, keepdims=True))
        acc(db_ref, jnp.sum(dv, axis=0, keepdims=True))
        dvh = dv * g_ref[...]
        dzv = rstd * (dvh - jnp.mean(dvh, axis=-1, keepdims=True) - vhat * jnp.mean(dvh * vhat, axis=-1, keepdims=True))
        dpu = jax.vjp(jax.nn.gelu, zu_ref[...] + bu_ref[...])[1](du)[0]
        dpv = jax.vjp(jax.nn.gelu, zv_ref[...] + bv_ref[...])[1](dzv)[0]
        dzu_ref[...] = dpu.astype(dzu_ref.dtype)
        dzv_ref[...] = dpv.astype(dzv_ref.dtype)
        acc(dbu_ref, jnp.sum(dpu, axis=0, keepdims=True))
        acc(dbv_ref, jnp.sum(dpv, axis=0, keepdims=True))

    zu, zv, pars, full = _gm_specs()
    out_specs = [zu, zu, full(1, GM_HALF), full(1, GM_HALF), full(1, GM_HALF), full(1, GM_HALF),
                 full(GM_HEADS, GM_CHUNK, GM_CHUNK), full(GM_HEADS, GM_CHUNK, 1)]
    out_shape = [jax.ShapeDtypeStruct((T, GM_HALF), _MXU)] * 2 + [jax.ShapeDtypeStruct((1, GM_HALF), F32)] * 4 + \
                [jax.ShapeDtypeStruct((GM_HEADS, GM_CHUNK, GM_CHUNK), F32), jax.ShapeDtypeStruct((GM_HEADS, GM_CHUNK, 1), F32)]
    dzu, dzv, dbu, dbv, dg, db, dws, dbs = _pcall(
        body, name="gmlp_chunk_bwd", grid=(NCH,), in_specs=[zu, zv, zu, *pars], out_specs=out_specs, out_shape=out_shape,
        compiler_params=_cp(("arbitrary",)))(zp, zp, do, b_in, b_in, ln_g, ln_b, w_s, b_s)
    return jnp.concatenate([dzu, dzv], axis=1), jnp.concatenate([dbu, dbv], axis=1), dg, db, dws, dbs


def _gmlp_mixer_fwd(h, w_in, b_in, ln_g, ln_b, w_s, b_s, w_out):
    zp = _mm(h, w_in, name="gmlp_in")
    ug = _gmlp_chunk(zp, b_in, ln_g, ln_b, w_s, b_s[..., None])
    return _mm(ug, w_out, name="gmlp_out"), (zp, ug)


def _gmlp_mixer_bwd(dy, h, saved, w_in, b_in, ln_g, ln_b, w_s, b_s, w_out):
    zp, ug = saved
    dw_out = _mm(ug, dy, ta=True, name="gmlp_out_dw")
    do = _mm(dy, w_out, tb=True, name="gmlp_out_dx")
    dzp, db_in, dg, db, dws, dbs = _gmlp_chunk_bwd(zp, do, b_in, ln_g, ln_b, w_s, b_s[..., None])
    dw_in = _mm(h, dzp, ta=True, name="gmlp_in_dw")
    dh = _mm(dzp, w_in, tb=True, name="gmlp_in_dx")
    return dh, dw_in, db_in, dg, db, dws, dbs[..., 0], dw_out


def _loss_head(x, target):
    nct = NCTX // TM

    def body(x_ref, t_ref, l_ref, dx_ref):
        i = pl.program_id(0)
        err = jnp.where(i >= nct, x_ref[...] - t_ref[...], 0.0)
        dx_ref[...] = err * (1.0 / D)
        part = 0.5 * jnp.sum(jnp.sum(jnp.square(err), axis=-1, keepdims=True) * (1.0 / D), axis=0, keepdims=True)

        @pl.when(i == 0)
        def _():
            l_ref[...] = part

        @pl.when(i > 0)
        def _():
            l_ref[...] += part

    return _pcall(body, name="loss_head", grid=(NT,),
                  in_specs=[pl.BlockSpec((TM, D), lambda i: (i, 0)), pl.BlockSpec((TM, D), lambda i: (jnp.maximum(i - nct, 0), 0))],
                  out_specs=[pl.BlockSpec((1, 1), lambda i: (0, 0)), pl.BlockSpec((TM, D), lambda i: (i, 0))],
                  out_shape=[jax.ShapeDtypeStruct((1, 1), F32), jax.ShapeDtypeStruct((T, D), F32)],
                  compiler_params=_cp(("arbitrary",)))(x, target)


def _as2d(a):
    return a.reshape(-1, a.shape[-1])


def _adamw(w, g, m, v):
    shape = w.shape
    w2, g2, m2, v2 = _as2d(w), _as2d(g), _as2d(m), _as2d(v)
    R, C = w2.shape
    tr = _tile(R, 512, SUBLANE)
    c1 = 1.0 - B1 ** STEP
    c2 = 1.0 - B2 ** STEP

    def body(w_ref, g_ref, m_ref, v_ref, d_ref, nm_ref, nv_ref):
        g = g_ref[...]
        m = B1 * m_ref[...] + (1.0 - B1) * g
        v = B2 * v_ref[...] + (1.0 - B2) * jnp.square(g)
        nm_ref[...] = m
        nv_ref[...] = v
        d_ref[...] = -LR * ((m / c1) / (jnp.sqrt(v / c2) + EPS) + WD * w_ref[...])

    spec = pl.BlockSpec((tr, C), lambda i: (i, 0))
    outs = _pcall(body, name="adamw", grid=(R // tr,), in_specs=[spec] * 4, out_specs=[spec] * 3,
                  out_shape=[jax.ShapeDtypeStruct((R, C), F32)] * 3, compiler_params=_cp(("parallel",)))(w2, g2, m2, v2)
    return tuple(o.reshape(shape) for o in outs)


def _sum_slabs(x):
    n = x.shape[0]
    x = x.reshape(n, -1, x.shape[-1])
    _, R, C = x.shape
    tr = _tile(R, 256, 16)

    def body(x_ref, o_ref):
        acc = x_ref[0].astype(F32)
        for k in range(1, n):
            acc = acc + x_ref[k].astype(F32)
        o_ref[...] = acc

    return _pcall(body, name="sum_slabs", grid=(R // tr,), in_specs=[pl.BlockSpec((n, tr, C), lambda i: (0, i, 0))],
                  out_specs=pl.BlockSpec((tr, C), lambda i: (i, 0)), out_shape=jax.ShapeDtypeStruct((R, C), F32),
                  compiler_params=_cp(("parallel",)))(x)


def _comm_call(body, xs, out_shape, name):
    n = len(xs)
    hbm = pl.BlockSpec(memory_space=pl.ANY)
    return _pcall(body, name=name, in_specs=[hbm] * n, out_specs=[hbm] * n, out_shape=out_shape,
                  scratch_shapes=[pltpu.SemaphoreType.DMA((n, NDEV - 1)), pltpu.SemaphoreType.DMA((n, NDEV - 1)),
                                  pltpu.SemaphoreType.DMA((n,))],
                  compiler_params=pltpu.CompilerParams(has_side_effects=True))(*xs)


def _exchange(xs, name):
    n = len(xs)

    def body(*refs):
        x_refs, o_refs = refs[:n], refs[n:2 * n]
        send_sems, recv_sems, loc_sems = refs[2 * n:]
        mx, my, mc = lax.axis_index("x"), lax.axis_index("y"), lax.axis_index("c")
        me = 4 * mx + 2 * my + mc
        pending = []
        for a in range(n):
            mine = pltpu.make_async_copy(x_refs[a].at[me], o_refs[a].at[me], loc_sems.at[a])
            mine.start()
            pending.append(mine)
            for k in range(1, NDEV):
                px = 1 - mx if k & 4 else mx
                py = 1 - my if k & 2 else my
                pc = 1 - mc if k & 1 else mc
                cp = pltpu.make_async_remote_copy(
                    src_ref=x_refs[a].at[4 * px + 2 * py + pc], dst_ref=o_refs[a].at[me],
                    send_sem=send_sems.at[a, k - 1], recv_sem=recv_sems.at[a, k - 1],
                    device_id=(px, py, pc), device_id_type=pl.DeviceIdType.MESH)
                cp.start()
                pending.append(cp)
        for cp in pending:
            cp.wait()

    return _comm_call(body, xs, [jax.ShapeDtypeStruct(tuple(x.shape), x.dtype) for x in xs], name)


NCHIP = NDEV // 2


def _sibling_exchange(xs, name):
    n = len(xs)

    def body(*refs):
        x_refs, o_refs = refs[:n], refs[n:2 * n]
        send_sems, recv_sems, _ = refs[2 * n:]
        mx, my, mc = lax.axis_index("x"), lax.axis_index("y"), lax.axis_index("c")
        pending = []
        for a in range(n):
            for b in range(NCHIP):
                cp = pltpu.make_async_remote_copy(
                    src_ref=x_refs[a].at[2 * b + (1 - mc)], dst_ref=o_refs[a].at[b],
                    send_sem=send_sems.at[a, b], recv_sem=recv_sems.at[a, b],
                    device_id=(mx, my, 1 - mc), device_id_type=pl.DeviceIdType.MESH)
                cp.start()
                pending.append(cp)
        for cp in pending:
            cp.wait()

    return _comm_call(body, xs, [jax.ShapeDtypeStruct((NCHIP,) + tuple(x.shape[1:]), x.dtype) for x in xs], name)


def _chip_exchange(xs, name):
    n = len(xs)

    def body(*refs):
        x_refs, o_refs = refs[:n], refs[n:2 * n]
        send_sems, recv_sems, loc_sems = refs[2 * n:]
        mx, my, mc = lax.axis_index("x"), lax.axis_index("y"), lax.axis_index("c")
        chip = 2 * mx + my
        pending = []
        for a in range(n):
            mine = pltpu.make_async_copy(x_refs[a].at[chip], o_refs[a].at[chip], loc_sems.at[a])
            mine.start()
            pending.append(mine)
            for k in range(1, NCHIP):
                px = 1 - mx if k & 2 else mx
                py = 1 - my if k & 1 else my
                cp = pltpu.make_async_remote_copy(
                    src_ref=x_refs[a].at[2 * px + py], dst_ref=o_refs[a].at[chip],
                    send_sem=send_sems.at[a, k - 1], recv_sem=recv_sems.at[a, k - 1],
                    device_id=(px, py, mc), device_id_type=pl.DeviceIdType.MESH)
                cp.start()
                pending.append(cp)
        for cp in pending:
            cp.wait()

    return _comm_call(body, xs, [jax.ShapeDtypeStruct(tuple(x.shape), x.dtype) for x in xs], name)


def _pair_sum(x, y):
    _, _, R, C = x.shape
    tr = _tile(R, 128, 16)

    def body(x_ref, y_ref, o_ref):
        mc = lax.axis_index("c")
        mine = jnp.where(mc == 0, x_ref[:, 0].astype(F32), x_ref[:, 1].astype(F32))
        o_ref[...] = (mine + y_ref[...].astype(F32)).astype(o_ref.dtype)

    return _pcall(body, name="pair_sum", grid=(R // tr,),
                  in_specs=[pl.BlockSpec((NCHIP, 2, tr, C), lambda i: (0, 0, i, 0)), pl.BlockSpec((NCHIP, tr, C), lambda i: (0, i, 0))],
                  out_specs=pl.BlockSpec((NCHIP, tr, C), lambda i: (0, i, 0)), out_shape=jax.ShapeDtypeStruct((NCHIP, R, C), x.dtype),
                  compiler_params=_cp(("parallel",)))(x, y)


def _reduce_scatter(xs, name):
    from_sibling = _sibling_exchange(xs, name + "_d2d")
    pair = [_pair_sum(x.reshape(NCHIP, 2, -1, x.shape[-1]), y.reshape(NCHIP, -1, x.shape[-1])) for x, y in zip(xs, from_sibling)]
    got = _chip_exchange(pair, name + "_ici")
    return [_sum_slabs(g).reshape(x.shape[1:]) for g, x in zip(got, xs)]


def _gather(xs, name):
    n = len(xs)

    def body(*refs):
        x_refs, o_refs = refs[:n], refs[n:2 * n]
        send_sems, recv_sems, loc_sems = refs[2 * n:]
        mx, my, mc = lax.axis_index("x"), lax.axis_index("y"), lax.axis_index("c")
        me = 4 * mx + 2 * my + mc
        sibling = (mx, my, 1 - mc)
        chips = [(1 - mx, my), (mx, 1 - my), (1 - mx, 1 - my)]
        slot = lambda px, py, pc: 4 * px + 2 * py + pc

        def copy(a, k, s, to, from_input=False):
            return pltpu.make_async_remote_copy(
                src_ref=x_refs[a] if from_input else o_refs[a].at[s], dst_ref=o_refs[a].at[s],
                send_sem=send_sems.at[a, k], recv_sem=recv_sems.at[a, k], device_id=to, device_id_type=pl.DeviceIdType.MESH)

        sends, mines = [], []
        for a in range(n):
            mine = pltpu.make_async_copy(x_refs[a], o_refs[a].at[me], loc_sems.at[a])
            mine.start()
            mines.append(mine)
            first = [copy(a, 0, me, sibling, True)] + [copy(a, 1 + j, me, (cx, cy, mc), True) for j, (cx, cy) in enumerate(chips)]
            for cp in first:
                cp.start()
            sends += first
        for a in range(n):
            for j, (cx, cy) in enumerate(chips):
                s = slot(cx, cy, mc)
                copy(a, 1 + j, s, sibling).wait_recv()
                passed = copy(a, 4 + j, s, sibling)
                passed.start()
                sends.append(passed)
        for a in range(n):
            copy(a, 0, slot(*sibling), sibling).wait_recv()
            for j, (cx, cy) in enumerate(chips):
                copy(a, 4 + j, slot(cx, cy, 1 - mc), sibling).wait_recv()
        for cp in sends:
            cp.wait_send()
        for mine in mines:
            mine.wait()

    return _comm_call(body, xs, [jax.ShapeDtypeStruct((NDEV,) + tuple(x.shape), x.dtype) for x in xs], name)


SLAB_W = 1024
SLAB_ROWS = 16


def _pack(parts, lead=None):
    if lead is None:
        flat = jnp.concatenate([p.reshape(-1) for p in parts])
        n = flat.shape[0]
        padn = -n % (SLAB_ROWS * SLAB_W)
        return jnp.pad(flat, (0, padn)).reshape(-1, SLAB_W)
    flat = jnp.concatenate([p.reshape(lead, -1) for p in parts], axis=1)
    n = flat.shape[1]
    padn = -n % (SLAB_ROWS * SLAB_W)
    return jnp.pad(flat, ((0, 0), (0, padn))).reshape(lead, -1, SLAB_W)


def _unpack(buf, shapes, lead=None):
    out, off = [], 0
    flat = buf.reshape(-1) if lead is None else buf.reshape(lead, -1)
    for s in shapes:
        n = int(np.prod(s))
        if lead is None:
            out.append(flat[off:off + n].reshape(s))
        else:
            out.append(flat[:, off:off + n].reshape((lead,) + tuple(s)))
        off += n
    return out


def _gathered(blk, ax):
    m = jnp.moveaxis(blk, 0, ax)
    s = list(m.shape)
    return m.reshape(s[:ax] + [s[ax] * s[ax + 1]] + s[ax + 2:])


def _scattered(full, ax):
    s = list(full.shape)
    m = full.reshape(s[:ax] + [NDEV, s[ax] // NDEV] + s[ax + 1:])
    return jnp.moveaxis(m, ax, 0)


_MM_SHARDED = (("ffn_w_up", 2), ("ffn_w_down", 1), ("pool_w", 2), ("attn_w_qkv", 2), ("attn_w_o", 1),
               ("ssm_w_glu_a", 1), ("ssm_w_glu_b", 1), ("gmlp_w_in", 2), ("gmlp_w_out", 1))
_VEC_SHARDED = (("ffn_conv_w", 2), ("ssm_d", 1), ("gmlp_b_in", 1), ("gmlp_ln_g", 1), ("gmlp_ln_b", 1))
_REPLICATED = ("ln1_g", "ln1_b", "ln2_g", "ln2_b", "ffn_conv_b", "pool_b", "pool_scale", "attn_sink",
               "ssm_lambda_re", "ssm_lambda_im", "ssm_log_dt", "ssm_b_re", "ssm_b_im", "ssm_c_re", "ssm_c_im",
               "gmlp_w_s", "gmlp_b_s")
_WEIGHTS = ("c_ctx", "ada_w", "ada_b", "ln1_g", "ln1_b", "ln2_g", "ln2_b", "ffn_w_up", "ffn_conv_w", "ffn_conv_b", "ffn_w_down",
            "pool_w", "pool_b", "pool_scale", "attn_w_qkv", "attn_w_o", "attn_sink", "ssm_lambda_re", "ssm_lambda_im",
            "ssm_log_dt", "ssm_b_re", "ssm_b_im", "ssm_c_re", "ssm_c_im", "ssm_d", "ssm_w_glu_a", "ssm_w_glu_b",
            "gmlp_w_in", "gmlp_b_in", "gmlp_ln_g", "gmlp_ln_b", "gmlp_w_s", "gmlp_b_s", "gmlp_w_out")
N_MODS = 6
ADA_COLS = N_MODS * D // NDEV
PAD_ROWS = 16


def _silu_rows(x):
    return _rows(lambda v: (jax.nn.silu(v),), [x], [], [(x.shape[1], F32)], [], name="silu", tm=x.shape[0])[0]


def _step(x, c, ctx, loss_target, w, m, v):
    mx, my, mc = lax.axis_index("x"), lax.axis_index("y"), lax.axis_index("c")
    me = 4 * mx + 2 * my + mc

    vec_buf = _pack([w[n] for n, _ in _VEC_SHARDED] + [c])
    *mm_parts, vec_all = _gather([w[n].astype(_MXU) for n, _ in _MM_SHARDED] + [vec_buf], "gather_weights")
    vec_parts = _unpack(vec_all, [w[n].shape for n, _ in _VEC_SHARDED] + [c.shape], lead=NDEV)
    full = {n: _gathered(p, ax) for (n, ax), p in zip(_MM_SHARDED, mm_parts)}
    full.update({n: _gathered(p, ax) for (n, ax), p in zip(_VEC_SHARDED, vec_parts[:-1])})
    c_all = vec_parts[-1].reshape(NDEV, D)

    cc = jnp.concatenate([c_all, w["c_ctx"].reshape(1, D), jnp.zeros((PAD_ROWS - NDEV - 1, D), F32)], axis=0)
    silu_cc = _silu_rows(cc)
    ada_b_mine = lax.dynamic_slice(w["ada_b"], (0, me * ADA_COLS), (DEPTH, ADA_COLS))
    mods_mine = jnp.stack([_mm(silu_cc, w["ada_w"][l], name="ada_mods") + ada_b_mine[l][None, :] for l in range(DEPTH)], axis=1)
    per_dev = mods_mine[:NDEV].reshape(NDEV, DEPTH * ADA_COLS)
    cm = jnp.broadcast_to(mods_mine[NDEV].reshape(1, DEPTH * ADA_COLS), (NDEV, DEPTH * ADA_COLS))
    mods_all = _exchange([_pack([per_dev, cm], lead=NDEV)], "scatter_mods")[0]
    got = _unpack(mods_all, [(DEPTH, ADA_COLS), (DEPTH, ADA_COLS)], lead=NDEV)
    mods = got[0].transpose(1, 0, 2).reshape(DEPTH, N_MODS * D)
    cmods = got[1].transpose(1, 0, 2).reshape(DEPTH, N_MODS * D)
    P = [[jnp.stack([cmods[l, k * D:(k + 1) * D], mods[l, k * D:(k + 1) * D]]).reshape(2, 1, D) for k in range(N_MODS)]
         for l in range(DEPTH)]
    row = lambda a, l: a[l].reshape(1, 1, D)

    xs = jnp.concatenate([ctx[0], x[0]], axis=0)
    sp = tuple(w[n][0] for n in ("ssm_lambda_re", "ssm_lambda_im", "ssm_log_dt", "ssm_b_re", "ssm_b_im", "ssm_c_re", "ssm_c_im")) + \
        (full["ssm_d"].reshape(1, 1, D),)
    gm = (full["gmlp_w_in"][0], full["gmlp_b_in"], full["gmlp_ln_g"], full["gmlp_ln_b"], w["gmlp_w_s"][0], w["gmlp_b_s"][0],
          full["gmlp_w_out"][0])
    pool_args = (full["pool_w"][0], w["pool_b"], w["pool_scale"])
    saved = []
    for l in range(DEPTH):
        sh1, sc1, gt1, sh2, sc2, gt2 = P[l]
        h1 = _pre_mixer(xs, sh1, sc1, _MXU if l in (1, 3) else F32)
        if l == 0:
            y, ms = _pool_fwd(h1, *pool_args), None
        elif l == 1:
            y, ms = _attn_mixer_fwd(h1, full["attn_w_qkv"][0], full["attn_w_o"][0], w["attn_sink"])
        elif l == 2:
            y, ms = _ssm_mixer_fwd(h1, sp, full["ssm_w_glu_a"][0], full["ssm_w_glu_b"][0])
        else:
            y, ms = _gmlp_mixer_fwd(h1, *gm)
        x1, h2 = _post_mixer(xs, y, gt1, row(w["ln1_g"], l), row(w["ln1_b"], l), sh2, sc2)
        f, (a, u) = _ffn_fwd(h2, full["ffn_w_up"][l], full["ffn_conv_w"][l], w["ffn_conv_b"][l][None, :], full["ffn_w_down"][l])
        x2 = _post_ffn(x1, f, gt2, row(w["ln2_g"], l), row(w["ln2_b"], l))
        saved.append((xs, h1, y, ms, x1, h2, a, u, f))
        xs = x2
    loss, dxs = _loss_head(xs, loss_target[0])

    g = {n: [None] * DEPTH for n in ("ln1_g", "ln1_b", "ln2_g", "ln2_b", "ffn_w_up", "ffn_conv_w", "ffn_conv_b", "ffn_w_down")}
    dP = [None] * DEPTH
    for l in reversed(range(DEPTH)):
        sh1, sc1, gt1, sh2, sc2, gt2 = P[l]
        x0, h1, y, ms, x1, h2, a, u, f = saved[l]
        dx1, df, dgt2, g["ln2_g"][l], g["ln2_b"][l] = _post_ffn_bwd(x1, f, dxs, gt2, row(w["ln2_g"], l), row(w["ln2_b"], l))
        dh2, g["ffn_w_up"][l], g["ffn_conv_w"][l], g["ffn_conv_b"][l], g["ffn_w_down"][l] = _ffn_bwd(
            df, h2, a, u, full["ffn_w_up"][l], full["ffn_conv_w"][l], w["ffn_conv_b"][l][None, :], full["ffn_w_down"][l])
        dx0, dy, dgt1, g["ln1_g"][l], g["ln1_b"][l], dsh2, dsc2 = _post_mixer_bwd(
            x0, y, dx1, dh2, gt1, row(w["ln1_g"], l), row(w["ln1_b"], l), sh2, sc2)
        if l == 0:
            dh, g["pool_w"], g["pool_b"], g["pool_scale"] = _pool_bwd(h1, dy, *pool_args)
            dhs = [dh]
        elif l == 1:
            dh, g["attn_w_qkv"], g["attn_w_o"], g["attn_sink"] = _attn_mixer_bwd(dy, h1, ms, full["attn_w_qkv"][0], full["attn_w_o"][0])
            dhs = [dh]
        elif l == 2:
            dhs, dsp, dd, g["ssm_w_glu_a"], g["ssm_w_glu_b"] = _ssm_mixer_bwd(dy, h1, ms, sp, full["ssm_w_glu_a"][0], full["ssm_w_glu_b"][0])
            for n, d_ in zip(("ssm_lambda_re", "ssm_lambda_im", "ssm_log_dt", "ssm_b_re", "ssm_b_im", "ssm_c_re", "ssm_c_im"), dsp):
                g[n] = d_
            g["ssm_d"] = dd.reshape(1, D)
        else:
            (dh, g["gmlp_w_in"], g["gmlp_b_in"], g["gmlp_ln_g"], g["gmlp_ln_b"], g["gmlp_w_s"], g["gmlp_b_s"],
             g["gmlp_w_out"]) = _gmlp_mixer_bwd(dy, h1, ms, *gm)
            dhs = [dh]
        dxs, dsh1, dsc1 = _pre_mixer_bwd(x0, dhs, dx0, sh1, sc1)
        dP[l] = (dsh1, dsc1, dgt1, dsh2, dsc2, dgt2)
    grad_x = dxs[NCTX:][None]
    dmods = jnp.stack([jnp.concatenate([p[1, 0] for p in dP[l]]) for l in range(DEPTH)])
    dcmods = jnp.stack([jnp.concatenate([p[0, 0] for p in dP[l]]) for l in range(DEPTH)])

    gfull = {n: (jnp.stack(g[n]) if isinstance(g[n], list) else g[n]) for n in g}
    sharded_names = [n for n, _ in _MM_SHARDED] + [n for n, _ in _VEC_SHARDED]
    sharded_axes = dict(_MM_SHARDED + _VEC_SHARDED)

    def as_param(n, a):
        shard = w[n].shape
        ax = sharded_axes.get(n)
        fs = tuple(s * NDEV if i == ax else s for i, s in enumerate(shard))
        return a.reshape(fs)

    rep = jnp.concatenate([as_param(n, gfull[n]).reshape(-1) for n in _REPLICATED])
    n_rep = rep.shape[0]
    rep = jnp.pad(rep, (0, -n_rep % (NDEV * SLAB_W))).reshape(NDEV, -1)
    by_dev = lambda a: a.reshape(DEPTH, NDEV, ADA_COLS).transpose(1, 0, 2)
    big = [_scattered(as_param(n, gfull[n]), ax).astype(_MXU) for n, ax in _MM_SHARDED]
    parts = [_scattered(as_param(n, gfull[n]), ax) for n, ax in _VEC_SHARDED] + [rep, by_dev(dmods), by_dev(dcmods)]
    grads = {n: r for (n, _), r in zip(_MM_SHARDED, _reduce_scatter(big, "scatter_grads"))}
    grads_in = _exchange([_pack(parts, lead=NDEV)], "scatter_small")[0]
    shapes = [w[n].shape for n, _ in _VEC_SHARDED] + [(rep.shape[1],), (DEPTH, ADA_COLS), (DEPTH, ADA_COLS)]
    red = _unpack(_sum_slabs(grads_in), shapes)
    grads.update({n: r for (n, _), r in zip(_VEC_SHARDED, red)})
    rep_mine, dcm = red[-3], red[-1]
    dm_all = _unpack(grads_in, shapes, lead=NDEV)[-2]

    e_rows = jnp.concatenate([dm_all, dcm[None], jnp.zeros((PAD_ROWS - NDEV - 1, DEPTH, ADA_COLS), F32)], axis=0)
    grads["ada_w"] = jnp.stack([_mm(silu_cc, e_rows[:, l], ta=True, name="ada_dw") for l in range(DEPTH)])
    ada_b_blk = jnp.sum(e_rows, axis=0)
    dcm_rows = jnp.concatenate([dcm[None], jnp.zeros((PAD_ROWS - 1, DEPTH, ADA_COLS), F32)], axis=0)
    cpart = sum(_mm(dcm_rows[:, l], w["ada_w"][l], tb=True, name="ada_dc")[0] for l in range(DEPTH))

    small_all = _gather([_pack([rep_mine, ada_b_blk, cpart])], "gather_small")[0]
    sm = _unpack(small_all, [rep_mine.shape, (DEPTH, ADA_COLS), (D,)], lead=NDEV)
    rep_full = sm[0].reshape(-1)[:n_rep]
    off = 0
    for n in _REPLICATED:
        k = int(np.prod(w[n].shape))
        grads[n] = rep_full[off:off + k].reshape(w[n].shape)
        off += k
    grads["ada_b"] = sm[1].transpose(1, 0, 2).reshape(DEPTH, N_MODS * D)
    csum = _unpack(_sum_slabs(small_all), [rep_mine.shape, (DEPTH, ADA_COLS), (D,)])[2]

    def dsilu(vv, dd):
        return (jax.vjp(jax.nn.silu, vv)[1](dd)[0],)

    grads["c_ctx"] = _rows(dsilu, [jnp.broadcast_to(w["c_ctx"][None], (SUBLANE, D)), jnp.broadcast_to(csum[None], (SUBLANE, D))],
                           [], [(D, F32)], [], name="dsilu", tm=SUBLANE)[0][0]

    delta, new_m, new_v = {}, {}, {}
    for n in _WEIGHTS:
        delta[n], new_m[n], new_v[n] = _adamw(w[n], grads[n], m[n], v[n])
    loss = lax.psum(loss[0, 0], ("x", "y", "c"))
    return loss, grad_x, grads, delta, new_m, new_v


def kernel(x, c, ctx, c_ctx, ada_w, ada_b, ln1_g, ln1_b, ln2_g, ln2_b, ffn_w_up, ffn_conv_w, ffn_conv_b, ffn_w_down, pool_w, pool_b, pool_scale, attn_w_qkv, attn_w_o, attn_sink, ssm_lambda_re, ssm_lambda_im, ssm_log_dt, ssm_b_re, ssm_b_im, ssm_c_re, ssm_c_im, ssm_d, ssm_w_glu_a, ssm_w_glu_b, gmlp_w_in, gmlp_b_in, gmlp_ln_g, gmlp_ln_b, gmlp_w_s, gmlp_b_s, gmlp_w_out, loss_target, m_c_ctx, m_ada_w, m_ada_b, m_ln1_g, m_ln1_b, m_ln2_g, m_ln2_b, m_ffn_w_up, m_ffn_conv_w, m_ffn_conv_b, m_ffn_w_down, m_pool_w, m_pool_b, m_pool_scale, m_attn_w_qkv, m_attn_w_o, m_attn_sink, m_ssm_lambda_re, m_ssm_lambda_im, m_ssm_log_dt, m_ssm_b_re, m_ssm_b_im, m_ssm_c_re, m_ssm_c_im, m_ssm_d, m_ssm_w_glu_a, m_ssm_w_glu_b, m_gmlp_w_in, m_gmlp_b_in, m_gmlp_ln_g, m_gmlp_ln_b, m_gmlp_w_s, m_gmlp_b_s, m_gmlp_w_out, v_c_ctx, v_ada_w, v_ada_b, v_ln1_g, v_ln1_b, v_ln2_g, v_ln2_b, v_ffn_w_up, v_ffn_conv_w, v_ffn_conv_b, v_ffn_w_down, v_pool_w, v_pool_b, v_pool_scale, v_attn_w_qkv, v_attn_w_o, v_attn_sink, v_ssm_lambda_re, v_ssm_lambda_im, v_ssm_log_dt, v_ssm_b_re, v_ssm_b_im, v_ssm_c_re, v_ssm_c_im, v_ssm_d, v_ssm_w_glu_a, v_ssm_w_glu_b, v_gmlp_w_in, v_gmlp_b_in, v_gmlp_ln_g, v_gmlp_ln_b, v_gmlp_w_s, v_gmlp_b_s, v_gmlp_w_out):
    args = dict(locals())
    w = {n: args[n] for n in _WEIGHTS}
    m = {n: args["m_" + n] for n in _WEIGHTS}
    v = {n: args["v_" + n] for n in _WEIGHTS}
    loss, grad_x, grads, delta, new_m, new_v = _step(x, c, ctx, loss_target, w, m, v)
    return (loss, grad_x, *[grads[n] for n in _WEIGHTS], *[delta[n] for n in _WEIGHTS],
            *[new_m[n] for n in _WEIGHTS], *[new_v[n] for n in _WEIGHTS])
```

```python
import functools
import math

import jax
import jax.numpy as jnp
import numpy as np
from jax import lax
from jax.experimental import pallas as pl
from jax.experimental.pallas import tpu as pltpu

D = 1024
SEQ = 4096
NCTX = 256
T = NCTX + SEQ
DEPTH = 4
NDEV = 8
GRID_W = 64
ALPHA = (2.0 * DEPTH) ** 0.25
LN_EPS = 1e-5
FFN_H = 2816
HEAD_DIM = 64
NQH, NKVH, GQA = 16, 4, 4
WINDOW = 128
ABLK = 128
NEG_INF = -1e30
ROPE_BASE = 10000.0
POOL_WINDOWS = (2, 4, 8, 16)
SSM_G, SSM_P, SSM_C = 64, 64, 16
GM_HALF = 2048
GM_HEADS = 8
GM_HD = GM_HALF // GM_HEADS
GM_CHUNK = 128
B1, B2, LR, EPS, WD, STEP = 0.9, 0.999, 0.001, 1e-8, 0.01, 10

LANE = 128
SUBLANE = 8
VMEM_LIMIT = 56 * 1024 * 1024
MM_OUT_TILE_BYTES = 6 * 1024 * 1024
TM = 256
NT = T // TM

_MXU = jnp.bfloat16
F32 = jnp.float32


def _pcall(body, **kw):
    return pl.pallas_call(body, **kw)


def _cp(sem):
    return pltpu.CompilerParams(dimension_semantics=sem, vmem_limit_bytes=VMEM_LIMIT)


def _tile(dim, pref, align):
    best = None
    for t in range(align, min(dim, pref) + 1, align):
        if dim % t == 0:
            best = t
    return dim if best is None else best


def _mm(a, b, *, ta=False, tb=False, out_dtype=F32, name):
    if ta:
        a = a.astype(_MXU).T
    M, K = a.shape
    if tb:
        N, K2 = b.shape
    else:
        K2, N = b.shape
    assert K == K2, (a.shape, b.shape, ta, tb)
    tm = _tile(M, 2304, 16)
    tn = _tile(N, max(512, MM_OUT_TILE_BYTES // (4 * tm)), LANE)
    tk = _tile(K, 2304, LANE)
    nk = K // tk
    dims = (((1,), (1,) if tb else (0,)), ((), ()))

    def body(a_ref, b_ref, o_ref, acc_ref):
        k = pl.program_id(2)
        r = lax.dot_general(a_ref[...].astype(_MXU), b_ref[...].astype(_MXU), dims, preferred_element_type=F32)
        if nk == 1:
            o_ref[...] = r.astype(o_ref.dtype)
            return

        @pl.when(k == 0)
        def _():
            acc_ref[...] = r

        @pl.when(k > 0)
        def _():
            acc_ref[...] += r

        @pl.when(k == nk - 1)
        def _():
            o_ref[...] = acc_ref[...].astype(o_ref.dtype)

    a_spec = pl.BlockSpec((tm, tk), lambda j, i, k: (i, k))
    b_spec = pl.BlockSpec((tn, tk), lambda j, i, k: (j, k)) if tb else pl.BlockSpec((tk, tn), lambda j, i, k: (k, j))
    return _pcall(
        body, name=name, grid=(N // tn, M // tm, nk), in_specs=[a_spec, b_spec],
        out_specs=pl.BlockSpec((tm, tn), lambda j, i, k: (i, j)),
        out_shape=jax.ShapeDtypeStruct((M, N), out_dtype),
        scratch_shapes=[pltpu.VMEM((tm, tn), F32)],
        compiler_params=_cp(("parallel", "parallel", "arbitrary")),
    )(a, b)


def _rows(fn, rows, pars, out_rows, out_pars, *, name, tm=TM, transposed=()):
    R = rows[0].shape[0]
    nt = R // tm
    nct = NCTX // tm
    n_r, n_p, n_or, n_op = len(rows), len(pars), len(out_rows), len(out_pars)

    def sel(S):
        if S == 1:
            return lambda i: 0
        return lambda i: jnp.where(i < nct, 0, 1)

    def body(*refs):
        r_in = refs[:n_r]
        p_in = refs[n_r:n_r + n_p]
        r_out = refs[n_r + n_p:n_r + n_p + n_or]
        p_out = refs[n_r + n_p + n_or:n_r + n_p + n_or + n_op]
        t_out = refs[n_r + n_p + n_or + n_op:]
        i = pl.program_id(0)
        vals = [r[...].astype(F32) for r in r_in] + [p[0] for p in p_in]
        outs = fn(*vals)
        for r, v in zip(r_out, outs[:n_or]):
            r[...] = v.astype(r.dtype)
        for r, k in zip(t_out, transposed):
            r[...] = outs[k].T.astype(r.dtype)
        for (S, _), r, v in zip(out_pars, p_out, outs[n_or:]):
            first = (i == 0) if S == 1 else jnp.logical_or(i == 0, i == nct)

            @pl.when(first)
            def _():
                r[0] = v

            @pl.when(jnp.logical_not(first))
            def _():
                r[0] += v

    def pspec(shape):
        S = shape[0]
        rest = tuple(shape[1:])
        s = sel(S)
        return pl.BlockSpec((1,) + rest, lambda i: (s(i),) + (0,) * len(rest))

    in_specs = [pl.BlockSpec((tm, r.shape[1]), lambda i: (i, 0)) for r in rows] + [pspec(p.shape) for p in pars]
    out_specs = [pl.BlockSpec((tm, w), lambda i: (i, 0)) for w, _ in out_rows] + [pspec((S,) + tuple(sh)) for S, sh in out_pars] + \
                [pl.BlockSpec((out_rows[k][0], tm), lambda i: (0, i)) for k in transposed]
    out_shape = [jax.ShapeDtypeStruct((R, w), dt) for w, dt in out_rows] + \
                [jax.ShapeDtypeStruct((S,) + tuple(sh), F32) for S, sh in out_pars] + \
                [jax.ShapeDtypeStruct((out_rows[k][0], R), out_rows[k][1]) for k in transposed]
    res = _pcall(body, name=name, grid=(nt,), in_specs=in_specs, out_specs=out_specs, out_shape=out_shape,
                 compiler_params=_cp(("arbitrary",)))(*rows, *pars)
    return res


def _ln(z, g, b):
    mu = jnp.mean(z, axis=-1, keepdims=True)
    var = jnp.mean(jnp.square(z - mu), axis=-1, keepdims=True)
    return (z - mu) * lax.rsqrt(var + LN_EPS) * g + b


def _f1(x, sh, sc):
    return x * (1.0 + sc) + sh


def _f2(x, y, gt, g, b, sh, sc):
    x1 = _ln(ALPHA * x + gt * y, g, b)
    return x1, x1 * (1.0 + sc) + sh


def _f3(x1, f, gt, g, b):
    return _ln(ALPHA * x1 + gt * f, g, b)


def _pre_mixer(x, sh, sc, dtype):
    return _rows(lambda x, sh, sc: (_f1(x, sh, sc),), [x], [sh, sc], [(D, dtype)], [], name="pre_mixer")[0]


def _pre_mixer_bwd(x, dhs, dx_prev, sh, sc):
    n = len(dhs)

    def fn(x, *rest):
        dh = rest[0]
        for t in rest[1:n]:
            dh = dh + t
        dxp, sh, sc = rest[n], rest[n + 1], rest[n + 2]
        _, vjp = jax.vjp(_f1, x, sh, sc)
        dx, dsh, dsc = vjp(dh)
        return dxp + dx, dsh, dsc

    return _rows(fn, [x, *dhs, dx_prev], [sh, sc], [(D, F32)], [(2, (1, D)), (2, (1, D))], name="pre_mixer_bwd")


def _post_mixer(x, y, gt, g, b, sh, sc):
    return _rows(_f2, [x, y], [gt, g, b, sh, sc], [(D, F32), (D, _MXU)], [], name="post_mixer", transposed=(1,))


def _post_mixer_bwd(x, y, dx1, dh2, gt, g, b, sh, sc):
    def fn(x, y, dx1, dh2a, dh2b, gt, g, b, sh, sc):
        _, vjp = jax.vjp(_f2, x, y, gt, g, b, sh, sc)
        return vjp((dx1, dh2a + dh2b))

    return _rows(fn, [x, y, dx1, *dh2], [gt, g, b, sh, sc], [(D, F32), (D, F32)],
                 [(2, (1, D)), (1, (1, D)), (1, (1, D)), (2, (1, D)), (2, (1, D))], name="post_mixer_bwd")


def _post_ffn(x1, f, gt, g, b):
    return _rows(lambda *a: (_f3(*a),), [x1, f], [gt, g, b], [(D, F32)], [], name="post_ffn")[0]


def _post_ffn_bwd(x1, f, dx2, gt, g, b):
    def fn(x1, f, dx2, gt, g, b):
        _, vjp = jax.vjp(_f3, x1, f, gt, g, b)
        return vjp(dx2)

    return _rows(fn, [x1, f, dx2], [gt, g, b], [(D, F32), (D, _MXU)],
                 [(2, (1, D)), (1, (1, D)), (1, (1, D))], name="post_ffn_bwd")


def _halo_specs(tm, w, col, active=None):
    r8 = tm // SUBLANE
    act = (lambda j, r: r) if active is None else (lambda j, r: jnp.where(active(j), r, 0))
    return [
        pl.BlockSpec((SUBLANE, w), lambda j, i: (act(j, jnp.maximum(i * r8 - 1, 0)), col(j))),
        pl.BlockSpec((tm, w), lambda j, i: (act(j, i), col(j))),
        pl.BlockSpec((SUBLANE, w), lambda j, i: (act(j, jnp.minimum((i + 1) * r8, T // SUBLANE - 1)), col(j))),
    ]


def _seg_flags(i, tm):
    nct = NCTX // tm
    first = jnp.logical_or(i == 0, i == nct)
    last = jnp.logical_or(i == nct - 1, i == T // tm - 1)
    return first, last


def _shift_rows(cur, prev8, next8, first, last):
    tm = cur.shape[0]
    rid = lax.broadcasted_iota(jnp.int32, cur.shape, 0)
    pr = jnp.where(first, 0.0, prev8[SUBLANE - 1:SUBLANE, :])
    nx = jnp.where(last, 0.0, next8[0:1, :])
    up = jnp.where(rid == 0, pr, pltpu.roll(cur, 1, 0))
    dn = jnp.where(rid == tm - 1, nx, pltpu.roll(cur, tm - 1, 0))
    return up, dn


FFN_TC = 1408
FFN_NCT = FFN_H // FFN_TC


def _conv3(cur, prev8, next8, w3, first, last):
    up, dn = _shift_rows(cur, prev8, next8, first, last)
    return up * w3[0:1] + cur * w3[1:2] + dn * w3[2:3], up, dn


def _ffn_mid(a, cw, cb):
    def body(vp, vc, vn, gp, gc, gn, cwv, cwg, cbv, cbg, o_ref, ot_ref):
        first, last = _seg_flags(pl.program_id(1), TM)
        val = _conv3(vc[...], vp[...], vn[...], cwv[...], first, last)[0] + cbv[...]
        gate = _conv3(gc[...], gp[...], gn[...], cwg[...], first, last)[0] + cbg[...]
        u = val * jax.nn.silu(gate)
        o_ref[...] = u.astype(o_ref.dtype)
        ot_ref[...] = u.T.astype(ot_ref.dtype)

    specs = _halo_specs(TM, FFN_TC, lambda j: j) + _halo_specs(TM, FFN_TC, lambda j: j + FFN_NCT)
    specs += [pl.BlockSpec((3, FFN_TC), lambda j, i: (0, j)), pl.BlockSpec((3, FFN_TC), lambda j, i: (0, j + FFN_NCT)),
              pl.BlockSpec((1, FFN_TC), lambda j, i: (0, j)), pl.BlockSpec((1, FFN_TC), lambda j, i: (0, j + FFN_NCT))]
    return _pcall(body, name="ffn_mid", grid=(FFN_NCT, NT), in_specs=specs,
                  out_specs=[pl.BlockSpec((TM, FFN_TC), lambda j, i: (i, j)), pl.BlockSpec((FFN_TC, TM), lambda j, i: (j, i))],
                  out_shape=[jax.ShapeDtypeStruct((T, FFN_H), _MXU), jax.ShapeDtypeStruct((FFN_H, T), _MXU)],
                  compiler_params=_cp(("parallel", "arbitrary")))(a, a, a, a, a, a, cw, cw, cb, cb)


def _ext_rows(p_ref, c_ref, n_ref, first, last):
    p8 = jnp.where(first, 0.0, p_ref[...])
    n8 = jnp.where(last, 0.0, n_ref[...])
    return jnp.concatenate([p8, c_ref[...], n8], axis=0)


def _ffn_mid_bwd(a, du, cw, cb):
    E = TM + 2 * SUBLANE
    ctr = slice(SUBLANE, SUBLANE + TM)

    def body(vp, vc, vn, gp, gc, gn, dp, dc, dn_, cwv, cwg, cbv, cbg, dav_ref, dag_ref, dcwv, dcwg, dcbv, dcbg):
        i = pl.program_id(1)
        first, last = _seg_flags(i, TM)
        ev = _ext_rows(vp, vc, vn, first, last)
        eg = _ext_rows(gp, gc, gn, first, last)
        edu = _ext_rows(dp, dc, dn_, first, last)
        wv, wg = cwv[...], cwg[...]
        vup, vdn = pltpu.roll(ev, 1, 0), pltpu.roll(ev, E - 1, 0)
        gup, gdn = pltpu.roll(eg, 1, 0), pltpu.roll(eg, E - 1, 0)
        val = vup * wv[0:1] + ev * wv[1:2] + vdn * wv[2:3] + cbv[...]
        gate = gup * wg[0:1] + eg * wg[1:2] + gdn * wg[2:3] + cbg[...]
        sg = jax.nn.sigmoid(gate)
        dval = edu * (gate * sg)
        dgate = edu * val * (sg * (1.0 + gate * (1.0 - sg)))

        def conv_t(d, w3):
            return (pltpu.roll(d, E - 1, 0) * w3[0:1] + d * w3[1:2] + pltpu.roll(d, 1, 0) * w3[2:3])[ctr]

        dav_ref[...] = conv_t(dval, wv).astype(dav_ref.dtype)
        dag_ref[...] = conv_t(dgate, wg).astype(dag_ref.dtype)

        def acc(ref, v):
            @pl.when(i == 0)
            def _():
                ref[...] = v

            @pl.when(i > 0)
            def _():
                ref[...] += v

        for dref, d, up, cur, dn, bref in ((dcwv, dval[ctr], vup[ctr], ev[ctr], vdn[ctr], dcbv),
                                           (dcwg, dgate[ctr], gup[ctr], eg[ctr], gdn[ctr], dcbg)):
            acc(dref, jnp.concatenate([jnp.sum(d * up, 0, keepdims=True), jnp.sum(d * cur, 0, keepdims=True),
                                       jnp.sum(d * dn, 0, keepdims=True)], axis=0))
            acc(bref, jnp.sum(d, 0, keepdims=True))

    specs = _halo_specs(TM, FFN_TC, lambda j: j) + _halo_specs(TM, FFN_TC, lambda j: j + FFN_NCT) + _halo_specs(TM, FFN_TC, lambda j: j)
    specs += [pl.BlockSpec((3, FFN_TC), lambda j, i: (0, j)), pl.BlockSpec((3, FFN_TC), lambda j, i: (0, j + FFN_NCT)),
              pl.BlockSpec((1, FFN_TC), lambda j, i: (0, j)), pl.BlockSpec((1, FFN_TC), lambda j, i: (0, j + FFN_NCT))]
    out_specs = [pl.BlockSpec((TM, FFN_TC), lambda j, i: (i, j)), pl.BlockSpec((TM, FFN_TC), lambda j, i: (i, j)),
                 pl.BlockSpec((3, FFN_TC), lambda j, i: (0, j)), pl.BlockSpec((3, FFN_TC), lambda j, i: (0, j)),
                 pl.BlockSpec((1, FFN_TC), lambda j, i: (0, j)), pl.BlockSpec((1, FFN_TC), lambda j, i: (0, j))]
    out_shape = [jax.ShapeDtypeStruct((T, FFN_H), _MXU)] * 2 + [jax.ShapeDtypeStruct((3, FFN_H), F32)] * 2 + \
                [jax.ShapeDtypeStruct((1, FFN_H), F32)] * 2
    dav, dag, dcwv, dcwg, dcbv, dcbg = _pcall(
        body, name="ffn_mid_bwd", grid=(FFN_NCT, NT), in_specs=specs, out_specs=out_specs, out_shape=out_shape,
        compiler_params=_cp(("parallel", "arbitrary")))(a, a, a, a, a, a, du, du, du, cw, cw, cb, cb)
    return dav, dag, jnp.concatenate([dcwv, dcwg], axis=1), jnp.concatenate([dcbv, dcbg], axis=1)


def _ffn_fwd(h2, w_up, cw, cb, w_down):
    a = _mm(h2, w_up, name="ffn_up")
    u, ut = _ffn_mid(a, cw, cb)
    f = _mm(u, w_down, name="ffn_down")
    return f, (a, ut)


def _ffn_bwd(df, h2t, a, ut, w_up, cw, cb, w_down):
    dw_down = _mm(ut, df, name="ffn_down_dw")
    du = _mm(df, w_down, tb=True, name="ffn_down_dx")
    dav, dag, dcw, dcb = _ffn_mid_bwd(a, du, cw, cb)
    dw_up = jnp.concatenate([_mm(h2t, dav, name="ffn_up_dw"), _mm(h2t, dag, name="ffn_up_dw")], axis=1)
    dh2 = [_mm(dav, w_up[:, :FFN_H], tb=True, name="ffn_up_dx"), _mm(dag, w_up[:, FFN_H:], tb=True, name="ffn_up_dx")]
    return dh2, dw_up, dcw, dcb, dw_down


def _winsum(e, lo, hi):
    n = e.shape[0]
    acc = None
    for o in range(lo, hi + 1):
        t = e if o == 0 else pltpu.roll(e, (-o) % n, 0)
        acc = t if acc is None else acc + t
    return acc


def _pool_cnt(i, w, rows, off):
    nct = NCTX // TM
    seg_len = jnp.where(i < nct, NCTX, SEQ)
    seg_tile = jnp.where(i < nct, i, i - nct)
    pos = lax.broadcasted_iota(jnp.int32, (rows, 1), 0) - off + seg_tile * TM
    lo = jnp.clip(pos - w // 2, 0, seg_len)
    hi = jnp.clip(pos - w // 2 + w, 0, seg_len)
    return jnp.maximum(hi - lo, 1).astype(F32)


def _pool_fwd(h, pw, pb, ps):
    def body(hp, hc, hn, w_ref, b_ref, s_ref, o_ref):
        i = pl.program_id(1)
        first, last = _seg_flags(i, TM)
        e = _ext_rows(hp, hc, hn, first, last)
        outs = []
        for g, w in enumerate(POOL_WINDOWS):
            sl = slice(256 * g, 256 * (g + 1))
            eg = e[:, sl]
            mean = _winsum(eg, -(w // 2), w // 2 - 1)[SUBLANE:SUBLANE + TM] / _pool_cnt(i, w, TM, 0)
            mixed = mean - hc[:, sl]
            outs.append(jnp.dot(mixed.astype(_MXU), w_ref[g].astype(_MXU), preferred_element_type=F32))
        o_ref[...] = (jnp.concatenate(outs, axis=1) + b_ref[...]) * s_ref[...]

    full = lambda *s: pl.BlockSpec(s, lambda j, i: (0,) * len(s))
    return _pcall(body, name="pool_fwd", grid=(1, NT), in_specs=_halo_specs(TM, D, lambda j: 0) + [full(4, 256, 256), full(1, D), full(1, D)],
                  out_specs=pl.BlockSpec((TM, D), lambda j, i: (i, 0)), out_shape=jax.ShapeDtypeStruct((T, D), F32),
                  compiler_params=_cp(("parallel", "arbitrary")))(h, h, h, pw, pb, ps)


def _pool_bwd(h, dy, pw, pb, ps):
    E = TM + 2 * SUBLANE

    def body(hp, hc, hn, dp, dc, dn, w_ref, b_ref, s_ref, dh_ref, dw_ref, db_ref, ds_ref):
        i = pl.program_id(1)
        first, last = _seg_flags(i, TM)
        e = _ext_rows(hp, hc, hn, first, last)
        de = _ext_rows(dp, dc, dn, first, last)
        dys = de * s_ref[...]
        dhs, pre = [], []
        for g, w in enumerate(POOL_WINDOWS):
            sl = slice(256 * g, 256 * (g + 1))
            wg = w_ref[g].astype(_MXU)
            dyg = dys[:, sl].astype(_MXU)
            dmix = lax.dot_general(dyg, wg, (((1,), (1,)), ((), ())), preferred_element_type=F32)
            q = dmix / _pool_cnt(i, w, E, SUBLANE)
            dhs.append(_winsum(q, -(w // 2) + 1, w // 2)[SUBLANE:SUBLANE + TM] - dmix[SUBLANE:SUBLANE + TM])
            mean = _winsum(e[:, sl], -(w // 2), w // 2 - 1)[SUBLANE:SUBLANE + TM] / _pool_cnt(i, w, TM, 0)
            mixed = (mean - hc[:, sl]).astype(_MXU)
            pre.append(jnp.dot(mixed, wg, preferred_element_type=F32))
            dwg = lax.dot_general(mixed, dyg[SUBLANE:SUBLANE + TM], (((0,), (0,)), ((), ())), preferred_element_type=F32)

            @pl.when(i == 0)
            def _():
                dw_ref[g] = dwg

            @pl.when(i > 0)
            def _():
                dw_ref[g] += dwg
        dh_ref[...] = jnp.concatenate(dhs, axis=1)
        db = jnp.sum(dys[SUBLANE:SUBLANE + TM], 0, keepdims=True)
        ds = jnp.sum(dc[...] * (jnp.concatenate(pre, axis=1) + b_ref[...]), 0, keepdims=True)

        @pl.when(i == 0)
        def _():
            db_ref[...] = db
            ds_ref[...] = ds

        @pl.when(i > 0)
        def _():
            db_ref[...] += db
            ds_ref[...] += ds

    full = lambda *s: pl.BlockSpec(s, lambda j, i: (0,) * len(s))
    specs = _halo_specs(TM, D, lambda j: 0) + _halo_specs(TM, D, lambda j: 0) + [full(4, 256, 256), full(1, D), full(1, D)]
    return _pcall(body, name="pool_bwd", grid=(1, NT), in_specs=specs,
                  out_specs=[pl.BlockSpec((TM, D), lambda j, i: (i, 0)), full(4, 256, 256), full(1, D), full(1, D)],
                  out_shape=[jax.ShapeDtypeStruct((T, D), F32), jax.ShapeDtypeStruct((4, 256, 256), F32),
                             jax.ShapeDtypeStruct((1, D), F32), jax.ShapeDtypeStruct((1, D), F32)],
                  compiler_params=_cp(("parallel", "arbitrary")))(h, h, h, dy, dy, dy, pw, pb, ps)


def _rope_tables():
    half = HEAD_DIM // 4
    t = jnp.arange(SEQ)
    freqs = ROPE_BASE ** (-jnp.arange(half, dtype=F32) / half)
    ang_r = (t // GRID_W).astype(F32)[:, None] * freqs[None, :]
    ang_c = (t % GRID_W).astype(F32)[:, None] * freqs[None, :]
    cos = jnp.concatenate([jnp.cos(ang_r), jnp.cos(ang_r), jnp.cos(ang_c), jnp.cos(ang_c)], axis=1)
    sin = jnp.concatenate([-jnp.sin(ang_r), jnp.sin(ang_r), -jnp.sin(ang_c), jnp.sin(ang_c)], axis=1)
    cos = jnp.concatenate([jnp.ones((NCTX, HEAD_DIM), F32), cos], axis=0)
    sin = jnp.concatenate([jnp.zeros((NCTX, HEAD_DIM), F32), sin], axis=0)
    return jnp.tile(cos, (1, 2)), jnp.tile(sin, (1, 2))


QK_W = (NQH + NKVH) * HEAD_DIM
QKV_W = QK_W + NKVH * HEAD_DIM


def _rope(x, cos, sin, sign):
    def body(x_ref, c_ref, s_ref, o_ref):
        c = c_ref[...]
        s = s_ref[...] * sign
        lane = lax.broadcasted_iota(jnp.int32, (TM, LANE), 1)
        lo = (lane % 32) < 16
        for k in range(QK_W // LANE):
            xk = x_ref[:, LANE * k:LANE * (k + 1)]
            partner = jnp.where(lo, pltpu.roll(xk, LANE - 16, 1), pltpu.roll(xk, 16, 1))
            o_ref[:, LANE * k:LANE * (k + 1)] = (xk * c + partner * s).astype(o_ref.dtype)
        o_ref[:, QK_W:] = x_ref[:, QK_W:].astype(o_ref.dtype)

    return _pcall(body, name="rope", grid=(NT,),
                  in_specs=[pl.BlockSpec((TM, QKV_W), lambda i: (i, 0)), pl.BlockSpec((TM, LANE), lambda i: (i, 0)),
                            pl.BlockSpec((TM, LANE), lambda i: (i, 0))],
                  out_specs=pl.BlockSpec((TM, QKV_W), lambda i: (i, 0)),
                  out_shape=jax.ShapeDtypeStruct((T, QKV_W), _MXU), compiler_params=_cp(("parallel",)))(x, cos, sin)


NQB = T // ABLK
KPAD = T + 2 * ABLK
NKEY = NCTX + 3 * ABLK


def _stack_heads(ref):
    return jnp.concatenate([ref[g] for g in range(GQA)], axis=0)


def _sink_rows(s_ref):
    return jnp.concatenate([jnp.broadcast_to(s_ref[g], (ABLK, 1)) for g in range(GQA)], axis=0)


def _attn_mask(i):
    r = lax.broadcasted_iota(jnp.int32, (GQA * ABLK, NKEY), 0) % ABLK
    c = lax.broadcasted_iota(jnp.int32, (GQA * ABLK, NKEY), 1)
    n = i - NCTX // ABLK
    kpos = (n - 1) * ABLK + (c - NCTX)
    qpos = n * ABLK + r
    loc = (c >= NCTX) & (jnp.abs(kpos - qpos) <= WINDOW) & (kpos >= 0) & (kpos < SEQ) & (n >= 0)
    return (c < NCTX) | loc


def _attn_specs():
    qs = pl.BlockSpec((GQA, ABLK, HEAD_DIM), lambda h, i: (h, i, 0))
    kc = pl.BlockSpec((1, NCTX, HEAD_DIM), lambda h, i: (h, 0, 0))
    kl = [pl.BlockSpec((1, ABLK, HEAD_DIM), functools.partial(lambda h, i, d: (h, i + d, 0), d=d)) for d in range(3)]
    sk = pl.BlockSpec((GQA, 1, 1), lambda h, i: (h, 0, 0))
    return qs, kc, kl, sk


def _attn_fwd(q, k, v, sink):
    scale = HEAD_DIM ** -0.5

    def body(q_ref, kc, k0, k1, k2, vc, v0, v1, v2, s_ref, o_ref, l_ref):
        valid = _attn_mask(pl.program_id(1))
        kk = jnp.concatenate([kc[0], k0[0], k1[0], k2[0]], axis=0)
        vv = jnp.concatenate([vc[0], v0[0], v1[0], v2[0]], axis=0)
        s = lax.dot_general(_stack_heads(q_ref), kk, (((1,), (1,)), ((), ())), preferred_element_type=F32) * scale
        s = jnp.where(valid, s, NEG_INF)
        sk = _sink_rows(s_ref)
        m = jnp.maximum(jnp.max(s, axis=-1, keepdims=True), sk)
        p = jnp.exp(s - m)
        l = jnp.sum(p, axis=-1, keepdims=True) + jnp.exp(sk - m)
        o = jnp.dot((p / l).astype(_MXU), vv, preferred_element_type=F32).astype(o_ref.dtype)
        lse = m + jnp.log(l)
        for g in range(GQA):
            o_ref[g] = o[ABLK * g:ABLK * (g + 1)]
            l_ref[g] = lse[ABLK * g:ABLK * (g + 1)]

    qs, kc, kl, sk = _attn_specs()
    return _pcall(body, name="attn_fwd", grid=(NKVH, NQB), in_specs=[qs, kc, *kl, kc, *kl, sk],
                  out_specs=[qs, pl.BlockSpec((GQA, ABLK, 1), lambda h, i: (h, i, 0))],
                  out_shape=[jax.ShapeDtypeStruct((NQH, T, HEAD_DIM), _MXU), jax.ShapeDtypeStruct((NQH, T, 1), F32)],
                  compiler_params=_cp(("parallel", "arbitrary")))(q, k, k, k, k, v, v, v, v, sink)


def _attn_bwd(q, k, v, sink, lse, do):
    scale = HEAD_DIM ** -0.5

    def body(q_ref, kc, k0, k1, k2, vc, v0, v1, v2, s_ref, l_ref, do_ref, dq_ref, dk_ref, dv_ref, ds_ref):
        i = pl.program_id(1)

        @pl.when(i == 0)
        def _():
            dk_ref[...] = jnp.zeros_like(dk_ref)
            dv_ref[...] = jnp.zeros_like(dv_ref)
            ds_ref[...] = jnp.zeros_like(ds_ref)

        valid = _attn_mask(i)
        kk = jnp.concatenate([kc[0], k0[0], k1[0], k2[0]], axis=0)
        vv = jnp.concatenate([vc[0], v0[0], v1[0], v2[0]], axis=0)
        qst = _stack_heads(q_ref)
        dos = _stack_heads(do_ref)
        lse = _stack_heads(l_ref)
        s = lax.dot_general(qst, kk, (((1,), (1,)), ((), ())), preferred_element_type=F32) * scale
        s = jnp.where(valid, s, NEG_INF)
        p = jnp.exp(s - lse)
        psink = jnp.exp(_sink_rows(s_ref) - lse)
        dp = lax.dot_general(dos, vv, (((1,), (1,)), ((), ())), preferred_element_type=F32)
        delta = jnp.sum(p * dp, axis=-1, keepdims=True)
        ds = p * (dp - delta)
        dsk = psink * delta
        dsq = (ds * scale).astype(_MXU)
        dq = jnp.dot(dsq, kk, preferred_element_type=F32)
        for g in range(GQA):
            dq_ref[g] = dq[ABLK * g:ABLK * (g + 1)]
            ds_ref[g] += -jnp.sum(dsk[ABLK * g:ABLK * (g + 1)], axis=0, keepdims=True)
        dkk = lax.dot_general(dsq, qst, (((0,), (0,)), ((), ())), preferred_element_type=F32)
        dvv = lax.dot_general(p.astype(_MXU), dos, (((0,), (0,)), ((), ())), preferred_element_type=F32)
        loc = pl.ds(pl.multiple_of(i * ABLK, ABLK), 3 * ABLK)
        dk_ref[0, 0:NCTX, :] += dkk[:NCTX]
        dv_ref[0, 0:NCTX, :] += dvv[:NCTX]
        dk_ref[0, loc, :] += dkk[NCTX:]
        dv_ref[0, loc, :] += dvv[NCTX:]

    qs, kc, kl, sk = _attn_specs()
    ls = pl.BlockSpec((GQA, ABLK, 1), lambda h, i: (h, i, 0))
    kfull = pl.BlockSpec((1, KPAD, HEAD_DIM), lambda h, i: (h, 0, 0))
    return _pcall(body, name="attn_bwd", grid=(NKVH, NQB), in_specs=[qs, kc, *kl, kc, *kl, sk, ls, qs],
                  out_specs=[qs, kfull, kfull, sk],
                  out_shape=[jax.ShapeDtypeStruct((NQH, T, HEAD_DIM), F32), jax.ShapeDtypeStruct((NKVH, KPAD, HEAD_DIM), F32),
                             jax.ShapeDtypeStruct((NKVH, KPAD, HEAD_DIM), F32), jax.ShapeDtypeStruct((NQH, 1, 1), F32)],
                  compiler_params=_cp(("parallel", "arbitrary")))(q, k, k, k, k, v, v, v, v, sink, lse, do)


def _split_heads(x, nh):
    return x.reshape(T, nh, HEAD_DIM).transpose(1, 0, 2)


def _merge_heads(x):
    return x.transpose(1, 0, 2).reshape(T, -1)


def _pad_keys(x):
    z = jnp.zeros((x.shape[0], ABLK, HEAD_DIM), x.dtype)
    return jnp.concatenate([x[:, :NCTX], z, x[:, NCTX:], z], axis=1)


def _unpad_keys(x):
    return jnp.concatenate([x[:, :NCTX], x[:, NCTX + ABLK:NCTX + ABLK + SEQ]], axis=1)


def _attn_mixer_fwd(h, w_qkv, w_o, sink):
    cos, sin = _rope_tables()
    qkv = _rope(_mm(h, w_qkv, name="attn_qkv"), cos, sin, 1.0)
    q = _split_heads(qkv[:, :NQH * HEAD_DIM], NQH)
    k = _pad_keys(_split_heads(qkv[:, NQH * HEAD_DIM:QK_W], NKVH))
    v = _pad_keys(_split_heads(qkv[:, QK_W:], NKVH))
    sk = sink.reshape(NQH, 1, 1)
    o, lse = _attn_fwd(q, k, v, sk)
    om = _merge_heads(o)
    y = _mm(om, w_o, name="attn_out")
    return y, (q, k, v, sk, lse, om)


def _attn_mixer_bwd(dy, h, saved, w_qkv, w_o):
    q, k, v, sk, lse, om = saved
    cos, sin = _rope_tables()
    dyb = dy.astype(_MXU)
    dw_o = _mm(om, dyb, ta=True, name="attn_out_dw")
    do = _split_heads(_mm(dyb, w_o, tb=True, out_dtype=_MXU, name="attn_out_dx"), NQH)
    dq, dk, dv, dsk = _attn_bwd(q, k, v, sk, lse, do)
    dqkv = jnp.concatenate([_merge_heads(dq), _merge_heads(_unpad_keys(dk)), _merge_heads(_unpad_keys(dv))], axis=1)
    dqkv = _rope(dqkv, cos, sin, -1.0)
    dw_qkv = _mm(h, dqkv, ta=True, name="attn_qkv_dw")
    dh = _mm(dqkv, w_qkv, tb=True, name="attn_qkv_dx")
    return dh, dw_qkv, dw_o, dsk.reshape(1, NQH)


SSM_S = SSM_G * SSM_P
SSM_SL = SSM_S // LANE
SSM_TS = 128
SSM_NTS = T // SSM_TS
SSM_NCT = NCTX // SSM_TS
SSM_JB = 4
SSM_NTR = 4
SSM_TR = T // SSM_NTR


def _to_slabs(cols):
    return jnp.swapaxes(jnp.stack(cols, axis=0), 0, 1)


def _from_slabs(s_ref):
    x = jnp.swapaxes(s_ref[...], 0, 1)
    return [jnp.concatenate([x[4 * hlf + q] for q in range(4)], axis=1).astype(_MXU) for hlf in range(2)]


def _proj3d(u, w_re, w_im):
    def body(u_ref, wr_ref, wi_ref, or_ref, oi_ref):
        for w_ref, o_ref in ((wr_ref, or_ref), (wi_ref, oi_ref)):
            cols = []
            for hlf in range(2):
                ub = u_ref[:, LANE * hlf:LANE * (hlf + 1)].astype(_MXU)
                r = jnp.dot(ub, w_ref[hlf].astype(_MXU), preferred_element_type=F32)
                cols += [r[:, LANE * q:LANE * (q + 1)] for q in range(4)]
            o_ref[...] = _to_slabs(cols)

    ws = pl.BlockSpec((2, LANE, 512), lambda i, j: (j, 0, 0))
    os_ = pl.BlockSpec((SSM_TR, 8, LANE), lambda i, j: (i, j, 0))
    return _pcall(body, name="ssm_proj", grid=(SSM_NTR, SSM_JB), in_specs=[pl.BlockSpec((SSM_TR, 2 * LANE), lambda i, j: (i, j)), ws, ws],
                  out_specs=[os_, os_], out_shape=[jax.ShapeDtypeStruct((T, SSM_SL, LANE), F32)] * 2,
                  compiler_params=_cp(("parallel", "parallel")))(u, w_re, w_im)


def _readout(s_re, s_im, w_re, w_im):
    def body(sr_ref, si_ref, wr_ref, wi_ref, o_ref):
        xs = [_from_slabs(sr_ref), _from_slabs(si_ref)]
        for hlf in range(2):
            acc = None
            for x, w_ref in zip(xs, (wr_ref, wi_ref)):
                r = jnp.dot(x[hlf], w_ref[hlf].astype(_MXU), preferred_element_type=F32)
                acc = r if acc is None else acc + r
            o_ref[:, LANE * hlf:LANE * (hlf + 1)] = acc

    ss = pl.BlockSpec((SSM_TR, 8, LANE), lambda i, j: (i, j, 0))
    ws = pl.BlockSpec((2, 512, LANE), lambda i, j: (j, 0, 0))
    return _pcall(body, name="ssm_readout", grid=(SSM_NTR, SSM_JB), in_specs=[ss, ss, ws, ws],
                  out_specs=pl.BlockSpec((SSM_TR, 2 * LANE), lambda i, j: (i, j)), out_shape=jax.ShapeDtypeStruct((T, D), F32),
                  compiler_params=_cp(("parallel", "parallel")))(s_re, s_im, w_re, w_im)


def _outer3d(s_re, s_im, y):
    def body(sr_ref, si_ref, y_ref, dr_ref, di_ref):
        i = pl.program_id(1)
        xs = [_from_slabs(sr_ref), _from_slabs(si_ref)]
        for hlf in range(2):
            yb = y_ref[:, LANE * hlf:LANE * (hlf + 1)].astype(_MXU)
            for x, d_ref in zip(xs, (dr_ref, di_ref)):
                r = lax.dot_general(yb, x[hlf], (((0,), (0,)), ((), ())), preferred_element_type=F32)

                @pl.when(i == 0)
                def _():
                    d_ref[hlf] = r

                @pl.when(i > 0)
                def _():
                    d_ref[hlf] += r

    ss = pl.BlockSpec((SSM_TR, 8, LANE), lambda j, i: (i, j, 0))
    ds = pl.BlockSpec((2, LANE, 512), lambda j, i: (j, 0, 0))
    return _pcall(body, name="ssm_outer", grid=(SSM_JB, SSM_NTR), in_specs=[ss, ss, pl.BlockSpec((SSM_TR, 2 * LANE), lambda j, i: (i, j))],
                  out_specs=[ds, ds], out_shape=[jax.ShapeDtypeStruct((8, LANE, 512), F32)] * 2,
                  compiler_params=_cp(("parallel", "arbitrary")))(s_re, s_im, y)


def _scan_order(order):
    n, c = SSM_NTS, SSM_NCT
    if order == "fwd":
        return (lambda i: i), False
    if order == "fwd_adj":
        return (lambda i: n - 1 - i), True
    if order == "rev":
        return (lambda i: jnp.where(i < c, c - 1 - i, n + c - 1 - i)), True
    if order == "rev_adj":
        return (lambda i: jnp.where(i < n - c, i + c, i - (n - c))), False
    raise ValueError(order)


def _scan(b_re, b_im, lam_re, lam_im, order):
    tile, down = _scan_order(order)

    def body(br_ref, bi_ref, lr_ref, li_ref, sr_ref, si_ref, cr, ci):
        @pl.when(pl.program_id(0) == 0)
        def _():
            cr[...] = jnp.zeros_like(cr)
            ci[...] = jnp.zeros_like(ci)

        lr = lr_ref[...]
        li = li_ref[...]

        def step(n, c):
            t = SSM_TS - 1 - n if down else n
            sr, si = c
            nr = lr * sr - li * si + br_ref[t]
            ni = lr * si + li * sr + bi_ref[t]
            sr_ref[t] = nr
            si_ref[t] = ni
            return nr, ni

        sr, si = lax.fori_loop(0, SSM_TS, step, (cr[...], ci[...]))
        cr[...] = sr
        ci[...] = si

    bs = pl.BlockSpec((SSM_TS, SSM_SL, LANE), lambda i: (tile(i), 0, 0))
    ps = pl.BlockSpec((SSM_SL, LANE), lambda i: (0, 0))
    return _pcall(body, name="ssm_scan_" + order, grid=(SSM_NTS,), in_specs=[bs, bs, ps, ps], out_specs=[bs, bs],
                  out_shape=[jax.ShapeDtypeStruct((T, SSM_SL, LANE), F32)] * 2,
                  scratch_shapes=[pltpu.VMEM((SSM_SL, LANE), F32)] * 2, compiler_params=_cp(("arbitrary",)))(b_re, b_im, lam_re, lam_im)


def _scan_adj(g_re, g_im, s_re, s_im, lam_re, lam_im, order):
    tile, down = _scan_order(order)

    def body(gr_ref, gi_ref, sr_ref, si_ref, lr_ref, li_ref, ar_ref, ai_ref, dlr_ref, dli_ref, cr, ci):
        @pl.when(pl.program_id(0) == 0)
        def _():
            cr[...] = jnp.zeros_like(cr)
            ci[...] = jnp.zeros_like(ci)
            dlr_ref[...] = jnp.zeros_like(dlr_ref)
            dli_ref[...] = jnp.zeros_like(dli_ref)

        lr = lr_ref[...]
        li = li_ref[...]

        def step(n, c):
            t = SSM_TS - 1 - n if down else n
            ar, ai, dr, di = c
            sr = sr_ref[t]
            si = si_ref[t]
            dr = dr + ar * sr + ai * si
            di = di + ai * sr - ar * si
            nr = gr_ref[t] + lr * ar + li * ai
            ni = gi_ref[t] + lr * ai - li * ar
            ar_ref[t] = nr
            ai_ref[t] = ni
            return nr, ni, dr, di

        ar, ai, dr, di = lax.fori_loop(0, SSM_TS, step, (cr[...], ci[...], dlr_ref[...], dli_ref[...]))
        cr[...] = ar
        ci[...] = ai
        dlr_ref[...] = dr
        dli_ref[...] = di

    bs = pl.BlockSpec((SSM_TS, SSM_SL, LANE), lambda i: (tile(i), 0, 0))
    ps = pl.BlockSpec((SSM_SL, LANE), lambda i: (0, 0))
    return _pcall(body, name="ssm_scan_" + order, grid=(SSM_NTS,), in_specs=[bs, bs, bs, bs, ps, ps], out_specs=[bs, bs, ps, ps],
                  out_shape=[jax.ShapeDtypeStruct((T, SSM_SL, LANE), F32)] * 2 + [jax.ShapeDtypeStruct((SSM_SL, LANE), F32)] * 2,
                  scratch_shapes=[pltpu.VMEM((SSM_SL, LANE), F32)] * 2,
                  compiler_params=_cp(("arbitrary",)))(g_re, g_im, s_re, s_im, lam_re, lam_im)


def _block_diag(x):
    x4 = x.reshape(8, 8, SSM_P, SSM_C)
    return jnp.einsum("jgpc,gh->jgphc", x4, jnp.eye(8, dtype=x.dtype)).reshape(8, 8 * SSM_P, 8 * SSM_C)


def _ssm_prep(lam_re, lam_im, log_dt, b_re, b_im, c_re, c_im):
    lam = lax.complex(lam_re, lam_im)
    dt = jnp.exp(log_dt)[:, None]
    lam_bar = jnp.exp(lam * dt)
    b_bar = ((lam_bar - 1.0) / lam)[..., None] * lax.complex(b_re, b_im)
    return (jnp.real(lam_bar).reshape(SSM_SL, LANE), jnp.imag(lam_bar).reshape(SSM_SL, LANE),
            _block_diag(jnp.real(b_bar)), _block_diag(jnp.imag(b_bar)),
            _block_diag(c_re.transpose(0, 2, 1)), _block_diag(-c_im.transpose(0, 2, 1)))


def _ssm_glue(h, yf, yr, d):
    return jax.nn.gelu(d * h + yf + yr)


def _glu(ga, gb):
    return ga * jax.nn.sigmoid(gb)


def _ssm_mixer_fwd(h, sp, w_a, w_b):
    lam_re, lam_im, log_dt, b_re, b_im, c_re, c_im, d_skip = sp
    ys, saved = [], []
    for di, order in enumerate(("fwd", "rev")):
        lr, li, wb_r, wb_i, wc_r, wc_i = _ssm_prep(lam_re[di], lam_im[di], log_dt[di], b_re[di], b_im[di], c_re[di], c_im[di])
        bu_r, bu_i = _proj3d(h, wb_r.transpose(0, 2, 1), wb_i.transpose(0, 2, 1))
        s_r, s_i = _scan(bu_r, bu_i, lr, li, order)
        ys.append(_readout(s_r, s_i, wc_r, wc_i))
        saved.append((s_r, s_i))
    g = _rows(lambda *a: (_ssm_glue(*a),), [h, ys[0], ys[1]], [d_skip], [(D, _MXU)], [], name="ssm_glue")[0]
    ga = _mm(g, w_a, name="ssm_glu_a")
    gb = _mm(g, w_b, name="ssm_glu_b")
    y = _rows(lambda *a: (_glu(*a),), [ga, gb], [], [(D, F32)], [], name="ssm_glu")[0]
    return y, (ys, saved, g, ga, gb)


def _ssm_mixer_bwd(dy, h, saved_all, sp, w_a, w_b):
    lam_re, lam_im, log_dt, b_re, b_im, c_re, c_im, d_skip = sp
    ys, saved, g, ga, gb = saved_all

    def glu_bwd(ga, gb, dy):
        _, vjp = jax.vjp(_glu, ga, gb)
        return vjp(dy)

    dga, dgb = _rows(glu_bwd, [ga, gb, dy], [], [(D, _MXU), (D, _MXU)], [], name="ssm_glu_bwd")
    dw_a = _mm(g, dga, ta=True, name="ssm_glu_a_dw")
    dw_b = _mm(g, dgb, ta=True, name="ssm_glu_b_dw")
    dg_a = _mm(dga, w_a, tb=True, name="ssm_glu_a_dx")
    dg_b = _mm(dgb, w_b, tb=True, name="ssm_glu_b_dx")

    def glue_bwd(h, yf, yr, dg_a, dg_b, d):
        _, vjp = jax.vjp(_ssm_glue, h, yf, yr, d)
        dh, dyl, _, dd = vjp(dg_a + dg_b)
        return dh, dyl, dd

    dh0, dyl, dd = _rows(glue_bwd, [h, ys[0], ys[1], dg_a, dg_b], [d_skip], [(D, F32), (D, F32)], [(1, (1, D))], name="ssm_glue_bwd")
    dhs = [dh0]
    dparams = []
    for di, (order, adj) in enumerate((("fwd", "fwd_adj"), ("rev", "rev_adj"))):
        args = (lam_re[di], lam_im[di], log_dt[di], b_re[di], b_im[di], c_re[di], c_im[di])
        (lr, li, wb_r, wb_i, wc_r, wc_i), prep_vjp = jax.vjp(_ssm_prep, *args)
        s_r, s_i = saved[di]
        dwc_r, dwc_i = _outer3d(s_r, s_i, dyl)
        g_r, g_i = _proj3d(dyl, wc_r.transpose(0, 2, 1), wc_i.transpose(0, 2, 1))
        a_r, a_i, dlr, dli = _scan_adj(g_r, g_i, s_r, s_i, lr, li, adj)
        dwb_r, dwb_i = _outer3d(a_r, a_i, h)
        dhs.append(_readout(a_r, a_i, wb_r, wb_i))
        dparams.append(prep_vjp((dlr, dli) + tuple(d.transpose(0, 2, 1) for d in (dwb_r, dwb_i, dwc_r, dwc_i))))
    dsp = [jnp.stack([dparams[0][k], dparams[1][k]], axis=0) for k in range(7)]
    return dhs, dsp, dd, dw_a, dw_b


NCH = T // GM_CHUNK


def _gm_specs():
    full = lambda *s: pl.BlockSpec(s, lambda i: (0,) * len(s))
    zu = pl.BlockSpec((GM_CHUNK, GM_HALF), lambda i: (i, 0))
    zv = pl.BlockSpec((GM_CHUNK, GM_HALF), lambda i: (i, 1))
    pars = [full(1, GM_HALF), pl.BlockSpec((1, GM_HALF), lambda i: (0, 1)), full(1, GM_HALF), full(1, GM_HALF),
            full(GM_HEADS, GM_CHUNK, GM_CHUNK), full(GM_HEADS, GM_CHUNK, 1)]
    return zu, zv, pars, full


def _gm_forward(zu_ref, zv_ref, bu_ref, bv_ref, g_ref, b_ref, ws_ref, bs_ref):
    u = jax.nn.gelu(zu_ref[...] + bu_ref[...])
    zv = jax.nn.gelu(zv_ref[...] + bv_ref[...])
    mu = jnp.mean(zv, axis=-1, keepdims=True)
    zc = zv - mu
    rstd = lax.rsqrt(jnp.mean(jnp.square(zc), axis=-1, keepdims=True) + LN_EPS)
    vhat = zc * rstd
    v = (vhat * g_ref[...] + b_ref[...]).astype(_MXU)
    gates = [jnp.dot(ws_ref[hd].astype(_MXU), v[:, GM_HD * hd:GM_HD * (hd + 1)], preferred_element_type=F32) + bs_ref[hd]
             for hd in range(GM_HEADS)]
    return u, vhat, rstd, v, jnp.concatenate(gates, axis=1)


def _gmlp_chunk(zp, b_in, ln_g, ln_b, w_s, b_s):
    def body(zu_ref, zv_ref, bu_ref, bv_ref, g_ref, b_ref, ws_ref, bs_ref, o_ref):
        u, _, _, _, gate = _gm_forward(zu_ref, zv_ref, bu_ref, bv_ref, g_ref, b_ref, ws_ref, bs_ref)
        o_ref[...] = (u * gate).astype(o_ref.dtype)

    zu, zv, pars, _ = _gm_specs()
    return _pcall(body, name="gmlp_chunk", grid=(NCH,), in_specs=[zu, zv, *pars], out_specs=zu,
                  out_shape=jax.ShapeDtypeStruct((T, GM_HALF), _MXU),
                  compiler_params=_cp(("parallel",)))(zp, zp, b_in, b_in, ln_g, ln_b, w_s, b_s)


def _gmlp_chunk_bwd(zp, do, b_in, ln_g, ln_b, w_s, b_s):
    def body(zu_ref, zv_ref, do_ref, bu_ref, bv_ref, g_ref, b_ref, ws_ref, bs_ref,
             dzu_ref, dzv_ref, dbu_ref, dbv_ref, dg_ref, db_ref, dws_ref, dbs_ref):
        i = pl.program_id(0)

        def acc(ref, val, idx=None):
            @pl.when(i == 0)
            def _():
                if idx is None:
                    ref[...] = val
                else:
                    ref[idx] = val

            @pl.when(i > 0)
            def _():
                if idx is None:
                    ref[...] += val
                else:
                    ref[idx] += val

        u, vhat, rstd, v, gate = _gm_forward(zu_ref, zv_ref, bu_ref, bv_ref, g_ref, b_ref, ws_ref, bs_ref)
        do = do_ref[...]
        du = do * gate
        dgate = do * u
        dvs = []
        for hd in range(GM_HEADS):
            sl = slice(GM_HD * hd, GM_HD * (hd + 1))
            dgh = dgate[:, sl]
            dghb = dgh.astype(_MXU)
            dvs.append(lax.dot_general(ws_ref[hd].astype(_MXU), dghb, (((0,), (0,)), ((), ())), preferred_element_type=F32))
            acc(dws_ref, lax.dot_general(dghb, v[:, sl], (((1,), (1,)), ((), ())), preferred_element_type=F32), hd)
            acc(dbs_ref, jnp.sum(dgh, axis=1, keepdims=True), hd)
        dv = jnp.concatenate(dvs, axis=1)
        acc(dg_ref, jnp.sum(dv * vhat, axis=0, keepdims=True))
        acc(db_ref, jnp.sum(dv, axis=0, keepdims=True))
        dvh = dv * g_ref[...]
        dzv = rstd * (dvh - jnp.mean(dvh, axis=-1, keepdims=True) - vhat * jnp.mean(dvh * vhat, axis=-1, keepdims=True))
        dpu = jax.vjp(jax.nn.gelu, zu_ref[...] + bu_ref[...])[1](du)[0]
        dpv = jax.vjp(jax.nn.gelu, zv_ref[...] + bv_ref[...])[1](dzv)[0]
        dzu_ref[...] = dpu.astype(dzu_ref.dtype)
        dzv_ref[...] = dpv.astype(dzv_ref.dtype)
        acc(dbu_ref, jnp.sum(dpu, axis=0, keepdims=True))
        acc(dbv_ref, jnp.sum(dpv, axis=0, keepdims=True))

    zu, zv, pars, full = _gm_specs()
    out_specs = [zu, zu, full(1, GM_HALF), full(1, GM_HALF), full(1, GM_HALF), full(1, GM_HALF),
                 full(GM_HEADS, GM_CHUNK, GM_CHUNK), full(GM_HEADS, GM_CHUNK, 1)]
    out_shape = [jax.ShapeDtypeStruct((T, GM_HALF), _MXU)] * 2 + [jax.ShapeDtypeStruct((1, GM_HALF), F32)] * 4 + \
                [jax.ShapeDtypeStruct((GM_HEADS, GM_CHUNK, GM_CHUNK), F32), jax.ShapeDtypeStruct((GM_HEADS, GM_CHUNK, 1), F32)]
    dzu, dzv, dbu, dbv, dg, db, dws, dbs = _pcall(
        body, name="gmlp_chunk_bwd", grid=(NCH,), in_specs=[zu, zv, zu, *pars], out_specs=out_specs, out_shape=out_shape,
        compiler_params=_cp(("arbitrary",)))(zp, zp, do, b_in, b_in, ln_g, ln_b, w_s, b_s)
    return jnp.concatenate([dzu, dzv], axis=1), jnp.concatenate([dbu, dbv], axis=1), dg, db, dws, dbs


def _gmlp_mixer_fwd(h, w_in, b_in, ln_g, ln_b, w_s, b_s, w_out):
    zp = _mm(h, w_in, name="gmlp_in")
    ug = _gmlp_chunk(zp, b_in, ln_g, ln_b, w_s, b_s[..., None])
    return _mm(ug, w_out, name="gmlp_out"), (zp, ug)


def _gmlp_mixer_bwd(dy, h, saved, w_in, b_in, ln_g, ln_b, w_s, b_s, w_out):
    zp, ug = saved
    dw_out = _mm(ug, dy, ta=True, name="gmlp_out_dw")
    do = _mm(dy, w_out, tb=True, name="gmlp_out_dx")
    dzp, db_in, dg, db, dws, dbs = _gmlp_chunk_bwd(zp, do, b_in, ln_g, ln_b, w_s, b_s[..., None])
    dw_in = _mm(h, dzp, ta=True, name="gmlp_in_dw")
    dh = _mm(dzp, w_in, tb=True, name="gmlp_in_dx")
    return dh, dw_in, db_in, dg, db, dws, dbs[..., 0], dw_out


def _loss_head(x, target):
    nct = NCTX // TM

    def body(x_ref, t_ref, l_ref, dx_ref):
        i = pl.program_id(0)
        err = jnp.where(i >= nct, x_ref[...] - t_ref[...], 0.0)
        dx_ref[...] = err * (1.0 / D)
        part = 0.5 * jnp.sum(jnp.sum(jnp.square(err), axis=-1, keepdims=True) * (1.0 / D), axis=0, keepdims=True)

        @pl.when(i == 0)
        def _():
            l_ref[...] = part

        @pl.when(i > 0)
        def _():
            l_ref[...] += part

    return _pcall(body, name="loss_head", grid=(NT,),
                  in_specs=[pl.BlockSpec((TM, D), lambda i: (i, 0)), pl.BlockSpec((TM, D), lambda i: (jnp.maximum(i - nct, 0), 0))],
                  out_specs=[pl.BlockSpec((1, 1), lambda i: (0, 0)), pl.BlockSpec((TM, D), lambda i: (i, 0))],
                  out_shape=[jax.ShapeDtypeStruct((1, 1), F32), jax.ShapeDtypeStruct((T, D), F32)],
                  compiler_params=_cp(("arbitrary",)))(x, target)


def _as2d(a):
    return a.reshape(-1, a.shape[-1])


def _adamw(w, g, m, v):
    shape = w.shape
    w2, g2, m2, v2 = _as2d(w), _as2d(g), _as2d(m), _as2d(v)
    R, C = w2.shape
    tr = _tile(R, 512, SUBLANE)
    c1 = 1.0 - B1 ** STEP
    c2 = 1.0 - B2 ** STEP

    def body(w_ref, g_ref, m_ref, v_ref, d_ref, nm_ref, nv_ref):
        g = g_ref[...]
        m = B1 * m_ref[...] + (1.0 - B1) * g
        v = B2 * v_ref[...] + (1.0 - B2) * jnp.square(g)
        nm_ref[...] = m
        nv_ref[...] = v
        d_ref[...] = -LR * ((m / c1) / (jnp.sqrt(v / c2) + EPS) + WD * w_ref[...])

    spec = pl.BlockSpec((tr, C), lambda i: (i, 0))
    outs = _pcall(body, name="adamw", grid=(R // tr,), in_specs=[spec] * 4, out_specs=[spec] * 3,
                  out_shape=[jax.ShapeDtypeStruct((R, C), F32)] * 3, compiler_params=_cp(("parallel",)))(w2, g2, m2, v2)
    return tuple(o.reshape(shape) for o in outs)


def _sum_slabs(x):
    n = x.shape[0]
    x = x.reshape(n, -1, x.shape[-1])
    _, R, C = x.shape
    tr = _tile(R, 256, 16)

    def body(x_ref, o_ref):
        acc = x_ref[0].astype(F32)
        for k in range(1, n):
            acc = acc + x_ref[k].astype(F32)
        o_ref[...] = acc

    return _pcall(body, name="sum_slabs", grid=(R // tr,), in_specs=[pl.BlockSpec((n, tr, C), lambda i: (0, i, 0))],
                  out_specs=pl.BlockSpec((tr, C), lambda i: (i, 0)), out_shape=jax.ShapeDtypeStruct((R, C), F32),
                  compiler_params=_cp(("parallel",)))(x)


def _comm_call(body, xs, out_shape, name):
    n = len(xs)
    hbm = pl.BlockSpec(memory_space=pl.ANY)
    return _pcall(body, name=name, in_specs=[hbm] * n, out_specs=[hbm] * n, out_shape=out_shape,
                  scratch_shapes=[pltpu.SemaphoreType.DMA((n, NDEV - 1)), pltpu.SemaphoreType.DMA((n, NDEV - 1)),
                                  pltpu.SemaphoreType.DMA((n,))],
                  compiler_params=pltpu.CompilerParams(has_side_effects=True))(*xs)


def _exchange(xs, name):
    n = len(xs)

    def body(*refs):
        x_refs, o_refs = refs[:n], refs[n:2 * n]
        send_sems, recv_sems, loc_sems = refs[2 * n:]
        mx, my, mc = lax.axis_index("x"), lax.axis_index("y"), lax.axis_index("c")
        me = 4 * mx + 2 * my + mc
        pending = []
        for a in range(n):
            mine = pltpu.make_async_copy(x_refs[a].at[me], o_refs[a].at[me], loc_sems.at[a])
            mine.start()
            pending.append(mine)
            for k in range(1, NDEV):
                px = 1 - mx if k & 4 else mx
                py = 1 - my if k & 2 else my
                pc = 1 - mc if k & 1 else mc
                cp = pltpu.make_async_remote_copy(
                    src_ref=x_refs[a].at[4 * px + 2 * py + pc], dst_ref=o_refs[a].at[me],
                    send_sem=send_sems.at[a, k - 1], recv_sem=recv_sems.at[a, k - 1],
                    device_id=(px, py, pc), device_id_type=pl.DeviceIdType.MESH)
                cp.start()
                pending.append(cp)
        for cp in pending:
            cp.wait()

    return _comm_call(body, xs, [jax.ShapeDtypeStruct(tuple(x.shape), x.dtype) for x in xs], name)


NCHIP = NDEV // 2


def _sibling_exchange(xs, name):
    n = len(xs)

    def body(*refs):
        x_refs, o_refs = refs[:n], refs[n:2 * n]
        send_sems, recv_sems, _ = refs[2 * n:]
        mx, my, mc = lax.axis_index("x"), lax.axis_index("y"), lax.axis_index("c")
        pending = []
        for a in range(n):
            for b in range(NCHIP):
                cp = pltpu.make_async_remote_copy(
                    src_ref=x_refs[a].at[2 * b + (1 - mc)], dst_ref=o_refs[a].at[b],
                    send_sem=send_sems.at[a, b], recv_sem=recv_sems.at[a, b],
                    device_id=(mx, my, 1 - mc), device_id_type=pl.DeviceIdType.MESH)
                cp.start()
                pending.append(cp)
        for cp in pending:
            cp.wait()

    return _comm_call(body, xs, [jax.ShapeDtypeStruct((NCHIP,) + tuple(x.shape[1:]), x.dtype) for x in xs], name)


def _chip_exchange(xs, name):
    n = len(xs)

    def body(*refs):
        x_refs, o_refs = refs[:n], refs[n:2 * n]
        send_sems, recv_sems, loc_sems = refs[2 * n:]
        mx, my, mc = lax.axis_index("x"), lax.axis_index("y"), lax.axis_index("c")
        chip = 2 * mx + my
        pending = []
        for a in range(n):
            mine = pltpu.make_async_copy(x_refs[a].at[chip], o_refs[a].at[chip], loc_sems.at[a])
            mine.start()
            pending.append(mine)
            for k in range(1, NCHIP):
                px = 1 - mx if k & 2 else mx
                py = 1 - my if k & 1 else my
                cp = pltpu.make_async_remote_copy(
                    src_ref=x_refs[a].at[2 * px + py], dst_ref=o_refs[a].at[chip],
                    send_sem=send_sems.at[a, k - 1], recv_sem=recv_sems.at[a, k - 1],
                    device_id=(px, py, mc), device_id_type=pl.DeviceIdType.MESH)
                cp.start()
                pending.append(cp)
        for cp in pending:
            cp.wait()

    return _comm_call(body, xs, [jax.ShapeDtypeStruct(tuple(x.shape), x.dtype) for x in xs], name)


def _pair_sum(x, y):
    _, _, R, C = x.shape
    tr = _tile(R, 128, 16)

    def body(x_ref, y_ref, o_ref):
        mc = lax.axis_index("c")
        mine = jnp.where(mc == 0, x_ref[:, 0].astype(F32), x_ref[:, 1].astype(F32))
        o_ref[...] = (mine + y_ref[...].astype(F32)).astype(o_ref.dtype)

    return _pcall(body, name="pair_sum", grid=(R // tr,),
                  in_specs=[pl.BlockSpec((NCHIP, 2, tr, C), lambda i: (0, 0, i, 0)), pl.BlockSpec((NCHIP, tr, C), lambda i: (0, i, 0))],
                  out_specs=pl.BlockSpec((NCHIP, tr, C), lambda i: (0, i, 0)), out_shape=jax.ShapeDtypeStruct((NCHIP, R, C), x.dtype),
                  compiler_params=_cp(("parallel",)))(x, y)


def _reduce_scatter(xs, name):
    from_sibling = _sibling_exchange(xs, name + "_d2d")
    pair = [_pair_sum(x.reshape(NCHIP, 2, -1, x.shape[-1]), y.reshape(NCHIP, -1, x.shape[-1])) for x, y in zip(xs, from_sibling)]
    got = _chip_exchange(pair, name + "_ici")
    return [_sum_slabs(g).reshape(x.shape[1:]) for g, x in zip(got, xs)]


def _gather(xs, name):
    n = len(xs)

    def body(*refs):
        x_refs, o_refs = refs[:n], refs[n:2 * n]
        send_sems, recv_sems, loc_sems = refs[2 * n:]
        mx, my, mc = lax.axis_index("x"), lax.axis_index("y"), lax.axis_index("c")
        me = 4 * mx + 2 * my + mc
        sibling = (mx, my, 1 - mc)
        chips = [(1 - mx, my), (mx, 1 - my), (1 - mx, 1 - my)]
        slot = lambda px, py, pc: 4 * px + 2 * py + pc

        def copy(a, k, s, to, from_input=False):
            return pltpu.make_async_remote_copy(
                src_ref=x_refs[a] if from_input else o_refs[a].at[s], dst_ref=o_refs[a].at[s],
                send_sem=send_sems.at[a, k], recv_sem=recv_sems.at[a, k], device_id=to, device_id_type=pl.DeviceIdType.MESH)

        sends, mines = [], []
        for a in range(n):
            mine = pltpu.make_async_copy(x_refs[a], o_refs[a].at[me], loc_sems.at[a])
            mine.start()
            mines.append(mine)
            first = [copy(a, 0, me, sibling, True)] + [copy(a, 1 + j, me, (cx, cy, mc), True) for j, (cx, cy) in enumerate(chips)]
            for cp in first:
                cp.start()
            sends += first
        for a in range(n):
            for j, (cx, cy) in enumerate(chips):
                s = slot(cx, cy, mc)
                copy(a, 1 + j, s, sibling).wait_recv()
                passed = copy(a, 4 + j, s, sibling)
                passed.start()
                sends.append(passed)
        for a in range(n):
            copy(a, 0, slot(*sibling), sibling).wait_recv()
            for j, (cx, cy) in enumerate(chips):
                copy(a, 4 + j, slot(cx, cy, 1 - mc), sibling).wait_recv()
        for cp in sends:
            cp.wait_send()
        for mine in mines:
            mine.wait()

    return _comm_call(body, xs, [jax.ShapeDtypeStruct((NDEV,) + tuple(x.shape), x.dtype) for x in xs], name)


SLAB_W = 1024
SLAB_ROWS = 16


def _pack(parts, lead=None):
    if lead is None:
        flat = jnp.concatenate([p.reshape(-1) for p in parts])
        n = flat.shape[0]
        padn = -n % (SLAB_ROWS * SLAB_W)
        return jnp.pad(flat, (0, padn)).reshape(-1, SLAB_W)
    flat = jnp.concatenate([p.reshape(lead, -1) for p in parts], axis=1)
    n = flat.shape[1]
    padn = -n % (SLAB_ROWS * SLAB_W)
    return jnp.pad(flat, ((0, 0), (0, padn))).reshape(lead, -1, SLAB_W)


def _unpack(buf, shapes, lead=None):
    out, off = [], 0
    flat = buf.reshape(-1) if lead is None else buf.reshape(lead, -1)
    for s in shapes:
        n = int(np.prod(s))
        if lead is None:
            out.append(flat[off:off + n].reshape(s))
        else:
            out.append(flat[:, off:off + n].reshape((lead,) + tuple(s)))
        off += n
    return out


def _gathered(blk, ax):
    m = jnp.moveaxis(blk, 0, ax)
    s = list(m.shape)
    return m.reshape(s[:ax] + [s[ax] * s[ax + 1]] + s[ax + 2:])


def _scattered(full, ax):
    s = list(full.shape)
    m = full.reshape(s[:ax] + [NDEV, s[ax] // NDEV] + s[ax + 1:])
    return jnp.moveaxis(m, ax, 0)


_MM_SHARDED = (("ffn_w_up", 2), ("ffn_w_down", 1), ("pool_w", 2), ("attn_w_qkv", 2), ("attn_w_o", 1),
               ("ssm_w_glu_a", 1), ("ssm_w_glu_b", 1), ("gmlp_w_in", 2), ("gmlp_w_out", 1))
_VEC_SHARDED = (("ffn_conv_w", 2), ("ssm_d", 1), ("gmlp_b_in", 1), ("gmlp_ln_g", 1), ("gmlp_ln_b", 1))
_REPLICATED = ("ln1_g", "ln1_b", "ln2_g", "ln2_b", "ffn_conv_b", "pool_b", "pool_scale", "attn_sink",
               "ssm_lambda_re", "ssm_lambda_im", "ssm_log_dt", "ssm_b_re", "ssm_b_im", "ssm_c_re", "ssm_c_im",
               "gmlp_w_s", "gmlp_b_s")
_WEIGHTS = ("c_ctx", "ada_w", "ada_b", "ln1_g", "ln1_b", "ln2_g", "ln2_b", "ffn_w_up", "ffn_conv_w", "ffn_conv_b", "ffn_w_down",
            "pool_w", "pool_b", "pool_scale", "attn_w_qkv", "attn_w_o", "attn_sink", "ssm_lambda_re", "ssm_lambda_im",
            "ssm_log_dt", "ssm_b_re", "ssm_b_im", "ssm_c_re", "ssm_c_im", "ssm_d", "ssm_w_glu_a", "ssm_w_glu_b",
            "gmlp_w_in", "gmlp_b_in", "gmlp_ln_g", "gmlp_ln_b", "gmlp_w_s", "gmlp_b_s", "gmlp_w_out")
N_MODS = 6
ADA_COLS = N_MODS * D // NDEV
PAD_ROWS = 16


def _silu_rows(x):
    return _rows(lambda v: (jax.nn.silu(v),), [x], [], [(x.shape[1], F32)], [], name="silu", tm=x.shape[0])[0]


def _step(x, c, ctx, loss_target, w, m, v):
    mx, my, mc = lax.axis_index("x"), lax.axis_index("y"), lax.axis_index("c")
    me = 4 * mx + 2 * my + mc

    vec_buf = _pack([w[n] for n, _ in _VEC_SHARDED] + [c])
    *mm_parts, vec_all = _gather([w[n].astype(_MXU) for n, _ in _MM_SHARDED] + [vec_buf], "gather_weights")
    vec_parts = _unpack(vec_all, [w[n].shape for n, _ in _VEC_SHARDED] + [c.shape], lead=NDEV)
    full = {n: _gathered(p, ax) for (n, ax), p in zip(_MM_SHARDED, mm_parts)}
    full.update({n: _gathered(p, ax) for (n, ax), p in zip(_VEC_SHARDED, vec_parts[:-1])})
    c_all = vec_parts[-1].reshape(NDEV, D)

    cc = jnp.concatenate([c_all, w["c_ctx"].reshape(1, D), jnp.zeros((PAD_ROWS - NDEV - 1, D), F32)], axis=0)
    silu_cc = _silu_rows(cc)
    ada_b_mine = lax.dynamic_slice(w["ada_b"], (0, me * ADA_COLS), (DEPTH, ADA_COLS))
    mods_mine = jnp.stack([_mm(silu_cc, w["ada_w"][l], name="ada_mods") + ada_b_mine[l][None, :] for l in range(DEPTH)], axis=1)
    per_dev = mods_mine[:NDEV].reshape(NDEV, DEPTH * ADA_COLS)
    cm = jnp.broadcast_to(mods_mine[NDEV].reshape(1, DEPTH * ADA_COLS), (NDEV, DEPTH * ADA_COLS))
    mods_all = _exchange([_pack([per_dev, cm], lead=NDEV)], "scatter_mods")[0]
    got = _unpack(mods_all, [(DEPTH, ADA_COLS), (DEPTH, ADA_COLS)], lead=NDEV)
    mods = got[0].transpose(1, 0, 2).reshape(DEPTH, N_MODS * D)
    cmods = got[1].transpose(1, 0, 2).reshape(DEPTH, N_MODS * D)
    P = [[jnp.stack([cmods[l, k * D:(k + 1) * D], mods[l, k * D:(k + 1) * D]]).reshape(2, 1, D) for k in range(N_MODS)]
         for l in range(DEPTH)]
    row = lambda a, l: a[l].reshape(1, 1, D)

    xs = jnp.concatenate([ctx[0], x[0]], axis=0)
    sp = tuple(w[n][0] for n in ("ssm_lambda_re", "ssm_lambda_im", "ssm_log_dt", "ssm_b_re", "ssm_b_im", "ssm_c_re", "ssm_c_im")) + \
        (full["ssm_d"].reshape(1, 1, D),)
    gm = (full["gmlp_w_in"][0], full["gmlp_b_in"], full["gmlp_ln_g"], full["gmlp_ln_b"], w["gmlp_w_s"][0], w["gmlp_b_s"][0],
          full["gmlp_w_out"][0])
    pool_args = (full["pool_w"][0], w["pool_b"], w["pool_scale"])
    saved = []
    for l in range(DEPTH):
        sh1, sc1, gt1, sh2, sc2, gt2 = P[l]
        h1 = _pre_mixer(xs, sh1, sc1, _MXU if l in (1, 3) else F32)
        if l == 0:
            y, ms = _pool_fwd(h1, *pool_args), None
        elif l == 1:
            y, ms = _attn_mixer_fwd(h1, full["attn_w_qkv"][0], full["attn_w_o"][0], w["attn_sink"])
        elif l == 2:
            y, ms = _ssm_mixer_fwd(h1, sp, full["ssm_w_glu_a"][0], full["ssm_w_glu_b"][0])
        else:
            y, ms = _gmlp_mixer_fwd(h1, *gm)
        x1, h2, h2t = _post_mixer(xs, y, gt1, row(w["ln1_g"], l), row(w["ln1_b"], l), sh2, sc2)
        f, (a, ut) = _ffn_fwd(h2, full["ffn_w_up"][l], full["ffn_conv_w"][l], w["ffn_conv_b"][l][None, :], full["ffn_w_down"][l])
        x2 = _post_ffn(x1, f, gt2, row(w["ln2_g"], l), row(w["ln2_b"], l))
        saved.append((xs, h1, y, ms, x1, h2t, a, ut, f))
        xs = x2
    loss, dxs = _loss_head(xs, loss_target[0])

    g = {n: [None] * DEPTH for n in ("ln1_g", "ln1_b", "ln2_g", "ln2_b", "ffn_w_up", "ffn_conv_w", "ffn_conv_b", "ffn_w_down")}
    dP = [None] * DEPTH
    for l in reversed(range(DEPTH)):
        sh1, sc1, gt1, sh2, sc2, gt2 = P[l]
        x0, h1, y, ms, x1, h2t, a, ut, f = saved[l]
        dx1, df, dgt2, g["ln2_g"][l], g["ln2_b"][l] = _post_ffn_bwd(x1, f, dxs, gt2, row(w["ln2_g"], l), row(w["ln2_b"], l))
        dh2, g["ffn_w_up"][l], g["ffn_conv_w"][l], g["ffn_conv_b"][l], g["ffn_w_down"][l] = _ffn_bwd(
            df, h2t, a, ut, full["ffn_w_up"][l], full["ffn_conv_w"][l], w["ffn_conv_b"][l][None, :], full["ffn_w_down"][l])
        dx0, dy, dgt1, g["ln1_g"][l], g["ln1_b"][l], dsh2, dsc2 = _post_mixer_bwd(
            x0, y, dx1, dh2, gt1, row(w["ln1_g"], l), row(w["ln1_b"], l), sh2, sc2)
        if l == 0:
            dh, g["pool_w"], g["pool_b"], g["pool_scale"] = _pool_bwd(h1, dy, *pool_args)
            dhs = [dh]
        elif l == 1:
            dh, g["attn_w_qkv"], g["attn_w_o"], g["attn_sink"] = _attn_mixer_bwd(dy, h1, ms, full["attn_w_qkv"][0], full["attn_w_o"][0])
            dhs = [dh]
        elif l == 2:
            dhs, dsp, dd, g["ssm_w_glu_a"], g["ssm_w_glu_b"] = _ssm_mixer_bwd(dy, h1, ms, sp, full["ssm_w_glu_a"][0], full["ssm_w_glu_b"][0])
            for n, d_ in zip(("ssm_lambda_re", "ssm_lambda_im", "ssm_log_dt", "ssm_b_re", "ssm_b_im", "ssm_c_re", "ssm_c_im"), dsp):
                g[n] = d_
            g["ssm_d"] = dd.reshape(1, D)
        else:
            (dh, g["gmlp_w_in"], g["gmlp_b_in"], g["gmlp_ln_g"], g["gmlp_ln_b"], g["gmlp_w_s"], g["gmlp_b_s"],
             g["gmlp_w_out"]) = _gmlp_mixer_bwd(dy, h1, ms, *gm)
            dhs = [dh]
        dxs, dsh1, dsc1 = _pre_mixer_bwd(x0, dhs, dx0, sh1, sc1)
        dP[l] = (dsh1, dsc1, dgt1, dsh2, dsc2, dgt2)
    grad_x = dxs[NCTX:][None]
    dmods = jnp.stack([jnp.concatenate([p[1, 0] for p in dP[l]]) for l in range(DEPTH)])
    dcmods = jnp.stack([jnp.concatenate([p[0, 0] for p in dP[l]]) for l in range(DEPTH)])

    gfull = {n: (jnp.stack(g[n]) if isinstance(g[n], list) else g[n]) for n in g}
    sharded_names = [n for n, _ in _MM_SHARDED] + [n for n, _ in _VEC_SHARDED]
    sharded_axes = dict(_MM_SHARDED + _VEC_SHARDED)

    def as_param(n, a):
        shard = w[n].shape
        ax = sharded_axes.get(n)
        fs = tuple(s * NDEV if i == ax else s for i, s in enumerate(shard))
        return a.reshape(fs)

    rep = jnp.concatenate([as_param(n, gfull[n]).reshape(-1) for n in _REPLICATED])
    n_rep = rep.shape[0]
    rep = jnp.pad(rep, (0, -n_rep % (NDEV * SLAB_W))).reshape(NDEV, -1)
    by_dev = lambda a: a.reshape(DEPTH, NDEV, ADA_COLS).transpose(1, 0, 2)
    big = [_scattered(as_param(n, gfull[n]), ax).astype(_MXU) for n, ax in _MM_SHARDED]
    parts = [_scattered(as_param(n, gfull[n]), ax) for n, ax in _VEC_SHARDED] + [rep, by_dev(dmods), by_dev(dcmods)]
    grads = {n: r for (n, _), r in zip(_MM_SHARDED, _reduce_scatter(big, "scatter_grads"))}
    grads_in = _exchange([_pack(parts, lead=NDEV)], "scatter_small")[0]
    shapes = [w[n].shape for n, _ in _VEC_SHARDED] + [(rep.shape[1],), (DEPTH, ADA_COLS), (DEPTH, ADA_COLS)]
    red = _unpack(_sum_slabs(grads_in), shapes)
    grads.update({n: r for (n, _), r in zip(_VEC_SHARDED, red)})
    rep_mine, dcm = red[-3], red[-1]
    dm_all = _unpack(grads_in, shapes, lead=NDEV)[-2]

    e_rows = jnp.concatenate([dm_all, dcm[None], jnp.zeros((PAD_ROWS - NDEV - 1, DEPTH, ADA_COLS), F32)], axis=0)
    grads["ada_w"] = jnp.stack([_mm(silu_cc, e_rows[:, l], ta=True, name="ada_dw") for l in range(DEPTH)])
    ada_b_blk = jnp.sum(e_rows, axis=0)
    dcm_rows = jnp.concatenate([dcm[None], jnp.zeros((PAD_ROWS - 1, DEPTH, ADA_COLS), F32)], axis=0)
    cpart = sum(_mm(dcm_rows[:, l], w["ada_w"][l], tb=True, name="ada_dc")[0] for l in range(DEPTH))

    small_all = _gather([_pack([rep_mine, ada_b_blk, cpart])], "gather_small")[0]
    sm = _unpack(small_all, [rep_mine.shape, (DEPTH, ADA_COLS), (D,)], lead=NDEV)
    rep_full = sm[0].reshape(-1)[:n_rep]
    off = 0
    for n in _REPLICATED:
        k = int(np.prod(w[n].shape))
        grads[n] = rep_full[off:off + k].reshape(w[n].shape)
        off += k
    grads["ada_b"] = sm[1].transpose(1, 0, 2).reshape(DEPTH, N_MODS * D)
    csum = _unpack(_sum_slabs(small_all), [rep_mine.shape, (DEPTH, ADA_COLS), (D,)])[2]

    def dsilu(vv, dd):
        return (jax.vjp(jax.nn.silu, vv)[1](dd)[0],)

    grads["c_ctx"] = _rows(dsilu, [jnp.broadcast_to(w["c_ctx"][None], (SUBLANE, D)), jnp.broadcast_to(csum[None], (SUBLANE, D))],
                           [], [(D, F32)], [], name="dsilu", tm=SUBLANE)[0][0]

    delta, new_m, new_v = {}, {}, {}
    for n in _WEIGHTS:
        delta[n], new_m[n], new_v[n] = _adamw(w[n], grads[n], m[n], v[n])
    loss = lax.psum(loss[0, 0], ("x", "y", "c"))
    return loss, grad_x, grads, delta, new_m, new_v


def kernel(x, c, ctx, c_ctx, ada_w, ada_b, ln1_g, ln1_b, ln2_g, ln2_b, ffn_w_up, ffn_conv_w, ffn_conv_b, ffn_w_down, pool_w, pool_b, pool_scale, attn_w_qkv, attn_w_o, attn_sink, ssm_lambda_re, ssm_lambda_im, ssm_log_dt, ssm_b_re, ssm_b_im, ssm_c_re, ssm_c_im, ssm_d, ssm_w_glu_a, ssm_w_glu_b, gmlp_w_in, gmlp_b_in, gmlp_ln_g, gmlp_ln_b, gmlp_w_s, gmlp_b_s, gmlp_w_out, loss_target, m_c_ctx, m_ada_w, m_ada_b, m_ln1_g, m_ln1_b, m_ln2_g, m_ln2_b, m_ffn_w_up, m_ffn_conv_w, m_ffn_conv_b, m_ffn_w_down, m_pool_w, m_pool_b, m_pool_scale, m_attn_w_qkv, m_attn_w_o, m_attn_sink, m_ssm_lambda_re, m_ssm_lambda_im, m_ssm_log_dt, m_ssm_b_re, m_ssm_b_im, m_ssm_c_re, m_ssm_c_im, m_ssm_d, m_ssm_w_glu_a, m_ssm_w_glu_b, m_gmlp_w_in, m_gmlp_b_in, m_gmlp_ln_g, m_gmlp_ln_b, m_gmlp_w_s, m_gmlp_b_s, m_gmlp_w_out, v_c_ctx, v_ada_w, v_ada_b, v_ln1_g, v_ln1_b, v_ln2_g, v_ln2_b, v_ffn_w_up, v_ffn_conv_w, v_ffn_conv_b, v_ffn_w_down, v_pool_w, v_pool_b, v_pool_scale, v_attn_w_qkv, v_attn_w_o, v_attn_sink, v_ssm_lambda_re, v_ssm_lambda_im, v_ssm_log_dt, v_ssm_b_re, v_ssm_b_im, v_ssm_c_re, v_ssm_c_im, v_ssm_d, v_ssm_w_glu_a, v_ssm_w_glu_b, v_gmlp_w_in, v_gmlp_b_in, v_gmlp_ln_g, v_gmlp_ln_b, v_gmlp_w_s, v_gmlp_b_s, v_gmlp_w_out):
    args = dict(locals())
    w = {n: args[n] for n in _WEIGHTS}
    m = {n: args["m_" + n] for n in _WEIGHTS}
    v = {n: args["v_" + n] for n in _WEIGHTS}
    loss, grad_x, grads, delta, new_m, new_v = _step(x, c, ctx, loss_target, w, m, v)
    return (loss, grad_x, *[grads[n] for n in _WEIGHTS], *[delta[n] for n in _WEIGHTS],
            *[new_m[n] for n in _WEIGHTS], *[new_v[n] for n in _WEIGHTS])
```

```python
import functools
import math

import jax
import jax.numpy as jnp
import numpy as np
from jax import lax
from jax.experimental import pallas as pl
from jax.experimental.pallas import tpu as pltpu

D = 1024
SEQ = 4096
NCTX = 256
T = NCTX + SEQ
DEPTH = 4
NDEV = 8
GRID_W = 64
ALPHA = (2.0 * DEPTH) ** 0.25
LN_EPS = 1e-5
FFN_H = 2816
HEAD_DIM = 64
NQH, NKVH, GQA = 16, 4, 4
WINDOW = 128
ABLK = 128
NEG_INF = -1e30
ROPE_BASE = 10000.0
POOL_WINDOWS = (2, 4, 8, 16)
SSM_G, SSM_P, SSM_C = 64, 64, 16
GM_HALF = 2048
GM_HEADS = 8
GM_HD = GM_HALF // GM_HEADS
GM_CHUNK = 128
B1, B2, LR, EPS, WD, STEP = 0.9, 0.999, 0.001, 1e-8, 0.01, 10

LANE = 128
SUBLANE = 8
VMEM_LIMIT = 56 * 1024 * 1024
MM_OUT_TILE_BYTES = 6 * 1024 * 1024
TM = 256
NT = T // TM

_MXU = jnp.bfloat16
F32 = jnp.float32


def _pcall(body, **kw):
    return pl.pallas_call(body, **kw)


def _cp(sem):
    return pltpu.CompilerParams(dimension_semantics=sem, vmem_limit_bytes=VMEM_LIMIT)


def _tile(dim, pref, align):
    best = None
    for t in range(align, min(dim, pref) + 1, align):
        if dim % t == 0:
            best = t
    return dim if best is None else best


def _mm(a, b, *, ta=False, tb=False, b_cols=None, out_dtype=F32, name):
    if ta:
        a = a.astype(_MXU).T
    M, K = a.shape
    if tb:
        N, K2 = b.shape
    else:
        K2, N = b.shape
    k_start = 0
    if b_cols is not None:
        assert tb
        k_start, K2 = b_cols
    assert K == K2, (a.shape, b.shape, ta, tb)
    tm = _tile(M, 2304, 16)
    tn = _tile(N, max(512, MM_OUT_TILE_BYTES // (4 * tm)), LANE)
    tk = _tile(K, 2304, LANE)
    nk = K // tk
    dims = (((1,), (1,) if tb else (0,)), ((), ()))

    def body(a_ref, b_ref, o_ref, acc_ref):
        k = pl.program_id(2)
        r = lax.dot_general(a_ref[...].astype(_MXU), b_ref[...].astype(_MXU), dims, preferred_element_type=F32)
        if nk == 1:
            o_ref[...] = r.astype(o_ref.dtype)
            return

        @pl.when(k == 0)
        def _():
            acc_ref[...] = r

        @pl.when(k > 0)
        def _():
            acc_ref[...] += r

        @pl.when(k == nk - 1)
        def _():
            o_ref[...] = acc_ref[...].astype(o_ref.dtype)

    a_spec = pl.BlockSpec((tm, tk), lambda j, i, k: (i, k))
    assert k_start % tk == 0
    k0 = k_start // tk
    b_spec = pl.BlockSpec((tn, tk), lambda j, i, k: (j, k + k0)) if tb else pl.BlockSpec((tk, tn), lambda j, i, k: (k, j))
    return _pcall(
        body, name=name, grid=(N // tn, M // tm, nk), in_specs=[a_spec, b_spec],
        out_specs=pl.BlockSpec((tm, tn), lambda j, i, k: (i, j)),
        out_shape=jax.ShapeDtypeStruct((M, N), out_dtype),
        scratch_shapes=[pltpu.VMEM((tm, tn), F32)],
        compiler_params=_cp(("parallel", "parallel", "arbitrary")),
    )(a, b)


def _rows(fn, rows, pars, out_rows, out_pars, *, name, tm=TM, transposed=()):
    R = rows[0].shape[0]
    nt = R // tm
    nct = NCTX // tm
    n_r, n_p, n_or, n_op = len(rows), len(pars), len(out_rows), len(out_pars)

    def sel(S):
        if S == 1:
            return lambda i: 0
        return lambda i: jnp.where(i < nct, 0, 1)

    def body(*refs):
        r_in = refs[:n_r]
        p_in = refs[n_r:n_r + n_p]
        r_out = refs[n_r + n_p:n_r + n_p + n_or]
        p_out = refs[n_r + n_p + n_or:n_r + n_p + n_or + n_op]
        t_out = refs[n_r + n_p + n_or + n_op:]
        i = pl.program_id(0)
        vals = [r[...].astype(F32) for r in r_in] + [p[0] for p in p_in]
        outs = fn(*vals)
        for r, v in zip(r_out, outs[:n_or]):
            r[...] = v.astype(r.dtype)
        for r, k in zip(t_out, transposed):
            r[...] = outs[k].T.astype(r.dtype)
        for (S, _), r, v in zip(out_pars, p_out, outs[n_or:]):
            first = (i == 0) if S == 1 else jnp.logical_or(i == 0, i == nct)

            @pl.when(first)
            def _():
                r[0] = v

            @pl.when(jnp.logical_not(first))
            def _():
                r[0] += v

    def pspec(shape):
        S = shape[0]
        rest = tuple(shape[1:])
        s = sel(S)
        return pl.BlockSpec((1,) + rest, lambda i: (s(i),) + (0,) * len(rest))

    in_specs = [pl.BlockSpec((tm, r.shape[1]), lambda i: (i, 0)) for r in rows] + [pspec(p.shape) for p in pars]
    out_specs = [pl.BlockSpec((tm, w), lambda i: (i, 0)) for w, _ in out_rows] + [pspec((S,) + tuple(sh)) for S, sh in out_pars] + \
                [pl.BlockSpec((out_rows[k][0], tm), lambda i: (0, i)) for k in transposed]
    out_shape = [jax.ShapeDtypeStruct((R, w), dt) for w, dt in out_rows] + \
                [jax.ShapeDtypeStruct((S,) + tuple(sh), F32) for S, sh in out_pars] + \
                [jax.ShapeDtypeStruct((out_rows[k][0], R), out_rows[k][1]) for k in transposed]
    res = _pcall(body, name=name, grid=(nt,), in_specs=in_specs, out_specs=out_specs, out_shape=out_shape,
                 compiler_params=_cp(("arbitrary",)))(*rows, *pars)
    return res


def _ln(z, g, b):
    mu = jnp.mean(z, axis=-1, keepdims=True)
    var = jnp.mean(jnp.square(z - mu), axis=-1, keepdims=True)
    return (z - mu) * lax.rsqrt(var + LN_EPS) * g + b


def _f1(x, sh, sc):
    return x * (1.0 + sc) + sh


def _f2(x, y, gt, g, b, sh, sc):
    x1 = _ln(ALPHA * x + gt * y, g, b)
    return x1, x1 * (1.0 + sc) + sh


def _f3(x1, f, gt, g, b):
    return _ln(ALPHA * x1 + gt * f, g, b)


def _pre_mixer(x, sh, sc, dtype):
    return _rows(lambda x, sh, sc: (_f1(x, sh, sc),), [x], [sh, sc], [(D, dtype)], [], name="pre_mixer")[0]


def _pre_mixer_bwd(x, dhs, dx_prev, sh, sc):
    n = len(dhs)

    def fn(x, *rest):
        dh = rest[0]
        for t in rest[1:n]:
            dh = dh + t
        dxp, sh, sc = rest[n], rest[n + 1], rest[n + 2]
        _, vjp = jax.vjp(_f1, x, sh, sc)
        dx, dsh, dsc = vjp(dh)
        return dxp + dx, dsh, dsc

    return _rows(fn, [x, *dhs, dx_prev], [sh, sc], [(D, F32)], [(2, (1, D)), (2, (1, D))], name="pre_mixer_bwd")


def _post_mixer(x, y, gt, g, b, sh, sc):
    return _rows(_f2, [x, y], [gt, g, b, sh, sc], [(D, F32), (D, _MXU)], [], name="post_mixer", transposed=(1,))


def _post_mixer_bwd(x, y, dx1, dh2, gt, g, b, sh, sc):
    def fn(x, y, dx1, dh2a, dh2b, gt, g, b, sh, sc):
        _, vjp = jax.vjp(_f2, x, y, gt, g, b, sh, sc)
        return vjp((dx1, dh2a + dh2b))

    return _rows(fn, [x, y, dx1, *dh2], [gt, g, b, sh, sc], [(D, F32), (D, F32)],
                 [(2, (1, D)), (1, (1, D)), (1, (1, D)), (2, (1, D)), (2, (1, D))], name="post_mixer_bwd")


def _post_ffn(x1, f, gt, g, b):
    return _rows(lambda *a: (_f3(*a),), [x1, f], [gt, g, b], [(D, F32)], [], name="post_ffn")[0]


def _post_ffn_bwd(x1, f, dx2, gt, g, b):
    def fn(x1, f, dx2, gt, g, b):
        _, vjp = jax.vjp(_f3, x1, f, gt, g, b)
        return vjp(dx2)

    return _rows(fn, [x1, f, dx2], [gt, g, b], [(D, F32), (D, _MXU)],
                 [(2, (1, D)), (1, (1, D)), (1, (1, D))], name="post_ffn_bwd")


def _halo_specs(tm, w, col, active=None):
    r8 = tm // SUBLANE
    act = (lambda j, r: r) if active is None else (lambda j, r: jnp.where(active(j), r, 0))
    return [
        pl.BlockSpec((SUBLANE, w), lambda j, i: (act(j, jnp.maximum(i * r8 - 1, 0)), col(j))),
        pl.BlockSpec((tm, w), lambda j, i: (act(j, i), col(j))),
        pl.BlockSpec((SUBLANE, w), lambda j, i: (act(j, jnp.minimum((i + 1) * r8, T // SUBLANE - 1)), col(j))),
    ]


def _seg_flags(i, tm):
    nct = NCTX // tm
    first = jnp.logical_or(i == 0, i == nct)
    last = jnp.logical_or(i == nct - 1, i == T // tm - 1)
    return first, last


def _shift_rows(cur, prev8, next8, first, last):
    tm = cur.shape[0]
    rid = lax.broadcasted_iota(jnp.int32, cur.shape, 0)
    pr = jnp.where(first, 0.0, prev8[SUBLANE - 1:SUBLANE, :])
    nx = jnp.where(last, 0.0, next8[0:1, :])
    up = jnp.where(rid == 0, pr, pltpu.roll(cur, 1, 0))
    dn = jnp.where(rid == tm - 1, nx, pltpu.roll(cur, tm - 1, 0))
    return up, dn


FFN_TC = 1408
FFN_NCT = FFN_H // FFN_TC


def _conv3(cur, prev8, next8, w3, first, last):
    up, dn = _shift_rows(cur, prev8, next8, first, last)
    return up * w3[0:1] + cur * w3[1:2] + dn * w3[2:3], up, dn


def _ffn_mid(a, cw, cb):
    def body(vp, vc, vn, gp, gc, gn, cwv, cwg, cbv, cbg, o_ref, ot_ref):
        first, last = _seg_flags(pl.program_id(1), TM)
        val = _conv3(vc[...], vp[...], vn[...], cwv[...], first, last)[0] + cbv[...]
        gate = _conv3(gc[...], gp[...], gn[...], cwg[...], first, last)[0] + cbg[...]
        u = val * jax.nn.silu(gate)
        o_ref[...] = u.astype(o_ref.dtype)
        ot_ref[...] = u.T.astype(ot_ref.dtype)

    specs = _halo_specs(TM, FFN_TC, lambda j: j) + _halo_specs(TM, FFN_TC, lambda j: j + FFN_NCT)
    specs += [pl.BlockSpec((3, FFN_TC), lambda j, i: (0, j)), pl.BlockSpec((3, FFN_TC), lambda j, i: (0, j + FFN_NCT)),
              pl.BlockSpec((1, FFN_TC), lambda j, i: (0, j)), pl.BlockSpec((1, FFN_TC), lambda j, i: (0, j + FFN_NCT))]
    return _pcall(body, name="ffn_mid", grid=(FFN_NCT, NT), in_specs=specs,
                  out_specs=[pl.BlockSpec((TM, FFN_TC), lambda j, i: (i, j)), pl.BlockSpec((FFN_TC, TM), lambda j, i: (j, i))],
                  out_shape=[jax.ShapeDtypeStruct((T, FFN_H), _MXU), jax.ShapeDtypeStruct((FFN_H, T), _MXU)],
                  compiler_params=_cp(("parallel", "arbitrary")))(a, a, a, a, a, a, cw, cw, cb, cb)


def _ext_rows(p_ref, c_ref, n_ref, first, last):
    p8 = jnp.where(first, 0.0, p_ref[...])
    n8 = jnp.where(last, 0.0, n_ref[...])
    return jnp.concatenate([p8, c_ref[...], n8], axis=0)


def _ffn_mid_bwd(a, du, cw, cb):
    E = TM + 2 * SUBLANE
    ctr = slice(SUBLANE, SUBLANE + TM)

    def body(vp, vc, vn, gp, gc, gn, dp, dc, dn_, cwv, cwg, cbv, cbg, dav_ref, dag_ref, dcwv, dcwg, dcbv, dcbg):
        i = pl.program_id(1)
        first, last = _seg_flags(i, TM)
        ev = _ext_rows(vp, vc, vn, first, last)
        eg = _ext_rows(gp, gc, gn, first, last)
        edu = _ext_rows(dp, dc, dn_, first, last)
        wv, wg = cwv[...], cwg[...]
        vup, vdn = pltpu.roll(ev, 1, 0), pltpu.roll(ev, E - 1, 0)
        gup, gdn = pltpu.roll(eg, 1, 0), pltpu.roll(eg, E - 1, 0)
        val = vup * wv[0:1] + ev * wv[1:2] + vdn * wv[2:3] + cbv[...]
        gate = gup * wg[0:1] + eg * wg[1:2] + gdn * wg[2:3] + cbg[...]
        sg = jax.nn.sigmoid(gate)
        dval = edu * (gate * sg)
        dgate = edu * val * (sg * (1.0 + gate * (1.0 - sg)))

        def conv_t(d, w3):
            return (pltpu.roll(d, E - 1, 0) * w3[0:1] + d * w3[1:2] + pltpu.roll(d, 1, 0) * w3[2:3])[ctr]

        dav_ref[...] = conv_t(dval, wv).astype(dav_ref.dtype)
        dag_ref[...] = conv_t(dgate, wg).astype(dag_ref.dtype)

        def acc(ref, v):
            @pl.when(i == 0)
            def _():
                ref[...] = v

            @pl.when(i > 0)
            def _():
                ref[...] += v

        for dref, d, up, cur, dn, bref in ((dcwv, dval[ctr], vup[ctr], ev[ctr], vdn[ctr], dcbv),
                                           (dcwg, dgate[ctr], gup[ctr], eg[ctr], gdn[ctr], dcbg)):
            acc(dref, jnp.concatenate([jnp.sum(d * up, 0, keepdims=True), jnp.sum(d * cur, 0, keepdims=True),
                                       jnp.sum(d * dn, 0, keepdims=True)], axis=0))
            acc(bref, jnp.sum(d, 0, keepdims=True))

    specs = _halo_specs(TM, FFN_TC, lambda j: j) + _halo_specs(TM, FFN_TC, lambda j: j + FFN_NCT) + _halo_specs(TM, FFN_TC, lambda j: j)
    specs += [pl.BlockSpec((3, FFN_TC), lambda j, i: (0, j)), pl.BlockSpec((3, FFN_TC), lambda j, i: (0, j + FFN_NCT)),
              pl.BlockSpec((1, FFN_TC), lambda j, i: (0, j)), pl.BlockSpec((1, FFN_TC), lambda j, i: (0, j + FFN_NCT))]
    out_specs = [pl.BlockSpec((TM, FFN_TC), lambda j, i: (i, j)), pl.BlockSpec((TM, FFN_TC), lambda j, i: (i, j)),
                 pl.BlockSpec((3, FFN_TC), lambda j, i: (0, j)), pl.BlockSpec((3, FFN_TC), lambda j, i: (0, j)),
                 pl.BlockSpec((1, FFN_TC), lambda j, i: (0, j)), pl.BlockSpec((1, FFN_TC), lambda j, i: (0, j))]
    out_shape = [jax.ShapeDtypeStruct((T, FFN_H), _MXU)] * 2 + [jax.ShapeDtypeStruct((3, FFN_H), F32)] * 2 + \
                [jax.ShapeDtypeStruct((1, FFN_H), F32)] * 2
    dav, dag, dcwv, dcwg, dcbv, dcbg = _pcall(
        body, name="ffn_mid_bwd", grid=(FFN_NCT, NT), in_specs=specs, out_specs=out_specs, out_shape=out_shape,
        compiler_params=_cp(("parallel", "arbitrary")))(a, a, a, a, a, a, du, du, du, cw, cw, cb, cb)
    return dav, dag, jnp.concatenate([dcwv, dcwg], axis=1), jnp.concatenate([dcbv, dcbg], axis=1)


def _ffn_fwd(h2, w_up, cw, cb, w_down):
    a = _mm(h2, w_up, name="ffn_up")
    u, ut = _ffn_mid(a, cw, cb)
    f = _mm(u, w_down, name="ffn_down")
    return f, (a, ut)


def _ffn_bwd(df, h2t, a, ut, w_up, cw, cb, w_down):
    dw_down = _mm(ut, df, name="ffn_down_dw")
    du = _mm(df, w_down, tb=True, name="ffn_down_dx")
    dav, dag, dcw, dcb = _ffn_mid_bwd(a, du, cw, cb)
    dw_up = jnp.concatenate([_mm(h2t, dav, name="ffn_up_dw"), _mm(h2t, dag, name="ffn_up_dw")], axis=1)
    dh2 = [_mm(dav, w_up, tb=True, b_cols=(0, FFN_H), name="ffn_up_dx"), _mm(dag, w_up, tb=True, b_cols=(FFN_H, FFN_H), name="ffn_up_dx")]
    return dh2, dw_up, dcw, dcb, dw_down


def _winsum(e, lo, hi):
    n = e.shape[0]
    acc = None
    for o in range(lo, hi + 1):
        t = e if o == 0 else pltpu.roll(e, (-o) % n, 0)
        acc = t if acc is None else acc + t
    return acc


def _pool_cnt(i, w, rows, off):
    nct = NCTX // TM
    seg_len = jnp.where(i < nct, NCTX, SEQ)
    seg_tile = jnp.where(i < nct, i, i - nct)
    pos = lax.broadcasted_iota(jnp.int32, (rows, 1), 0) - off + seg_tile * TM
    lo = jnp.clip(pos - w // 2, 0, seg_len)
    hi = jnp.clip(pos - w // 2 + w, 0, seg_len)
    return jnp.maximum(hi - lo, 1).astype(F32)


def _pool_fwd(h, pw, pb, ps):
    def body(hp, hc, hn, w_ref, b_ref, s_ref, o_ref):
        i = pl.program_id(1)
        first, last = _seg_flags(i, TM)
        e = _ext_rows(hp, hc, hn, first, last)
        outs = []
        for g, w in enumerate(POOL_WINDOWS):
            sl = slice(256 * g, 256 * (g + 1))
            eg = e[:, sl]
            mean = _winsum(eg, -(w // 2), w // 2 - 1)[SUBLANE:SUBLANE + TM] / _pool_cnt(i, w, TM, 0)
            mixed = mean - hc[:, sl]
            outs.append(jnp.dot(mixed.astype(_MXU), w_ref[g].astype(_MXU), preferred_element_type=F32))
        o_ref[...] = (jnp.concatenate(outs, axis=1) + b_ref[...]) * s_ref[...]

    full = lambda *s: pl.BlockSpec(s, lambda j, i: (0,) * len(s))
    return _pcall(body, name="pool_fwd", grid=(1, NT), in_specs=_halo_specs(TM, D, lambda j: 0) + [full(4, 256, 256), full(1, D), full(1, D)],
                  out_specs=pl.BlockSpec((TM, D), lambda j, i: (i, 0)), out_shape=jax.ShapeDtypeStruct((T, D), F32),
                  compiler_params=_cp(("parallel", "arbitrary")))(h, h, h, pw, pb, ps)


def _pool_bwd(h, dy, pw, pb, ps):
    E = TM + 2 * SUBLANE

    def body(hp, hc, hn, dp, dc, dn, w_ref, b_ref, s_ref, dh_ref, dw_ref, db_ref, ds_ref):
        i = pl.program_id(1)
        first, last = _seg_flags(i, TM)
        e = _ext_rows(hp, hc, hn, first, last)
        de = _ext_rows(dp, dc, dn, first, last)
        dys = de * s_ref[...]
        dhs, pre = [], []
        for g, w in enumerate(POOL_WINDOWS):
            sl = slice(256 * g, 256 * (g + 1))
            wg = w_ref[g].astype(_MXU)
            dyg = dys[:, sl].astype(_MXU)
            dmix = lax.dot_general(dyg, wg, (((1,), (1,)), ((), ())), preferred_element_type=F32)
            q = dmix / _pool_cnt(i, w, E, SUBLANE)
            dhs.append(_winsum(q, -(w // 2) + 1, w // 2)[SUBLANE:SUBLANE + TM] - dmix[SUBLANE:SUBLANE + TM])
            mean = _winsum(e[:, sl], -(w // 2), w // 2 - 1)[SUBLANE:SUBLANE + TM] / _pool_cnt(i, w, TM, 0)
            mixed = (mean - hc[:, sl]).astype(_MXU)
            pre.append(jnp.dot(mixed, wg, preferred_element_type=F32))
            dwg = lax.dot_general(mixed, dyg[SUBLANE:SUBLANE + TM], (((0,), (0,)), ((), ())), preferred_element_type=F32)

            @pl.when(i == 0)
            def _():
                dw_ref[g] = dwg

            @pl.when(i > 0)
            def _():
                dw_ref[g] += dwg
        dh_ref[...] = jnp.concatenate(dhs, axis=1)
        db = jnp.sum(dys[SUBLANE:SUBLANE + TM], 0, keepdims=True)
        ds = jnp.sum(dc[...] * (jnp.concatenate(pre, axis=1) + b_ref[...]), 0, keepdims=True)

        @pl.when(i == 0)
        def _():
            db_ref[...] = db
            ds_ref[...] = ds

        @pl.when(i > 0)
        def _():
            db_ref[...] += db
            ds_ref[...] += ds

    full = lambda *s: pl.BlockSpec(s, lambda j, i: (0,) * len(s))
    specs = _halo_specs(TM, D, lambda j: 0) + _halo_specs(TM, D, lambda j: 0) + [full(4, 256, 256), full(1, D), full(1, D)]
    return _pcall(body, name="pool_bwd", grid=(1, NT), in_specs=specs,
                  out_specs=[pl.BlockSpec((TM, D), lambda j, i: (i, 0)), full(4, 256, 256), full(1, D), full(1, D)],
                  out_shape=[jax.ShapeDtypeStruct((T, D), F32), jax.ShapeDtypeStruct((4, 256, 256), F32),
                             jax.ShapeDtypeStruct((1, D), F32), jax.ShapeDtypeStruct((1, D), F32)],
                  compiler_params=_cp(("parallel", "arbitrary")))(h, h, h, dy, dy, dy, pw, pb, ps)


def _rope_tables():
    half = HEAD_DIM // 4
    t = jnp.arange(SEQ)
    freqs = ROPE_BASE ** (-jnp.arange(half, dtype=F32) / half)
    ang_r = (t // GRID_W).astype(F32)[:, None] * freqs[None, :]
    ang_c = (t % GRID_W).astype(F32)[:, None] * freqs[None, :]
    cos = jnp.concatenate([jnp.cos(ang_r), jnp.cos(ang_r), jnp.cos(ang_c), jnp.cos(ang_c)], axis=1)
    sin = jnp.concatenate([-jnp.sin(ang_r), jnp.sin(ang_r), -jnp.sin(ang_c), jnp.sin(ang_c)], axis=1)
    cos = jnp.concatenate([jnp.ones((NCTX, HEAD_DIM), F32), cos], axis=0)
    sin = jnp.concatenate([jnp.zeros((NCTX, HEAD_DIM), F32), sin], axis=0)
    return jnp.tile(cos, (1, 2)), jnp.tile(sin, (1, 2))


QK_W = (NQH + NKVH) * HEAD_DIM
QKV_W = QK_W + NKVH * HEAD_DIM


def _rope(x, cos, sin, sign):
    def body(x_ref, c_ref, s_ref, o_ref):
        c = c_ref[...]
        s = s_ref[...] * sign
        lane = lax.broadcasted_iota(jnp.int32, (TM, LANE), 1)
        lo = (lane % 32) < 16
        for k in range(QK_W // LANE):
            xk = x_ref[:, LANE * k:LANE * (k + 1)]
            partner = jnp.where(lo, pltpu.roll(xk, LANE - 16, 1), pltpu.roll(xk, 16, 1))
            o_ref[:, LANE * k:LANE * (k + 1)] = (xk * c + partner * s).astype(o_ref.dtype)
        o_ref[:, QK_W:] = x_ref[:, QK_W:].astype(o_ref.dtype)

    return _pcall(body, name="rope", grid=(NT,),
                  in_specs=[pl.BlockSpec((TM, QKV_W), lambda i: (i, 0)), pl.BlockSpec((TM, LANE), lambda i: (i, 0)),
                            pl.BlockSpec((TM, LANE), lambda i: (i, 0))],
                  out_specs=pl.BlockSpec((TM, QKV_W), lambda i: (i, 0)),
                  out_shape=jax.ShapeDtypeStruct((T, QKV_W), _MXU), compiler_params=_cp(("parallel",)))(x, cos, sin)


NQB = T // ABLK
KPAD = T + 2 * ABLK
NKEY = NCTX + 3 * ABLK


def _stack_heads(ref):
    return jnp.concatenate([ref[g] for g in range(GQA)], axis=0)


def _sink_rows(s_ref):
    return jnp.concatenate([jnp.broadcast_to(s_ref[g], (ABLK, 1)) for g in range(GQA)], axis=0)


def _attn_mask(i):
    r = lax.broadcasted_iota(jnp.int32, (GQA * ABLK, NKEY), 0) % ABLK
    c = lax.broadcasted_iota(jnp.int32, (GQA * ABLK, NKEY), 1)
    n = i - NCTX // ABLK
    kpos = (n - 1) * ABLK + (c - NCTX)
    qpos = n * ABLK + r
    loc = (c >= NCTX) & (jnp.abs(kpos - qpos) <= WINDOW) & (kpos >= 0) & (kpos < SEQ) & (n >= 0)
    return (c < NCTX) | loc


def _attn_specs():
    qs = pl.BlockSpec((GQA, ABLK, HEAD_DIM), lambda h, i: (h, i, 0))
    kc = pl.BlockSpec((1, NCTX, HEAD_DIM), lambda h, i: (h, 0, 0))
    kl = [pl.BlockSpec((1, ABLK, HEAD_DIM), functools.partial(lambda h, i, d: (h, i + d, 0), d=d)) for d in range(3)]
    sk = pl.BlockSpec((GQA, 1, 1), lambda h, i: (h, 0, 0))
    return qs, kc, kl, sk


def _attn_fwd(q, k, v, sink):
    scale = HEAD_DIM ** -0.5

    def body(q_ref, kc, k0, k1, k2, vc, v0, v1, v2, s_ref, o_ref, l_ref):
        valid = _attn_mask(pl.program_id(1))
        kk = jnp.concatenate([kc[0], k0[0], k1[0], k2[0]], axis=0)
        vv = jnp.concatenate([vc[0], v0[0], v1[0], v2[0]], axis=0)
        s = lax.dot_general(_stack_heads(q_ref), kk, (((1,), (1,)), ((), ())), preferred_element_type=F32) * scale
        s = jnp.where(valid, s, NEG_INF)
        sk = _sink_rows(s_ref)
        m = jnp.maximum(jnp.max(s, axis=-1, keepdims=True), sk)
        p = jnp.exp(s - m)
        l = jnp.sum(p, axis=-1, keepdims=True) + jnp.exp(sk - m)
        o = jnp.dot((p / l).astype(_MXU), vv, preferred_element_type=F32).astype(o_ref.dtype)
        lse = m + jnp.log(l)
        for g in range(GQA):
            o_ref[g] = o[ABLK * g:ABLK * (g + 1)]
            l_ref[g] = lse[ABLK * g:ABLK * (g + 1)]

    qs, kc, kl, sk = _attn_specs()
    return _pcall(body, name="attn_fwd", grid=(NKVH, NQB), in_specs=[qs, kc, *kl, kc, *kl, sk],
                  out_specs=[qs, pl.BlockSpec((GQA, ABLK, 1), lambda h, i: (h, i, 0))],
                  out_shape=[jax.ShapeDtypeStruct((NQH, T, HEAD_DIM), _MXU), jax.ShapeDtypeStruct((NQH, T, 1), F32)],
                  compiler_params=_cp(("parallel", "arbitrary")))(q, k, k, k, k, v, v, v, v, sink)


def _attn_bwd(q, k, v, sink, lse, do):
    scale = HEAD_DIM ** -0.5

    def body(q_ref, kc, k0, k1, k2, vc, v0, v1, v2, s_ref, l_ref, do_ref, dq_ref, dk_ref, dv_ref, ds_ref):
        i = pl.program_id(1)

        @pl.when(i == 0)
        def _():
            dk_ref[...] = jnp.zeros_like(dk_ref)
            dv_ref[...] = jnp.zeros_like(dv_ref)
            ds_ref[...] = jnp.zeros_like(ds_ref)

        valid = _attn_mask(i)
        kk = jnp.concatenate([kc[0], k0[0], k1[0], k2[0]], axis=0)
        vv = jnp.concatenate([vc[0], v0[0], v1[0], v2[0]], axis=0)
        qst = _stack_heads(q_ref)
        dos = _stack_heads(do_ref)
        lse = _stack_heads(l_ref)
        s = lax.dot_general(qst, kk, (((1,), (1,)), ((), ())), preferred_element_type=F32) * scale
        s = jnp.where(valid, s, NEG_INF)
        p = jnp.exp(s - lse)
        psink = jnp.exp(_sink_rows(s_ref) - lse)
        dp = lax.dot_general(dos, vv, (((1,), (1,)), ((), ())), preferred_element_type=F32)
        delta = jnp.sum(p * dp, axis=-1, keepdims=True)
        ds = p * (dp - delta)
        dsk = psink * delta
        dsq = (ds * scale).astype(_MXU)
        dq = jnp.dot(dsq, kk, preferred_element_type=F32)
        for g in range(GQA):
            dq_ref[g] = dq[ABLK * g:ABLK * (g + 1)]
            ds_ref[g] += -jnp.sum(dsk[ABLK * g:ABLK * (g + 1)], axis=0, keepdims=True)
        dkk = lax.dot_general(dsq, qst, (((0,), (0,)), ((), ())), preferred_element_type=F32)
        dvv = lax.dot_general(p.astype(_MXU), dos, (((0,), (0,)), ((), ())), preferred_element_type=F32)
        loc = pl.ds(pl.multiple_of(i * ABLK, ABLK), 3 * ABLK)
        dk_ref[0, 0:NCTX, :] += dkk[:NCTX]
        dv_ref[0, 0:NCTX, :] += dvv[:NCTX]
        dk_ref[0, loc, :] += dkk[NCTX:]
        dv_ref[0, loc, :] += dvv[NCTX:]

    qs, kc, kl, sk = _attn_specs()
    ls = pl.BlockSpec((GQA, ABLK, 1), lambda h, i: (h, i, 0))
    kfull = pl.BlockSpec((1, KPAD, HEAD_DIM), lambda h, i: (h, 0, 0))
    return _pcall(body, name="attn_bwd", grid=(NKVH, NQB), in_specs=[qs, kc, *kl, kc, *kl, sk, ls, qs],
                  out_specs=[qs, kfull, kfull, sk],
                  out_shape=[jax.ShapeDtypeStruct((NQH, T, HEAD_DIM), F32), jax.ShapeDtypeStruct((NKVH, KPAD, HEAD_DIM), F32),
                             jax.ShapeDtypeStruct((NKVH, KPAD, HEAD_DIM), F32), jax.ShapeDtypeStruct((NQH, 1, 1), F32)],
                  compiler_params=_cp(("parallel", "arbitrary")))(q, k, k, k, k, v, v, v, v, sink, lse, do)


def _split_heads(x, nh):
    return x.reshape(T, nh, HEAD_DIM).transpose(1, 0, 2)


def _merge_heads(x):
    return x.transpose(1, 0, 2).reshape(T, -1)


def _pad_keys(x):
    z = jnp.zeros((x.shape[0], ABLK, HEAD_DIM), x.dtype)
    return jnp.concatenate([x[:, :NCTX], z, x[:, NCTX:], z], axis=1)


def _unpad_keys(x):
    return jnp.concatenate([x[:, :NCTX], x[:, NCTX + ABLK:NCTX + ABLK + SEQ]], axis=1)


def _attn_mixer_fwd(h, w_qkv, w_o, sink):
    cos, sin = _rope_tables()
    qkv = _rope(_mm(h, w_qkv, name="attn_qkv"), cos, sin, 1.0)
    q = _split_heads(qkv[:, :NQH * HEAD_DIM], NQH)
    k = _pad_keys(_split_heads(qkv[:, NQH * HEAD_DIM:QK_W], NKVH))
    v = _pad_keys(_split_heads(qkv[:, QK_W:], NKVH))
    sk = sink.reshape(NQH, 1, 1)
    o, lse = _attn_fwd(q, k, v, sk)
    om = _merge_heads(o)
    y = _mm(om, w_o, name="attn_out")
    return y, (q, k, v, sk, lse, om)


def _attn_mixer_bwd(dy, h, saved, w_qkv, w_o):
    q, k, v, sk, lse, om = saved
    cos, sin = _rope_tables()
    dyb = dy.astype(_MXU)
    dw_o = _mm(om, dyb, ta=True, name="attn_out_dw")
    do = _split_heads(_mm(dyb, w_o, tb=True, out_dtype=_MXU, name="attn_out_dx"), NQH)
    dq, dk, dv, dsk = _attn_bwd(q, k, v, sk, lse, do)
    dqkv = jnp.concatenate([_merge_heads(dq), _merge_heads(_unpad_keys(dk)), _merge_heads(_unpad_keys(dv))], axis=1)
    dqkv = _rope(dqkv, cos, sin, -1.0)
    dw_qkv = _mm(h, dqkv, ta=True, name="attn_qkv_dw")
    dh = _mm(dqkv, w_qkv, tb=True, name="attn_qkv_dx")
    return dh, dw_qkv, dw_o, dsk.reshape(1, NQH)


SSM_S = SSM_G * SSM_P
SSM_SL = SSM_S // LANE
SSM_TS = 128
SSM_NTS = T // SSM_TS
SSM_NCT = NCTX // SSM_TS
SSM_JB = 4
SSM_NTR = 4
SSM_TR = T // SSM_NTR


def _to_slabs(cols):
    return jnp.swapaxes(jnp.stack(cols, axis=0), 0, 1)


def _from_slabs(s_ref):
    x = jnp.swapaxes(s_ref[...], 0, 1)
    return [jnp.concatenate([x[4 * hlf + q] for q in range(4)], axis=1).astype(_MXU) for hlf in range(2)]


def _proj3d(u, w_re, w_im):
    def body(u_ref, wr_ref, wi_ref, or_ref, oi_ref):
        for w_ref, o_ref in ((wr_ref, or_ref), (wi_ref, oi_ref)):
            cols = []
            for hlf in range(2):
                ub = u_ref[:, LANE * hlf:LANE * (hlf + 1)].astype(_MXU)
                r = jnp.dot(ub, w_ref[hlf].astype(_MXU), preferred_element_type=F32)
                cols += [r[:, LANE * q:LANE * (q + 1)] for q in range(4)]
            o_ref[...] = _to_slabs(cols)

    ws = pl.BlockSpec((2, LANE, 512), lambda i, j: (j, 0, 0))
    os_ = pl.BlockSpec((SSM_TR, 8, LANE), lambda i, j: (i, j, 0))
    return _pcall(body, name="ssm_proj", grid=(SSM_NTR, SSM_JB), in_specs=[pl.BlockSpec((SSM_TR, 2 * LANE), lambda i, j: (i, j)), ws, ws],
                  out_specs=[os_, os_], out_shape=[jax.ShapeDtypeStruct((T, SSM_SL, LANE), F32)] * 2,
                  compiler_params=_cp(("parallel", "parallel")))(u, w_re, w_im)


def _readout(s_re, s_im, w_re, w_im):
    def body(sr_ref, si_ref, wr_ref, wi_ref, o_ref):
        xs = [_from_slabs(sr_ref), _from_slabs(si_ref)]
        for hlf in range(2):
            acc = None
            for x, w_ref in zip(xs, (wr_ref, wi_ref)):
                r = jnp.dot(x[hlf], w_ref[hlf].astype(_MXU), preferred_element_type=F32)
                acc = r if acc is None else acc + r
            o_ref[:, LANE * hlf:LANE * (hlf + 1)] = acc

    ss = pl.BlockSpec((SSM_TR, 8, LANE), lambda i, j: (i, j, 0))
    ws = pl.BlockSpec((2, 512, LANE), lambda i, j: (j, 0, 0))
    return _pcall(body, name="ssm_readout", grid=(SSM_NTR, SSM_JB), in_specs=[ss, ss, ws, ws],
                  out_specs=pl.BlockSpec((SSM_TR, 2 * LANE), lambda i, j: (i, j)), out_shape=jax.ShapeDtypeStruct((T, D), F32),
                  compiler_params=_cp(("parallel", "parallel")))(s_re, s_im, w_re, w_im)


def _outer3d(s_re, s_im, y):
    def body(sr_ref, si_ref, y_ref, dr_ref, di_ref):
        i = pl.program_id(1)
        xs = [_from_slabs(sr_ref), _from_slabs(si_ref)]
        for hlf in range(2):
            yb = y_ref[:, LANE * hlf:LANE * (hlf + 1)].astype(_MXU)
            for x, d_ref in zip(xs, (dr_ref, di_ref)):
                r = lax.dot_general(yb, x[hlf], (((0,), (0,)), ((), ())), preferred_element_type=F32)

                @pl.when(i == 0)
                def _():
                    d_ref[hlf] = r

                @pl.when(i > 0)
                def _():
                    d_ref[hlf] += r

    ss = pl.BlockSpec((SSM_TR, 8, LANE), lambda j, i: (i, j, 0))
    ds = pl.BlockSpec((2, LANE, 512), lambda j, i: (j, 0, 0))
    return _pcall(body, name="ssm_outer", grid=(SSM_JB, SSM_NTR), in_specs=[ss, ss, pl.BlockSpec((SSM_TR, 2 * LANE), lambda j, i: (i, j))],
                  out_specs=[ds, ds], out_shape=[jax.ShapeDtypeStruct((8, LANE, 512), F32)] * 2,
                  compiler_params=_cp(("parallel", "arbitrary")))(s_re, s_im, y)


def _scan_order(order):
    n, c = SSM_NTS, SSM_NCT
    if order == "fwd":
        return (lambda i: i), False
    if order == "fwd_adj":
        return (lambda i: n - 1 - i), True
    if order == "rev":
        return (lambda i: jnp.where(i < c, c - 1 - i, n + c - 1 - i)), True
    if order == "rev_adj":
        return (lambda i: jnp.where(i < n - c, i + c, i - (n - c))), False
    raise ValueError(order)


def _scan(b_re, b_im, lam_re, lam_im, order):
    tile, down = _scan_order(order)

    def body(br_ref, bi_ref, lr_ref, li_ref, sr_ref, si_ref, cr, ci):
        @pl.when(pl.program_id(0) == 0)
        def _():
            cr[...] = jnp.zeros_like(cr)
            ci[...] = jnp.zeros_like(ci)

        lr = lr_ref[...]
        li = li_ref[...]

        def step(n, c):
            t = SSM_TS - 1 - n if down else n
            sr, si = c
            nr = lr * sr - li * si + br_ref[t]
            ni = lr * si + li * sr + bi_ref[t]
            sr_ref[t] = nr
            si_ref[t] = ni
            return nr, ni

        sr, si = lax.fori_loop(0, SSM_TS, step, (cr[...], ci[...]))
        cr[...] = sr
        ci[...] = si

    bs = pl.BlockSpec((SSM_TS, SSM_SL, LANE), lambda i: (tile(i), 0, 0))
    ps = pl.BlockSpec((SSM_SL, LANE), lambda i: (0, 0))
    return _pcall(body, name="ssm_scan_" + order, grid=(SSM_NTS,), in_specs=[bs, bs, ps, ps], out_specs=[bs, bs],
                  out_shape=[jax.ShapeDtypeStruct((T, SSM_SL, LANE), F32)] * 2,
                  scratch_shapes=[pltpu.VMEM((SSM_SL, LANE), F32)] * 2, compiler_params=_cp(("arbitrary",)))(b_re, b_im, lam_re, lam_im)


def _scan_adj(g_re, g_im, s_re, s_im, lam_re, lam_im, order):
    tile, down = _scan_order(order)

    def body(gr_ref, gi_ref, sr_ref, si_ref, lr_ref, li_ref, ar_ref, ai_ref, dlr_ref, dli_ref, cr, ci):
        @pl.when(pl.program_id(0) == 0)
        def _():
            cr[...] = jnp.zeros_like(cr)
            ci[...] = jnp.zeros_like(ci)
            dlr_ref[...] = jnp.zeros_like(dlr_ref)
            dli_ref[...] = jnp.zeros_like(dli_ref)

        lr = lr_ref[...]
        li = li_ref[...]

        def step(n, c):
            t = SSM_TS - 1 - n if down else n
            ar, ai, dr, di = c
            sr = sr_ref[t]
            si = si_ref[t]
            dr = dr + ar * sr + ai * si
            di = di + ai * sr - ar * si
            nr = gr_ref[t] + lr * ar + li * ai
            ni = gi_ref[t] + lr * ai - li * ar
            ar_ref[t] = nr
            ai_ref[t] = ni
            return nr, ni, dr, di

        ar, ai, dr, di = lax.fori_loop(0, SSM_TS, step, (cr[...], ci[...], dlr_ref[...], dli_ref[...]))
        cr[...] = ar
        ci[...] = ai
        dlr_ref[...] = dr
        dli_ref[...] = di

    bs = pl.BlockSpec((SSM_TS, SSM_SL, LANE), lambda i: (tile(i), 0, 0))
    ps = pl.BlockSpec((SSM_SL, LANE), lambda i: (0, 0))
    return _pcall(body, name="ssm_scan_" + order, grid=(SSM_NTS,), in_specs=[bs, bs, bs, bs, ps, ps], out_specs=[bs, bs, ps, ps],
                  out_shape=[jax.ShapeDtypeStruct((T, SSM_SL, LANE), F32)] * 2 + [jax.ShapeDtypeStruct((SSM_SL, LANE), F32)] * 2,
                  scratch_shapes=[pltpu.VMEM((SSM_SL, LANE), F32)] * 2,
                  compiler_params=_cp(("arbitrary",)))(g_re, g_im, s_re, s_im, lam_re, lam_im)


def _block_diag(x):
    x4 = x.reshape(8, 8, SSM_P, SSM_C)
    return jnp.einsum("jgpc,gh->jgphc", x4, jnp.eye(8, dtype=x.dtype)).reshape(8, 8 * SSM_P, 8 * SSM_C)


def _ssm_prep(lam_re, lam_im, log_dt, b_re, b_im, c_re, c_im):
    lam = lax.complex(lam_re, lam_im)
    dt = jnp.exp(log_dt)[:, None]
    lam_bar = jnp.exp(lam * dt)
    b_bar = ((lam_bar - 1.0) / lam)[..., None] * lax.complex(b_re, b_im)
    return (jnp.real(lam_bar).reshape(SSM_SL, LANE), jnp.imag(lam_bar).reshape(SSM_SL, LANE),
            _block_diag(jnp.real(b_bar)), _block_diag(jnp.imag(b_bar)),
            _block_diag(c_re.transpose(0, 2, 1)), _block_diag(-c_im.transpose(0, 2, 1)))


def _ssm_glue(h, yf, yr, d):
    return jax.nn.gelu(d * h + yf + yr)


def _glu(ga, gb):
    return ga * jax.nn.sigmoid(gb)


def _ssm_mixer_fwd(h, sp, w_a, w_b):
    lam_re, lam_im, log_dt, b_re, b_im, c_re, c_im, d_skip = sp
    ys, saved = [], []
    for di, order in enumerate(("fwd", "rev")):
        lr, li, wb_r, wb_i, wc_r, wc_i = _ssm_prep(lam_re[di], lam_im[di], log_dt[di], b_re[di], b_im[di], c_re[di], c_im[di])
        bu_r, bu_i = _proj3d(h, wb_r.transpose(0, 2, 1), wb_i.transpose(0, 2, 1))
        s_r, s_i = _scan(bu_r, bu_i, lr, li, order)
        ys.append(_readout(s_r, s_i, wc_r, wc_i))
        saved.append((s_r, s_i))
    g, gt = _rows(lambda *a: (_ssm_glue(*a),), [h, ys[0], ys[1]], [d_skip], [(D, _MXU)], [], name="ssm_glue", transposed=(0,))
    ga = _mm(g, w_a, name="ssm_glu_a")
    gb = _mm(g, w_b, name="ssm_glu_b")
    y = _rows(lambda *a: (_glu(*a),), [ga, gb], [], [(D, F32)], [], name="ssm_glu")[0]
    return y, (ys, saved, gt, ga, gb)


def _ssm_mixer_bwd(dy, h, saved_all, sp, w_a, w_b):
    lam_re, lam_im, log_dt, b_re, b_im, c_re, c_im, d_skip = sp
    ys, saved, gt, ga, gb = saved_all

    def glu_bwd(ga, gb, dy):
        _, vjp = jax.vjp(_glu, ga, gb)
        return vjp(dy)

    dga, dgb = _rows(glu_bwd, [ga, gb, dy], [], [(D, _MXU), (D, _MXU)], [], name="ssm_glu_bwd")
    dw_a = _mm(gt, dga, name="ssm_glu_a_dw")
    dw_b = _mm(gt, dgb, name="ssm_glu_b_dw")
    dg_a = _mm(dga, w_a, tb=True, name="ssm_glu_a_dx")
    dg_b = _mm(dgb, w_b, tb=True, name="ssm_glu_b_dx")

    def glue_bwd(h, yf, yr, dg_a, dg_b, d):
        _, vjp = jax.vjp(_ssm_glue, h, yf, yr, d)
        dh, dyl, _, dd = vjp(dg_a + dg_b)
        return dh, dyl, dd

    dh0, dyl, dd = _rows(glue_bwd, [h, ys[0], ys[1], dg_a, dg_b], [d_skip], [(D, F32), (D, F32)], [(1, (1, D))], name="ssm_glue_bwd")
    dhs = [dh0]
    dparams = []
    for di, (order, adj) in enumerate((("fwd", "fwd_adj"), ("rev", "rev_adj"))):
        args = (lam_re[di], lam_im[di], log_dt[di], b_re[di], b_im[di], c_re[di], c_im[di])
        (lr, li, wb_r, wb_i, wc_r, wc_i), prep_vjp = jax.vjp(_ssm_prep, *args)
        s_r, s_i = saved[di]
        dwc_r, dwc_i = _outer3d(s_r, s_i, dyl)
        g_r, g_i = _proj3d(dyl, wc_r.transpose(0, 2, 1), wc_i.transpose(0, 2, 1))
        a_r, a_i, dlr, dli = _scan_adj(g_r, g_i, s_r, s_i, lr, li, adj)
        dwb_r, dwb_i = _outer3d(a_r, a_i, h)
        dhs.append(_readout(a_r, a_i, wb_r, wb_i))
        dparams.append(prep_vjp((dlr, dli) + tuple(d.transpose(0, 2, 1) for d in (dwb_r, dwb_i, dwc_r, dwc_i))))
    dsp = [jnp.stack([dparams[0][k], dparams[1][k]], axis=0) for k in range(7)]
    return dhs, dsp, dd, dw_a, dw_b


NCH = T // GM_CHUNK


def _gm_specs():
    full = lambda *s: pl.BlockSpec(s, lambda i: (0,) * len(s))
    zu = pl.BlockSpec((GM_CHUNK, GM_HALF), lambda i: (i, 0))
    zv = pl.BlockSpec((GM_CHUNK, GM_HALF), lambda i: (i, 1))
    pars = [full(1, GM_HALF), pl.BlockSpec((1, GM_HALF), lambda i: (0, 1)), full(1, GM_HALF), full(1, GM_HALF),
            full(GM_HEADS, GM_CHUNK, GM_CHUNK), full(GM_HEADS, GM_CHUNK, 1)]
    return zu, zv, pars, full


def _gm_forward(zu_ref, zv_ref, bu_ref, bv_ref, g_ref, b_ref, ws_ref, bs_ref):
    u = jax.nn.gelu(zu_ref[...] + bu_ref[...])
    zv = jax.nn.gelu(zv_ref[...] + bv_ref[...])
    mu = jnp.mean(zv, axis=-1, keepdims=True)
    zc = zv - mu
    rstd = lax.rsqrt(jnp.mean(jnp.square(zc), axis=-1, keepdims=True) + LN_EPS)
    vhat = zc * rstd
    v = (vhat * g_ref[...] + b_ref[...]).astype(_MXU)
    gates = [jnp.dot(ws_ref[hd].astype(_MXU), v[:, GM_HD * hd:GM_HD * (hd + 1)], preferred_element_type=F32) + bs_ref[hd]
             for hd in range(GM_HEADS)]
    return u, vhat, rstd, v, jnp.concatenate(gates, axis=1)


def _gmlp_chunk(zp, b_in, ln_g, ln_b, w_s, b_s):
    def body(zu_ref, zv_ref, bu_ref, bv_ref, g_ref, b_ref, ws_ref, bs_ref, o_ref, ot_ref):
        u, _, _, _, gate = _gm_forward(zu_ref, zv_ref, bu_ref, bv_ref, g_ref, b_ref, ws_ref, bs_ref)
        ug = u * gate
        o_ref[...] = ug.astype(o_ref.dtype)
        ot_ref[...] = ug.T.astype(ot_ref.dtype)

    zu, zv, pars, _ = _gm_specs()
    return _pcall(body, name="gmlp_chunk", grid=(NCH,), in_specs=[zu, zv, *pars],
                  out_specs=[zu, pl.BlockSpec((GM_HALF, GM_CHUNK), lambda i: (0, i))],
                  out_shape=[jax.ShapeDtypeStruct((T, GM_HALF), _MXU), jax.ShapeDtypeStruct((GM_HALF, T), _MXU)],
                  compiler_params=_cp(("parallel",)))(zp, zp, b_in, b_in, ln_g, ln_b, w_s, b_s)


def _gmlp_chunk_bwd(zp, do, b_in, ln_g, ln_b, w_s, b_s):
    def body(zu_ref, zv_ref, do_ref, bu_ref, bv_ref, g_ref, b_ref, ws_ref, bs_ref,
             dzu_ref, dzv_ref, dbu_ref, dbv_ref, dg_ref, db_ref, dws_ref, dbs_ref):
        i = pl.program_id(0)

        def acc(ref, val, idx=None):
            @pl.when(i == 0)
            def _():
                if idx is None:
                    ref[...] = val
                else:
                    ref[idx] = val

            @pl.when(i > 0)
            def _():
                if idx is None:
                    ref[...] += val
                else:
                    ref[idx] += val

        u, vhat, rstd, v, gate = _gm_forward(zu_ref, zv_ref, bu_ref, bv_ref, g_ref, b_ref, ws_ref, bs_ref)
        do = do_ref[...]
        du = do * gate
        dgate = do * u
        dvs = []
        for hd in range(GM_HEADS):
            sl = slice(GM_HD * hd, GM_HD * (hd + 1))
            dgh = dgate[:, sl]
            dghb = dgh.astype(_MXU)
            dvs.append(lax.dot_general(ws_ref[hd].astype(_MXU), dghb, (((0,), (0,)), ((), ())), preferred_element_type=F32))
            acc(dws_ref, lax.dot_general(dghb, v[:, sl], (((1,), (1,)), ((), ())), preferred_element_type=F32), hd)
            acc(dbs_ref, jnp.sum(dgh, axis=1, keepdims=True), hd)
        dv = jnp.concatenate(dvs, axis=1)
        acc(dg_ref, jnp.sum(dv * vhat, axis=0, keepdims=True))
        acc(db_ref, jnp.sum(dv, axis=0, keepdims=True))
        dvh = dv * g_ref[...]
        dzv = rstd * (dvh - jnp.mean(dvh, axis=-1, keepdims=True) - vhat * jnp.mean(dvh * vhat, axis=-1, keepdims=True))
        dpu = jax.vjp(jax.nn.gelu, zu_ref[...] + bu_ref[...])[1](du)[0]
        dpv = jax.vjp(jax.nn.gelu, zv_ref[...] + bv_ref[...])[1](dzv)[0]
        dzu_ref[...] = dpu.astype(dzu_ref.dtype)
        dzv_ref[...] = dpv.astype(dzv_ref.dtype)
        acc(dbu_ref, jnp.sum(dpu, axis=0, keepdims=True))
        acc(dbv_ref, jnp.sum(dpv, axis=0, keepdims=True))

    zu, zv, pars, full = _gm_specs()
    out_specs = [zu, zu, full(1, GM_HALF), full(1, GM_HALF), full(1, GM_HALF), full(1, GM_HALF),
                 full(GM_HEADS, GM_CHUNK, GM_CHUNK), full(GM_HEADS, GM_CHUNK, 1)]
    out_shape = [jax.ShapeDtypeStruct((T, GM_HALF), _MXU)] * 2 + [jax.ShapeDtypeStruct((1, GM_HALF), F32)] * 4 + \
                [jax.ShapeDtypeStruct((GM_HEADS, GM_CHUNK, GM_CHUNK), F32), jax.ShapeDtypeStruct((GM_HEADS, GM_CHUNK, 1), F32)]
    dzu, dzv, dbu, dbv, dg, db, dws, dbs = _pcall(
        body, name="gmlp_chunk_bwd", grid=(NCH,), in_specs=[zu, zv, zu, *pars], out_specs=out_specs, out_shape=out_shape,
        compiler_params=_cp(("arbitrary",)))(zp, zp, do, b_in, b_in, ln_g, ln_b, w_s, b_s)
    return jnp.concatenate([dzu, dzv], axis=1), jnp.concatenate([dbu, dbv], axis=1), dg, db, dws, dbs


def _gmlp_mixer_fwd(h, w_in, b_in, ln_g, ln_b, w_s, b_s, w_out):
    zp = _mm(h, w_in, name="gmlp_in")
    ug, ugt = _gmlp_chunk(zp, b_in, ln_g, ln_b, w_s, b_s[..., None])
    return _mm(ug, w_out, name="gmlp_out"), (zp, ugt)


def _gmlp_mixer_bwd(dy, h, saved, w_in, b_in, ln_g, ln_b, w_s, b_s, w_out):
    zp, ugt = saved
    dw_out = _mm(ugt, dy, name="gmlp_out_dw")
    do = _mm(dy, w_out, tb=True, name="gmlp_out_dx")
    dzp, db_in, dg, db, dws, dbs = _gmlp_chunk_bwd(zp, do, b_in, ln_g, ln_b, w_s, b_s[..., None])
    dw_in = _mm(h, dzp, ta=True, name="gmlp_in_dw")
    dh = _mm(dzp, w_in, tb=True, name="gmlp_in_dx")
    return dh, dw_in, db_in, dg, db, dws, dbs[..., 0], dw_out


def _loss_head(x, target):
    nct = NCTX // TM

    def body(x_ref, t_ref, l_ref, dx_ref):
        i = pl.program_id(0)
        err = jnp.where(i >= nct, x_ref[...] - t_ref[...], 0.0)
        dx_ref[...] = err * (1.0 / D)
        part = 0.5 * jnp.sum(jnp.sum(jnp.square(err), axis=-1, keepdims=True) * (1.0 / D), axis=0, keepdims=True)

        @pl.when(i == 0)
        def _():
            l_ref[...] = part

        @pl.when(i > 0)
        def _():
            l_ref[...] += part

    return _pcall(body, name="loss_head", grid=(NT,),
                  in_specs=[pl.BlockSpec((TM, D), lambda i: (i, 0)), pl.BlockSpec((TM, D), lambda i: (jnp.maximum(i - nct, 0), 0))],
                  out_specs=[pl.BlockSpec((1, 1), lambda i: (0, 0)), pl.BlockSpec((TM, D), lambda i: (i, 0))],
                  out_shape=[jax.ShapeDtypeStruct((1, 1), F32), jax.ShapeDtypeStruct((T, D), F32)],
                  compiler_params=_cp(("arbitrary",)))(x, target)


def _as2d(a):
    return a.reshape(-1, a.shape[-1])


def _adamw(w, g, m, v):
    shape = w.shape
    w2, g2, m2, v2 = _as2d(w), _as2d(g), _as2d(m), _as2d(v)
    R, C = w2.shape
    tr = _tile(R, 512, SUBLANE)
    c1 = 1.0 - B1 ** STEP
    c2 = 1.0 - B2 ** STEP

    def body(w_ref, g_ref, m_ref, v_ref, d_ref, nm_ref, nv_ref):
        g = g_ref[...]
        m = B1 * m_ref[...] + (1.0 - B1) * g
        v = B2 * v_ref[...] + (1.0 - B2) * jnp.square(g)
        nm_ref[...] = m
        nv_ref[...] = v
        d_ref[...] = -LR * ((m / c1) / (jnp.sqrt(v / c2) + EPS) + WD * w_ref[...])

    spec = pl.BlockSpec((tr, C), lambda i: (i, 0))
    outs = _pcall(body, name="adamw", grid=(R // tr,), in_specs=[spec] * 4, out_specs=[spec] * 3,
                  out_shape=[jax.ShapeDtypeStruct((R, C), F32)] * 3, compiler_params=_cp(("parallel",)))(w2, g2, m2, v2)
    return tuple(o.reshape(shape) for o in outs)


def _sum_slabs(x):
    n = x.shape[0]
    x = x.reshape(n, -1, x.shape[-1])
    _, R, C = x.shape
    tr = _tile(R, 256, 16)

    def body(x_ref, o_ref):
        acc = x_ref[0].astype(F32)
        for k in range(1, n):
            acc = acc + x_ref[k].astype(F32)
        o_ref[...] = acc

    return _pcall(body, name="sum_slabs", grid=(R // tr,), in_specs=[pl.BlockSpec((n, tr, C), lambda i: (0, i, 0))],
                  out_specs=pl.BlockSpec((tr, C), lambda i: (i, 0)), out_shape=jax.ShapeDtypeStruct((R, C), F32),
                  compiler_params=_cp(("parallel",)))(x)


def _comm_call(body, xs, out_shape, name):
    n = len(xs)
    hbm = pl.BlockSpec(memory_space=pl.ANY)
    return _pcall(body, name=name, in_specs=[hbm] * n, out_specs=[hbm] * n, out_shape=out_shape,
                  scratch_shapes=[pltpu.SemaphoreType.DMA((n, NDEV - 1)), pltpu.SemaphoreType.DMA((n, NDEV - 1)),
                                  pltpu.SemaphoreType.DMA((n,))],
                  compiler_params=pltpu.CompilerParams(has_side_effects=True))(*xs)


def _exchange(xs, name):
    n = len(xs)

    def body(*refs):
        x_refs, o_refs = refs[:n], refs[n:2 * n]
        send_sems, recv_sems, loc_sems = refs[2 * n:]
        mx, my, mc = lax.axis_index("x"), lax.axis_index("y"), lax.axis_index("c")
        me = 4 * mx + 2 * my + mc
        pending = []
        for a in range(n):
            mine = pltpu.make_async_copy(x_refs[a].at[me], o_refs[a].at[me], loc_sems.at[a])
            mine.start()
            pending.append(mine)
            for k in range(1, NDEV):
                px = 1 - mx if k & 4 else mx
                py = 1 - my if k & 2 else my
                pc = 1 - mc if k & 1 else mc
                cp = pltpu.make_async_remote_copy(
                    src_ref=x_refs[a].at[4 * px + 2 * py + pc], dst_ref=o_refs[a].at[me],
                    send_sem=send_sems.at[a, k - 1], recv_sem=recv_sems.at[a, k - 1],
                    device_id=(px, py, pc), device_id_type=pl.DeviceIdType.MESH)
                cp.start()
                pending.append(cp)
        for cp in pending:
            cp.wait()

    return _comm_call(body, xs, [jax.ShapeDtypeStruct(tuple(x.shape), x.dtype) for x in xs], name)


NCHIP = NDEV // 2


def _sibling_exchange(xs, name):
    n = len(xs)

    def body(*refs):
        x_refs, o_refs = refs[:n], refs[n:2 * n]
        send_sems, recv_sems, _ = refs[2 * n:]
        mx, my, mc = lax.axis_index("x"), lax.axis_index("y"), lax.axis_index("c")
        pending = []
        for a in range(n):
            for b in range(NCHIP):
                cp = pltpu.make_async_remote_copy(
                    src_ref=x_refs[a].at[2 * b + (1 - mc)], dst_ref=o_refs[a].at[b],
                    send_sem=send_sems.at[a, b], recv_sem=recv_sems.at[a, b],
                    device_id=(mx, my, 1 - mc), device_id_type=pl.DeviceIdType.MESH)
                cp.start()
                pending.append(cp)
        for cp in pending:
            cp.wait()

    return _comm_call(body, xs, [jax.ShapeDtypeStruct((NCHIP,) + tuple(x.shape[1:]), x.dtype) for x in xs], name)


def _chip_exchange(xs, name):
    n = len(xs)

    def body(*refs):
        x_refs, o_refs = refs[:n], refs[n:2 * n]
        send_sems, recv_sems, loc_sems = refs[2 * n:]
        mx, my, mc = lax.axis_index("x"), lax.axis_index("y"), lax.axis_index("c")
        chip = 2 * mx + my
        pending = []
        for a in range(n):
            mine = pltpu.make_async_copy(x_refs[a].at[chip], o_refs[a].at[chip], loc_sems.at[a])
            mine.start()
            pending.append(mine)
            for k in range(1, NCHIP):
                px = 1 - mx if k & 2 else mx
                py = 1 - my if k & 1 else my
                cp = pltpu.make_async_remote_copy(
                    src_ref=x_refs[a].at[2 * px + py], dst_ref=o_refs[a].at[chip],
                    send_sem=send_sems.at[a, k - 1], recv_sem=recv_sems.at[a, k - 1],
                    device_id=(px, py, mc), device_id_type=pl.DeviceIdType.MESH)
                cp.start()
                pending.append(cp)
        for cp in pending:
            cp.wait()

    return _comm_call(body, xs, [jax.ShapeDtypeStruct(tuple(x.shape), x.dtype) for x in xs], name)


def _pair_sum(x, y):
    _, _, R, C = x.shape
    tr = _tile(R, 128, 16)

    def body(x_ref, y_ref, o_ref):
        mc = lax.axis_index("c")
        mine = jnp.where(mc == 0, x_ref[:, 0].astype(F32), x_ref[:, 1].astype(F32))
        o_ref[...] = (mine + y_ref[...].astype(F32)).astype(o_ref.dtype)

    return _pcall(body, name="pair_sum", grid=(R // tr,),
                  in_specs=[pl.BlockSpec((NCHIP, 2, tr, C), lambda i: (0, 0, i, 0)), pl.BlockSpec((NCHIP, tr, C), lambda i: (0, i, 0))],
                  out_specs=pl.BlockSpec((NCHIP, tr, C), lambda i: (0, i, 0)), out_shape=jax.ShapeDtypeStruct((NCHIP, R, C), x.dtype),
                  compiler_params=_cp(("parallel",)))(x, y)


def _reduce_scatter(xs, name):
    from_sibling = _sibling_exchange(xs, name + "_d2d")
    pair = [_pair_sum(x.reshape(NCHIP, 2, -1, x.shape[-1]), y.reshape(NCHIP, -1, x.shape[-1])) for x, y in zip(xs, from_sibling)]
    got = _chip_exchange(pair, name + "_ici")
    return [_sum_slabs(g).reshape(x.shape[1:]) for g, x in zip(got, xs)]


def _gather(xs, name):
    n = len(xs)

    def body(*refs):
        x_refs, o_refs = refs[:n], refs[n:2 * n]
        send_sems, recv_sems, loc_sems = refs[2 * n:]
        mx, my, mc = lax.axis_index("x"), lax.axis_index("y"), lax.axis_index("c")
        me = 4 * mx + 2 * my + mc
        sibling = (mx, my, 1 - mc)
        chips = [(1 - mx, my), (mx, 1 - my), (1 - mx, 1 - my)]
        slot = lambda px, py, pc: 4 * px + 2 * py + pc

        def copy(a, k, s, to, from_input=False):
            return pltpu.make_async_remote_copy(
                src_ref=x_refs[a] if from_input else o_refs[a].at[s], dst_ref=o_refs[a].at[s],
                send_sem=send_sems.at[a, k], recv_sem=recv_sems.at[a, k], device_id=to, device_id_type=pl.DeviceIdType.MESH)

        sends, mines = [], []
        for a in range(n):
            mine = pltpu.make_async_copy(x_refs[a], o_refs[a].at[me], loc_sems.at[a])
            mine.start()
            mines.append(mine)
            first = [copy(a, 0, me, sibling, True)] + [copy(a, 1 + j, me, (cx, cy, mc), True) for j, (cx, cy) in enumerate(chips)]
            for cp in first:
                cp.start()
            sends += first
        for a in range(n):
            for j, (cx, cy) in enumerate(chips):
                s = slot(cx, cy, mc)
                copy(a, 1 + j, s, sibling).wait_recv()
                passed = copy(a, 4 + j, s, sibling)
                passed.start()
                sends.append(passed)
        for a in range(n):
            copy(a, 0, slot(*sibling), sibling).wait_recv()
            for j, (cx, cy) in enumerate(chips):
                copy(a, 4 + j, slot(cx, cy, 1 - mc), sibling).wait_recv()
        for cp in sends:
            cp.wait_send()
        for mine in mines:
            mine.wait()

    return _comm_call(body, xs, [jax.ShapeDtypeStruct((NDEV,) + tuple(x.shape), x.dtype) for x in xs], name)


SLAB_W = 1024
SLAB_ROWS = 16


def _pack(parts, lead=None):
    if lead is None:
        flat = jnp.concatenate([p.reshape(-1) for p in parts])
        n = flat.shape[0]
        padn = -n % (SLAB_ROWS * SLAB_W)
        return jnp.pad(flat, (0, padn)).reshape(-1, SLAB_W)
    flat = jnp.concatenate([p.reshape(lead, -1) for p in parts], axis=1)
    n = flat.shape[1]
    padn = -n % (SLAB_ROWS * SLAB_W)
    return jnp.pad(flat, ((0, 0), (0, padn))).reshape(lead, -1, SLAB_W)


def _unpack(buf, shapes, lead=None):
    out, off = [], 0
    flat = buf.reshape(-1) if lead is None else buf.reshape(lead, -1)
    for s in shapes:
        n = int(np.prod(s))
        if lead is None:
            out.append(flat[off:off + n].reshape(s))
        else:
            out.append(flat[:, off:off + n].reshape((lead,) + tuple(s)))
        off += n
    return out


def _gathered(blk, ax):
    m = jnp.moveaxis(blk, 0, ax)
    s = list(m.shape)
    return m.reshape(s[:ax] + [s[ax] * s[ax + 1]] + s[ax + 2:])


def _scattered(full, ax):
    s = list(full.shape)
    m = full.reshape(s[:ax] + [NDEV, s[ax] // NDEV] + s[ax + 1:])
    return jnp.moveaxis(m, ax, 0)


_MM_SHARDED = (("ffn_w_up", 2), ("ffn_w_down", 1), ("pool_w", 2), ("attn_w_qkv", 2), ("attn_w_o", 1),
               ("ssm_w_glu_a", 1), ("ssm_w_glu_b", 1), ("gmlp_w_in", 2), ("gmlp_w_out", 1))
_VEC_SHARDED = (("ffn_conv_w", 2), ("ssm_d", 1), ("gmlp_b_in", 1), ("gmlp_ln_g", 1), ("gmlp_ln_b", 1))
_REPLICATED = ("ln1_g", "ln1_b", "ln2_g", "ln2_b", "ffn_conv_b", "pool_b", "pool_scale", "attn_sink",
               "ssm_lambda_re", "ssm_lambda_im", "ssm_log_dt", "ssm_b_re", "ssm_b_im", "ssm_c_re", "ssm_c_im",
               "gmlp_w_s", "gmlp_b_s")
_WEIGHTS = ("c_ctx", "ada_w", "ada_b", "ln1_g", "ln1_b", "ln2_g", "ln2_b", "ffn_w_up", "ffn_conv_w", "ffn_conv_b", "ffn_w_down",
            "pool_w", "pool_b", "pool_scale", "attn_w_qkv", "attn_w_o", "attn_sink", "ssm_lambda_re", "ssm_lambda_im",
            "ssm_log_dt", "ssm_b_re", "ssm_b_im", "ssm_c_re", "ssm_c_im", "ssm_d", "ssm_w_glu_a", "ssm_w_glu_b",
            "gmlp_w_in", "gmlp_b_in", "gmlp_ln_g", "gmlp_ln_b", "gmlp_w_s", "gmlp_b_s", "gmlp_w_out")
N_MODS = 6
ADA_COLS = N_MODS * D // NDEV
PAD_ROWS = 16


def _silu_rows(x):
    return _rows(lambda v: (jax.nn.silu(v),), [x], [], [(x.shape[1], F32)], [], name="silu", tm=x.shape[0])[0]


def _step(x, c, ctx, loss_target, w, m, v):
    mx, my, mc = lax.axis_index("x"), lax.axis_index("y"), lax.axis_index("c")
    me = 4 * mx + 2 * my + mc

    vec_buf = _pack([w[n] for n, _ in _VEC_SHARDED] + [c])
    *mm_parts, vec_all = _gather([w[n].astype(_MXU) for n, _ in _MM_SHARDED] + [vec_buf], "gather_weights")
    vec_parts = _unpack(vec_all, [w[n].shape for n, _ in _VEC_SHARDED] + [c.shape], lead=NDEV)
    full = {n: _gathered(p, ax) for (n, ax), p in zip(_MM_SHARDED, mm_parts)}
    full.update({n: _gathered(p, ax) for (n, ax), p in zip(_VEC_SHARDED, vec_parts[:-1])})
    c_all = vec_parts[-1].reshape(NDEV, D)

    cc = jnp.concatenate([c_all, w["c_ctx"].reshape(1, D), jnp.zeros((PAD_ROWS - NDEV - 1, D), F32)], axis=0)
    silu_cc = _silu_rows(cc)
    ada_b_mine = lax.dynamic_slice(w["ada_b"], (0, me * ADA_COLS), (DEPTH, ADA_COLS))
    mods_mine = jnp.stack([_mm(silu_cc, w["ada_w"][l], name="ada_mods") + ada_b_mine[l][None, :] for l in range(DEPTH)], axis=1)
    per_dev = mods_mine[:NDEV].reshape(NDEV, DEPTH * ADA_COLS)
    cm = jnp.broadcast_to(mods_mine[NDEV].reshape(1, DEPTH * ADA_COLS), (NDEV, DEPTH * ADA_COLS))
    mods_all = _exchange([_pack([per_dev, cm], lead=NDEV)], "scatter_mods")[0]
    got = _unpack(mods_all, [(DEPTH, ADA_COLS), (DEPTH, ADA_COLS)], lead=NDEV)
    mods = got[0].transpose(1, 0, 2).reshape(DEPTH, N_MODS * D)
    cmods = got[1].transpose(1, 0, 2).reshape(DEPTH, N_MODS * D)
    P = [[jnp.stack([cmods[l, k * D:(k + 1) * D], mods[l, k * D:(k + 1) * D]]).reshape(2, 1, D) for k in range(N_MODS)]
         for l in range(DEPTH)]
    row = lambda a, l: a[l].reshape(1, 1, D)

    xs = jnp.concatenate([ctx[0], x[0]], axis=0)
    sp = tuple(w[n][0] for n in ("ssm_lambda_re", "ssm_lambda_im", "ssm_log_dt", "ssm_b_re", "ssm_b_im", "ssm_c_re", "ssm_c_im")) + \
        (full["ssm_d"].reshape(1, 1, D),)
    gm = (full["gmlp_w_in"][0], full["gmlp_b_in"], full["gmlp_ln_g"], full["gmlp_ln_b"], w["gmlp_w_s"][0], w["gmlp_b_s"][0],
          full["gmlp_w_out"][0])
    pool_args = (full["pool_w"][0], w["pool_b"], w["pool_scale"])
    saved = []
    for l in range(DEPTH):
        sh1, sc1, gt1, sh2, sc2, gt2 = P[l]
        h1 = _pre_mixer(xs, sh1, sc1, _MXU if l in (1, 3) else F32)
        if l == 0:
            y, ms = _pool_fwd(h1, *pool_args), None
        elif l == 1:
            y, ms = _attn_mixer_fwd(h1, full["attn_w_qkv"][0], full["attn_w_o"][0], w["attn_sink"])
        elif l == 2:
            y, ms = _ssm_mixer_fwd(h1, sp, full["ssm_w_glu_a"][0], full["ssm_w_glu_b"][0])
        else:
            y, ms = _gmlp_mixer_fwd(h1, *gm)
        x1, h2, h2t = _post_mixer(xs, y, gt1, row(w["ln1_g"], l), row(w["ln1_b"], l), sh2, sc2)
        f, (a, ut) = _ffn_fwd(h2, full["ffn_w_up"][l], full["ffn_conv_w"][l], w["ffn_conv_b"][l][None, :], full["ffn_w_down"][l])
        x2 = _post_ffn(x1, f, gt2, row(w["ln2_g"], l), row(w["ln2_b"], l))
        saved.append((xs, h1, y, ms, x1, h2t, a, ut, f))
        xs = x2
    loss, dxs = _loss_head(xs, loss_target[0])

    g = {n: [None] * DEPTH for n in ("ln1_g", "ln1_b", "ln2_g", "ln2_b", "ffn_w_up", "ffn_conv_w", "ffn_conv_b", "ffn_w_down")}
    dP = [None] * DEPTH
    for l in reversed(range(DEPTH)):
        sh1, sc1, gt1, sh2, sc2, gt2 = P[l]
        x0, h1, y, ms, x1, h2t, a, ut, f = saved[l]
        dx1, df, dgt2, g["ln2_g"][l], g["ln2_b"][l] = _post_ffn_bwd(x1, f, dxs, gt2, row(w["ln2_g"], l), row(w["ln2_b"], l))
        dh2, g["ffn_w_up"][l], g["ffn_conv_w"][l], g["ffn_conv_b"][l], g["ffn_w_down"][l] = _ffn_bwd(
            df, h2t, a, ut, full["ffn_w_up"][l], full["ffn_conv_w"][l], w["ffn_conv_b"][l][None, :], full["ffn_w_down"][l])
        dx0, dy, dgt1, g["ln1_g"][l], g["ln1_b"][l], dsh2, dsc2 = _post_mixer_bwd(
            x0, y, dx1, dh2, gt1, row(w["ln1_g"], l), row(w["ln1_b"], l), sh2, sc2)
        if l == 0:
            dh, g["pool_w"], g["pool_b"], g["pool_scale"] = _pool_bwd(h1, dy, *pool_args)
            dhs = [dh]
        elif l == 1:
            dh, g["attn_w_qkv"], g["attn_w_o"], g["attn_sink"] = _attn_mixer_bwd(dy, h1, ms, full["attn_w_qkv"][0], full["attn_w_o"][0])
            dhs = [dh]
        elif l == 2:
            dhs, dsp, dd, g["ssm_w_glu_a"], g["ssm_w_glu_b"] = _ssm_mixer_bwd(dy, h1, ms, sp, full["ssm_w_glu_a"][0], full["ssm_w_glu_b"][0])
            for n, d_ in zip(("ssm_lambda_re", "ssm_lambda_im", "ssm_log_dt", "ssm_b_re", "ssm_b_im", "ssm_c_re", "ssm_c_im"), dsp):
                g[n] = d_
            g["ssm_d"] = dd.reshape(1, D)
        else:
            (dh, g["gmlp_w_in"], g["gmlp_b_in"], g["gmlp_ln_g"], g["gmlp_ln_b"], g["gmlp_w_s"], g["gmlp_b_s"],
             g["gmlp_w_out"]) = _gmlp_mixer_bwd(dy, h1, ms, *gm)
            dhs = [dh]
        dxs, dsh1, dsc1 = _pre_mixer_bwd(x0, dhs, dx0, sh1, sc1)
        dP[l] = (dsh1, dsc1, dgt1, dsh2, dsc2, dgt2)
    grad_x = dxs[NCTX:][None]
    dmods = jnp.stack([jnp.concatenate([p[1, 0] for p in dP[l]]) for l in range(DEPTH)])
    dcmods = jnp.stack([jnp.concatenate([p[0, 0] for p in dP[l]]) for l in range(DEPTH)])

    gfull = {n: (jnp.stack(g[n]) if isinstance(g[n], list) else g[n]) for n in g}
    sharded_names = [n for n, _ in _MM_SHARDED] + [n for n, _ in _VEC_SHARDED]
    sharded_axes = dict(_MM_SHARDED + _VEC_SHARDED)

    def as_param(n, a):
        shard = w[n].shape
        ax = sharded_axes.get(n)
        fs = tuple(s * NDEV if i == ax else s for i, s in enumerate(shard))
        return a.reshape(fs)

    rep = jnp.concatenate([as_param(n, gfull[n]).reshape(-1) for n in _REPLICATED])
    n_rep = rep.shape[0]
    rep = jnp.pad(rep, (0, -n_rep % (NDEV * SLAB_W))).reshape(NDEV, -1)
    by_dev = lambda a: a.reshape(DEPTH, NDEV, ADA_COLS).transpose(1, 0, 2)
    big = [_scattered(as_param(n, gfull[n]), ax).astype(_MXU) for n, ax in _MM_SHARDED]
    parts = [_scattered(as_param(n, gfull[n]), ax) for n, ax in _VEC_SHARDED] + [rep, by_dev(dmods), by_dev(dcmods)]
    grads = {n: r for (n, _), r in zip(_MM_SHARDED, _reduce_scatter(big, "scatter_grads"))}
    grads_in = _exchange([_pack(parts, lead=NDEV)], "scatter_small")[0]
    shapes = [w[n].shape for n, _ in _VEC_SHARDED] + [(rep.shape[1],), (DEPTH, ADA_COLS), (DEPTH, ADA_COLS)]
    red = _unpack(_sum_slabs(grads_in), shapes)
    grads.update({n: r for (n, _), r in zip(_VEC_SHARDED, red)})
    rep_mine, dcm = red[-3], red[-1]
    dm_all = _unpack(grads_in, shapes, lead=NDEV)[-2]

    e_rows = jnp.concatenate([dm_all, dcm[None], jnp.zeros((PAD_ROWS - NDEV - 1, DEPTH, ADA_COLS), F32)], axis=0)
    grads["ada_w"] = jnp.stack([_mm(silu_cc, e_rows[:, l], ta=True, name="ada_dw") for l in range(DEPTH)])
    ada_b_blk = jnp.sum(e_rows, axis=0)
    dcm_rows = jnp.concatenate([dcm[None], jnp.zeros((PAD_ROWS - 1, DEPTH, ADA_COLS), F32)], axis=0)
    cpart = sum(_mm(dcm_rows[:, l], w["ada_w"][l], tb=True, name="ada_dc")[0] for l in range(DEPTH))

    small_all = _gather([_pack([rep_mine, ada_b_blk, cpart])], "gather_small")[0]
    sm = _unpack(small_all, [rep_mine.shape, (DEPTH, ADA_COLS), (D,)], lead=NDEV)
    rep_full = sm[0].reshape(-1)[:n_rep]
    off = 0
    for n in _REPLICATED:
        k = int(np.prod(w[n].shape))
        grads[n] = rep_full[off:off + k].reshape(w[n].shape)
        off += k
    grads["ada_b"] = sm[1].transpose(1, 0, 2).reshape(DEPTH, N_MODS * D)
    csum = _unpack(_sum_slabs(small_all), [rep_mine.shape, (DEPTH, ADA_COLS), (D,)])[2]

    def dsilu(vv, dd):
        return (jax.vjp(jax.nn.silu, vv)[1](dd)[0],)

    grads["c_ctx"] = _rows(dsilu, [jnp.broadcast_to(w["c_ctx"][None], (SUBLANE, D)), jnp.broadcast_to(csum[None], (SUBLANE, D))],
                           [], [(D, F32)], [], name="dsilu", tm=SUBLANE)[0][0]

    delta, new_m, new_v = {}, {}, {}
    for n in _WEIGHTS:
        delta[n], new_m[n], new_v[n] = _adamw(w[n], grads[n], m[n], v[n])
    loss = lax.psum(loss[0, 0], ("x", "y", "c"))
    return loss, grad_x, grads, delta, new_m, new_v


def kernel(x, c, ctx, c_ctx, ada_w, ada_b, ln1_g, ln1_b, ln2_g, ln2_b, ffn_w_up, ffn_conv_w, ffn_conv_b, ffn_w_down, pool_w, pool_b, pool_scale, attn_w_qkv, attn_w_o, attn_sink, ssm_lambda_re, ssm_lambda_im, ssm_log_dt, ssm_b_re, ssm_b_im, ssm_c_re, ssm_c_im, ssm_d, ssm_w_glu_a, ssm_w_glu_b, gmlp_w_in, gmlp_b_in, gmlp_ln_g, gmlp_ln_b, gmlp_w_s, gmlp_b_s, gmlp_w_out, loss_target, m_c_ctx, m_ada_w, m_ada_b, m_ln1_g, m_ln1_b, m_ln2_g, m_ln2_b, m_ffn_w_up, m_ffn_conv_w, m_ffn_conv_b, m_ffn_w_down, m_pool_w, m_pool_b, m_pool_scale, m_attn_w_qkv, m_attn_w_o, m_attn_sink, m_ssm_lambda_re, m_ssm_lambda_im, m_ssm_log_dt, m_ssm_b_re, m_ssm_b_im, m_ssm_c_re, m_ssm_c_im, m_ssm_d, m_ssm_w_glu_a, m_ssm_w_glu_b, m_gmlp_w_in, m_gmlp_b_in, m_gmlp_ln_g, m_gmlp_ln_b, m_gmlp_w_s, m_gmlp_b_s, m_gmlp_w_out, v_c_ctx, v_ada_w, v_ada_b, v_ln1_g, v_ln1_b, v_ln2_g, v_ln2_b, v_ffn_w_up, v_ffn_conv_w, v_ffn_conv_b, v_ffn_w_down, v_pool_w, v_pool_b, v_pool_scale, v_attn_w_qkv, v_attn_w_o, v_attn_sink, v_ssm_lambda_re, v_ssm_lambda_im, v_ssm_log_dt, v_ssm_b_re, v_ssm_b_im, v_ssm_c_re, v_ssm_c_im, v_ssm_d, v_ssm_w_glu_a, v_ssm_w_glu_b, v_gmlp_w_in, v_gmlp_b_in, v_gmlp_ln_g, v_gmlp_ln_b, v_gmlp_w_s, v_gmlp_b_s, v_gmlp_w_out):
    args = dict(locals())
    w = {n: args[n] for n in _WEIGHTS}
    m = {n: args["m_" + n] for n in _WEIGHTS}
    v = {n: args["v_" + n] for n in _WEIGHTS}
    loss, grad_x, grads, delta, new_m, new_v = _step(x, c, ctx, loss_target, w, m, v)
    return (loss, grad_x, *[grads[n] for n in _WEIGHTS], *[delta[n] for n in _WEIGHTS],
            *[new_m[n] for n in _WEIGHTS], *[new_v[n] for n in _WEIGHTS])
```

```python
import functools
import math

import jax
import jax.numpy as jnp
import numpy as np
from jax import lax
from jax.experimental import pallas as pl
from jax.experimental.pallas import tpu as pltpu

D = 1024
SEQ = 4096
NCTX = 256
T = NCTX + SEQ
DEPTH = 4
NDEV = 8
GRID_W = 64
ALPHA = (2.0 * DEPTH) ** 0.25
LN_EPS = 1e-5
FFN_H = 2816
HEAD_DIM = 64
NQH, NKVH, GQA = 16, 4, 4
WINDOW = 128
ABLK = 128
NEG_INF = -1e30
ROPE_BASE = 10000.0
POOL_WINDOWS = (2, 4, 8, 16)
SSM_G, SSM_P, SSM_C = 64, 64, 16
GM_HALF = 2048
GM_HEADS = 8
GM_HD = GM_HALF // GM_HEADS
GM_CHUNK = 128
B1, B2, LR, EPS, WD, STEP = 0.9, 0.999, 0.001, 1e-8, 0.01, 10

LANE = 128
SUBLANE = 8
VMEM_LIMIT = 56 * 1024 * 1024
MM_OUT_TILE_BYTES = 6 * 1024 * 1024
TM = 256
NT = T // TM

_MXU = jnp.bfloat16
F32 = jnp.float32


def _pcall(body, **kw):
    return pl.pallas_call(body, **kw)


def _cp(sem):
    return pltpu.CompilerParams(dimension_semantics=sem, vmem_limit_bytes=VMEM_LIMIT)


def _tile(dim, pref, align):
    best = None
    for t in range(align, min(dim, pref) + 1, align):
        if dim % t == 0:
            best = t
    return dim if best is None else best


def _mm(a, b, *, ta=False, tb=False, b_cols=None, out_dtype=F32, name):
    if ta:
        a = a.astype(_MXU).T
    M, K = a.shape
    if tb:
        N, K2 = b.shape
    else:
        K2, N = b.shape
    k_start = 0
    if b_cols is not None:
        assert tb
        k_start, K2 = b_cols
    assert K == K2, (a.shape, b.shape, ta, tb)
    tm = _tile(M, 2304, 16)
    tn = _tile(N, max(512, MM_OUT_TILE_BYTES // (4 * tm)), LANE)
    tk = _tile(K, 2304, LANE)
    nk = K // tk
    dims = (((1,), (1,) if tb else (0,)), ((), ()))

    def body(a_ref, b_ref, o_ref, acc_ref):
        k = pl.program_id(2)
        r = lax.dot_general(a_ref[...].astype(_MXU), b_ref[...].astype(_MXU), dims, preferred_element_type=F32)
        if nk == 1:
            o_ref[...] = r.astype(o_ref.dtype)
            return

        @pl.when(k == 0)
        def _():
            acc_ref[...] = r

        @pl.when(k > 0)
        def _():
            acc_ref[...] += r

        @pl.when(k == nk - 1)
        def _():
            o_ref[...] = acc_ref[...].astype(o_ref.dtype)

    a_spec = pl.BlockSpec((tm, tk), lambda j, i, k: (i, k))
    assert k_start % tk == 0
    k0 = k_start // tk
    b_spec = pl.BlockSpec((tn, tk), lambda j, i, k: (j, k + k0)) if tb else pl.BlockSpec((tk, tn), lambda j, i, k: (k, j))
    return _pcall(
        body, name=name, grid=(N // tn, M // tm, nk), in_specs=[a_spec, b_spec],
        out_specs=pl.BlockSpec((tm, tn), lambda j, i, k: (i, j)),
        out_shape=jax.ShapeDtypeStruct((M, N), out_dtype),
        scratch_shapes=[pltpu.VMEM((tm, tn), F32)],
        compiler_params=_cp(("parallel", "parallel", "arbitrary")),
    )(a, b)


def _rows(fn, rows, pars, out_rows, out_pars, *, name, tm=TM, transposed=()):
    R = rows[0].shape[0]
    nt = R // tm
    nct = NCTX // tm
    n_r, n_p, n_or, n_op = len(rows), len(pars), len(out_rows), len(out_pars)

    def sel(S):
        if S == 1:
            return lambda i: 0
        return lambda i: jnp.where(i < nct, 0, 1)

    def body(*refs):
        r_in = refs[:n_r]
        p_in = refs[n_r:n_r + n_p]
        r_out = refs[n_r + n_p:n_r + n_p + n_or]
        p_out = refs[n_r + n_p + n_or:n_r + n_p + n_or + n_op]
        t_out = refs[n_r + n_p + n_or + n_op:]
        i = pl.program_id(0)
        vals = [r[...].astype(F32) for r in r_in] + [p[0] for p in p_in]
        outs = fn(*vals)
        for r, v in zip(r_out, outs[:n_or]):
            r[...] = v.astype(r.dtype)
        for r, k in zip(t_out, transposed):
            r[...] = outs[k].T.astype(r.dtype)
        for (S, _), r, v in zip(out_pars, p_out, outs[n_or:]):
            first = (i == 0) if S == 1 else jnp.logical_or(i == 0, i == nct)

            @pl.when(first)
            def _():
                r[0] = v

            @pl.when(jnp.logical_not(first))
            def _():
                r[0] += v

    def pspec(shape):
        S = shape[0]
        rest = tuple(shape[1:])
        s = sel(S)
        return pl.BlockSpec((1,) + rest, lambda i: (s(i),) + (0,) * len(rest))

    in_specs = [pl.BlockSpec((tm, r.shape[1]), lambda i: (i, 0)) for r in rows] + [pspec(p.shape) for p in pars]
    out_specs = [pl.BlockSpec((tm, w), lambda i: (i, 0)) for w, _ in out_rows] + [pspec((S,) + tuple(sh)) for S, sh in out_pars] + \
                [pl.BlockSpec((out_rows[k][0], tm), lambda i: (0, i)) for k in transposed]
    out_shape = [jax.ShapeDtypeStruct((R, w), dt) for w, dt in out_rows] + \
                [jax.ShapeDtypeStruct((S,) + tuple(sh), F32) for S, sh in out_pars] + \
                [jax.ShapeDtypeStruct((out_rows[k][0], R), out_rows[k][1]) for k in transposed]
    res = _pcall(body, name=name, grid=(nt,), in_specs=in_specs, out_specs=out_specs, out_shape=out_shape,
                 compiler_params=_cp(("arbitrary",)))(*rows, *pars)
    return res


def _ln(z, g, b):
    mu = jnp.mean(z, axis=-1, keepdims=True)
    var = jnp.mean(jnp.square(z - mu), axis=-1, keepdims=True)
    return (z - mu) * lax.rsqrt(var + LN_EPS) * g + b


def _f1(x, sh, sc):
    return x * (1.0 + sc) + sh


def _f2(x, y, gt, g, b, sh, sc):
    x1 = _ln(ALPHA * x + gt * y, g, b)
    return x1, x1 * (1.0 + sc) + sh


def _f3(x1, f, gt, g, b):
    return _ln(ALPHA * x1 + gt * f, g, b)


def _pre_mixer(x, sh, sc, dtype):
    return _rows(lambda x, sh, sc: (_f1(x, sh, sc),), [x], [sh, sc], [(D, dtype)], [], name="pre_mixer")[0]


def _pre_mixer_bwd(x, dhs, dx_prev, sh, sc):
    n = len(dhs)

    def fn(x, *rest):
        dh = rest[0]
        for t in rest[1:n]:
            dh = dh + t
        dxp, sh, sc = rest[n], rest[n + 1], rest[n + 2]
        _, vjp = jax.vjp(_f1, x, sh, sc)
        dx, dsh, dsc = vjp(dh)
        return dxp + dx, dsh, dsc

    return _rows(fn, [x, *dhs, dx_prev], [sh, sc], [(D, F32)], [(2, (1, D)), (2, (1, D))], name="pre_mixer_bwd")


def _post_mixer(x, y, gt, g, b, sh, sc):
    return _rows(_f2, [x, y], [gt, g, b, sh, sc], [(D, F32), (D, _MXU)], [], name="post_mixer", transposed=(1,))


def _post_mixer_bwd(x, y, dx1, dh2, gt, g, b, sh, sc):
    def fn(x, y, dx1, dh2a, dh2b, gt, g, b, sh, sc):
        _, vjp = jax.vjp(_f2, x, y, gt, g, b, sh, sc)
        return vjp((dx1, dh2a + dh2b))

    return _rows(fn, [x, y, dx1, *dh2], [gt, g, b, sh, sc], [(D, F32), (D, F32)],
                 [(2, (1, D)), (1, (1, D)), (1, (1, D)), (2, (1, D)), (2, (1, D))], name="post_mixer_bwd")


def _post_ffn(x1, f, gt, g, b):
    return _rows(lambda *a: (_f3(*a),), [x1, f], [gt, g, b], [(D, F32)], [], name="post_ffn")[0]


def _post_ffn_pre_mixer(x1, f, gt, g, b, sh_next, sc_next, dtype):
    def fn(x1, f, gt, g, b, sh, sc):
        x2 = _f3(x1, f, gt, g, b)
        return x2, _f1(x2, sh, sc)

    return _rows(fn, [x1, f], [gt, g, b, sh_next, sc_next], [(D, F32), (D, dtype)], [], name="post_ffn_pre_mixer")


def _post_ffn_bwd(x1, f, dx2, gt, g, b):
    def fn(x1, f, dx2, gt, g, b):
        _, vjp = jax.vjp(_f3, x1, f, gt, g, b)
        return vjp(dx2)

    return _rows(fn, [x1, f, dx2], [gt, g, b], [(D, F32), (D, _MXU)],
                 [(2, (1, D)), (1, (1, D)), (1, (1, D))], name="post_ffn_bwd")


def _halo_specs(tm, w, col, active=None):
    r8 = tm // SUBLANE
    act = (lambda j, r: r) if active is None else (lambda j, r: jnp.where(active(j), r, 0))
    return [
        pl.BlockSpec((SUBLANE, w), lambda j, i: (act(j, jnp.maximum(i * r8 - 1, 0)), col(j))),
        pl.BlockSpec((tm, w), lambda j, i: (act(j, i), col(j))),
        pl.BlockSpec((SUBLANE, w), lambda j, i: (act(j, jnp.minimum((i + 1) * r8, T // SUBLANE - 1)), col(j))),
    ]


def _seg_flags(i, tm):
    nct = NCTX // tm
    first = jnp.logical_or(i == 0, i == nct)
    last = jnp.logical_or(i == nct - 1, i == T // tm - 1)
    return first, last


def _shift_rows(cur, prev8, next8, first, last):
    tm = cur.shape[0]
    rid = lax.broadcasted_iota(jnp.int32, cur.shape, 0)
    pr = jnp.where(first, 0.0, prev8[SUBLANE - 1:SUBLANE, :])
    nx = jnp.where(last, 0.0, next8[0:1, :])
    up = jnp.where(rid == 0, pr, pltpu.roll(cur, 1, 0))
    dn = jnp.where(rid == tm - 1, nx, pltpu.roll(cur, tm - 1, 0))
    return up, dn


FFN_TC = 1408
FFN_NCT = FFN_H // FFN_TC


def _conv3(cur, prev8, next8, w3, first, last):
    up, dn = _shift_rows(cur, prev8, next8, first, last)
    return up * w3[0:1] + cur * w3[1:2] + dn * w3[2:3], up, dn


def _ffn_mid(a, cw, cb):
    def body(vp, vc, vn, gp, gc, gn, cwv, cwg, cbv, cbg, o_ref, ot_ref):
        first, last = _seg_flags(pl.program_id(1), TM)
        val = _conv3(vc[...], vp[...], vn[...], cwv[...], first, last)[0] + cbv[...]
        gate = _conv3(gc[...], gp[...], gn[...], cwg[...], first, last)[0] + cbg[...]
        u = val * jax.nn.silu(gate)
        o_ref[...] = u.astype(o_ref.dtype)
        ot_ref[...] = u.T.astype(ot_ref.dtype)

    specs = _halo_specs(TM, FFN_TC, lambda j: j) + _halo_specs(TM, FFN_TC, lambda j: j + FFN_NCT)
    specs += [pl.BlockSpec((3, FFN_TC), lambda j, i: (0, j)), pl.BlockSpec((3, FFN_TC), lambda j, i: (0, j + FFN_NCT)),
              pl.BlockSpec((1, FFN_TC), lambda j, i: (0, j)), pl.BlockSpec((1, FFN_TC), lambda j, i: (0, j + FFN_NCT))]
    return _pcall(body, name="ffn_mid", grid=(FFN_NCT, NT), in_specs=specs,
                  out_specs=[pl.BlockSpec((TM, FFN_TC), lambda j, i: (i, j)), pl.BlockSpec((FFN_TC, TM), lambda j, i: (j, i))],
                  out_shape=[jax.ShapeDtypeStruct((T, FFN_H), _MXU), jax.ShapeDtypeStruct((FFN_H, T), _MXU)],
                  compiler_params=_cp(("parallel", "arbitrary")))(a, a, a, a, a, a, cw, cw, cb, cb)


def _ext_rows(p_ref, c_ref, n_ref, first, last):
    p8 = jnp.where(first, 0.0, p_ref[...])
    n8 = jnp.where(last, 0.0, n_ref[...])
    return jnp.concatenate([p8, c_ref[...], n8], axis=0)


def _ffn_mid_bwd(a, du, cw, cb):
    E = TM + 2 * SUBLANE
    ctr = slice(SUBLANE, SUBLANE + TM)

    def body(vp, vc, vn, gp, gc, gn, dp, dc, dn_, cwv, cwg, cbv, cbg, dav_ref, dag_ref, dcwv, dcwg, dcbv, dcbg):
        i = pl.program_id(1)
        first, last = _seg_flags(i, TM)
        ev = _ext_rows(vp, vc, vn, first, last)
        eg = _ext_rows(gp, gc, gn, first, last)
        edu = _ext_rows(dp, dc, dn_, first, last)
        wv, wg = cwv[...], cwg[...]
        vup, vdn = pltpu.roll(ev, 1, 0), pltpu.roll(ev, E - 1, 0)
        gup, gdn = pltpu.roll(eg, 1, 0), pltpu.roll(eg, E - 1, 0)
        val = vup * wv[0:1] + ev * wv[1:2] + vdn * wv[2:3] + cbv[...]
        gate = gup * wg[0:1] + eg * wg[1:2] + gdn * wg[2:3] + cbg[...]
        sg = jax.nn.sigmoid(gate)
        dval = edu * (gate * sg)
        dgate = edu * val * (sg * (1.0 + gate * (1.0 - sg)))

        def conv_t(d, w3):
            return (pltpu.roll(d, E - 1, 0) * w3[0:1] + d * w3[1:2] + pltpu.roll(d, 1, 0) * w3[2:3])[ctr]

        dav_ref[...] = conv_t(dval, wv).astype(dav_ref.dtype)
        dag_ref[...] = conv_t(dgate, wg).astype(dag_ref.dtype)

        def acc(ref, v):
            @pl.when(i == 0)
            def _():
                ref[...] = v

            @pl.when(i > 0)
            def _():
                ref[...] += v

        for dref, d, up, cur, dn, bref in ((dcwv, dval[ctr], vup[ctr], ev[ctr], vdn[ctr], dcbv),
                                           (dcwg, dgate[ctr], gup[ctr], eg[ctr], gdn[ctr], dcbg)):
            acc(dref, jnp.concatenate([jnp.sum(d * up, 0, keepdims=True), jnp.sum(d * cur, 0, keepdims=True),
                                       jnp.sum(d * dn, 0, keepdims=True)], axis=0))
            acc(bref, jnp.sum(d, 0, keepdims=True))

    specs = _halo_specs(TM, FFN_TC, lambda j: j) + _halo_specs(TM, FFN_TC, lambda j: j + FFN_NCT) + _halo_specs(TM, FFN_TC, lambda j: j)
    specs += [pl.BlockSpec((3, FFN_TC), lambda j, i: (0, j)), pl.BlockSpec((3, FFN_TC), lambda j, i: (0, j + FFN_NCT)),
              pl.BlockSpec((1, FFN_TC), lambda j, i: (0, j)), pl.BlockSpec((1, FFN_TC), lambda j, i: (0, j + FFN_NCT))]
    out_specs = [pl.BlockSpec((TM, FFN_TC), lambda j, i: (i, j)), pl.BlockSpec((TM, FFN_TC), lambda j, i: (i, j)),
                 pl.BlockSpec((3, FFN_TC), lambda j, i: (0, j)), pl.BlockSpec((3, FFN_TC), lambda j, i: (0, j)),
                 pl.BlockSpec((1, FFN_TC), lambda j, i: (0, j)), pl.BlockSpec((1, FFN_TC), lambda j, i: (0, j))]
    out_shape = [jax.ShapeDtypeStruct((T, FFN_H), _MXU)] * 2 + [jax.ShapeDtypeStruct((3, FFN_H), F32)] * 2 + \
                [jax.ShapeDtypeStruct((1, FFN_H), F32)] * 2
    dav, dag, dcwv, dcwg, dcbv, dcbg = _pcall(
        body, name="ffn_mid_bwd", grid=(FFN_NCT, NT), in_specs=specs, out_specs=out_specs, out_shape=out_shape,
        compiler_params=_cp(("parallel", "arbitrary")))(a, a, a, a, a, a, du, du, du, cw, cw, cb, cb)
    return dav, dag, jnp.concatenate([dcwv, dcwg], axis=1), jnp.concatenate([dcbv, dcbg], axis=1)


def _ffn_fwd(h2, w_up, cw, cb, w_down):
    a = _mm(h2, w_up, name="ffn_up")
    u, ut = _ffn_mid(a, cw, cb)
    f = _mm(u, w_down, name="ffn_down")
    return f, (a, ut)


def _ffn_bwd(df, h2t, a, ut, w_up, cw, cb, w_down):
    dw_down = _mm(ut, df, name="ffn_down_dw")
    du = _mm(df, w_down, tb=True, name="ffn_down_dx")
    dav, dag, dcw, dcb = _ffn_mid_bwd(a, du, cw, cb)
    dw_up = jnp.concatenate([_mm(h2t, dav, name="ffn_up_dw"), _mm(h2t, dag, name="ffn_up_dw")], axis=1)
    dh2 = [_mm(dav, w_up, tb=True, b_cols=(0, FFN_H), name="ffn_up_dx"), _mm(dag, w_up, tb=True, b_cols=(FFN_H, FFN_H), name="ffn_up_dx")]
    return dh2, dw_up, dcw, dcb, dw_down


def _winsum(e, lo, hi):
    n = e.shape[0]
    acc = None
    for o in range(lo, hi + 1):
        t = e if o == 0 else pltpu.roll(e, (-o) % n, 0)
        acc = t if acc is None else acc + t
    return acc


def _pool_cnt(i, w, rows, off):
    nct = NCTX // TM
    seg_len = jnp.where(i < nct, NCTX, SEQ)
    seg_tile = jnp.where(i < nct, i, i - nct)
    pos = lax.broadcasted_iota(jnp.int32, (rows, 1), 0) - off + seg_tile * TM
    lo = jnp.clip(pos - w // 2, 0, seg_len)
    hi = jnp.clip(pos - w // 2 + w, 0, seg_len)
    return jnp.maximum(hi - lo, 1).astype(F32)


def _pool_fwd(h, pw, pb, ps):
    def body(hp, hc, hn, w_ref, b_ref, s_ref, o_ref):
        i = pl.program_id(1)
        first, last = _seg_flags(i, TM)
        e = _ext_rows(hp, hc, hn, first, last)
        outs = []
        for g, w in enumerate(POOL_WINDOWS):
            sl = slice(256 * g, 256 * (g + 1))
            eg = e[:, sl]
            mean = _winsum(eg, -(w // 2), w // 2 - 1)[SUBLANE:SUBLANE + TM] / _pool_cnt(i, w, TM, 0)
            mixed = mean - hc[:, sl]
            outs.append(jnp.dot(mixed.astype(_MXU), w_ref[g].astype(_MXU), preferred_element_type=F32))
        o_ref[...] = (jnp.concatenate(outs, axis=1) + b_ref[...]) * s_ref[...]

    full = lambda *s: pl.BlockSpec(s, lambda j, i: (0,) * len(s))
    return _pcall(body, name="pool_fwd", grid=(1, NT), in_specs=_halo_specs(TM, D, lambda j: 0) + [full(4, 256, 256), full(1, D), full(1, D)],
                  out_specs=pl.BlockSpec((TM, D), lambda j, i: (i, 0)), out_shape=jax.ShapeDtypeStruct((T, D), F32),
                  compiler_params=_cp(("parallel", "arbitrary")))(h, h, h, pw, pb, ps)


def _pool_bwd(h, dy, pw, pb, ps):
    E = TM + 2 * SUBLANE

    def body(hp, hc, hn, dp, dc, dn, w_ref, b_ref, s_ref, dh_ref, dw_ref, db_ref, ds_ref):
        i = pl.program_id(1)
        first, last = _seg_flags(i, TM)
        e = _ext_rows(hp, hc, hn, first, last)
        de = _ext_rows(dp, dc, dn, first, last)
        dys = de * s_ref[...]
        dhs, pre = [], []
        for g, w in enumerate(POOL_WINDOWS):
            sl = slice(256 * g, 256 * (g + 1))
            wg = w_ref[g].astype(_MXU)
            dyg = dys[:, sl].astype(_MXU)
            dmix = lax.dot_general(dyg, wg, (((1,), (1,)), ((), ())), preferred_element_type=F32)
            q = dmix / _pool_cnt(i, w, E, SUBLANE)
            dhs.append(_winsum(q, -(w // 2) + 1, w // 2)[SUBLANE:SUBLANE + TM] - dmix[SUBLANE:SUBLANE + TM])
            mean = _winsum(e[:, sl], -(w // 2), w // 2 - 1)[SUBLANE:SUBLANE + TM] / _pool_cnt(i, w, TM, 0)
            mixed = (mean - hc[:, sl]).astype(_MXU)
            pre.append(jnp.dot(mixed, wg, preferred_element_type=F32))
            dwg = lax.dot_general(mixed, dyg[SUBLANE:SUBLANE + TM], (((0,), (0,)), ((), ())), preferred_element_type=F32)

            @pl.when(i == 0)
            def _():
                dw_ref[g] = dwg

            @pl.when(i > 0)
            def _():
                dw_ref[g] += dwg
        dh_ref[...] = jnp.concatenate(dhs, axis=1)
        db = jnp.sum(dys[SUBLANE:SUBLANE + TM], 0, keepdims=True)
        ds = jnp.sum(dc[...] * (jnp.concatenate(pre, axis=1) + b_ref[...]), 0, keepdims=True)

        @pl.when(i == 0)
        def _():
            db_ref[...] = db
            ds_ref[...] = ds

        @pl.when(i > 0)
        def _():
            db_ref[...] += db
            ds_ref[...] += ds

    full = lambda *s: pl.BlockSpec(s, lambda j, i: (0,) * len(s))
    specs = _halo_specs(TM, D, lambda j: 0) + _halo_specs(TM, D, lambda j: 0) + [full(4, 256, 256), full(1, D), full(1, D)]
    return _pcall(body, name="pool_bwd", grid=(1, NT), in_specs=specs,
                  out_specs=[pl.BlockSpec((TM, D), lambda j, i: (i, 0)), full(4, 256, 256), full(1, D), full(1, D)],
                  out_shape=[jax.ShapeDtypeStruct((T, D), F32), jax.ShapeDtypeStruct((4, 256, 256), F32),
                             jax.ShapeDtypeStruct((1, D), F32), jax.ShapeDtypeStruct((1, D), F32)],
                  compiler_params=_cp(("parallel", "arbitrary")))(h, h, h, dy, dy, dy, pw, pb, ps)


def _rope_tables():
    half = HEAD_DIM // 4
    t = jnp.arange(SEQ)
    freqs = ROPE_BASE ** (-jnp.arange(half, dtype=F32) / half)
    ang_r = (t // GRID_W).astype(F32)[:, None] * freqs[None, :]
    ang_c = (t % GRID_W).astype(F32)[:, None] * freqs[None, :]
    cos = jnp.concatenate([jnp.cos(ang_r), jnp.cos(ang_r), jnp.cos(ang_c), jnp.cos(ang_c)], axis=1)
    sin = jnp.concatenate([-jnp.sin(ang_r), jnp.sin(ang_r), -jnp.sin(ang_c), jnp.sin(ang_c)], axis=1)
    cos = jnp.concatenate([jnp.ones((NCTX, HEAD_DIM), F32), cos], axis=0)
    sin = jnp.concatenate([jnp.zeros((NCTX, HEAD_DIM), F32), sin], axis=0)
    return jnp.tile(cos, (1, 2)), jnp.tile(sin, (1, 2))


QK_W = (NQH + NKVH) * HEAD_DIM
QKV_W = QK_W + NKVH * HEAD_DIM


def _rope(x, cos, sin, sign):
    def body(x_ref, c_ref, s_ref, o_ref):
        c = c_ref[...]
        s = s_ref[...] * sign
        lane = lax.broadcasted_iota(jnp.int32, (TM, LANE), 1)
        lo = (lane % 32) < 16
        for k in range(QK_W // LANE):
            xk = x_ref[:, LANE * k:LANE * (k + 1)]
            partner = jnp.where(lo, pltpu.roll(xk, LANE - 16, 1), pltpu.roll(xk, 16, 1))
            o_ref[:, LANE * k:LANE * (k + 1)] = (xk * c + partner * s).astype(o_ref.dtype)
        o_ref[:, QK_W:] = x_ref[:, QK_W:].astype(o_ref.dtype)

    return _pcall(body, name="rope", grid=(NT,),
                  in_specs=[pl.BlockSpec((TM, QKV_W), lambda i: (i, 0)), pl.BlockSpec((TM, LANE), lambda i: (i, 0)),
                            pl.BlockSpec((TM, LANE), lambda i: (i, 0))],
                  out_specs=pl.BlockSpec((TM, QKV_W), lambda i: (i, 0)),
                  out_shape=jax.ShapeDtypeStruct((T, QKV_W), _MXU), compiler_params=_cp(("parallel",)))(x, cos, sin)


NQB = T // ABLK
KPAD = T + 2 * ABLK
NKEY = NCTX + 3 * ABLK


def _stack_heads(ref):
    return jnp.concatenate([ref[g] for g in range(GQA)], axis=0)


def _sink_rows(s_ref):
    return jnp.concatenate([jnp.broadcast_to(s_ref[g], (ABLK, 1)) for g in range(GQA)], axis=0)


def _attn_mask(i):
    r = lax.broadcasted_iota(jnp.int32, (GQA * ABLK, NKEY), 0) % ABLK
    c = lax.broadcasted_iota(jnp.int32, (GQA * ABLK, NKEY), 1)
    n = i - NCTX // ABLK
    kpos = (n - 1) * ABLK + (c - NCTX)
    qpos = n * ABLK + r
    loc = (c >= NCTX) & (jnp.abs(kpos - qpos) <= WINDOW) & (kpos >= 0) & (kpos < SEQ) & (n >= 0)
    return (c < NCTX) | loc


def _attn_specs():
    qs = pl.BlockSpec((GQA, ABLK, HEAD_DIM), lambda h, i: (h, i, 0))
    kc = pl.BlockSpec((1, NCTX, HEAD_DIM), lambda h, i: (h, 0, 0))
    kl = [pl.BlockSpec((1, ABLK, HEAD_DIM), functools.partial(lambda h, i, d: (h, i + d, 0), d=d)) for d in range(3)]
    sk = pl.BlockSpec((GQA, 1, 1), lambda h, i: (h, 0, 0))
    return qs, kc, kl, sk


def _attn_fwd(q, k, v, sink):
    scale = HEAD_DIM ** -0.5

    def body(q_ref, kc, k0, k1, k2, vc, v0, v1, v2, s_ref, o_ref, l_ref):
        valid = _attn_mask(pl.program_id(1))
        kk = jnp.concatenate([kc[0], k0[0], k1[0], k2[0]], axis=0)
        vv = jnp.concatenate([vc[0], v0[0], v1[0], v2[0]], axis=0)
        s = lax.dot_general(_stack_heads(q_ref), kk, (((1,), (1,)), ((), ())), preferred_element_type=F32) * scale
        s = jnp.where(valid, s, NEG_INF)
        sk = _sink_rows(s_ref)
        m = jnp.maximum(jnp.max(s, axis=-1, keepdims=True), sk)
        p = jnp.exp(s - m)
        l = jnp.sum(p, axis=-1, keepdims=True) + jnp.exp(sk - m)
        o = jnp.dot((p / l).astype(_MXU), vv, preferred_element_type=F32).astype(o_ref.dtype)
        lse = m + jnp.log(l)
        for g in range(GQA):
            o_ref[g] = o[ABLK * g:ABLK * (g + 1)]
            l_ref[g] = lse[ABLK * g:ABLK * (g + 1)]

    qs, kc, kl, sk = _attn_specs()
    return _pcall(body, name="attn_fwd", grid=(NKVH, NQB), in_specs=[qs, kc, *kl, kc, *kl, sk],
                  out_specs=[qs, pl.BlockSpec((GQA, ABLK, 1), lambda h, i: (h, i, 0))],
                  out_shape=[jax.ShapeDtypeStruct((NQH, T, HEAD_DIM), _MXU), jax.ShapeDtypeStruct((NQH, T, 1), F32)],
                  compiler_params=_cp(("parallel", "arbitrary")))(q, k, k, k, k, v, v, v, v, sink)


def _attn_bwd(q, k, v, sink, lse, do):
    scale = HEAD_DIM ** -0.5

    def body(q_ref, kc, k0, k1, k2, vc, v0, v1, v2, s_ref, l_ref, do_ref, dq_ref, dk_ref, dv_ref, ds_ref):
        i = pl.program_id(1)

        @pl.when(i == 0)
        def _():
            dk_ref[...] = jnp.zeros_like(dk_ref)
            dv_ref[...] = jnp.zeros_like(dv_ref)
            ds_ref[...] = jnp.zeros_like(ds_ref)

        valid = _attn_mask(i)
        kk = jnp.concatenate([kc[0], k0[0], k1[0], k2[0]], axis=0)
        vv = jnp.concatenate([vc[0], v0[0], v1[0], v2[0]], axis=0)
        qst = _stack_heads(q_ref)
        dos = _stack_heads(do_ref)
        lse = _stack_heads(l_ref)
        s = lax.dot_general(qst, kk, (((1,), (1,)), ((), ())), preferred_element_type=F32) * scale
        s = jnp.where(valid, s, NEG_INF)
        p = jnp.exp(s - lse)
        psink = jnp.exp(_sink_rows(s_ref) - lse)
        dp = lax.dot_general(dos, vv, (((1,), (1,)), ((), ())), preferred_element_type=F32)
        delta = jnp.sum(p * dp, axis=-1, keepdims=True)
        ds = p * (dp - delta)
        dsk = psink * delta
        dsq = (ds * scale).astype(_MXU)
        dq = jnp.dot(dsq, kk, preferred_element_type=F32)
        for g in range(GQA):
            dq_ref[g] = dq[ABLK * g:ABLK * (g + 1)]
            ds_ref[g] += -jnp.sum(dsk[ABLK * g:ABLK * (g + 1)], axis=0, keepdims=True)
        dkk = lax.dot_general(dsq, qst, (((0,), (0,)), ((), ())), preferred_element_type=F32)
        dvv = lax.dot_general(p.astype(_MXU), dos, (((0,), (0,)), ((), ())), preferred_element_type=F32)
        loc = pl.ds(pl.multiple_of(i * ABLK, ABLK), 3 * ABLK)
        dk_ref[0, 0:NCTX, :] += dkk[:NCTX]
        dv_ref[0, 0:NCTX, :] += dvv[:NCTX]
        dk_ref[0, loc, :] += dkk[NCTX:]
        dv_ref[0, loc, :] += dvv[NCTX:]

    qs, kc, kl, sk = _attn_specs()
    ls = pl.BlockSpec((GQA, ABLK, 1), lambda h, i: (h, i, 0))
    kfull = pl.BlockSpec((1, KPAD, HEAD_DIM), lambda h, i: (h, 0, 0))
    return _pcall(body, name="attn_bwd", grid=(NKVH, NQB), in_specs=[qs, kc, *kl, kc, *kl, sk, ls, qs],
                  out_specs=[qs, kfull, kfull, sk],
                  out_shape=[jax.ShapeDtypeStruct((NQH, T, HEAD_DIM), F32), jax.ShapeDtypeStruct((NKVH, KPAD, HEAD_DIM), F32),
                             jax.ShapeDtypeStruct((NKVH, KPAD, HEAD_DIM), F32), jax.ShapeDtypeStruct((NQH, 1, 1), F32)],
                  compiler_params=_cp(("parallel", "arbitrary")))(q, k, k, k, k, v, v, v, v, sink, lse, do)


def _split_heads(x, nh):
    return x.reshape(T, nh, HEAD_DIM).transpose(1, 0, 2)


def _merge_heads(x):
    return x.transpose(1, 0, 2).reshape(T, -1)


def _pad_keys(x):
    z = jnp.zeros((x.shape[0], ABLK, HEAD_DIM), x.dtype)
    return jnp.concatenate([x[:, :NCTX], z, x[:, NCTX:], z], axis=1)


def _unpad_keys(x):
    return jnp.concatenate([x[:, :NCTX], x[:, NCTX + ABLK:NCTX + ABLK + SEQ]], axis=1)


def _attn_mixer_fwd(h, w_qkv, w_o, sink):
    cos, sin = _rope_tables()
    qkv = _rope(_mm(h, w_qkv, name="attn_qkv"), cos, sin, 1.0)
    q = _split_heads(qkv[:, :NQH * HEAD_DIM], NQH)
    k = _pad_keys(_split_heads(qkv[:, NQH * HEAD_DIM:QK_W], NKVH))
    v = _pad_keys(_split_heads(qkv[:, QK_W:], NKVH))
    sk = sink.reshape(NQH, 1, 1)
    o, lse = _attn_fwd(q, k, v, sk)
    om = _merge_heads(o)
    y = _mm(om, w_o, name="attn_out")
    return y, (q, k, v, sk, lse, om)


def _attn_mixer_bwd(dy, h, saved, w_qkv, w_o):
    q, k, v, sk, lse, om = saved
    cos, sin = _rope_tables()
    dyb = dy.astype(_MXU)
    dw_o = _mm(om, dyb, ta=True, name="attn_out_dw")
    do = _split_heads(_mm(dyb, w_o, tb=True, out_dtype=_MXU, name="attn_out_dx"), NQH)
    dq, dk, dv, dsk = _attn_bwd(q, k, v, sk, lse, do)
    dqkv = jnp.concatenate([_merge_heads(dq), _merge_heads(_unpad_keys(dk)), _merge_heads(_unpad_keys(dv))], axis=1)
    dqkv = _rope(dqkv, cos, sin, -1.0)
    dw_qkv = _mm(h, dqkv, ta=True, name="attn_qkv_dw")
    dh = _mm(dqkv, w_qkv, tb=True, name="attn_qkv_dx")
    return dh, dw_qkv, dw_o, dsk.reshape(1, NQH)


SSM_S = SSM_G * SSM_P
SSM_SL = SSM_S // LANE
SSM_TS = 128
SSM_NTS = T // SSM_TS
SSM_NCT = NCTX // SSM_TS
SSM_JB = 4
SSM_NTR = 4
SSM_TR = T // SSM_NTR


def _to_slabs(cols):
    return jnp.swapaxes(jnp.stack(cols, axis=0), 0, 1)


def _from_slabs(s_ref):
    x = jnp.swapaxes(s_ref[...], 0, 1)
    return [jnp.concatenate([x[4 * hlf + q] for q in range(4)], axis=1).astype(_MXU) for hlf in range(2)]


def _proj3d(u, w_re, w_im):
    def body(u_ref, wr_ref, wi_ref, or_ref, oi_ref):
        for w_ref, o_ref in ((wr_ref, or_ref), (wi_ref, oi_ref)):
            cols = []
            for hlf in range(2):
                ub = u_ref[:, LANE * hlf:LANE * (hlf + 1)].astype(_MXU)
                r = jnp.dot(ub, w_ref[hlf].astype(_MXU), preferred_element_type=F32)
                cols += [r[:, LANE * q:LANE * (q + 1)] for q in range(4)]
            o_ref[...] = _to_slabs(cols)

    ws = pl.BlockSpec((2, LANE, 512), lambda i, j: (j, 0, 0))
    os_ = pl.BlockSpec((SSM_TR, 8, LANE), lambda i, j: (i, j, 0))
    return _pcall(body, name="ssm_proj", grid=(SSM_NTR, SSM_JB), in_specs=[pl.BlockSpec((SSM_TR, 2 * LANE), lambda i, j: (i, j)), ws, ws],
                  out_specs=[os_, os_], out_shape=[jax.ShapeDtypeStruct((T, SSM_SL, LANE), F32)] * 2,
                  compiler_params=_cp(("parallel", "parallel")))(u, w_re, w_im)


def _readout(s_re, s_im, w_re, w_im):
    def body(sr_ref, si_ref, wr_ref, wi_ref, o_ref):
        xs = [_from_slabs(sr_ref), _from_slabs(si_ref)]
        for hlf in range(2):
            acc = None
            for x, w_ref in zip(xs, (wr_ref, wi_ref)):
                r = jnp.dot(x[hlf], w_ref[hlf].astype(_MXU), preferred_element_type=F32)
                acc = r if acc is None else acc + r
            o_ref[:, LANE * hlf:LANE * (hlf + 1)] = acc

    ss = pl.BlockSpec((SSM_TR, 8, LANE), lambda i, j: (i, j, 0))
    ws = pl.BlockSpec((2, 512, LANE), lambda i, j: (j, 0, 0))
    return _pcall(body, name="ssm_readout", grid=(SSM_NTR, SSM_JB), in_specs=[ss, ss, ws, ws],
                  out_specs=pl.BlockSpec((SSM_TR, 2 * LANE), lambda i, j: (i, j)), out_shape=jax.ShapeDtypeStruct((T, D), F32),
                  compiler_params=_cp(("parallel", "parallel")))(s_re, s_im, w_re, w_im)


def _outer3d(s_re, s_im, y):
    def body(sr_ref, si_ref, y_ref, dr_ref, di_ref):
        i = pl.program_id(1)
        xs = [_from_slabs(sr_ref), _from_slabs(si_ref)]
        for hlf in range(2):
            yb = y_ref[:, LANE * hlf:LANE * (hlf + 1)].astype(_MXU)
            for x, d_ref in zip(xs, (dr_ref, di_ref)):
                r = lax.dot_general(yb, x[hlf], (((0,), (0,)), ((), ())), preferred_element_type=F32)

                @pl.when(i == 0)
                def _():
                    d_ref[hlf] = r

                @pl.when(i > 0)
                def _():
                    d_ref[hlf] += r

    ss = pl.BlockSpec((SSM_TR, 8, LANE), lambda j, i: (i, j, 0))
    ds = pl.BlockSpec((2, LANE, 512), lambda j, i: (j, 0, 0))
    return _pcall(body, name="ssm_outer", grid=(SSM_JB, SSM_NTR), in_specs=[ss, ss, pl.BlockSpec((SSM_TR, 2 * LANE), lambda j, i: (i, j))],
                  out_specs=[ds, ds], out_shape=[jax.ShapeDtypeStruct((8, LANE, 512), F32)] * 2,
                  compiler_params=_cp(("parallel", "arbitrary")))(s_re, s_im, y)


def _scan_order(order):
    n, c = SSM_NTS, SSM_NCT
    if order == "fwd":
        return (lambda i: i), False
    if order == "fwd_adj":
        return (lambda i: n - 1 - i), True
    if order == "rev":
        return (lambda i: jnp.where(i < c, c - 1 - i, n + c - 1 - i)), True
    if order == "rev_adj":
        return (lambda i: jnp.where(i < n - c, i + c, i - (n - c))), False
    raise ValueError(order)


def _scan(b_re, b_im, lam_re, lam_im, order):
    tile, down = _scan_order(order)

    def body(br_ref, bi_ref, lr_ref, li_ref, sr_ref, si_ref, cr, ci):
        @pl.when(pl.program_id(0) == 0)
        def _():
            cr[...] = jnp.zeros_like(cr)
            ci[...] = jnp.zeros_like(ci)

        lr = lr_ref[...]
        li = li_ref[...]

        def step(n, c):
            t = SSM_TS - 1 - n if down else n
            sr, si = c
            nr = lr * sr - li * si + br_ref[t]
            ni = lr * si + li * sr + bi_ref[t]
            sr_ref[t] = nr
            si_ref[t] = ni
            return nr, ni

        sr, si = lax.fori_loop(0, SSM_TS, step, (cr[...], ci[...]))
        cr[...] = sr
        ci[...] = si

    bs = pl.BlockSpec((SSM_TS, SSM_SL, LANE), lambda i: (tile(i), 0, 0))
    ps = pl.BlockSpec((SSM_SL, LANE), lambda i: (0, 0))
    return _pcall(body, name="ssm_scan_" + order, grid=(SSM_NTS,), in_specs=[bs, bs, ps, ps], out_specs=[bs, bs],
                  out_shape=[jax.ShapeDtypeStruct((T, SSM_SL, LANE), F32)] * 2,
                  scratch_shapes=[pltpu.VMEM((SSM_SL, LANE), F32)] * 2, compiler_params=_cp(("arbitrary",)))(b_re, b_im, lam_re, lam_im)


def _scan_adj(g_re, g_im, s_re, s_im, lam_re, lam_im, order):
    tile, down = _scan_order(order)

    def body(gr_ref, gi_ref, sr_ref, si_ref, lr_ref, li_ref, ar_ref, ai_ref, dlr_ref, dli_ref, cr, ci):
        @pl.when(pl.program_id(0) == 0)
        def _():
            cr[...] = jnp.zeros_like(cr)
            ci[...] = jnp.zeros_like(ci)
            dlr_ref[...] = jnp.zeros_like(dlr_ref)
            dli_ref[...] = jnp.zeros_like(dli_ref)

        lr = lr_ref[...]
        li = li_ref[...]

        def step(n, c):
            t = SSM_TS - 1 - n if down else n
            ar, ai, dr, di = c
            sr = sr_ref[t]
            si = si_ref[t]
            dr = dr + ar * sr + ai * si
            di = di + ai * sr - ar * si
            nr = gr_ref[t] + lr * ar + li * ai
            ni = gi_ref[t] + lr * ai - li * ar
            ar_ref[t] = nr
            ai_ref[t] = ni
            return nr, ni, dr, di

        ar, ai, dr, di = lax.fori_loop(0, SSM_TS, step, (cr[...], ci[...], dlr_ref[...], dli_ref[...]))
        cr[...] = ar
        ci[...] = ai
        dlr_ref[...] = dr
        dli_ref[...] = di

    bs = pl.BlockSpec((SSM_TS, SSM_SL, LANE), lambda i: (tile(i), 0, 0))
    ps = pl.BlockSpec((SSM_SL, LANE), lambda i: (0, 0))
    return _pcall(body, name="ssm_scan_" + order, grid=(SSM_NTS,), in_specs=[bs, bs, bs, bs, ps, ps], out_specs=[bs, bs, ps, ps],
                  out_shape=[jax.ShapeDtypeStruct((T, SSM_SL, LANE), F32)] * 2 + [jax.ShapeDtypeStruct((SSM_SL, LANE), F32)] * 2,
                  scratch_shapes=[pltpu.VMEM((SSM_SL, LANE), F32)] * 2,
                  compiler_params=_cp(("arbitrary",)))(g_re, g_im, s_re, s_im, lam_re, lam_im)


def _block_diag(x):
    x4 = x.reshape(8, 8, SSM_P, SSM_C)
    return jnp.einsum("jgpc,gh->jgphc", x4, jnp.eye(8, dtype=x.dtype)).reshape(8, 8 * SSM_P, 8 * SSM_C)


def _ssm_prep(lam_re, lam_im, log_dt, b_re, b_im, c_re, c_im):
    lam = lax.complex(lam_re, lam_im)
    dt = jnp.exp(log_dt)[:, None]
    lam_bar = jnp.exp(lam * dt)
    b_bar = ((lam_bar - 1.0) / lam)[..., None] * lax.complex(b_re, b_im)
    return (jnp.real(lam_bar).reshape(SSM_SL, LANE), jnp.imag(lam_bar).reshape(SSM_SL, LANE),
            _block_diag(jnp.real(b_bar)), _block_diag(jnp.imag(b_bar)),
            _block_diag(c_re.transpose(0, 2, 1)), _block_diag(-c_im.transpose(0, 2, 1)))


def _ssm_glue(h, yf, yr, d):
    return jax.nn.gelu(d * h + yf + yr)


def _glu(ga, gb):
    return ga * jax.nn.sigmoid(gb)


def _ssm_mixer_fwd(h, sp, w_a, w_b):
    lam_re, lam_im, log_dt, b_re, b_im, c_re, c_im, d_skip = sp
    ys, saved = [], []
    for di, order in enumerate(("fwd", "rev")):
        lr, li, wb_r, wb_i, wc_r, wc_i = _ssm_prep(lam_re[di], lam_im[di], log_dt[di], b_re[di], b_im[di], c_re[di], c_im[di])
        bu_r, bu_i = _proj3d(h, wb_r.transpose(0, 2, 1), wb_i.transpose(0, 2, 1))
        s_r, s_i = _scan(bu_r, bu_i, lr, li, order)
        ys.append(_readout(s_r, s_i, wc_r, wc_i))
        saved.append((s_r, s_i))
    g, gt = _rows(lambda *a: (_ssm_glue(*a),), [h, ys[0], ys[1]], [d_skip], [(D, _MXU)], [], name="ssm_glue", transposed=(0,))
    ga = _mm(g, w_a, name="ssm_glu_a")
    gb = _mm(g, w_b, name="ssm_glu_b")
    y = _rows(lambda *a: (_glu(*a),), [ga, gb], [], [(D, F32)], [], name="ssm_glu")[0]
    return y, (ys, saved, gt, ga, gb)


def _ssm_mixer_bwd(dy, h, saved_all, sp, w_a, w_b):
    lam_re, lam_im, log_dt, b_re, b_im, c_re, c_im, d_skip = sp
    ys, saved, gt, ga, gb = saved_all

    def glu_bwd(ga, gb, dy):
        _, vjp = jax.vjp(_glu, ga, gb)
        return vjp(dy)

    dga, dgb = _rows(glu_bwd, [ga, gb, dy], [], [(D, _MXU), (D, _MXU)], [], name="ssm_glu_bwd")
    dw_a = _mm(gt, dga, name="ssm_glu_a_dw")
    dw_b = _mm(gt, dgb, name="ssm_glu_b_dw")
    dg_a = _mm(dga, w_a, tb=True, name="ssm_glu_a_dx")
    dg_b = _mm(dgb, w_b, tb=True, name="ssm_glu_b_dx")

    def glue_bwd(h, yf, yr, dg_a, dg_b, d):
        _, vjp = jax.vjp(_ssm_glue, h, yf, yr, d)
        dh, dyl, _, dd = vjp(dg_a + dg_b)
        return dh, dyl, dd

    dh0, dyl, dd = _rows(glue_bwd, [h, ys[0], ys[1], dg_a, dg_b], [d_skip], [(D, F32), (D, F32)], [(1, (1, D))], name="ssm_glue_bwd")
    dhs = [dh0]
    dparams = []
    for di, (order, adj) in enumerate((("fwd", "fwd_adj"), ("rev", "rev_adj"))):
        args = (lam_re[di], lam_im[di], log_dt[di], b_re[di], b_im[di], c_re[di], c_im[di])
        (lr, li, wb_r, wb_i, wc_r, wc_i), prep_vjp = jax.vjp(_ssm_prep, *args)
        s_r, s_i = saved[di]
        dwc_r, dwc_i = _outer3d(s_r, s_i, dyl)
        g_r, g_i = _proj3d(dyl, wc_r.transpose(0, 2, 1), wc_i.transpose(0, 2, 1))
        a_r, a_i, dlr, dli = _scan_adj(g_r, g_i, s_r, s_i, lr, li, adj)
        dwb_r, dwb_i = _outer3d(a_r, a_i, h)
        dhs.append(_readout(a_r, a_i, wb_r, wb_i))
        dparams.append(prep_vjp((dlr, dli) + tuple(d.transpose(0, 2, 1) for d in (dwb_r, dwb_i, dwc_r, dwc_i))))
    dsp = [jnp.stack([dparams[0][k], dparams[1][k]], axis=0) for k in range(7)]
    return dhs, dsp, dd, dw_a, dw_b


NCH = T // GM_CHUNK


def _gm_specs():
    full = lambda *s: pl.BlockSpec(s, lambda i: (0,) * len(s))
    zu = pl.BlockSpec((GM_CHUNK, GM_HALF), lambda i: (i, 0))
    zv = pl.BlockSpec((GM_CHUNK, GM_HALF), lambda i: (i, 1))
    pars = [full(1, GM_HALF), pl.BlockSpec((1, GM_HALF), lambda i: (0, 1)), full(1, GM_HALF), full(1, GM_HALF),
            full(GM_HEADS, GM_CHUNK, GM_CHUNK), full(GM_HEADS, GM_CHUNK, 1)]
    return zu, zv, pars, full


def _gm_forward(zu_ref, zv_ref, bu_ref, bv_ref, g_ref, b_ref, ws_ref, bs_ref):
    u = jax.nn.gelu(zu_ref[...] + bu_ref[...])
    zv = jax.nn.gelu(zv_ref[...] + bv_ref[...])
    mu = jnp.mean(zv, axis=-1, keepdims=True)
    zc = zv - mu
    rstd = lax.rsqrt(jnp.mean(jnp.square(zc), axis=-1, keepdims=True) + LN_EPS)
    vhat = zc * rstd
    v = (vhat * g_ref[...] + b_ref[...]).astype(_MXU)
    gates = [jnp.dot(ws_ref[hd].astype(_MXU), v[:, GM_HD * hd:GM_HD * (hd + 1)], preferred_element_type=F32) + bs_ref[hd]
             for hd in range(GM_HEADS)]
    return u, vhat, rstd, v, jnp.concatenate(gates, axis=1)


def _gmlp_chunk(zp, b_in, ln_g, ln_b, w_s, b_s):
    def body(zu_ref, zv_ref, bu_ref, bv_ref, g_ref, b_ref, ws_ref, bs_ref, o_ref, ot_ref):
        u, _, _, _, gate = _gm_forward(zu_ref, zv_ref, bu_ref, bv_ref, g_ref, b_ref, ws_ref, bs_ref)
        ug = u * gate
        o_ref[...] = ug.astype(o_ref.dtype)
        ot_ref[...] = ug.T.astype(ot_ref.dtype)

    zu, zv, pars, _ = _gm_specs()
    return _pcall(body, name="gmlp_chunk", grid=(NCH,), in_specs=[zu, zv, *pars],
                  out_specs=[zu, pl.BlockSpec((GM_HALF, GM_CHUNK), lambda i: (0, i))],
                  out_shape=[jax.ShapeDtypeStruct((T, GM_HALF), _MXU), jax.ShapeDtypeStruct((GM_HALF, T), _MXU)],
                  compiler_params=_cp(("parallel",)))(zp, zp, b_in, b_in, ln_g, ln_b, w_s, b_s)


def _gmlp_chunk_bwd(zp, do, b_in, ln_g, ln_b, w_s, b_s):
    def body(zu_ref, zv_ref, do_ref, bu_ref, bv_ref, g_ref, b_ref, ws_ref, bs_ref,
             dzu_ref, dzv_ref, dbu_ref, dbv_ref, dg_ref, db_ref, dws_ref, dbs_ref):
        i = pl.program_id(0)

        def acc(ref, val, idx=None):
            @pl.when(i == 0)
            def _():
                if idx is None:
                    ref[...] = val
                else:
                    ref[idx] = val

            @pl.when(i > 0)
            def _():
                if idx is None:
                    ref[...] += val
                else:
                    ref[idx] += val

        u, vhat, rstd, v, gate = _gm_forward(zu_ref, zv_ref, bu_ref, bv_ref, g_ref, b_ref, ws_ref, bs_ref)
        do = do_ref[...]
        du = do * gate
        dgate = do * u
        dvs = []
        for hd in range(GM_HEADS):
            sl = slice(GM_HD * hd, GM_HD * (hd + 1))
            dgh = dgate[:, sl]
            dghb = dgh.astype(_MXU)
            dvs.append(lax.dot_general(ws_ref[hd].astype(_MXU), dghb, (((0,), (0,)), ((), ())), preferred_element_type=F32))
            acc(dws_ref, lax.dot_general(dghb, v[:, sl], (((1,), (1,)), ((), ())), preferred_element_type=F32), hd)
            acc(dbs_ref, jnp.sum(dgh, axis=1, keepdims=True), hd)
        dv = jnp.concatenate(dvs, axis=1)
        acc(dg_ref, jnp.sum(dv * vhat, axis=0, keepdims=True))
        acc(db_ref, jnp.sum(dv, axis=0, keepdims=True))
        dvh = dv * g_ref[...]
        dzv = rstd * (dvh - jnp.mean(dvh, axis=-1, keepdims=True) - vhat * jnp.mean(dvh * vhat, axis=-1, keepdims=True))
        dpu = jax.vjp(jax.nn.gelu, zu_ref[...] + bu_ref[...])[1](du)[0]
        dpv = jax.vjp(jax.nn.gelu, zv_ref[...] + bv_ref[...])[1](dzv)[0]
        dzu_ref[...] = dpu.astype(dzu_ref.dtype)
        dzv_ref[...] = dpv.astype(dzv_ref.dtype)
        acc(dbu_ref, jnp.sum(dpu, axis=0, keepdims=True))
        acc(dbv_ref, jnp.sum(dpv, axis=0, keepdims=True))

    zu, zv, pars, full = _gm_specs()
    out_specs = [zu, zu, full(1, GM_HALF), full(1, GM_HALF), full(1, GM_HALF), full(1, GM_HALF),
                 full(GM_HEADS, GM_CHUNK, GM_CHUNK), full(GM_HEADS, GM_CHUNK, 1)]
    out_shape = [jax.ShapeDtypeStruct((T, GM_HALF), _MXU)] * 2 + [jax.ShapeDtypeStruct((1, GM_HALF), F32)] * 4 + \
                [jax.ShapeDtypeStruct((GM_HEADS, GM_CHUNK, GM_CHUNK), F32), jax.ShapeDtypeStruct((GM_HEADS, GM_CHUNK, 1), F32)]
    dzu, dzv, dbu, dbv, dg, db, dws, dbs = _pcall(
        body, name="gmlp_chunk_bwd", grid=(NCH,), in_specs=[zu, zv, zu, *pars], out_specs=out_specs, out_shape=out_shape,
        compiler_params=_cp(("arbitrary",)))(zp, zp, do, b_in, b_in, ln_g, ln_b, w_s, b_s)
    return jnp.concatenate([dzu, dzv], axis=1), jnp.concatenate([dbu, dbv], axis=1), dg, db, dws, dbs


def _gmlp_mixer_fwd(h, w_in, b_in, ln_g, ln_b, w_s, b_s, w_out):
    zp = _mm(h, w_in, name="gmlp_in")
    ug, ugt = _gmlp_chunk(zp, b_in, ln_g, ln_b, w_s, b_s[..., None])
    return _mm(ug, w_out, name="gmlp_out"), (zp, ugt)


def _gmlp_mixer_bwd(dy, h, saved, w_in, b_in, ln_g, ln_b, w_s, b_s, w_out):
    zp, ugt = saved
    dw_out = _mm(ugt, dy, name="gmlp_out_dw")
    do = _mm(dy, w_out, tb=True, name="gmlp_out_dx")
    dzp, db_in, dg, db, dws, dbs = _gmlp_chunk_bwd(zp, do, b_in, ln_g, ln_b, w_s, b_s[..., None])
    dw_in = _mm(h, dzp, ta=True, name="gmlp_in_dw")
    dh = _mm(dzp, w_in, tb=True, name="gmlp_in_dx")
    return dh, dw_in, db_in, dg, db, dws, dbs[..., 0], dw_out


def _loss_head(x, target):
    nct = NCTX // TM

    def body(x_ref, t_ref, l_ref, dx_ref):
        i = pl.program_id(0)
        err = jnp.where(i >= nct, x_ref[...] - t_ref[...], 0.0)
        dx_ref[...] = err * (1.0 / D)
        part = 0.5 * jnp.sum(jnp.sum(jnp.square(err), axis=-1, keepdims=True) * (1.0 / D), axis=0, keepdims=True)

        @pl.when(i == 0)
        def _():
            l_ref[...] = part

        @pl.when(i > 0)
        def _():
            l_ref[...] += part

    return _pcall(body, name="loss_head", grid=(NT,),
                  in_specs=[pl.BlockSpec((TM, D), lambda i: (i, 0)), pl.BlockSpec((TM, D), lambda i: (jnp.maximum(i - nct, 0), 0))],
                  out_specs=[pl.BlockSpec((1, 1), lambda i: (0, 0)), pl.BlockSpec((TM, D), lambda i: (i, 0))],
                  out_shape=[jax.ShapeDtypeStruct((1, 1), F32), jax.ShapeDtypeStruct((T, D), F32)],
                  compiler_params=_cp(("arbitrary",)))(x, target)


def _as2d(a):
    return a.reshape(-1, a.shape[-1])


def _adamw(w, g, m, v):
    shape = w.shape
    w2, g2, m2, v2 = _as2d(w), _as2d(g), _as2d(m), _as2d(v)
    R, C = w2.shape
    tr = _tile(R, 512, SUBLANE)
    c1 = 1.0 - B1 ** STEP
    c2 = 1.0 - B2 ** STEP

    def body(w_ref, g_ref, m_ref, v_ref, d_ref, nm_ref, nv_ref):
        g = g_ref[...]
        m = B1 * m_ref[...] + (1.0 - B1) * g
        v = B2 * v_ref[...] + (1.0 - B2) * jnp.square(g)
        nm_ref[...] = m
        nv_ref[...] = v
        d_ref[...] = -LR * ((m / c1) / (jnp.sqrt(v / c2) + EPS) + WD * w_ref[...])

    spec = pl.BlockSpec((tr, C), lambda i: (i, 0))
    outs = _pcall(body, name="adamw", grid=(R // tr,), in_specs=[spec] * 4, out_specs=[spec] * 3,
                  out_shape=[jax.ShapeDtypeStruct((R, C), F32)] * 3, compiler_params=_cp(("parallel",)))(w2, g2, m2, v2)
    return tuple(o.reshape(shape) for o in outs)


def _sum_slabs(x):
    n = x.shape[0]
    x = x.reshape(n, -1, x.shape[-1])
    _, R, C = x.shape
    tr = _tile(R, 256, 16)

    def body(x_ref, o_ref):
        acc = x_ref[0].astype(F32)
        for k in range(1, n):
            acc = acc + x_ref[k].astype(F32)
        o_ref[...] = acc

    return _pcall(body, name="sum_slabs", grid=(R // tr,), in_specs=[pl.BlockSpec((n, tr, C), lambda i: (0, i, 0))],
                  out_specs=pl.BlockSpec((tr, C), lambda i: (i, 0)), out_shape=jax.ShapeDtypeStruct((R, C), F32),
                  compiler_params=_cp(("parallel",)))(x)


def _comm_call(body, xs, out_shape, name):
    n = len(xs)
    hbm = pl.BlockSpec(memory_space=pl.ANY)
    return _pcall(body, name=name, in_specs=[hbm] * n, out_specs=[hbm] * n, out_shape=out_shape,
                  scratch_shapes=[pltpu.SemaphoreType.DMA((n, NDEV - 1)), pltpu.SemaphoreType.DMA((n, NDEV - 1)),
                                  pltpu.SemaphoreType.DMA((n,))],
                  compiler_params=pltpu.CompilerParams(has_side_effects=True))(*xs)


def _exchange(xs, name):
    n = len(xs)

    def body(*refs):
        x_refs, o_refs = refs[:n], refs[n:2 * n]
        send_sems, recv_sems, loc_sems = refs[2 * n:]
        mx, my, mc = lax.axis_index("x"), lax.axis_index("y"), lax.axis_index("c")
        me = 4 * mx + 2 * my + mc
        pending = []
        for a in range(n):
            mine = pltpu.make_async_copy(x_refs[a].at[me], o_refs[a].at[me], loc_sems.at[a])
            mine.start()
            pending.append(mine)
            for k in range(1, NDEV):
                px = 1 - mx if k & 4 else mx
                py = 1 - my if k & 2 else my
                pc = 1 - mc if k & 1 else mc
                cp = pltpu.make_async_remote_copy(
                    src_ref=x_refs[a].at[4 * px + 2 * py + pc], dst_ref=o_refs[a].at[me],
                    send_sem=send_sems.at[a, k - 1], recv_sem=recv_sems.at[a, k - 1],
                    device_id=(px, py, pc), device_id_type=pl.DeviceIdType.MESH)
                cp.start()
                pending.append(cp)
        for cp in pending:
            cp.wait()

    return _comm_call(body, xs, [jax.ShapeDtypeStruct(tuple(x.shape), x.dtype) for x in xs], name)


NCHIP = NDEV // 2


def _sibling_exchange(xs, name):
    n = len(xs)

    def body(*refs):
        x_refs, o_refs = refs[:n], refs[n:2 * n]
        send_sems, recv_sems, _ = refs[2 * n:]
        mx, my, mc = lax.axis_index("x"), lax.axis_index("y"), lax.axis_index("c")
        pending = []
        for a in range(n):
            for b in range(NCHIP):
                cp = pltpu.make_async_remote_copy(
                    src_ref=x_refs[a].at[2 * b + (1 - mc)], dst_ref=o_refs[a].at[b],
                    send_sem=send_sems.at[a, b], recv_sem=recv_sems.at[a, b],
                    device_id=(mx, my, 1 - mc), device_id_type=pl.DeviceIdType.MESH)
                cp.start()
                pending.append(cp)
        for cp in pending:
            cp.wait()

    return _comm_call(body, xs, [jax.ShapeDtypeStruct((NCHIP,) + tuple(x.shape[1:]), x.dtype) for x in xs], name)


def _chip_exchange(xs, name):
    n = len(xs)

    def body(*refs):
        x_refs, o_refs = refs[:n], refs[n:2 * n]
        send_sems, recv_sems, loc_sems = refs[2 * n:]
        mx, my, mc = lax.axis_index("x"), lax.axis_index("y"), lax.axis_index("c")
        chip = 2 * mx + my
        pending = []
        for a in range(n):
            mine = pltpu.make_async_copy(x_refs[a].at[chip], o_refs[a].at[chip], loc_sems.at[a])
            mine.start()
            pending.append(mine)
            for k in range(1, NCHIP):
                px = 1 - mx if k & 2 else mx
                py = 1 - my if k & 1 else my
                cp = pltpu.make_async_remote_copy(
                    src_ref=x_refs[a].at[2 * px + py], dst_ref=o_refs[a].at[chip],
                    send_sem=send_sems.at[a, k - 1], recv_sem=recv_sems.at[a, k - 1],
                    device_id=(px, py, mc), device_id_type=pl.DeviceIdType.MESH)
                cp.start()
                pending.append(cp)
        for cp in pending:
            cp.wait()

    return _comm_call(body, xs, [jax.ShapeDtypeStruct(tuple(x.shape), x.dtype) for x in xs], name)


def _pair_sum(x, y):
    _, _, R, C = x.shape
    tr = _tile(R, 128, 16)

    def body(x_ref, y_ref, o_ref):
        mc = lax.axis_index("c")
        mine = jnp.where(mc == 0, x_ref[:, 0].astype(F32), x_ref[:, 1].astype(F32))
        o_ref[...] = (mine + y_ref[...].astype(F32)).astype(o_ref.dtype)

    return _pcall(body, name="pair_sum", grid=(R // tr,),
                  in_specs=[pl.BlockSpec((NCHIP, 2, tr, C), lambda i: (0, 0, i, 0)), pl.BlockSpec((NCHIP, tr, C), lambda i: (0, i, 0))],
                  out_specs=pl.BlockSpec((NCHIP, tr, C), lambda i: (0, i, 0)), out_shape=jax.ShapeDtypeStruct((NCHIP, R, C), x.dtype),
                  compiler_params=_cp(("parallel",)))(x, y)


def _reduce_scatter(xs, name):
    from_sibling = _sibling_exchange(xs, name + "_d2d")
    pair = [_pair_sum(x.reshape(NCHIP, 2, -1, x.shape[-1]), y.reshape(NCHIP, -1, x.shape[-1])) for x, y in zip(xs, from_sibling)]
    got = _chip_exchange(pair, name + "_ici")
    return [_sum_slabs(g).reshape(x.shape[1:]) for g, x in zip(got, xs)]


def _gather(xs, name):
    n = len(xs)

    def body(*refs):
        x_refs, o_refs = refs[:n], refs[n:2 * n]
        send_sems, recv_sems, loc_sems = refs[2 * n:]
        mx, my, mc = lax.axis_index("x"), lax.axis_index("y"), lax.axis_index("c")
        me = 4 * mx + 2 * my + mc
        sibling = (mx, my, 1 - mc)
        chips = [(1 - mx, my), (mx, 1 - my), (1 - mx, 1 - my)]
        slot = lambda px, py, pc: 4 * px + 2 * py + pc

        def copy(a, k, s, to, from_input=False):
            return pltpu.make_async_remote_copy(
                src_ref=x_refs[a] if from_input else o_refs[a].at[s], dst_ref=o_refs[a].at[s],
                send_sem=send_sems.at[a, k], recv_sem=recv_sems.at[a, k], device_id=to, device_id_type=pl.DeviceIdType.MESH)

        sends, mines = [], []
        for a in range(n):
            mine = pltpu.make_async_copy(x_refs[a], o_refs[a].at[me], loc_sems.at[a])
            mine.start()
            mines.append(mine)
            first = [copy(a, 0, me, sibling, True)] + [copy(a, 1 + j, me, (cx, cy, mc), True) for j, (cx, cy) in enumerate(chips)]
            for cp in first:
                cp.start()
            sends += first
        for a in range(n):
            for j, (cx, cy) in enumerate(chips):
                s = slot(cx, cy, mc)
                copy(a, 1 + j, s, sibling).wait_recv()
                passed = copy(a, 4 + j, s, sibling)
                passed.start()
                sends.append(passed)
        for a in range(n):
            copy(a, 0, slot(*sibling), sibling).wait_recv()
            for j, (cx, cy) in enumerate(chips):
                copy(a, 4 + j, slot(cx, cy, 1 - mc), sibling).wait_recv()
        for cp in sends:
            cp.wait_send()
        for mine in mines:
            mine.wait()

    return _comm_call(body, xs, [jax.ShapeDtypeStruct((NDEV,) + tuple(x.shape), x.dtype) for x in xs], name)


SLAB_W = 1024
SLAB_ROWS = 16


def _pack(parts, lead=None):
    if lead is None:
        flat = jnp.concatenate([p.reshape(-1) for p in parts])
        n = flat.shape[0]
        padn = -n % (SLAB_ROWS * SLAB_W)
        return jnp.pad(flat, (0, padn)).reshape(-1, SLAB_W)
    flat = jnp.concatenate([p.reshape(lead, -1) for p in parts], axis=1)
    n = flat.shape[1]
    padn = -n % (SLAB_ROWS * SLAB_W)
    return jnp.pad(flat, ((0, 0), (0, padn))).reshape(lead, -1, SLAB_W)


def _unpack(buf, shapes, lead=None):
    out, off = [], 0
    flat = buf.reshape(-1) if lead is None else buf.reshape(lead, -1)
    for s in shapes:
        n = int(np.prod(s))
        if lead is None:
            out.append(flat[off:off + n].reshape(s))
        else:
            out.append(flat[:, off:off + n].reshape((lead,) + tuple(s)))
        off += n
    return out


def _gathered(blk, ax):
    m = jnp.moveaxis(blk, 0, ax)
    s = list(m.shape)
    return m.reshape(s[:ax] + [s[ax] * s[ax + 1]] + s[ax + 2:])


def _scattered(full, ax):
    s = list(full.shape)
    m = full.reshape(s[:ax] + [NDEV, s[ax] // NDEV] + s[ax + 1:])
    return jnp.moveaxis(m, ax, 0)


_MM_SHARDED = (("ffn_w_up", 2), ("ffn_w_down", 1), ("pool_w", 2), ("attn_w_qkv", 2), ("attn_w_o", 1),
               ("ssm_w_glu_a", 1), ("ssm_w_glu_b", 1), ("gmlp_w_in", 2), ("gmlp_w_out", 1))
_VEC_SHARDED = (("ffn_conv_w", 2), ("ssm_d", 1), ("gmlp_b_in", 1), ("gmlp_ln_g", 1), ("gmlp_ln_b", 1))
_REPLICATED = ("ln1_g", "ln1_b", "ln2_g", "ln2_b", "ffn_conv_b", "pool_b", "pool_scale", "attn_sink",
               "ssm_lambda_re", "ssm_lambda_im", "ssm_log_dt", "ssm_b_re", "ssm_b_im", "ssm_c_re", "ssm_c_im",
               "gmlp_w_s", "gmlp_b_s")
_WEIGHTS = ("c_ctx", "ada_w", "ada_b", "ln1_g", "ln1_b", "ln2_g", "ln2_b", "ffn_w_up", "ffn_conv_w", "ffn_conv_b", "ffn_w_down",
            "pool_w", "pool_b", "pool_scale", "attn_w_qkv", "attn_w_o", "attn_sink", "ssm_lambda_re", "ssm_lambda_im",
            "ssm_log_dt", "ssm_b_re", "ssm_b_im", "ssm_c_re", "ssm_c_im", "ssm_d", "ssm_w_glu_a", "ssm_w_glu_b",
            "gmlp_w_in", "gmlp_b_in", "gmlp_ln_g", "gmlp_ln_b", "gmlp_w_s", "gmlp_b_s", "gmlp_w_out")
N_MODS = 6
ADA_COLS = N_MODS * D // NDEV
PAD_ROWS = 16


def _silu_rows(x):
    return _rows(lambda v: (jax.nn.silu(v),), [x], [], [(x.shape[1], F32)], [], name="silu", tm=x.shape[0])[0]


def _step(x, c, ctx, loss_target, w, m, v):
    mx, my, mc = lax.axis_index("x"), lax.axis_index("y"), lax.axis_index("c")
    me = 4 * mx + 2 * my + mc

    vec_buf = _pack([w[n] for n, _ in _VEC_SHARDED] + [c])
    *mm_parts, vec_all = _gather([w[n].astype(_MXU) for n, _ in _MM_SHARDED] + [vec_buf], "gather_weights")
    vec_parts = _unpack(vec_all, [w[n].shape for n, _ in _VEC_SHARDED] + [c.shape], lead=NDEV)
    full = {n: _gathered(p, ax) for (n, ax), p in zip(_MM_SHARDED, mm_parts)}
    full.update({n: _gathered(p, ax) for (n, ax), p in zip(_VEC_SHARDED, vec_parts[:-1])})
    c_all = vec_parts[-1].reshape(NDEV, D)

    cc = jnp.concatenate([c_all, w["c_ctx"].reshape(1, D), jnp.zeros((PAD_ROWS - NDEV - 1, D), F32)], axis=0)
    silu_cc = _silu_rows(cc)
    ada_b_mine = lax.dynamic_slice(w["ada_b"], (0, me * ADA_COLS), (DEPTH, ADA_COLS))
    mods_mine = jnp.stack([_mm(silu_cc, w["ada_w"][l], name="ada_mods") + ada_b_mine[l][None, :] for l in range(DEPTH)], axis=1)
    per_dev = mods_mine[:NDEV].reshape(NDEV, DEPTH * ADA_COLS)
    cm = jnp.broadcast_to(mods_mine[NDEV].reshape(1, DEPTH * ADA_COLS), (NDEV, DEPTH * ADA_COLS))
    mods_all = _exchange([_pack([per_dev, cm], lead=NDEV)], "scatter_mods")[0]
    got = _unpack(mods_all, [(DEPTH, ADA_COLS), (DEPTH, ADA_COLS)], lead=NDEV)
    mods = got[0].transpose(1, 0, 2).reshape(DEPTH, N_MODS * D)
    cmods = got[1].transpose(1, 0, 2).reshape(DEPTH, N_MODS * D)
    P = [[jnp.stack([cmods[l, k * D:(k + 1) * D], mods[l, k * D:(k + 1) * D]]).reshape(2, 1, D) for k in range(N_MODS)]
         for l in range(DEPTH)]
    row = lambda a, l: a[l].reshape(1, 1, D)

    xs = jnp.concatenate([ctx[0], x[0]], axis=0)
    sp = tuple(w[n][0] for n in ("ssm_lambda_re", "ssm_lambda_im", "ssm_log_dt", "ssm_b_re", "ssm_b_im", "ssm_c_re", "ssm_c_im")) + \
        (full["ssm_d"].reshape(1, 1, D),)
    gm = (full["gmlp_w_in"][0], full["gmlp_b_in"], full["gmlp_ln_g"], full["gmlp_ln_b"], w["gmlp_w_s"][0], w["gmlp_b_s"][0],
          full["gmlp_w_out"][0])
    pool_args = (full["pool_w"][0], w["pool_b"], w["pool_scale"])
    saved = []
    h1_dtype = lambda l: _MXU if l in (1, 3) else F32
    h1 = _pre_mixer(xs, P[0][0], P[0][1], h1_dtype(0))
    for l in range(DEPTH):
        sh1, sc1, gt1, sh2, sc2, gt2 = P[l]
        if l == 0:
            y, ms = _pool_fwd(h1, *pool_args), None
        elif l == 1:
            y, ms = _attn_mixer_fwd(h1, full["attn_w_qkv"][0], full["attn_w_o"][0], w["attn_sink"])
        elif l == 2:
            y, ms = _ssm_mixer_fwd(h1, sp, full["ssm_w_glu_a"][0], full["ssm_w_glu_b"][0])
        else:
            y, ms = _gmlp_mixer_fwd(h1, *gm)
        x1, h2, h2t = _post_mixer(xs, y, gt1, row(w["ln1_g"], l), row(w["ln1_b"], l), sh2, sc2)
        f, (a, ut) = _ffn_fwd(h2, full["ffn_w_up"][l], full["ffn_conv_w"][l], w["ffn_conv_b"][l][None, :], full["ffn_w_down"][l])
        saved.append((xs, h1, y, ms, x1, h2t, a, ut, f))
        if l + 1 < DEPTH:
            xs, h1 = _post_ffn_pre_mixer(x1, f, gt2, row(w["ln2_g"], l), row(w["ln2_b"], l), P[l + 1][0], P[l + 1][1], h1_dtype(l + 1))
        else:
            xs = _post_ffn(x1, f, gt2, row(w["ln2_g"], l), row(w["ln2_b"], l))
    loss, dxs = _loss_head(xs, loss_target[0])

    g = {n: [None] * DEPTH for n in ("ln1_g", "ln1_b", "ln2_g", "ln2_b", "ffn_w_up", "ffn_conv_w", "ffn_conv_b", "ffn_w_down")}
    dP = [None] * DEPTH
    for l in reversed(range(DEPTH)):
        sh1, sc1, gt1, sh2, sc2, gt2 = P[l]
        x0, h1, y, ms, x1, h2t, a, ut, f = saved[l]
        dx1, df, dgt2, g["ln2_g"][l], g["ln2_b"][l] = _post_ffn_bwd(x1, f, dxs, gt2, row(w["ln2_g"], l), row(w["ln2_b"], l))
        dh2, g["ffn_w_up"][l], g["ffn_conv_w"][l], g["ffn_conv_b"][l], g["ffn_w_down"][l] = _ffn_bwd(
            df, h2t, a, ut, full["ffn_w_up"][l], full["ffn_conv_w"][l], w["ffn_conv_b"][l][None, :], full["ffn_w_down"][l])
        dx0, dy, dgt1, g["ln1_g"][l], g["ln1_b"][l], dsh2, dsc2 = _post_mixer_bwd(
            x0, y, dx1, dh2, gt1, row(w["ln1_g"], l), row(w["ln1_b"], l), sh2, sc2)
        if l == 0:
            dh, g["pool_w"], g["pool_b"], g["pool_scale"] = _pool_bwd(h1, dy, *pool_args)
            dhs = [dh]
        elif l == 1:
            dh, g["attn_w_qkv"], g["attn_w_o"], g["attn_sink"] = _attn_mixer_bwd(dy, h1, ms, full["attn_w_qkv"][0], full["attn_w_o"][0])
            dhs = [dh]
        elif l == 2:
            dhs, dsp, dd, g["ssm_w_glu_a"], g["ssm_w_glu_b"] = _ssm_mixer_bwd(dy, h1, ms, sp, full["ssm_w_glu_a"][0], full["ssm_w_glu_b"][0])
            for n, d_ in zip(("ssm_lambda_re", "ssm_lambda_im", "ssm_log_dt", "ssm_b_re", "ssm_b_im", "ssm_c_re", "ssm_c_im"), dsp):
                g[n] = d_
            g["ssm_d"] = dd.reshape(1, D)
        else:
            (dh, g["gmlp_w_in"], g["gmlp_b_in"], g["gmlp_ln_g"], g["gmlp_ln_b"], g["gmlp_w_s"], g["gmlp_b_s"],
             g["gmlp_w_out"]) = _gmlp_mixer_bwd(dy, h1, ms, *gm)
            dhs = [dh]
        dxs, dsh1, dsc1 = _pre_mixer_bwd(x0, dhs, dx0, sh1, sc1)
        dP[l] = (dsh1, dsc1, dgt1, dsh2, dsc2, dgt2)
    grad_x = dxs[NCTX:][None]
    dmods = jnp.stack([jnp.concatenate([p[1, 0] for p in dP[l]]) for l in range(DEPTH)])
    dcmods = jnp.stack([jnp.concatenate([p[0, 0] for p in dP[l]]) for l in range(DEPTH)])

    gfull = {n: (jnp.stack(g[n]) if isinstance(g[n], list) else g[n]) for n in g}
    sharded_names = [n for n, _ in _MM_SHARDED] + [n for n, _ in _VEC_SHARDED]
    sharded_axes = dict(_MM_SHARDED + _VEC_SHARDED)

    def as_param(n, a):
        shard = w[n].shape
        ax = sharded_axes.get(n)
        fs = tuple(s * NDEV if i == ax else s for i, s in enumerate(shard))
        return a.reshape(fs)

    rep = jnp.concatenate([as_param(n, gfull[n]).reshape(-1) for n in _REPLICATED])
    n_rep = rep.shape[0]
    rep = jnp.pad(rep, (0, -n_rep % (NDEV * SLAB_W))).reshape(NDEV, -1)
    by_dev = lambda a: a.reshape(DEPTH, NDEV, ADA_COLS).transpose(1, 0, 2)
    big = [_scattered(as_param(n, gfull[n]), ax).astype(_MXU) for n, ax in _MM_SHARDED]
    parts = [_scattered(as_param(n, gfull[n]), ax) for n, ax in _VEC_SHARDED] + [rep, by_dev(dmods), by_dev(dcmods)]
    grads = {n: r for (n, _), r in zip(_MM_SHARDED, _reduce_scatter(big, "scatter_grads"))}
    grads_in = _exchange([_pack(parts, lead=NDEV)], "scatter_small")[0]
    shapes = [w[n].shape for n, _ in _VEC_SHARDED] + [(rep.shape[1],), (DEPTH, ADA_COLS), (DEPTH, ADA_COLS)]
    red = _unpack(_sum_slabs(grads_in), shapes)
    grads.update({n: r for (n, _), r in zip(_VEC_SHARDED, red)})
    rep_mine, dcm = red[-3], red[-1]
    dm_all = _unpack(grads_in, shapes, lead=NDEV)[-2]

    e_rows = jnp.concatenate([dm_all, dcm[None], jnp.zeros((PAD_ROWS - NDEV - 1, DEPTH, ADA_COLS), F32)], axis=0)
    grads["ada_w"] = jnp.stack([_mm(silu_cc, e_rows[:, l], ta=True, name="ada_dw") for l in range(DEPTH)])
    ada_b_blk = jnp.sum(e_rows, axis=0)
    dcm_rows = jnp.concatenate([dcm[None], jnp.zeros((PAD_ROWS - 1, DEPTH, ADA_COLS), F32)], axis=0)
    cpart = sum(_mm(dcm_rows[:, l], w["ada_w"][l], tb=True, name="ada_dc")[0] for l in range(DEPTH))

    small_all = _gather([_pack([rep_mine, ada_b_blk, cpart])], "gather_small")[0]
    sm = _unpack(small_all, [rep_mine.shape, (DEPTH, ADA_COLS), (D,)], lead=NDEV)
    rep_full = sm[0].reshape(-1)[:n_rep]
    off = 0
    for n in _REPLICATED:
        k = int(np.prod(w[n].shape))
        grads[n] = rep_full[off:off + k].reshape(w[n].shape)
        off += k
    grads["ada_b"] = sm[1].transpose(1, 0, 2).reshape(DEPTH, N_MODS * D)
    csum = _unpack(_sum_slabs(small_all), [rep_mine.shape, (DEPTH, ADA_COLS), (D,)])[2]

    def dsilu(vv, dd):
        return (jax.vjp(jax.nn.silu, vv)[1](dd)[0],)

    grads["c_ctx"] = _rows(dsilu, [jnp.broadcast_to(w["c_ctx"][None], (SUBLANE, D)), jnp.broadcast_to(csum[None], (SUBLANE, D))],
                           [], [(D, F32)], [], name="dsilu", tm=SUBLANE)[0][0]

    delta, new_m, new_v = {}, {}, {}
    for n in _WEIGHTS:
        delta[n], new_m[n], new_v[n] = _adamw(w[n], grads[n], m[n], v[n])
    loss = lax.psum(loss[0, 0], ("x", "y", "c"))
    return loss, grad_x, grads, delta, new_m, new_v


def kernel(x, c, ctx, c_ctx, ada_w, ada_b, ln1_g, ln1_b, ln2_g, ln2_b, ffn_w_up, ffn_conv_w, ffn_conv_b, ffn_w_down, pool_w, pool_b, pool_scale, attn_w_qkv, attn_w_o, attn_sink, ssm_lambda_re, ssm_lambda_im, ssm_log_dt, ssm_b_re, ssm_b_im, ssm_c_re, ssm_c_im, ssm_d, ssm_w_glu_a, ssm_w_glu_b, gmlp_w_in, gmlp_b_in, gmlp_ln_g, gmlp_ln_b, gmlp_w_s, gmlp_b_s, gmlp_w_out, loss_target, m_c_ctx, m_ada_w, m_ada_b, m_ln1_g, m_ln1_b, m_ln2_g, m_ln2_b, m_ffn_w_up, m_ffn_conv_w, m_ffn_conv_b, m_ffn_w_down, m_pool_w, m_pool_b, m_pool_scale, m_attn_w_qkv, m_attn_w_o, m_attn_sink, m_ssm_lambda_re, m_ssm_lambda_im, m_ssm_log_dt, m_ssm_b_re, m_ssm_b_im, m_ssm_c_re, m_ssm_c_im, m_ssm_d, m_ssm_w_glu_a, m_ssm_w_glu_b, m_gmlp_w_in, m_gmlp_b_in, m_gmlp_ln_g, m_gmlp_ln_b, m_gmlp_w_s, m_gmlp_b_s, m_gmlp_w_out, v_c_ctx, v_ada_w, v_ada_b, v_ln1_g, v_ln1_b, v_ln2_g, v_ln2_b, v_ffn_w_up, v_ffn_conv_w, v_ffn_conv_b, v_ffn_w_down, v_pool_w, v_pool_b, v_pool_scale, v_attn_w_qkv, v_attn_w_o, v_attn_sink, v_ssm_lambda_re, v_ssm_lambda_im, v_ssm_log_dt, v_ssm_b_re, v_ssm_b_im, v_ssm_c_re, v_ssm_c_im, v_ssm_d, v_ssm_w_glu_a, v_ssm_w_glu_b, v_gmlp_w_in, v_gmlp_b_in, v_gmlp_ln_g, v_gmlp_ln_b, v_gmlp_w_s, v_gmlp_b_s, v_gmlp_w_out):
    args = dict(locals())
    w = {n: args[n] for n in _WEIGHTS}
    m = {n: args["m_" + n] for n in _WEIGHTS}
    v = {n: args["v_" + n] for n in _WEIGHTS}
    loss, grad_x, grads, delta, new_m, new_v = _step(x, c, ctx, loss_target, w, m, v)
    return (loss, grad_x, *[grads[n] for n in _WEIGHTS], *[delta[n] for n in _WEIGHTS],
            *[new_m[n] for n in _WEIGHTS], *[new_v[n] for n in _WEIGHTS])
```
